```python
import jax, jax.numpy as jnp
from jax import lax
import numpy as np

D_MODEL = 1024
BATCH = 2
SEQ = 8192
DEPTH = 2
DEC_BATCH = 128
DEC_SEQ = 1
PAST_LEN = 8192
PAGE_SIZE = 128

N_MIXERS = 2
N_GLA_LAYERS = (DEPTH + 1) // 2
N_SWA_LAYERS = DEPTH // 2
D_FF = 4 * D_MODEL
NORM_EPS = 1e-6

GLA_HEADS = 4
GLA_DK = D_MODEL // 2
GLA_DV = D_MODEL
GLA_DK_HEAD = GLA_DK // GLA_HEADS
GLA_DV_HEAD = GLA_DV // GLA_HEADS
GLA_GATE_RANK = 16
GLA_TAU = 16.0
GLA_CHUNK = 64
GLA_IN = 2 * GLA_DK + 2 * GLA_DV + GLA_GATE_RANK

SWA_HEAD_DIM = 64
SWA_HEADS = D_MODEL // SWA_HEAD_DIM
SWA_KV_HEADS = 4
SWA_GROUP = SWA_HEADS // SWA_KV_HEADS
SWA_WINDOW = 128
SWA_BLOCK = 128
SWA_QKV = (SWA_HEADS + 2 * SWA_KV_HEADS) * SWA_HEAD_DIM
ROPE_THETA = 500000.0
ROPE_DIM = SWA_HEAD_DIM // 4

kernel_name = 'hybrid_gla_swa_sink_decoder_step'


def rms_norm(x, g):
    xf = x.astype(jnp.float32)
    y = xf * lax.rsqrt(jnp.mean(xf * xf, axis=-1, keepdims=True) + NORM_EPS)
    return (y * g.astype(jnp.float32)).astype(x.dtype)


def partial_rope(x, pos):
    half = ROPE_DIM // 2
    inv = jnp.power(ROPE_THETA, -jnp.arange(half, dtype=jnp.float32) * 2.0 / ROPE_DIM)
    ang = pos.astype(jnp.float32)[:, None] * inv[None, :]
    cos = jnp.cos(ang)[None, :, None, :]
    sin = jnp.sin(ang)[None, :, None, :]
    xf = x.astype(jnp.float32)
    x1 = xf[..., :half]
    x2 = xf[..., half:ROPE_DIM]
    out = jnp.concatenate([x1 * cos - x2 * sin, x2 * cos + x1 * sin, xf[..., ROPE_DIM:]], axis=-1)
    return out.astype(x.dtype)


def gla_recurrence(q, k, v, log_a, s0):
    b_, L = q.shape[:2]
    c = min(GLA_CHUNK, L)
    n = -(-L // c)
    pad = n * c - L
    q, k, v, log_a = [t.astype(jnp.float32) for t in (q, k, v, log_a)]
    if pad:
        pw = ((0, 0), (0, pad), (0, 0), (0, 0))
        q, k, v, log_a = [jnp.pad(t, pw) for t in (q, k, v, log_a)]
    q, k, v, log_a = [t.reshape(b_, n, c, *t.shape[2:]) for t in (q, k, v, log_a)]
    cum = jnp.cumsum(log_a, axis=2)
    total = cum[:, :, -1]
    q_dec = q * jnp.exp(cum)
    k_inv = k * jnp.exp(-cum)
    k_end = k * jnp.exp(total[:, :, None] - cum)
    causal = jnp.tril(jnp.ones((c, c), dtype=bool))
    att = jnp.where(causal, jnp.einsum('bnthd,bnshd->bnhts', q_dec, k_inv), 0.0)
    o_intra = jnp.einsum('bnhts,bnshv->bnthv', att, v)

    def step(s, xs):
        qd, ke, vv, tot = xs
        o = jnp.einsum('bthd,bhdv->bthv', qd, s)
        s = jnp.exp(tot)[..., None] * s + jnp.einsum('bshd,bshv->bhdv', ke, vv)
        return s, o

    xs = tuple(jnp.moveaxis(t, 1, 0) for t in (q_dec, k_end, v, total))
    s_fin, o_inter = lax.scan(step, s0.astype(jnp.float32), xs)
    o = o_intra + jnp.moveaxis(o_inter, 0, 1)
    o = o.reshape(b_, n * c, *o.shape[3:])[:, :L]
    return o, s_fin


def gla_mixer(x, s0, w_in, w_gate2, b_gate, g_head, w_out):
    b_, L, _ = x.shape
    h = x @ w_in
    q, k, v, r, z = jnp.split(h, [GLA_DK, 2 * GLA_DK, 2 * GLA_DK + GLA_DV, 2 * GLA_DK + 2 * GLA_DV], axis=-1)
    log_a = jax.nn.log_sigmoid((z @ w_gate2 + b_gate).astype(jnp.float32)) / GLA_TAU
    q = q.reshape(b_, L, GLA_HEADS, GLA_DK_HEAD) * (GLA_DK_HEAD ** -0.5)
    k = k.reshape(b_, L, GLA_HEADS, GLA_DK_HEAD)
    v = v.reshape(b_, L, GLA_HEADS, GLA_DV_HEAD)
    log_a = log_a.reshape(b_, L, GLA_HEADS, GLA_DK_HEAD)
    o, s_new = gla_recurrence(q, k, v, log_a, s0)
    o = rms_norm(o, g_head).astype(x.dtype).reshape(b_, L, GLA_DV) * jax.nn.silu(r)
    return o @ w_out, s_new


def swa_qkv(x, pos, w_qkv, b_qkv):
    b_, L, _ = x.shape
    h = x @ w_qkv + b_qkv
    q, k, v = jnp.split(h, [SWA_HEADS * SWA_HEAD_DIM, (SWA_HEADS + SWA_KV_HEADS) * SWA_HEAD_DIM], axis=-1)
    q = partial_rope(q.reshape(b_, L, SWA_HEADS, SWA_HEAD_DIM), pos)
    k = partial_rope(k.reshape(b_, L, SWA_KV_HEADS, SWA_HEAD_DIM), pos)
    v = v.reshape(b_, L, SWA_KV_HEADS, SWA_HEAD_DIM)
    return q, k, v


def sink_probs(scores, mask, sinks):
    s = jnp.where(mask, scores, -jnp.inf)
    sk = sinks.astype(jnp.float32).reshape(SWA_KV_HEADS, SWA_GROUP)[:, :, None, None]
    m = jnp.maximum(jnp.max(s, axis=-1, keepdims=True), sk)
    p = jnp.exp(s - m)
    return p / (jnp.sum(p, axis=-1, keepdims=True) + jnp.exp(sk - m))


def swa_attend_prompt(q, k, v, sinks):
    b_, L = q.shape[:2]
    nb = L // SWA_BLOCK
    qb = q.reshape(b_, nb, SWA_BLOCK, SWA_KV_HEADS, SWA_GROUP, SWA_HEAD_DIM)

    def with_prev(t):
        t = t.reshape(b_, nb, SWA_BLOCK, SWA_KV_HEADS, SWA_HEAD_DIM)
        prev = jnp.concatenate([jnp.zeros_like(t[:, :1]), t[:, :-1]], axis=1)
        return jnp.concatenate([prev, t], axis=2)

    kk, vv = with_prev(k), with_prev(v)
    scores = jnp.einsum('bnqkgd,bnskd->bnkgqs', qb, kk).astype(jnp.float32) * (SWA_HEAD_DIM ** -0.5)
    i = jnp.arange(SWA_BLOCK)[:, None]
    j = jnp.arange(2 * SWA_BLOCK)[None, :]
    diff = SWA_BLOCK + i - j
    key_pos = jnp.arange(nb)[:, None, None] * SWA_BLOCK - SWA_BLOCK + j[None]
    mask = (diff >= 0) & (diff <= SWA_WINDOW) & (key_pos >= 0)
    p = sink_probs(scores, mask[None, :, None, None], sinks)
    o = jnp.einsum('bnkgqs,bnskd->bnqkgd', p.astype(vv.dtype), vv)
    return o.reshape(b_, L, SWA_HEADS * SWA_HEAD_DIM)


def swa_attend_sample(q, k_new, v_new, buf_k, buf_v, sinks):
    db, T = q.shape[:2]
    w = buf_k.shape[1]
    kk = jnp.concatenate([buf_k, k_new], axis=1)
    vv = jnp.concatenate([buf_v, v_new], axis=1)
    qg = q.reshape(db, T, SWA_KV_HEADS, SWA_GROUP, SWA_HEAD_DIM)
    scores = jnp.einsum('bqkgd,bskd->bkgqs', qg, kk).astype(jnp.float32) * (SWA_HEAD_DIM ** -0.5)
    diff = w + jnp.arange(T)[:, None] - jnp.arange(w + T)[None, :]
    mask = (diff >= 0) & (diff <= SWA_WINDOW)
    p = sink_probs(scores, mask, sinks)
    o = jnp.einsum('bkgqs,bskd->bqkgd', p.astype(vv.dtype), vv)
    return o.reshape(db, T, SWA_HEADS * SWA_HEAD_DIM), kk[:, -w:], vv[:, -w:]


def sqrelu_mlp(x, w_up, w_down):
    return jnp.square(jax.nn.relu(x @ w_up)) @ w_down


def setup_inputs(seed: int = 0) -> dict:
    key = jax.random.key(seed)
    ks = jax.random.split(key, 24)
    f32 = jnp.float32

    def nrm(k, shape, scale):
        return jax.random.normal(k, shape, f32) * scale

    def gains(k, n, d):
        return 1.0 + 0.05 * jax.random.normal(k, (n, d), f32)

    win_buf = min(SWA_WINDOW, PAST_LEN)
    return {
        'x_prompt': nrm(ks[0], (BATCH, SEQ, D_MODEL), 1.0),
        'x_sample': nrm(ks[1], (DEC_BATCH, DEC_SEQ, D_MODEL), 1.0),
        'state_gla': nrm(ks[2], (N_GLA_LAYERS, DEC_BATCH, GLA_HEADS, GLA_DK_HEAD, GLA_DV_HEAD), 0.5),
        'cache_swa_k': nrm(ks[3], (N_SWA_LAYERS, DEC_BATCH, win_buf, SWA_KV_HEADS, SWA_HEAD_DIM), 1.0),
        'cache_swa_v': nrm(ks[4], (N_SWA_LAYERS, DEC_BATCH, win_buf, SWA_KV_HEADS, SWA_HEAD_DIM), 1.0),
        'gla_w_in': nrm(ks[5], (N_GLA_LAYERS, D_MODEL, GLA_IN), D_MODEL ** -0.5),
        'gla_w_gate2': nrm(ks[6], (N_GLA_LAYERS, GLA_GATE_RANK, GLA_DK), GLA_GATE_RANK ** -0.5),
        'gla_b_gate': nrm(ks[7], (N_GLA_LAYERS, GLA_DK), 0.1),
        'gla_g_head': gains(ks[8], N_GLA_LAYERS, GLA_DV_HEAD),
        'gla_w_out': nrm(ks[9], (N_GLA_LAYERS, GLA_DV, D_MODEL), GLA_DV ** -0.5),
        'swa_w_qkv': nrm(ks[10], (N_SWA_LAYERS, D_MODEL, SWA_QKV), D_MODEL ** -0.5),
        'swa_b_qkv': nrm(ks[11], (N_SWA_LAYERS, SWA_QKV), 0.02),
        'swa_sinks': nrm(ks[12], (N_SWA_LAYERS, SWA_HEADS), 1.0),
        'swa_w_out': nrm(ks[13], (N_SWA_LAYERS, SWA_HEADS * SWA_HEAD_DIM, D_MODEL), (SWA_HEADS * SWA_HEAD_DIM) ** -0.5),
        'swa_b_out': nrm(ks[14], (N_SWA_LAYERS, D_MODEL), 0.02),
        'norm_mix_pre': gains(ks[15], DEPTH, D_MODEL),
        'norm_mix_post': gains(ks[16], DEPTH, D_MODEL),
        'norm_ffn_pre': gains(ks[17], DEPTH, D_MODEL),
        'norm_ffn_post': gains(ks[18], DEPTH, D_MODEL),
        'ffn_w_up': nrm(ks[19], (DEPTH, D_MODEL, D_FF), D_MODEL ** -0.5),
        'ffn_w_down': nrm(ks[20], (DEPTH, D_FF, D_MODEL), D_FF ** -0.5),
    }


def reference(x_prompt, x_sample, state_gla, cache_swa_k, cache_swa_v,
              gla_w_in, gla_w_gate2, gla_b_gate, gla_g_head, gla_w_out,
              swa_w_qkv, swa_b_qkv, swa_sinks, swa_w_out, swa_b_out,
              norm_mix_pre, norm_mix_post, norm_ffn_pre, norm_ffn_post,
              ffn_w_up, ffn_w_down):
    b_p, L_p = x_prompt.shape[:2]
    T_s = x_sample.shape[1]
    pos_p = jnp.arange(L_p)
    pos_s = PAST_LEN + jnp.arange(T_s)
    hp, hs = x_prompt, x_sample
    gla_p, gla_s, kp_l, vp_l, ks_l, vs_l = [], [], [], [], [], []
    for layer in range(DEPTH):
        slot = layer // N_MIXERS
        ap = rms_norm(hp, norm_mix_pre[layer])
        a_s = rms_norm(hs, norm_mix_pre[layer])
        if layer % N_MIXERS == 0:
            s0 = jnp.zeros((b_p, GLA_HEADS, GLA_DK_HEAD, GLA_DV_HEAD), jnp.float32)
            mp, sp = gla_mixer(ap, s0, gla_w_in[slot], gla_w_gate2[slot], gla_b_gate[slot], gla_g_head[slot], gla_w_out[slot])
            ms, ss = gla_mixer(a_s, state_gla[slot], gla_w_in[slot], gla_w_gate2[slot], gla_b_gate[slot], gla_g_head[slot], gla_w_out[slot])
            gla_p.append(sp)
            gla_s.append(ss)
        else:
            qp, kp, vp = swa_qkv(ap, pos_p, swa_w_qkv[slot], swa_b_qkv[slot])
            op = swa_attend_prompt(qp, kp, vp, swa_sinks[slot])
            wp = min(SWA_WINDOW, L_p)
            kp_l.append(kp[:, -wp:])
            vp_l.append(vp[:, -wp:])
            qs, k_s, v_s = swa_qkv(a_s, pos_s, swa_w_qkv[slot], swa_b_qkv[slot])
            os_, nk, nv = swa_attend_sample(qs, k_s, v_s, cache_swa_k[slot], cache_swa_v[slot], swa_sinks[slot])
            ks_l.append(nk)
            vs_l.append(nv)
            mp = op @ swa_w_out[slot] + swa_b_out[slot]
            ms = os_ @ swa_w_out[slot] + swa_b_out[slot]
        hp = hp + rms_norm(mp, norm_mix_post[layer])
        hs = hs + rms_norm(ms, norm_mix_post[layer])
        hp = hp + rms_norm(sqrelu_mlp(rms_norm(hp, norm_ffn_pre[layer]), ffn_w_up[layer], ffn_w_down[layer]), norm_ffn_post[layer])
        hs = hs + rms_norm(sqrelu_mlp(rms_norm(hs, norm_ffn_pre[layer]), ffn_w_up[layer], ffn_w_down[layer]), norm_ffn_post[layer])
    return (hp, hs, jnp.stack(gla_p), jnp.stack(gla_s), jnp.stack(kp_l), jnp.stack(vp_l), jnp.stack(ks_l), jnp.stack(vs_l))
```

```python
import functools

import jax
import jax.numpy as jnp
from jax import lax
from jax.experimental import pallas as pl
from jax.experimental.pallas import tpu as pltpu

F32 = jnp.float32
BF16 = jnp.bfloat16

D_MODEL = 1024
D_FF = 4 * D_MODEL
NORM_EPS = 1e-6

GLA_HEADS = 4
GLA_DK = D_MODEL // 2
GLA_DV = D_MODEL
GLA_DK_HEAD = GLA_DK // GLA_HEADS
GLA_DV_HEAD = GLA_DV // GLA_HEADS
GLA_GATE_RANK = 16
GLA_TAU = 16.0
GLA_CHUNK = 64
GLA_MAIN = 2 * GLA_DK + 2 * GLA_DV

SWA_HEAD_DIM = 64
SWA_HEADS = D_MODEL // SWA_HEAD_DIM
SWA_KV_HEADS = 4
SWA_GROUP = SWA_HEADS // SWA_KV_HEADS
SWA_WINDOW = 128
SWA_Q = SWA_HEADS * SWA_HEAD_DIM
SWA_KV = SWA_KV_HEADS * SWA_HEAD_DIM
SWA_QKV = SWA_Q + 2 * SWA_KV
ROPE_THETA = 500000.0
ROPE_DIM = SWA_HEAD_DIM // 4
ROPE_HALF = ROPE_DIM // 2

LANES = 128
FFN_CHUNK = 512
VMEM_LIMIT = 56 * 1024 * 1024
NEG_BIG = -1e30


def _mm(a, b):
    return jnp.dot(a, b, preferred_element_type=F32)


def _mm_nt(a, b):
    return lax.dot_general(a, b, (((1,), (1,)), ((), ())), preferred_element_type=F32)


def _mm_tn(a, b):
    return lax.dot_general(a, b, (((0,), (0,)), ((), ())), preferred_element_type=F32)


def _rms(x, g):
    ms = jnp.mean(x * x, axis=-1, keepdims=True)
    return x * lax.rsqrt(ms + NORM_EPS) * g


def _split3(x):
    hi = x.astype(BF16)
    r1 = x - hi.astype(F32)
    mid = r1.astype(BF16)
    lo = (r1 - mid.astype(F32)).astype(BF16)
    return hi, mid, lo


def _ffn(a_bf16, wup_ref, wdn_ref):
    acc = None
    for c in range(D_FF // FFN_CHUNK):
        cols = slice(c * FFN_CHUNK, (c + 1) * FFN_CHUNK)
        u = _mm(a_bf16, wup_ref[:, cols])
        u = jnp.square(jnp.maximum(u, 0.0)).astype(BF16)
        p = _mm(u, wdn_ref[cols, :])
        acc = p if acc is None else acc + p
    return acc


def _in0_body(x_ref, g_ref, w_ref, wz_ref, wg_ref, bg_ref, q_ref, k_ref, v_ref, r_ref, la_ref):
    a = _rms(x_ref[...], g_ref[...]).astype(BF16)
    q_ref[...] = _mm(a, w_ref[:, 0:GLA_DK]) * (GLA_DK_HEAD ** -0.5)
    k_ref[...] = _mm(a, w_ref[:, GLA_DK:2 * GLA_DK])
    for c in range(GLA_DV // 512):
        cols = slice(c * 512, (c + 1) * 512)
        v_ref[:, cols] = _mm(a, w_ref[:, 2 * GLA_DK + c * 512:2 * GLA_DK + (c + 1) * 512]).astype(v_ref.dtype)
        r_ref[:, cols] = _mm(a, w_ref[:, 2 * GLA_DK + GLA_DV + c * 512:2 * GLA_DK + GLA_DV + (c + 1) * 512])
    z = _mm(a, wz_ref[...]).astype(BF16)
    zg = _mm(z, wg_ref[...]) + bg_ref[...]
    la_ref[...] = (jnp.minimum(zg, 0.0) - jnp.log1p(jnp.exp(-jnp.abs(zg)))) * (1.0 / GLA_TAU)


def _mid_body(o_ref, r_ref, h_ref, rc_ref, ra_ref, rb_ref,
              gh_ref, wo_ref, gpost_ref, gfpre_ref, wup_ref, wdn_ref, gfpost_ref,
              gpre1_ref, wqkv_ref, bqkv_ref,
              h2_ref, q1_ref, k1_ref, v1_ref):
    m = None
    for hh in range(GLA_HEADS):
        cols = slice(hh * GLA_DV_HEAD, (hh + 1) * GLA_DV_HEAD)
        on = _rms(o_ref[:, cols], gh_ref[...])
        r = r_ref[:, cols]
        u = (on * (r * (1.0 / (1.0 + jnp.exp(-r))))).astype(BF16)
        p = _mm(u, wo_ref[cols, :])
        m = p if m is None else m + p
    h1 = h_ref[...] + _rms(m, gpost_ref[...])
    f = _ffn(_rms(h1, gfpre_ref[...]).astype(BF16), wup_ref, wdn_ref)
    h2 = h1 + _rms(f, gfpost_ref[...])
    h2_ref[...] = h2
    a3 = _rms(h2, gpre1_ref[...]).astype(BF16)
    rc, ra, rb = rc_ref[...], ra_ref[...], rb_ref[...]
    for c in range((SWA_Q + SWA_KV) // LANES):
        cols = slice(c * LANES, (c + 1) * LANES)
        x = _mm(a3, wqkv_ref[:, cols]) + bqkv_ref[:, cols]
        y = x * rc + pltpu.roll(x, LANES - ROPE_HALF, axis=1) * ra + pltpu.roll(x, ROPE_HALF, axis=1) * rb
        if c < SWA_Q // LANES:
            q1_ref[:, cols] = y
        else:
            k1_ref[:, c * LANES - SWA_Q:(c + 1) * LANES - SWA_Q] = y
    v1_ref[...] = _mm(a3, wqkv_ref[:, SWA_Q + SWA_KV:SWA_QKV]) + bqkv_ref[:, SWA_Q + SWA_KV:SWA_QKV]


def _out_body(at_ref, h_ref, wo_ref, bo_ref, gpost_ref, gfpre_ref, wup_ref, wdn_ref, gfpost_ref, y_ref):
    m = _mm(at_ref[...].astype(BF16), wo_ref[...]) + bo_ref[...]
    h1 = h_ref[...] + _rms(m, gpost_ref[...])
    f = _ffn(_rms(h1, gfpre_ref[...]).astype(BF16), wup_ref, wdn_ref)
    y_ref[...] = h1 + _rms(f, gfpost_ref[...])


def _tok_call(body, n_rows, tm, row_inputs, const_inputs, out_widths, out_dtypes, name):
    assert n_rows % tm == 0
    in_specs, args = [], []
    for arr, imap in row_inputs:
        in_specs.append(pl.BlockSpec((tm, arr.shape[1]), imap if imap is not None else (lambda i: (i, 0))))
        args.append(arr)
    for arr in const_inputs:
        in_specs.append(pl.BlockSpec(arr.shape, lambda i: (0, 0), pipeline_mode=pl.Buffered(1)))
        args.append(arr)
    out_specs = [pl.BlockSpec((tm, w), lambda i: (i, 0)) for w in out_widths]
    out_shape = [jax.ShapeDtypeStruct((n_rows, w), dt) for w, dt in zip(out_widths, out_dtypes)]
    return pl.pallas_call(
        body,
        grid=(n_rows // tm,),
        in_specs=in_specs,
        out_specs=out_specs,
        out_shape=out_shape,
        compiler_params=pltpu.CompilerParams(dimension_semantics=("arbitrary",), vmem_limit_bytes=VMEM_LIMIT),
        name=name,
    )(*args)


def _gla_prompt_body(q_ref, k_ref, v_ref, la_ref, o_ref, sfin_ref, st_ref):
    t = pl.program_id(2)
    tg = q_ref.shape[0]
    c_len = GLA_CHUNK

    @pl.when(t == 0)
    def _():
        st_ref[...] = jnp.zeros_like(st_ref)

    row = lax.broadcasted_iota(jnp.int32, (tg, tg), 0)
    col = lax.broadcasted_iota(jnp.int32, (tg, tg), 1)
    lower = jnp.where((row // c_len == col // c_len) & (col <= row), 1.0, 0.0).astype(BF16)
    hi, mid, lo = _split3(la_ref[...])
    cum = _mm(lower, hi) + _mm(lower, mid) + _mm(lower, lo)
    causal = lax.broadcasted_iota(jnp.int32, (c_len, c_len), 0) >= lax.broadcasted_iota(jnp.int32, (c_len, c_len), 1)
    for ci in range(tg // c_len):
        rows = slice(ci * c_len, (ci + 1) * c_len)
        cum_c = cum[rows]
        tot = cum_c[c_len - 1:c_len, :]
        q_c = q_ref[rows, :]
        k_c = k_ref[rows, :]
        v_c = v_ref[rows, :]
        qd = (q_c * jnp.exp(cum_c)).astype(BF16)
        ki = (k_c * jnp.exp(-cum_c)).astype(BF16)
        ke = (k_c * jnp.exp(tot - cum_c)).astype(BF16)
        att = jnp.where(causal, _mm_nt(qd, ki), 0.0).astype(BF16)
        st = st_ref[...]
        o_ref[rows, :] = _mm(att, v_c) + _mm_nt(qd, st.astype(BF16))
        st_ref[...] = jnp.exp(tot) * st + _mm_tn(v_c, ke)

    @pl.when(t == pl.num_programs(2) - 1)
    def _():
        sfin_ref[0, 0] = st_ref[...].T


def _gla_prompt(q, k, v, la, batch, seq, tg=256):
    nt = seq // tg
    qk_spec = pl.BlockSpec((tg, GLA_DK_HEAD), lambda b, h, t: (b * nt + t, h))
    v_spec = pl.BlockSpec((tg, GLA_DV_HEAD), lambda b, h, t: (b * nt + t, h))
    return pl.pallas_call(
        _gla_prompt_body,
        grid=(batch, GLA_HEADS, nt),
        in_specs=[qk_spec, qk_spec, v_spec, qk_spec],
        out_specs=[v_spec, pl.BlockSpec((1, 1, GLA_DK_HEAD, GLA_DV_HEAD), lambda b, h, t: (b, h, 0, 0))],
        out_shape=[jax.ShapeDtypeStruct((batch * seq, GLA_DV), F32),
                   jax.ShapeDtypeStruct((batch, GLA_HEADS, GLA_DK_HEAD, GLA_DV_HEAD), F32)],
        scratch_shapes=[pltpu.VMEM((GLA_DV_HEAD, GLA_DK_HEAD), F32)],
        compiler_params=pltpu.CompilerParams(dimension_semantics=("arbitrary", "arbitrary", "arbitrary")),
        name="gla_prompt",
    )(q, k, v, la)


def _gla_sample_body(q_ref, k_ref, v_ref, la_ref, s_ref, o_ref, sn_ref):
    bt = q_ref.shape[0]
    for h in range(GLA_HEADS):
        kc = slice(h * GLA_DK_HEAD, (h + 1) * GLA_DK_HEAD)
        vc = slice(h * GLA_DV_HEAD, (h + 1) * GLA_DV_HEAD)
        a_t = jnp.exp(la_ref[:, kc]).T
        k_t = k_ref[:, kc].T
        q_t = q_ref[:, kc].T
        for j in range(bt):
            s_new = a_t[:, j:j + 1] * s_ref[j, h] + k_t[:, j:j + 1] * v_ref[j:j + 1, vc]
            sn_ref[j, h] = s_new
            o_ref[j:j + 1, vc] = jnp.sum(q_t[:, j:j + 1] * s_new, axis=0, keepdims=True)


def _gla_sample(q, k, v, la, state, bt=8):
    nb = q.shape[0]
    row = lambda w: pl.BlockSpec((bt, w), lambda i: (i, 0))
    st_spec = pl.BlockSpec((bt, GLA_HEADS, GLA_DK_HEAD, GLA_DV_HEAD), lambda i: (i, 0, 0, 0))
    return pl.pallas_call(
        _gla_sample_body,
        grid=(nb // bt,),
        in_specs=[row(GLA_DK), row(GLA_DK), row(GLA_DV), row(GLA_DK), st_spec],
        out_specs=[row(GLA_DV), st_spec],
        out_shape=[jax.ShapeDtypeStruct((nb, GLA_DV), F32), jax.ShapeDtypeStruct(state.shape, F32)],
        compiler_params=pltpu.CompilerParams(dimension_semantics=("arbitrary",), vmem_limit_bytes=VMEM_LIMIT),
        name="gla_sample",
    )(q, k, v, la, state)


def _swa_prompt_body(sink_ref, q_ref, kc_ref, kp_ref, vc_ref, vp_ref, o_ref):
    n = pl.program_id(1)
    w = SWA_WINDOW
    i = lax.broadcasted_iota(jnp.int32, (w, 2 * w), 0)
    j = lax.broadcasted_iota(jnp.int32, (w, 2 * w), 1)
    has_prev = jnp.minimum(n, 1)
    mask = jnp.where(j < w, jnp.where(j >= i, has_prev, 0), jnp.where(j - w <= i, 1, 0)) > 0
    outs = []
    for g in range(SWA_KV_HEADS):
        kcols = slice(g * SWA_HEAD_DIM, (g + 1) * SWA_HEAD_DIM)
        kk = jnp.concatenate([kp_ref[:, kcols], kc_ref[:, kcols]], axis=0).astype(BF16)
        vv = jnp.concatenate([vp_ref[:, kcols], vc_ref[:, kcols]], axis=0).astype(BF16)
        for ii in range(SWA_GROUP):
            hh = g * SWA_GROUP + ii
            q = q_ref[:, hh * SWA_HEAD_DIM:(hh + 1) * SWA_HEAD_DIM].astype(BF16)
            s = jnp.where(mask, _mm_nt(q, kk) * (SWA_HEAD_DIM ** -0.5), NEG_BIG)
            sk = sink_ref[hh]
            m = jnp.maximum(jnp.max(s, axis=-1, keepdims=True), sk)
            p = jnp.exp(s - m)
            den = jnp.sum(p, axis=-1, keepdims=True) + jnp.exp(sk - m)
            outs.append(_mm(p.astype(BF16), vv) / den)
    o_ref[...] = jnp.concatenate(outs, axis=-1)


def _swa_prompt(sinks, q, k, v, batch, seq):
    w = SWA_WINDOW
    nb = seq // w
    cur = lambda width: pl.BlockSpec((w, width), lambda b, n: (b * nb + n, 0))
    prev = lambda width: pl.BlockSpec((w, width), lambda b, n: (b * nb + jnp.maximum(n - 1, 0), 0))
    return pl.pallas_call(
        _swa_prompt_body,
        grid=(batch, nb),
        in_specs=[pl.BlockSpec(memory_space=pltpu.SMEM), cur(SWA_Q), cur(SWA_KV), prev(SWA_KV), cur(SWA_KV), prev(SWA_KV)],
        out_specs=cur(SWA_Q),
        out_shape=jax.ShapeDtypeStruct((batch * seq, SWA_Q), F32),
        compiler_params=pltpu.CompilerParams(dimension_semantics=("arbitrary", "arbitrary")),
        name="swa_prompt",
    )(sinks, q, k, k, v, v)


def _swa_sample_body(sk_ref, q_ref, kn_ref, vn_ref, ck_ref, cv_ref, o_ref, nk_ref, nv_ref):
    bt = q_ref.shape[0]
    w = ck_ref.shape[1]
    hrow = lax.broadcasted_iota(jnp.int32, (SWA_HEADS, 1), 0)
    for j in range(bt):
        nk_ref[j, 0:w - 1, :] = ck_ref[j, 1:w, :]
        nk_ref[j, w - 1:w, :] = kn_ref[j:j + 1, :]
        nv_ref[j, 0:w - 1, :] = cv_ref[j, 1:w, :]
        nv_ref[j, w - 1:w, :] = vn_ref[j:j + 1, :]
        q = q_ref[j]
        qb = q.astype(BF16)
        s_main = jnp.zeros((SWA_HEADS, w), F32)
        s_old = jnp.zeros((SWA_HEADS, 1), F32)
        for g in range(SWA_KV_HEADS):
            kcols = slice(g * SWA_HEAD_DIM, (g + 1) * SWA_HEAD_DIM)
            in_g = (hrow // SWA_GROUP) == g
            s_main = jnp.where(in_g, _mm_nt(qb, nk_ref[j, :, kcols].astype(BF16)), s_main)
            s_old = jnp.where(in_g, jnp.sum(q * ck_ref[j, 0:1, kcols], axis=-1, keepdims=True), s_old)
        scale = SWA_HEAD_DIM ** -0.5
        s_main = s_main * scale
        s_old = s_old * scale
        sk = sk_ref[...]
        m = jnp.maximum(jnp.maximum(jnp.max(s_main, axis=-1, keepdims=True), s_old), sk)
        p_main = jnp.exp(s_main - m)
        p_old = jnp.exp(s_old - m)
        den = jnp.sum(p_main, axis=-1, keepdims=True) + p_old + jnp.exp(sk - m)
        pb = p_main.astype(BF16)
        o = jnp.zeros((SWA_HEADS, SWA_HEAD_DIM), F32)
        for g in range(SWA_KV_HEADS):
            kcols = slice(g * SWA_HEAD_DIM, (g + 1) * SWA_HEAD_DIM)
            in_g = (hrow // SWA_GROUP) == g
            o_g = _mm(pb, nv_ref[j, :, kcols].astype(BF16)) + p_old * cv_ref[j, 0:1, kcols]
            o = jnp.where(in_g, o_g, o)
        o_ref[j] = o / den


def _swa_sample(sinks, q3, k_new, v_new, cache_k, cache_v, bt=8):
    nb, w, _ = cache_k.shape
    row = lambda width: pl.BlockSpec((bt, width), lambda i: (i, 0))
    q_spec = pl.BlockSpec((bt, SWA_HEADS, SWA_HEAD_DIM), lambda i: (i, 0, 0))
    c_spec = pl.BlockSpec((bt, w, SWA_KV), lambda i: (i, 0, 0))
    return pl.pallas_call(
        _swa_sample_body,
        grid=(nb // bt,),
        in_specs=[pl.BlockSpec((SWA_HEADS, 1), lambda i: (0, 0)), q_spec, row(SWA_KV), row(SWA_KV), c_spec, c_spec],
        out_specs=[q_spec, c_spec, c_spec],
        out_shape=[jax.ShapeDtypeStruct((nb, SWA_HEADS, SWA_HEAD_DIM), F32),
                   jax.ShapeDtypeStruct(cache_k.shape, F32), jax.ShapeDtypeStruct(cache_v.shape, F32)],
        compiler_params=pltpu.CompilerParams(dimension_semantics=("arbitrary",)),
        name="swa_sample",
    )(sinks, q3, k_new, v_new, cache_k, cache_v)


def _rope_tables(pos):
    inv = jnp.power(ROPE_THETA, -jnp.arange(ROPE_HALF, dtype=F32) * 2.0 / ROPE_DIM)
    ang = pos.astype(F32)[:, None] * inv[None, :]
    cos, sin = jnp.cos(ang), jnp.sin(ang)
    n = pos.shape[0]
    rest = SWA_HEAD_DIM - ROPE_DIM
    rc = jnp.concatenate([cos, cos, jnp.ones((n, rest), F32)], axis=-1)
    ra = jnp.concatenate([-sin, jnp.zeros((n, ROPE_HALF + rest), F32)], axis=-1)
    rb = jnp.concatenate([jnp.zeros((n, ROPE_HALF), F32), sin, jnp.zeros((n, rest), F32)], axis=-1)
    reps = LANES // SWA_HEAD_DIM
    return tuple(jnp.tile(t, (1, reps)) for t in (rc, ra, rb))


def kernel(x_prompt, x_sample, state_gla, cache_swa_k, cache_swa_v, gla_w_in, gla_w_gate2, gla_b_gate, gla_g_head, gla_w_out, swa_w_qkv, swa_b_qkv, swa_sinks, swa_w_out, swa_b_out, norm_mix_pre, norm_mix_post, norm_ffn_pre, norm_ffn_post, ffn_w_up, ffn_w_down):
    batch, seq, _ = x_prompt.shape
    dec_batch, dec_seq, _ = x_sample.shape
    assert dec_seq == 1 and seq % SWA_WINDOW == 0
    past_len = seq
    n_p, n_s = batch * seq, dec_batch * dec_seq
    xp = x_prompt.reshape(n_p, D_MODEL)
    xs = x_sample.reshape(n_s, D_MODEL)

    w_in = gla_w_in[0]
    w_main = w_in[:, :GLA_MAIN].astype(BF16)
    w_z = jnp.pad(w_in[:, GLA_MAIN:], ((0, 0), (0, LANES - GLA_GATE_RANK))).astype(BF16)
    w_g2 = jnp.pad(gla_w_gate2[0], ((0, LANES - GLA_GATE_RANK), (0, 0))).astype(BF16)
    b_g = gla_b_gate[0][None, :]
    g_head = gla_g_head[0][None, :]
    w_gout = gla_w_out[0].astype(BF16)
    w_qkv = swa_w_qkv[0].astype(BF16)
    b_qkv = swa_b_qkv[0][None, :]
    w_sout = swa_w_out[0].astype(BF16)
    b_sout = swa_b_out[0][None, :]
    w_up = ffn_w_up.astype(BF16)
    w_dn = ffn_w_down.astype(BF16)
    row = lambda t, i: t[i][None, :]

    in0_consts = [row(norm_mix_pre, 0), w_main, w_z, w_g2, b_g]
    mid_consts = [g_head, w_gout, row(norm_mix_post, 0), row(norm_ffn_pre, 0), w_up[0], w_dn[0], row(norm_ffn_post, 0),
                  row(norm_mix_pre, 1), w_qkv, b_qkv]
    out_consts = [w_sout, b_sout, row(norm_mix_post, 1), row(norm_ffn_pre, 1), w_up[1], w_dn[1], row(norm_ffn_post, 1)]
    in0_widths = [GLA_DK, GLA_DK, GLA_DV, GLA_DV, GLA_DK]
    mid_widths = [D_MODEL, SWA_Q, SWA_KV, SWA_KV]

    tm = 256
    q, k, v, r, la = _tok_call(_in0_body, n_p, tm, [(xp, None)], in0_consts, in0_widths,
                               [F32, F32, BF16, F32, F32], "in0_prompt")
    o, s_fin_p = _gla_prompt(q, k, v, la, batch, seq)
    tabs = _rope_tables(jnp.arange(seq))
    tab_map = lambda i: (i % (seq // tm), 0)
    h2, q1, k1, v1 = _tok_call(_mid_body, n_p, tm,
                               [(o, None), (r, None), (xp, None)] + [(t, tab_map) for t in tabs],
                               mid_consts, mid_widths, [F32] * 4, "mid_prompt")
    attn = _swa_prompt(swa_sinks[0], q1, k1, v1, batch, seq)
    (y_p,) = _tok_call(_out_body, n_p, tm, [(attn, None), (h2, None)], out_consts, [D_MODEL], [F32], "out_prompt")
    wp = min(SWA_WINDOW, seq)
    k_tail = k1.reshape(batch, seq, SWA_KV_HEADS, SWA_HEAD_DIM)[:, seq - wp:]
    v_tail = v1.reshape(batch, seq, SWA_KV_HEADS, SWA_HEAD_DIM)[:, seq - wp:]

    ts = n_s
    qs, ks, vs, rs, las = _tok_call(_in0_body, n_s, ts, [(xs, None)], in0_consts, in0_widths, [F32] * 5, "in0_sample")
    o_s, s_new = _gla_sample(qs, ks, vs, las, state_gla[0])
    tabs_s = _rope_tables(jnp.full((n_s,), past_len, jnp.int32))
    h2s, q1s, k1s, v1s = _tok_call(_mid_body, n_s, ts,
                                   [(o_s, None), (rs, None), (xs, None)] + [(t, None) for t in tabs_s],
                                   mid_consts, mid_widths, [F32] * 4, "mid_sample")
    win = cache_swa_k.shape[2]
    attn_s, nk, nv = _swa_sample(swa_sinks[0][:, None], q1s.reshape(n_s, SWA_HEADS, SWA_HEAD_DIM), k1s, v1s,
                                 cache_swa_k[0].reshape(dec_batch, win, SWA_KV), cache_swa_v[0].reshape(dec_batch, win, SWA_KV))
    (y_s,) = _tok_call(_out_body, n_s, ts, [(attn_s.reshape(n_s, SWA_Q), None), (h2s, None)], out_consts,
                       [D_MODEL], [F32], "out_sample")

    kv_shape = (1, dec_batch, win, SWA_KV_HEADS, SWA_HEAD_DIM)
    return (y_p.reshape(batch, seq, D_MODEL), y_s.reshape(dec_batch, dec_seq, D_MODEL),
            s_fin_p[None], s_new[None], k_tail[None], v_tail[None], nk.reshape(kv_shape), nv.reshape(kv_shape))
```

```python
import functools

import jax
import jax.numpy as jnp
from jax import lax
from jax.experimental import pallas as pl
from jax.experimental.pallas import tpu as pltpu

F32 = jnp.float32
BF16 = jnp.bfloat16

D_MODEL = 1024
D_FF = 4 * D_MODEL
NORM_EPS = 1e-6

GLA_HEADS = 4
GLA_DK = D_MODEL // 2
GLA_DV = D_MODEL
GLA_DK_HEAD = GLA_DK // GLA_HEADS
GLA_DV_HEAD = GLA_DV // GLA_HEADS
GLA_GATE_RANK = 16
GLA_TAU = 16.0
GLA_CHUNK = 64
GLA_MAIN = 2 * GLA_DK + 2 * GLA_DV

SWA_HEAD_DIM = 64
SWA_HEADS = D_MODEL // SWA_HEAD_DIM
SWA_KV_HEADS = 4
SWA_GROUP = SWA_HEADS // SWA_KV_HEADS
SWA_WINDOW = 128
SWA_Q = SWA_HEADS * SWA_HEAD_DIM
SWA_KV = SWA_KV_HEADS * SWA_HEAD_DIM
SWA_QKV = SWA_Q + 2 * SWA_KV
ROPE_THETA = 500000.0
ROPE_DIM = SWA_HEAD_DIM // 4
ROPE_HALF = ROPE_DIM // 2

LANES = 128
FFN_CHUNK = 512
VMEM_LIMIT = 56 * 1024 * 1024
NEG_BIG = -1e30


def _mm(a, b):
    return jnp.dot(a, b, preferred_element_type=F32)


def _mm_nt(a, b):
    return lax.dot_general(a, b, (((1,), (1,)), ((), ())), preferred_element_type=F32)


def _mm_tn(a, b):
    return lax.dot_general(a, b, (((0,), (0,)), ((), ())), preferred_element_type=F32)


def _rms(x, g):
    ms = jnp.mean(x * x, axis=-1, keepdims=True)
    return x * lax.rsqrt(ms + NORM_EPS) * g


def _split3(x):
    hi = x.astype(BF16)
    r1 = x - hi.astype(F32)
    mid = r1.astype(BF16)
    lo = (r1 - mid.astype(F32)).astype(BF16)
    return hi, mid, lo


def _ffn(a_bf16, wup_ref, wdn_ref):
    acc = None
    for c in range(D_FF // FFN_CHUNK):
        cols = slice(c * FFN_CHUNK, (c + 1) * FFN_CHUNK)
        u = _mm(a_bf16, wup_ref[:, cols])
        u = jnp.square(jnp.maximum(u, 0.0)).astype(BF16)
        p = _mm(u, wdn_ref[cols, :])
        acc = p if acc is None else acc + p
    return acc


def _in0_body(x_ref, g_ref, w_ref, wz_ref, wg_ref, bg_ref, q_ref, k_ref, v_ref, r_ref, la_ref):
    a = _rms(x_ref[...], g_ref[...]).astype(BF16)
    q_ref[...] = _mm(a, w_ref[:, 0:GLA_DK]) * (GLA_DK_HEAD ** -0.5)
    k_ref[...] = _mm(a, w_ref[:, GLA_DK:2 * GLA_DK])
    for c in range(GLA_DV // 512):
        cols = slice(c * 512, (c + 1) * 512)
        v_ref[:, cols] = _mm(a, w_ref[:, 2 * GLA_DK + c * 512:2 * GLA_DK + (c + 1) * 512]).astype(v_ref.dtype)
        r_ref[:, cols] = _mm(a, w_ref[:, 2 * GLA_DK + GLA_DV + c * 512:2 * GLA_DK + GLA_DV + (c + 1) * 512])
    z = _mm(a, wz_ref[...]).astype(BF16)
    zg = _mm(z, wg_ref[...]) + bg_ref[...]
    la_ref[...] = (jnp.minimum(zg, 0.0) - jnp.log1p(jnp.exp(-jnp.abs(zg)))) * (1.0 / GLA_TAU)


def _mid_body(o_ref, r_ref, h_ref, rc_ref, ra_ref, rb_ref,
              gh_ref, wo_ref, gpost_ref, gfpre_ref, wup_ref, wdn_ref, gfpost_ref,
              gpre1_ref, wqkv_ref, bqkv_ref,
              h2_ref, q1_ref, k1_ref, v1_ref):
    m = None
    for hh in range(GLA_HEADS):
        cols = slice(hh * GLA_DV_HEAD, (hh + 1) * GLA_DV_HEAD)
        on = _rms(o_ref[:, cols], gh_ref[...])
        r = r_ref[:, cols]
        u = (on * (r * (1.0 / (1.0 + jnp.exp(-r))))).astype(BF16)
        p = _mm(u, wo_ref[cols, :])
        m = p if m is None else m + p
    h1 = h_ref[...] + _rms(m, gpost_ref[...])
    f = _ffn(_rms(h1, gfpre_ref[...]).astype(BF16), wup_ref, wdn_ref)
    h2 = h1 + _rms(f, gfpost_ref[...])
    h2_ref[...] = h2
    a3 = _rms(h2, gpre1_ref[...]).astype(BF16)
    rc, ra, rb = rc_ref[...], ra_ref[...], rb_ref[...]
    for c in range((SWA_Q + SWA_KV) // LANES):
        cols = slice(c * LANES, (c + 1) * LANES)
        x = _mm(a3, wqkv_ref[:, cols]) + bqkv_ref[:, cols]
        y = x * rc + pltpu.roll(x, LANES - ROPE_HALF, axis=1) * ra + pltpu.roll(x, ROPE_HALF, axis=1) * rb
        if c < SWA_Q // LANES:
            q1_ref[:, cols] = y
        else:
            k1_ref[:, c * LANES - SWA_Q:(c + 1) * LANES - SWA_Q] = y
    v1_ref[...] = _mm(a3, wqkv_ref[:, SWA_Q + SWA_KV:SWA_QKV]) + bqkv_ref[:, SWA_Q + SWA_KV:SWA_QKV]


def _out_body(at_ref, h_ref, wo_ref, bo_ref, gpost_ref, gfpre_ref, wup_ref, wdn_ref, gfpost_ref, y_ref):
    m = _mm(at_ref[...].astype(BF16), wo_ref[...]) + bo_ref[...]
    h1 = h_ref[...] + _rms(m, gpost_ref[...])
    f = _ffn(_rms(h1, gfpre_ref[...]).astype(BF16), wup_ref, wdn_ref)
    y_ref[...] = h1 + _rms(f, gfpost_ref[...])


def _tok_call(body, n_rows, tm, row_inputs, const_inputs, out_widths, out_dtypes, name):
    assert n_rows % tm == 0
    in_specs, args = [], []
    for arr, imap in row_inputs:
        in_specs.append(pl.BlockSpec((tm, arr.shape[1]), imap if imap is not None else (lambda i: (i, 0))))
        args.append(arr)
    for arr in const_inputs:
        in_specs.append(pl.BlockSpec(arr.shape, lambda i: (0, 0), pipeline_mode=pl.Buffered(1)))
        args.append(arr)
    out_specs = [pl.BlockSpec((tm, w), lambda i: (i, 0)) for w in out_widths]
    out_shape = [jax.ShapeDtypeStruct((n_rows, w), dt) for w, dt in zip(out_widths, out_dtypes)]
    return pl.pallas_call(
        body,
        grid=(n_rows // tm,),
        in_specs=in_specs,
        out_specs=out_specs,
        out_shape=out_shape,
        compiler_params=pltpu.CompilerParams(dimension_semantics=("arbitrary",), vmem_limit_bytes=VMEM_LIMIT),
        name=name,
    )(*args)


def _gla_prompt_body(q_ref, k_ref, v_ref, la_ref, o_ref, sfin_ref, st_ref):
    t = pl.program_id(1)
    tg = q_ref.shape[0]
    c_len = GLA_CHUNK

    @pl.when(t == 0)
    def _():
        st_ref[...] = jnp.zeros_like(st_ref)

    row = lax.broadcasted_iota(jnp.int32, (tg, tg), 0)
    col = lax.broadcasted_iota(jnp.int32, (tg, tg), 1)
    lower = jnp.where((row // c_len == col // c_len) & (col <= row), 1.0, 0.0).astype(BF16)
    hi, mid, lo = _split3(la_ref[...])
    cum = _mm(lower, hi) + _mm(lower, mid) + _mm(lower, lo)
    causal = lax.broadcasted_iota(jnp.int32, (c_len, c_len), 0) >= lax.broadcasted_iota(jnp.int32, (c_len, c_len), 1)
    for h in range(GLA_HEADS):
        kc = slice(h * GLA_DK_HEAD, (h + 1) * GLA_DK_HEAD)
        vc = slice(h * GLA_DV_HEAD, (h + 1) * GLA_DV_HEAD)
        for ci in range(tg // c_len):
            rows = slice(ci * c_len, (ci + 1) * c_len)
            cum_c = cum[rows, kc]
            tot = cum_c[c_len - 1:c_len, :]
            e_col = jnp.exp(cum_c[c_len - 8:c_len, :]).T[:, 7:8]
            q_c = q_ref[rows, kc]
            k_c = k_ref[rows, kc]
            v_c = v_ref[rows, vc]
            qd = (q_c * jnp.exp(cum_c)).astype(BF16)
            ki = (k_c * jnp.exp(-cum_c)).astype(BF16)
            ke = (k_c * jnp.exp(tot - cum_c)).astype(BF16)
            att = jnp.where(causal, _mm_nt(qd, ki), 0.0).astype(BF16)
            st = st_ref[h]
            o_ref[rows, vc] = _mm(att, v_c) + _mm(qd, st.astype(BF16))
            st_ref[h] = e_col * st + _mm_tn(ke, v_c)

    @pl.when(t == pl.num_programs(1) - 1)
    def _():
        sfin_ref[0] = st_ref[...]


def _gla_prompt(q, k, v, la, batch, seq, tg=256):
    nt = seq // tg
    qk_spec = pl.BlockSpec((tg, GLA_DK), lambda b, t: (b * nt + t, 0))
    v_spec = pl.BlockSpec((tg, GLA_DV), lambda b, t: (b * nt + t, 0))
    st_shape = (GLA_HEADS, GLA_DK_HEAD, GLA_DV_HEAD)
    return pl.pallas_call(
        _gla_prompt_body,
        grid=(batch, nt),
        in_specs=[qk_spec, qk_spec, v_spec, qk_spec],
        out_specs=[v_spec, pl.BlockSpec((1,) + st_shape, lambda b, t: (b, 0, 0, 0))],
        out_shape=[jax.ShapeDtypeStruct((batch * seq, GLA_DV), F32),
                   jax.ShapeDtypeStruct((batch,) + st_shape, F32)],
        scratch_shapes=[pltpu.VMEM(st_shape, F32)],
        compiler_params=pltpu.CompilerParams(dimension_semantics=("arbitrary", "arbitrary"), vmem_limit_bytes=VMEM_LIMIT),
        name="gla_prompt",
    )(q, k, v, la)


def _gla_sample_body(q_ref, k_ref, v_ref, la_ref, s_ref, o_ref, sn_ref):
    bt = q_ref.shape[0]
    for h in range(GLA_HEADS):
        kc = slice(h * GLA_DK_HEAD, (h + 1) * GLA_DK_HEAD)
        vc = slice(h * GLA_DV_HEAD, (h + 1) * GLA_DV_HEAD)
        a_t = jnp.exp(la_ref[:, kc]).T
        k_t = k_ref[:, kc].T
        q_t = q_ref[:, kc].T
        for j in range(bt):
            s_new = a_t[:, j:j + 1] * s_ref[j, h] + k_t[:, j:j + 1] * v_ref[j:j + 1, vc]
            sn_ref[j, h] = s_new
            o_ref[j:j + 1, vc] = jnp.sum(q_t[:, j:j + 1] * s_new, axis=0, keepdims=True)


def _gla_sample(q, k, v, la, state, bt=8):
    nb = q.shape[0]
    row = lambda w: pl.BlockSpec((bt, w), lambda i: (i, 0))
    st_spec = pl.BlockSpec((bt, GLA_HEADS, GLA_DK_HEAD, GLA_DV_HEAD), lambda i: (i, 0, 0, 0))
    return pl.pallas_call(
        _gla_sample_body,
        grid=(nb // bt,),
        in_specs=[row(GLA_DK), row(GLA_DK), row(GLA_DV), row(GLA_DK), st_spec],
        out_specs=[row(GLA_DV), st_spec],
        out_shape=[jax.ShapeDtypeStruct((nb, GLA_DV), F32), jax.ShapeDtypeStruct(state.shape, F32)],
        compiler_params=pltpu.CompilerParams(dimension_semantics=("arbitrary",), vmem_limit_bytes=VMEM_LIMIT),
        name="gla_sample",
    )(q, k, v, la, state)


def _swa_prompt_body(sink_ref, q_ref, kc_ref, kp_ref, vc_ref, vp_ref, o_ref):
    n = pl.program_id(1)
    w = SWA_WINDOW
    i = lax.broadcasted_iota(jnp.int32, (w, 2 * w), 0)
    j = lax.broadcasted_iota(jnp.int32, (w, 2 * w), 1)
    has_prev = jnp.minimum(n, 1)
    mask = jnp.where(j < w, jnp.where(j >= i, has_prev, 0), jnp.where(j - w <= i, 1, 0)) > 0
    outs = []
    for g in range(SWA_KV_HEADS):
        kcols = slice(g * SWA_HEAD_DIM, (g + 1) * SWA_HEAD_DIM)
        kk = jnp.concatenate([kp_ref[:, kcols], kc_ref[:, kcols]], axis=0).astype(BF16)
        vv = jnp.concatenate([vp_ref[:, kcols], vc_ref[:, kcols]], axis=0).astype(BF16)
        for ii in range(SWA_GROUP):
            hh = g * SWA_GROUP + ii
            q = q_ref[:, hh * SWA_HEAD_DIM:(hh + 1) * SWA_HEAD_DIM].astype(BF16)
            s = jnp.where(mask, _mm_nt(q, kk) * (SWA_HEAD_DIM ** -0.5), NEG_BIG)
            sk = sink_ref[hh]
            m = jnp.maximum(jnp.max(s, axis=-1, keepdims=True), sk)
            p = jnp.exp(s - m)
            den = jnp.sum(p, axis=-1, keepdims=True) + jnp.exp(sk - m)
            outs.append(_mm(p.astype(BF16), vv) / den)
    o_ref[...] = jnp.concatenate(outs, axis=-1)


def _swa_prompt(sinks, q, k, v, batch, seq):
    w = SWA_WINDOW
    nb = seq // w
    cur = lambda width: pl.BlockSpec((w, width), lambda b, n: (b * nb + n, 0))
    prev = lambda width: pl.BlockSpec((w, width), lambda b, n: (b * nb + jnp.maximum(n - 1, 0), 0))
    return pl.pallas_call(
        _swa_prompt_body,
        grid=(batch, nb),
        in_specs=[pl.BlockSpec(memory_space=pltpu.SMEM), cur(SWA_Q), cur(SWA_KV), prev(SWA_KV), cur(SWA_KV), prev(SWA_KV)],
        out_specs=cur(SWA_Q),
        out_shape=jax.ShapeDtypeStruct((batch * seq, SWA_Q), F32),
        compiler_params=pltpu.CompilerParams(dimension_semantics=("arbitrary", "arbitrary")),
        name="swa_prompt",
    )(sinks, q, k, k, v, v)


def _swa_sample_body(sk_ref, q_ref, kn_ref, vn_ref, ck_ref, cv_ref, o_ref, nk_ref, nv_ref):
    bt = q_ref.shape[0]
    w = ck_ref.shape[1]
    hrow = lax.broadcasted_iota(jnp.int32, (SWA_HEADS, 1), 0)
    for j in range(bt):
        nk_ref[j, 0:w - 1, :] = ck_ref[j, 1:w, :]
        nk_ref[j, w - 1:w, :] = kn_ref[j:j + 1, :]
        nv_ref[j, 0:w - 1, :] = cv_ref[j, 1:w, :]
        nv_ref[j, w - 1:w, :] = vn_ref[j:j + 1, :]
        q = q_ref[j]
        qb = q.astype(BF16)
        s_main = jnp.zeros((SWA_HEADS, w), F32)
        s_old = jnp.zeros((SWA_HEADS, 1), F32)
        for g in range(SWA_KV_HEADS):
            kcols = slice(g * SWA_HEAD_DIM, (g + 1) * SWA_HEAD_DIM)
            in_g = (hrow // SWA_GROUP) == g
            s_main = jnp.where(in_g, _mm_nt(qb, nk_ref[j, :, kcols].astype(BF16)), s_main)
            s_old = jnp.where(in_g, jnp.sum(q * ck_ref[j, 0:1, kcols], axis=-1, keepdims=True), s_old)
        scale = SWA_HEAD_DIM ** -0.5
        s_main = s_main * scale
        s_old = s_old * scale
        sk = sk_ref[...]
        m = jnp.maximum(jnp.maximum(jnp.max(s_main, axis=-1, keepdims=True), s_old), sk)
        p_main = jnp.exp(s_main - m)
        p_old = jnp.exp(s_old - m)
        den = jnp.sum(p_main, axis=-1, keepdims=True) + p_old + jnp.exp(sk - m)
        pb = p_main.astype(BF16)
        o = jnp.zeros((SWA_HEADS, SWA_HEAD_DIM), F32)
        for g in range(SWA_KV_HEADS):
            kcols = slice(g * SWA_HEAD_DIM, (g + 1) * SWA_HEAD_DIM)
            in_g = (hrow // SWA_GROUP) == g
            o_g = _mm(pb, nv_ref[j, :, kcols].astype(BF16)) + p_old * cv_ref[j, 0:1, kcols]
            o = jnp.where(in_g, o_g, o)
        o_ref[j] = o / den


def _swa_sample(sinks, q3, k_new, v_new, cache_k, cache_v, bt=8):
    nb, w, _ = cache_k.shape
    row = lambda width: pl.BlockSpec((bt, width), lambda i: (i, 0))
    q_spec = pl.BlockSpec((bt, SWA_HEADS, SWA_HEAD_DIM), lambda i: (i, 0, 0))
    c_spec = pl.BlockSpec((bt, w, SWA_KV), lambda i: (i, 0, 0))
    return pl.pallas_call(
        _swa_sample_body,
        grid=(nb // bt,),
        in_specs=[pl.BlockSpec((SWA_HEADS, 1), lambda i: (0, 0)), q_spec, row(SWA_KV), row(SWA_KV), c_spec, c_spec],
        out_specs=[q_spec, c_spec, c_spec],
        out_shape=[jax.ShapeDtypeStruct((nb, SWA_HEADS, SWA_HEAD_DIM), F32),
                   jax.ShapeDtypeStruct(cache_k.shape, F32), jax.ShapeDtypeStruct(cache_v.shape, F32)],
        compiler_params=pltpu.CompilerParams(dimension_semantics=("arbitrary",)),
        name="swa_sample",
    )(sinks, q3, k_new, v_new, cache_k, cache_v)


def _rope_tables(pos):
    inv = jnp.power(ROPE_THETA, -jnp.arange(ROPE_HALF, dtype=F32) * 2.0 / ROPE_DIM)
    ang = pos.astype(F32)[:, None] * inv[None, :]
    cos, sin = jnp.cos(ang), jnp.sin(ang)
    n = pos.shape[0]
    rest = SWA_HEAD_DIM - ROPE_DIM
    rc = jnp.concatenate([cos, cos, jnp.ones((n, rest), F32)], axis=-1)
    ra = jnp.concatenate([-sin, jnp.zeros((n, ROPE_HALF + rest), F32)], axis=-1)
    rb = jnp.concatenate([jnp.zeros((n, ROPE_HALF), F32), sin, jnp.zeros((n, rest), F32)], axis=-1)
    reps = LANES // SWA_HEAD_DIM
    return tuple(jnp.tile(t, (1, reps)) for t in (rc, ra, rb))


def kernel(x_prompt, x_sample, state_gla, cache_swa_k, cache_swa_v, gla_w_in, gla_w_gate2, gla_b_gate, gla_g_head, gla_w_out, swa_w_qkv, swa_b_qkv, swa_sinks, swa_w_out, swa_b_out, norm_mix_pre, norm_mix_post, norm_ffn_pre, norm_ffn_post, ffn_w_up, ffn_w_down):
    batch, seq, _ = x_prompt.shape
    dec_batch, dec_seq, _ = x_sample.shape
    assert dec_seq == 1 and seq % SWA_WINDOW == 0
    past_len = seq
    n_p, n_s = batch * seq, dec_batch * dec_seq
    xp = x_prompt.reshape(n_p, D_MODEL)
    xs = x_sample.reshape(n_s, D_MODEL)

    w_in = gla_w_in[0]
    w_main = w_in[:, :GLA_MAIN].astype(BF16)
    w_z = jnp.pad(w_in[:, GLA_MAIN:], ((0, 0), (0, LANES - GLA_GATE_RANK))).astype(BF16)
    w_g2 = jnp.pad(gla_w_gate2[0], ((0, LANES - GLA_GATE_RANK), (0, 0))).astype(BF16)
    b_g = gla_b_gate[0][None, :]
    g_head = gla_g_head[0][None, :]
    w_gout = gla_w_out[0].astype(BF16)
    w_qkv = swa_w_qkv[0].astype(BF16)
    b_qkv = swa_b_qkv[0][None, :]
    w_sout = swa_w_out[0].astype(BF16)
    b_sout = swa_b_out[0][None, :]
    w_up = ffn_w_up.astype(BF16)
    w_dn = ffn_w_down.astype(BF16)
    row = lambda t, i: t[i][None, :]

    in0_consts = [row(norm_mix_pre, 0), w_main, w_z, w_g2, b_g]
    mid_consts = [g_head, w_gout, row(norm_mix_post, 0), row(norm_ffn_pre, 0), w_up[0], w_dn[0], row(norm_ffn_post, 0),
                  row(norm_mix_pre, 1), w_qkv, b_qkv]
    out_consts = [w_sout, b_sout, row(norm_mix_post, 1), row(norm_ffn_pre, 1), w_up[1], w_dn[1], row(norm_ffn_post, 1)]
    in0_widths = [GLA_DK, GLA_DK, GLA_DV, GLA_DV, GLA_DK]
    mid_widths = [D_MODEL, SWA_Q, SWA_KV, SWA_KV]

    tm = 512
    q, k, v, r, la = _tok_call(_in0_body, n_p, tm, [(xp, None)], in0_consts, in0_widths,
                               [F32, F32, BF16, F32, F32], "in0_prompt")
    o, s_fin_p = _gla_prompt(q, k, v, la, batch, seq)
    tabs = _rope_tables(jnp.arange(seq))
    tab_map = lambda i: (i % (seq // tm), 0)
    h2, q1, k1, v1 = _tok_call(_mid_body, n_p, tm,
                               [(o, None), (r, None), (xp, None)] + [(t, tab_map) for t in tabs],
                               mid_consts, mid_widths, [F32] * 4, "mid_prompt")
    attn = _swa_prompt(swa_sinks[0], q1, k1, v1, batch, seq)
    (y_p,) = _tok_call(_out_body, n_p, tm, [(attn, None), (h2, None)], out_consts, [D_MODEL], [F32], "out_prompt")
    wp = min(SWA_WINDOW, seq)
    k_tail = k1.reshape(batch, seq, SWA_KV_HEADS, SWA_HEAD_DIM)[:, seq - wp:]
    v_tail = v1.reshape(batch, seq, SWA_KV_HEADS, SWA_HEAD_DIM)[:, seq - wp:]

    ts = n_s
    qs, ks, vs, rs, las = _tok_call(_in0_body, n_s, ts, [(xs, None)], in0_consts, in0_widths, [F32] * 5, "in0_sample")
    o_s, s_new = _gla_sample(qs, ks, vs, las, state_gla[0])
    tabs_s = _rope_tables(jnp.full((n_s,), past_len, jnp.int32))
    h2s, q1s, k1s, v1s = _tok_call(_mid_body, n_s, ts,
                                   [(o_s, None), (rs, None), (xs, None)] + [(t, None) for t in tabs_s],
                                   mid_consts, mid_widths, [F32] * 4, "mid_sample")
    win = cache_swa_k.shape[2]
    attn_s, nk, nv = _swa_sample(swa_sinks[0][:, None], q1s.reshape(n_s, SWA_HEADS, SWA_HEAD_DIM), k1s, v1s,
                                 cache_swa_k[0].reshape(dec_batch, win, SWA_KV), cache_swa_v[0].reshape(dec_batch, win, SWA_KV))
    (y_s,) = _tok_call(_out_body, n_s, ts, [(attn_s.reshape(n_s, SWA_Q), None), (h2s, None)], out_consts,
                       [D_MODEL], [F32], "out_sample")

    kv_shape = (1, dec_batch, win, SWA_KV_HEADS, SWA_HEAD_DIM)
    return (y_p.reshape(batch, seq, D_MODEL), y_s.reshape(dec_batch, dec_seq, D_MODEL),
            s_fin_p[None], s_new[None], k_tail[None], v_tail[None], nk.reshape(kv_shape), nv.reshape(kv_shape))
```

```python
import functools

import jax
import jax.numpy as jnp
from jax import lax
from jax.experimental import pallas as pl
from jax.experimental.pallas import tpu as pltpu

F32 = jnp.float32
BF16 = jnp.bfloat16

D_MODEL = 1024
D_FF = 4 * D_MODEL
NORM_EPS = 1e-6

GLA_HEADS = 4
GLA_DK = D_MODEL // 2
GLA_DV = D_MODEL
GLA_DK_HEAD = GLA_DK // GLA_HEADS
GLA_DV_HEAD = GLA_DV // GLA_HEADS
GLA_GATE_RANK = 16
GLA_TAU = 16.0
GLA_CHUNK = 64
GLA_MAIN = 2 * GLA_DK + 2 * GLA_DV

SWA_HEAD_DIM = 64
SWA_HEADS = D_MODEL // SWA_HEAD_DIM
SWA_KV_HEADS = 4
SWA_GROUP = SWA_HEADS // SWA_KV_HEADS
SWA_WINDOW = 128
SWA_Q = SWA_HEADS * SWA_HEAD_DIM
SWA_KV = SWA_KV_HEADS * SWA_HEAD_DIM
SWA_QKV = SWA_Q + 2 * SWA_KV
ROPE_THETA = 500000.0
ROPE_DIM = SWA_HEAD_DIM // 4
ROPE_HALF = ROPE_DIM // 2

LANES = 128
FFN_CHUNK = 512
VMEM_LIMIT = 56 * 1024 * 1024
NEG_BIG = -1e30
LOG2E = 1.4426950408889634


def _mm(a, b):
    return jnp.dot(a, b, preferred_element_type=F32)


def _mm_nt(a, b):
    return lax.dot_general(a, b, (((1,), (1,)), ((), ())), preferred_element_type=F32)


def _mm_tn(a, b):
    return lax.dot_general(a, b, (((0,), (0,)), ((), ())), preferred_element_type=F32)


def _rms(x, g):
    ms = jnp.mean(x * x, axis=-1, keepdims=True)
    return x * lax.rsqrt(ms + NORM_EPS) * g


def _split3(x):
    hi = x.astype(BF16)
    r1 = x - hi.astype(F32)
    mid = r1.astype(BF16)
    lo = (r1 - mid.astype(F32)).astype(BF16)
    return hi, mid, lo


def _ffn(a_bf16, wup_ref, wdn_ref):
    acc = None
    for c in range(D_FF // FFN_CHUNK):
        cols = slice(c * FFN_CHUNK, (c + 1) * FFN_CHUNK)
        u = _mm(a_bf16, wup_ref[:, cols])
        u = jnp.square(jnp.maximum(u, 0.0)).astype(BF16)
        p = _mm(u, wdn_ref[cols, :])
        acc = p if acc is None else acc + p
    return acc


def _in0_body(x_ref, g_ref, w_ref, wz_ref, wg_ref, bg_ref, q_ref, k_ref, v_ref, r_ref, la_ref):
    a = _rms(x_ref[...], g_ref[...]).astype(BF16)
    q_ref[...] = _mm(a, w_ref[:, 0:GLA_DK]) * (GLA_DK_HEAD ** -0.5)
    k_ref[...] = _mm(a, w_ref[:, GLA_DK:2 * GLA_DK])
    for c in range(GLA_DV // 512):
        cols = slice(c * 512, (c + 1) * 512)
        v_ref[:, cols] = _mm(a, w_ref[:, 2 * GLA_DK + c * 512:2 * GLA_DK + (c + 1) * 512]).astype(v_ref.dtype)
        r_ref[:, cols] = _mm(a, w_ref[:, 2 * GLA_DK + GLA_DV + c * 512:2 * GLA_DK + GLA_DV + (c + 1) * 512])
    z = _mm(a, wz_ref[...]).astype(BF16)
    zg = _mm(z, wg_ref[...]) + bg_ref[...]
    la_ref[...] = (jnp.minimum(zg, 0.0) - jnp.log1p(jnp.exp(-jnp.abs(zg)))) * (1.0 / GLA_TAU)


def _mid_body(o_ref, r_ref, h_ref, rc_ref, ra_ref, rb_ref,
              gh_ref, wo_ref, gpost_ref, gfpre_ref, wup_ref, wdn_ref, gfpost_ref,
              gpre1_ref, wqkv_ref, bqkv_ref,
              h2_ref, q1_ref, k1_ref, v1_ref):
    m = None
    for hh in range(GLA_HEADS):
        cols = slice(hh * GLA_DV_HEAD, (hh + 1) * GLA_DV_HEAD)
        on = _rms(o_ref[:, cols], gh_ref[...])
        r = r_ref[:, cols]
        u = (on * (r * (1.0 / (1.0 + jnp.exp(-r))))).astype(BF16)
        p = _mm(u, wo_ref[cols, :])
        m = p if m is None else m + p
    h1 = h_ref[...] + _rms(m, gpost_ref[...])
    f = _ffn(_rms(h1, gfpre_ref[...]).astype(BF16), wup_ref, wdn_ref)
    h2 = h1 + _rms(f, gfpost_ref[...])
    h2_ref[...] = h2
    a3 = _rms(h2, gpre1_ref[...]).astype(BF16)
    rc, ra, rb = rc_ref[...], ra_ref[...], rb_ref[...]
    for c in range((SWA_Q + SWA_KV) // LANES):
        cols = slice(c * LANES, (c + 1) * LANES)
        x = _mm(a3, wqkv_ref[:, cols]) + bqkv_ref[:, cols]
        y = x * rc + pltpu.roll(x, LANES - ROPE_HALF, axis=1) * ra + pltpu.roll(x, ROPE_HALF, axis=1) * rb
        if c < SWA_Q // LANES:
            q1_ref[:, cols] = y
        else:
            k1_ref[:, c * LANES - SWA_Q:(c + 1) * LANES - SWA_Q] = y
    v1_ref[...] = _mm(a3, wqkv_ref[:, SWA_Q + SWA_KV:SWA_QKV]) + bqkv_ref[:, SWA_Q + SWA_KV:SWA_QKV]


def _out_body(at_ref, h_ref, wo_ref, bo_ref, gpost_ref, gfpre_ref, wup_ref, wdn_ref, gfpost_ref, y_ref):
    m = _mm(at_ref[...].astype(BF16), wo_ref[...]) + bo_ref[...]
    h1 = h_ref[...] + _rms(m, gpost_ref[...])
    f = _ffn(_rms(h1, gfpre_ref[...]).astype(BF16), wup_ref, wdn_ref)
    y_ref[...] = h1 + _rms(f, gfpost_ref[...])


def _tok_call(body, n_rows, tm, row_inputs, const_inputs, out_widths, out_dtypes, name):
    assert n_rows % tm == 0
    in_specs, args = [], []
    for arr, imap in row_inputs:
        in_specs.append(pl.BlockSpec((tm, arr.shape[1]), imap if imap is not None else (lambda i: (i, 0))))
        args.append(arr)
    for arr in const_inputs:
        in_specs.append(pl.BlockSpec(arr.shape, lambda i: (0, 0), pipeline_mode=pl.Buffered(1)))
        args.append(arr)
    out_specs = [pl.BlockSpec((tm, w), lambda i: (i, 0)) for w in out_widths]
    out_shape = [jax.ShapeDtypeStruct((n_rows, w), dt) for w, dt in zip(out_widths, out_dtypes)]
    return pl.pallas_call(
        body,
        grid=(n_rows // tm,),
        in_specs=in_specs,
        out_specs=out_specs,
        out_shape=out_shape,
        compiler_params=pltpu.CompilerParams(dimension_semantics=("arbitrary",), vmem_limit_bytes=VMEM_LIMIT),
        name=name,
    )(*args)


def _gla_prompt_body(q_ref, k_ref, v_ref, la_ref, o_ref, sfin_ref, st_ref):
    t = pl.program_id(1)
    tg = q_ref.shape[0]
    c_len = GLA_CHUNK

    @pl.when(t == 0)
    def _():
        st_ref[...] = jnp.zeros_like(st_ref)

    row = lax.broadcasted_iota(jnp.int32, (tg, tg), 0)
    col = lax.broadcasted_iota(jnp.int32, (tg, tg), 1)
    lower = jnp.where((row // c_len == col // c_len) & (col <= row), 1.0, 0.0).astype(BF16)
    hi, mid, lo = _split3(la_ref[...])
    cum = _mm(lower, hi) + _mm(lower, mid) + _mm(lower, lo)
    causal = lax.broadcasted_iota(jnp.int32, (c_len, c_len), 0) >= lax.broadcasted_iota(jnp.int32, (c_len, c_len), 1)
    for h in range(GLA_HEADS):
        kc = slice(h * GLA_DK_HEAD, (h + 1) * GLA_DK_HEAD)
        vc = slice(h * GLA_DV_HEAD, (h + 1) * GLA_DV_HEAD)
        for ci in range(tg // c_len):
            rows = slice(ci * c_len, (ci + 1) * c_len)
            cum_c = cum[rows, kc]
            tot = cum_c[c_len - 1:c_len, :]
            e_col = jnp.exp(cum_c[c_len - 8:c_len, :]).T[:, 7:8]
            q_c = q_ref[rows, kc]
            k_c = k_ref[rows, kc]
            v_c = v_ref[rows, vc]
            qd = (q_c * jnp.exp(cum_c)).astype(BF16)
            ki = (k_c * jnp.exp(-cum_c)).astype(BF16)
            ke = (k_c * jnp.exp(tot - cum_c)).astype(BF16)
            att = jnp.where(causal, _mm_nt(qd, ki), 0.0).astype(BF16)
            st = st_ref[h]
            o_ref[rows, vc] = _mm(att, v_c) + _mm(qd, st.astype(BF16))
            st_ref[h] = e_col * st + _mm_tn(ke, v_c)

    @pl.when(t == pl.num_programs(1) - 1)
    def _():
        sfin_ref[0] = st_ref[...]


def _gla_prompt(q, k, v, la, batch, seq, tg=256):
    nt = seq // tg
    qk_spec = pl.BlockSpec((tg, GLA_DK), lambda b, t: (b * nt + t, 0))
    v_spec = pl.BlockSpec((tg, GLA_DV), lambda b, t: (b * nt + t, 0))
    st_shape = (GLA_HEADS, GLA_DK_HEAD, GLA_DV_HEAD)
    return pl.pallas_call(
        _gla_prompt_body,
        grid=(batch, nt),
        in_specs=[qk_spec, qk_spec, v_spec, qk_spec],
        out_specs=[v_spec, pl.BlockSpec((1,) + st_shape, lambda b, t: (b, 0, 0, 0))],
        out_shape=[jax.ShapeDtypeStruct((batch * seq, GLA_DV), F32),
                   jax.ShapeDtypeStruct((batch,) + st_shape, F32)],
        scratch_shapes=[pltpu.VMEM(st_shape, F32)],
        compiler_params=pltpu.CompilerParams(dimension_semantics=("arbitrary", "arbitrary"), vmem_limit_bytes=VMEM_LIMIT),
        name="gla_prompt",
    )(q, k, v, la)


def _gla_sample_body(q_ref, k_ref, v_ref, la_ref, s_ref, o_ref, sn_ref):
    bt = q_ref.shape[0]
    for h in range(GLA_HEADS):
        kc = slice(h * GLA_DK_HEAD, (h + 1) * GLA_DK_HEAD)
        vc = slice(h * GLA_DV_HEAD, (h + 1) * GLA_DV_HEAD)
        a_t = jnp.exp(la_ref[:, kc]).T
        k_t = k_ref[:, kc].T
        q_t = q_ref[:, kc].T
        for j in range(bt):
            s_new = a_t[:, j:j + 1] * s_ref[j, h] + k_t[:, j:j + 1] * v_ref[j:j + 1, vc]
            sn_ref[j, h] = s_new
            o_ref[j:j + 1, vc] = jnp.sum(q_t[:, j:j + 1] * s_new, axis=0, keepdims=True)


def _gla_sample(q, k, v, la, state, bt=8):
    nb = q.shape[0]
    row = lambda w: pl.BlockSpec((bt, w), lambda i: (i, 0))
    st_spec = pl.BlockSpec((bt, GLA_HEADS, GLA_DK_HEAD, GLA_DV_HEAD), lambda i: (i, 0, 0, 0))
    return pl.pallas_call(
        _gla_sample_body,
        grid=(nb // bt,),
        in_specs=[row(GLA_DK), row(GLA_DK), row(GLA_DV), row(GLA_DK), st_spec],
        out_specs=[row(GLA_DV), st_spec],
        out_shape=[jax.ShapeDtypeStruct((nb, GLA_DV), F32), jax.ShapeDtypeStruct(state.shape, F32)],
        compiler_params=pltpu.CompilerParams(dimension_semantics=("arbitrary",), vmem_limit_bytes=VMEM_LIMIT),
        name="gla_sample",
    )(q, k, v, la, state)


def _swa_attend_tile(sink_ref, q_ref, k_full, v_full, has_prev, o_ref):
    w = SWA_WINDOW
    hd = SWA_HEAD_DIM
    tq = q_ref.shape[0]
    nkv = k_full.shape[0]
    lane_q = lax.broadcasted_iota(jnp.int32, (w, LANES), 1) < hd
    lane_kv = lax.broadcasted_iota(jnp.int32, (nkv, LANES), 1) < hd
    i = lax.broadcasted_iota(jnp.int32, (w, 2 * w), 0)
    j = lax.broadcasted_iota(jnp.int32, (w, 2 * w), 1)
    band = jnp.where(j < w, jnp.where(j >= i, 1, 0), jnp.where(j - w <= i, 1, 0))
    band_first = jnp.where(j < w, has_prev, 1) * band
    lane_2w = lax.broadcasted_iota(jnp.int32, (2 * w, LANES), 1) < hd
    ones_lo = jnp.where(lane_2w, 1.0, 0.0).astype(BF16)
    ones_hi = jnp.where(lane_2w, 0.0, 1.0).astype(BF16)
    c2 = (hd ** -0.5) * LOG2E

    k_prep, v_prep = [], []
    for p in range(SWA_KV // LANES):
        cols = slice(p * LANES, (p + 1) * LANES)
        k_p, v_p = k_full[:, cols], v_full[:, cols]
        k_prep.append((k_p.astype(BF16), pltpu.roll(k_p, hd, axis=1).astype(BF16)))
        v_r = pltpu.roll(v_p, hd, axis=1)
        v_prep.append(((jnp.where(lane_kv, v_p, 0.0).astype(BF16), jnp.where(lane_kv, 0.0, v_r).astype(BF16)),
                       (jnp.where(lane_kv, v_r, 0.0).astype(BF16), jnp.where(lane_kv, 0.0, v_p).astype(BF16))))

    def softmax_part(s, hh, mask):
        s2 = jnp.where(mask, s, NEG_BIG)
        sk2 = jnp.full((w, 1), sink_ref[hh], F32) * LOG2E
        m2 = jnp.maximum(jnp.max(s2, axis=-1, keepdims=True), sk2)
        return jnp.exp2(s2 - m2).astype(BF16), sk2 - m2

    for b in range(tq // w):
        rows = slice(b * w, (b + 1) * w)
        krows = slice(b * w, (b + 2) * w)
        mask = (band_first if b == 0 else band) > 0
        for p in range(SWA_KV // LANES):
            q_lo, q_hi = [], []
            for x in range(4):
                q_c = q_ref[rows, (4 * p + x) * LANES:(4 * p + x + 1) * LANES] * c2
                q_lo.append(jnp.where(lane_q, q_c, 0.0).astype(BF16))
                q_hi.append(jnp.where(lane_q, 0.0, q_c).astype(BF16))
            s_self = _mm_nt(jnp.concatenate([q_lo[0], q_lo[1], q_hi[2], q_hi[3]], axis=0), k_prep[p][0][krows])
            s_roll = _mm_nt(jnp.concatenate([q_hi[0], q_hi[1], q_lo[2], q_lo[3]], axis=0), k_prep[p][1][krows])
            for x in range(4):
                c = 4 * p + x
                gh = x // 2
                xr = slice(x * w, (x + 1) * w)
                s_lo, s_hi = (s_self[xr], s_roll[xr]) if gh == 0 else (s_roll[xr], s_self[xr])
                p_lo, d_lo = softmax_part(s_lo, 2 * c, mask)
                p_hi, d_hi = softmax_part(s_hi, 2 * c + 1, mask)
                v_lo, v_hi = v_prep[p][gh]
                rhs = jnp.concatenate([jnp.concatenate([v_lo[krows], ones_lo], axis=1),
                                       jnp.concatenate([v_hi[krows], ones_hi], axis=1)], axis=0)
                ext = _mm(jnp.concatenate([p_lo, p_hi], axis=1), rhs)
                den = ext[:, LANES:] + jnp.exp2(jnp.where(lane_q, d_lo, d_hi))
                o_ref[rows, c * LANES:(c + 1) * LANES] = ext[:, :LANES] / den


def _swa_prompt_body(sink_ref, q_ref, kc_ref, kp_ref, vc_ref, vp_ref, o_ref):
    has_prev = jnp.minimum(pl.program_id(1), 1)
    k_full = jnp.concatenate([kp_ref[...], kc_ref[...]], axis=0)
    v_full = jnp.concatenate([vp_ref[...], vc_ref[...]], axis=0)
    _swa_attend_tile(sink_ref, q_ref, k_full, v_full, has_prev, o_ref)


def _swa_prompt(sinks, q, k, v, batch, seq, tq=512):
    w = SWA_WINDOW
    nt = seq // tq
    cur = lambda width: pl.BlockSpec((tq, width), lambda b, t: (b * nt + t, 0))
    prev = lambda width: pl.BlockSpec((w, width), lambda b, t: (b * (seq // w) + jnp.maximum(t * (tq // w) - 1, 0), 0))
    return pl.pallas_call(
        _swa_prompt_body,
        grid=(batch, nt),
        in_specs=[pl.BlockSpec(memory_space=pltpu.SMEM), cur(SWA_Q), cur(SWA_KV), prev(SWA_KV), cur(SWA_KV), prev(SWA_KV)],
        out_specs=cur(SWA_Q),
        out_shape=jax.ShapeDtypeStruct((batch * seq, SWA_Q), F32),
        compiler_params=pltpu.CompilerParams(dimension_semantics=("arbitrary", "arbitrary"), vmem_limit_bytes=VMEM_LIMIT),
        name="swa_prompt",
    )(sinks, q, k, k, v, v)


def _swa_sample_body(sk_ref, q_ref, kn_ref, vn_ref, ck_ref, cv_ref, o_ref, nk_ref, nv_ref):
    bt = q_ref.shape[0]
    w = ck_ref.shape[1]
    hrow = lax.broadcasted_iota(jnp.int32, (SWA_HEADS, 1), 0)
    for j in range(bt):
        nk_ref[j, 0:w - 1, :] = ck_ref[j, 1:w, :]
        nk_ref[j, w - 1:w, :] = kn_ref[j:j + 1, :]
        nv_ref[j, 0:w - 1, :] = cv_ref[j, 1:w, :]
        nv_ref[j, w - 1:w, :] = vn_ref[j:j + 1, :]
        q = q_ref[j]
        qb = q.astype(BF16)
        s_main = jnp.zeros((SWA_HEADS, w), F32)
        s_old = jnp.zeros((SWA_HEADS, 1), F32)
        for g in range(SWA_KV_HEADS):
            kcols = slice(g * SWA_HEAD_DIM, (g + 1) * SWA_HEAD_DIM)
            in_g = (hrow // SWA_GROUP) == g
            s_main = jnp.where(in_g, _mm_nt(qb, nk_ref[j, :, kcols].astype(BF16)), s_main)
            s_old = jnp.where(in_g, jnp.sum(q * ck_ref[j, 0:1, kcols], axis=-1, keepdims=True), s_old)
        scale = SWA_HEAD_DIM ** -0.5
        s_main = s_main * scale
        s_old = s_old * scale
        sk = sk_ref[...]
        m = jnp.maximum(jnp.maximum(jnp.max(s_main, axis=-1, keepdims=True), s_old), sk)
        p_main = jnp.exp(s_main - m)
        p_old = jnp.exp(s_old - m)
        den = jnp.sum(p_main, axis=-1, keepdims=True) + p_old + jnp.exp(sk - m)
        pb = p_main.astype(BF16)
        o = jnp.zeros((SWA_HEADS, SWA_HEAD_DIM), F32)
        for g in range(SWA_KV_HEADS):
            kcols = slice(g * SWA_HEAD_DIM, (g + 1) * SWA_HEAD_DIM)
            in_g = (hrow // SWA_GROUP) == g
            o_g = _mm(pb, nv_ref[j, :, kcols].astype(BF16)) + p_old * cv_ref[j, 0:1, kcols]
            o = jnp.where(in_g, o_g, o)
        o_ref[j] = o / den


def _swa_sample(sinks, q3, k_new, v_new, cache_k, cache_v, bt=8):
    nb, w, _ = cache_k.shape
    row = lambda width: pl.BlockSpec((bt, width), lambda i: (i, 0))
    q_spec = pl.BlockSpec((bt, SWA_HEADS, SWA_HEAD_DIM), lambda i: (i, 0, 0))
    c_spec = pl.BlockSpec((bt, w, SWA_KV), lambda i: (i, 0, 0))
    return pl.pallas_call(
        _swa_sample_body,
        grid=(nb // bt,),
        in_specs=[pl.BlockSpec((SWA_HEADS, 1), lambda i: (0, 0)), q_spec, row(SWA_KV), row(SWA_KV), c_spec, c_spec],
        out_specs=[q_spec, c_spec, c_spec],
        out_shape=[jax.ShapeDtypeStruct((nb, SWA_HEADS, SWA_HEAD_DIM), F32),
                   jax.ShapeDtypeStruct(cache_k.shape, F32), jax.ShapeDtypeStruct(cache_v.shape, F32)],
        compiler_params=pltpu.CompilerParams(dimension_semantics=("arbitrary",)),
        name="swa_sample",
    )(sinks, q3, k_new, v_new, cache_k, cache_v)


def _rope_tables(pos):
    inv = jnp.power(ROPE_THETA, -jnp.arange(ROPE_HALF, dtype=F32) * 2.0 / ROPE_DIM)
    ang = pos.astype(F32)[:, None] * inv[None, :]
    cos, sin = jnp.cos(ang), jnp.sin(ang)
    n = pos.shape[0]
    rest = SWA_HEAD_DIM - ROPE_DIM
    rc = jnp.concatenate([cos, cos, jnp.ones((n, rest), F32)], axis=-1)
    ra = jnp.concatenate([-sin, jnp.zeros((n, ROPE_HALF + rest), F32)], axis=-1)
    rb = jnp.concatenate([jnp.zeros((n, ROPE_HALF), F32), sin, jnp.zeros((n, rest), F32)], axis=-1)
    reps = LANES // SWA_HEAD_DIM
    return tuple(jnp.tile(t, (1, reps)) for t in (rc, ra, rb))


def kernel(x_prompt, x_sample, state_gla, cache_swa_k, cache_swa_v, gla_w_in, gla_w_gate2, gla_b_gate, gla_g_head, gla_w_out, swa_w_qkv, swa_b_qkv, swa_sinks, swa_w_out, swa_b_out, norm_mix_pre, norm_mix_post, norm_ffn_pre, norm_ffn_post, ffn_w_up, ffn_w_down):
    batch, seq, _ = x_prompt.shape
    dec_batch, dec_seq, _ = x_sample.shape
    assert dec_seq == 1 and seq % SWA_WINDOW == 0
    past_len = seq
    n_p, n_s = batch * seq, dec_batch * dec_seq
    xp = x_prompt.reshape(n_p, D_MODEL)
    xs = x_sample.reshape(n_s, D_MODEL)

    w_in = gla_w_in[0]
    w_main = w_in[:, :GLA_MAIN].astype(BF16)
    w_z = jnp.pad(w_in[:, GLA_MAIN:], ((0, 0), (0, LANES - GLA_GATE_RANK))).astype(BF16)
    w_g2 = jnp.pad(gla_w_gate2[0], ((0, LANES - GLA_GATE_RANK), (0, 0))).astype(BF16)
    b_g = gla_b_gate[0][None, :]
    g_head = gla_g_head[0][None, :]
    w_gout = gla_w_out[0].astype(BF16)
    w_qkv = swa_w_qkv[0].astype(BF16)
    b_qkv = swa_b_qkv[0][None, :]
    w_sout = swa_w_out[0].astype(BF16)
    b_sout = swa_b_out[0][None, :]
    w_up = ffn_w_up.astype(BF16)
    w_dn = ffn_w_down.astype(BF16)
    row = lambda t, i: t[i][None, :]

    in0_consts = [row(norm_mix_pre, 0), w_main, w_z, w_g2, b_g]
    mid_consts = [g_head, w_gout, row(norm_mix_post, 0), row(norm_ffn_pre, 0), w_up[0], w_dn[0], row(norm_ffn_post, 0),
                  row(norm_mix_pre, 1), w_qkv, b_qkv]
    out_consts = [w_sout, b_sout, row(norm_mix_post, 1), row(norm_ffn_pre, 1), w_up[1], w_dn[1], row(norm_ffn_post, 1)]
    in0_widths = [GLA_DK, GLA_DK, GLA_DV, GLA_DV, GLA_DK]
    mid_widths = [D_MODEL, SWA_Q, SWA_KV, SWA_KV]

    tm = 512
    q, k, v, r, la = _tok_call(_in0_body, n_p, tm, [(xp, None)], in0_consts, in0_widths,
                               [F32, F32, BF16, F32, F32], "in0_prompt")
    o, s_fin_p = _gla_prompt(q, k, v, la, batch, seq)
    tabs = _rope_tables(jnp.arange(seq))
    tab_map = lambda i: (i % (seq // tm), 0)
    h2, q1, k1, v1 = _tok_call(_mid_body, n_p, tm,
                               [(o, None), (r, None), (xp, None)] + [(t, tab_map) for t in tabs],
                               mid_consts, mid_widths, [F32] * 4, "mid_prompt")
    attn = _swa_prompt(swa_sinks[0], q1, k1, v1, batch, seq)
    (y_p,) = _tok_call(_out_body, n_p, tm, [(attn, None), (h2, None)], out_consts, [D_MODEL], [F32], "out_prompt")
    wp = min(SWA_WINDOW, seq)
    k_tail = k1.reshape(batch, seq, SWA_KV_HEADS, SWA_HEAD_DIM)[:, seq - wp:]
    v_tail = v1.reshape(batch, seq, SWA_KV_HEADS, SWA_HEAD_DIM)[:, seq - wp:]

    ts = n_s
    qs, ks, vs, rs, las = _tok_call(_in0_body, n_s, ts, [(xs, None)], in0_consts, in0_widths, [F32] * 5, "in0_sample")
    o_s, s_new = _gla_sample(qs, ks, vs, las, state_gla[0])
    tabs_s = _rope_tables(jnp.full((n_s,), past_len, jnp.int32))
    h2s, q1s, k1s, v1s = _tok_call(_mid_body, n_s, ts,
                                   [(o_s, None), (rs, None), (xs, None)] + [(t, None) for t in tabs_s],
                                   mid_consts, mid_widths, [F32] * 4, "mid_sample")
    win = cache_swa_k.shape[2]
    attn_s, nk, nv = _swa_sample(swa_sinks[0][:, None], q1s.reshape(n_s, SWA_HEADS, SWA_HEAD_DIM), k1s, v1s,
                                 cache_swa_k[0].reshape(dec_batch, win, SWA_KV), cache_swa_v[0].reshape(dec_batch, win, SWA_KV))
    (y_s,) = _tok_call(_out_body, n_s, ts, [(attn_s.reshape(n_s, SWA_Q), None), (h2s, None)], out_consts,
                       [D_MODEL], [F32], "out_sample")

    kv_shape = (1, dec_batch, win, SWA_KV_HEADS, SWA_HEAD_DIM)
    return (y_p.reshape(batch, seq, D_MODEL), y_s.reshape(dec_batch, dec_seq, D_MODEL),
            s_fin_p[None], s_new[None], k_tail[None], v_tail[None], nk.reshape(kv_shape), nv.reshape(kv_shape))
```

```python
import functools

import jax
import jax.numpy as jnp
from jax import lax
from jax.experimental import pallas as pl
from jax.experimental.pallas import tpu as pltpu

F32 = jnp.float32
BF16 = jnp.bfloat16

D_MODEL = 1024
D_FF = 4 * D_MODEL
NORM_EPS = 1e-6

GLA_HEADS = 4
GLA_DK = D_MODEL // 2
GLA_DV = D_MODEL
GLA_DK_HEAD = GLA_DK // GLA_HEADS
GLA_DV_HEAD = GLA_DV // GLA_HEADS
GLA_GATE_RANK = 16
GLA_TAU = 16.0
GLA_CHUNK = 64
GLA_MAIN = 2 * GLA_DK + 2 * GLA_DV

SWA_HEAD_DIM = 64
SWA_HEADS = D_MODEL // SWA_HEAD_DIM
SWA_KV_HEADS = 4
SWA_GROUP = SWA_HEADS // SWA_KV_HEADS
SWA_WINDOW = 128
SWA_Q = SWA_HEADS * SWA_HEAD_DIM
SWA_KV = SWA_KV_HEADS * SWA_HEAD_DIM
SWA_QKV = SWA_Q + 2 * SWA_KV
ROPE_THETA = 500000.0
ROPE_DIM = SWA_HEAD_DIM // 4
ROPE_HALF = ROPE_DIM // 2

LANES = 128
FFN_CHUNK = 512
VMEM_LIMIT = 56 * 1024 * 1024
NEG_BIG = -1e30
LOG2E = 1.4426950408889634


def _mm(a, b):
    return jnp.dot(a, b, preferred_element_type=F32)


def _mm_nt(a, b):
    return lax.dot_general(a, b, (((1,), (1,)), ((), ())), preferred_element_type=F32)


def _mm_tn(a, b):
    return lax.dot_general(a, b, (((0,), (0,)), ((), ())), preferred_element_type=F32)


def _rms(x, g):
    ms = jnp.mean(x * x, axis=-1, keepdims=True)
    return x * lax.rsqrt(ms + NORM_EPS) * g


def _split3(x):
    hi = x.astype(BF16)
    r1 = x - hi.astype(F32)
    mid = r1.astype(BF16)
    lo = (r1 - mid.astype(F32)).astype(BF16)
    return hi, mid, lo


def _ffn(a_bf16, wup_ref, wdn_ref):
    acc = None
    for c in range(D_FF // FFN_CHUNK):
        cols = slice(c * FFN_CHUNK, (c + 1) * FFN_CHUNK)
        u = _mm(a_bf16, wup_ref[:, cols])
        u = jnp.square(jnp.maximum(u, 0.0)).astype(BF16)
        p = _mm(u, wdn_ref[cols, :])
        acc = p if acc is None else acc + p
    return acc


def _in0_body(x_ref, g_ref, w_ref, wg_ref, bg_ref, q_ref, k_ref, v_ref, r_ref, la_ref):
    a = _rms(x_ref[...], g_ref[...]).astype(BF16)
    q_ref[...] = _mm(a, w_ref[:, 0:GLA_DK]) * (GLA_DK_HEAD ** -0.5)
    k_ref[...] = _mm(a, w_ref[:, GLA_DK:2 * GLA_DK])
    for c in range(GLA_DV // 512):
        cols = slice(c * 512, (c + 1) * 512)
        v_ref[:, cols] = _mm(a, w_ref[:, 2 * GLA_DK + c * 512:2 * GLA_DK + (c + 1) * 512]).astype(v_ref.dtype)
        r_ref[:, cols] = _mm(a, w_ref[:, 2 * GLA_DK + GLA_DV + c * 512:2 * GLA_DK + GLA_DV + (c + 1) * 512])
    z = _mm(a, w_ref[:, GLA_MAIN:GLA_MAIN + LANES]).astype(BF16)
    zg = _mm(z, wg_ref[...]) + bg_ref[...]
    la_ref[...] = (jnp.minimum(zg, 0.0) - jnp.log1p(jnp.exp(-jnp.abs(zg)))) * (1.0 / GLA_TAU)


def _mid_body(o_ref, r_ref, h_ref, rc_ref, ra_ref, rb_ref,
              gh_ref, wo_ref, gpost_ref, gfpre_ref, wup_ref, wdn_ref, gfpost_ref,
              gpre1_ref, wqkv_ref, bqkv_ref,
              h2_ref, q1_ref, k1_ref, v1_ref):
    m = None
    for hh in range(GLA_HEADS):
        cols = slice(hh * GLA_DV_HEAD, (hh + 1) * GLA_DV_HEAD)
        on = _rms(o_ref[:, cols], gh_ref[...])
        r = r_ref[:, cols]
        u = (on * (r * (1.0 / (1.0 + jnp.exp(-r))))).astype(BF16)
        p = _mm(u, wo_ref[cols, :])
        m = p if m is None else m + p
    h1 = h_ref[...] + _rms(m, gpost_ref[...])
    f = _ffn(_rms(h1, gfpre_ref[...]).astype(BF16), wup_ref, wdn_ref)
    h2 = h1 + _rms(f, gfpost_ref[...])
    h2_ref[...] = h2
    a3 = _rms(h2, gpre1_ref[...]).astype(BF16)
    rc, ra, rb = rc_ref[...], ra_ref[...], rb_ref[...]
    wide = 2 * LANES
    for c2 in range((SWA_Q + SWA_KV) // wide):
        x2 = _mm(a3, wqkv_ref[:, c2 * wide:(c2 + 1) * wide]) + bqkv_ref[:, c2 * wide:(c2 + 1) * wide]
        for half in range(2):
            c = 2 * c2 + half
            x = x2[:, half * LANES:(half + 1) * LANES]
            y = x * rc + pltpu.roll(x, LANES - ROPE_HALF, axis=1) * ra + pltpu.roll(x, ROPE_HALF, axis=1) * rb
            if c < SWA_Q // LANES:
                q1_ref[:, c * LANES:(c + 1) * LANES] = y
            else:
                k1_ref[:, c * LANES - SWA_Q:(c + 1) * LANES - SWA_Q] = y
    v1_ref[...] = _mm(a3, wqkv_ref[:, SWA_Q + SWA_KV:SWA_QKV]) + bqkv_ref[:, SWA_Q + SWA_KV:SWA_QKV]


def _out_body(at_ref, h_ref, wo_ref, bo_ref, gpost_ref, gfpre_ref, wup_ref, wdn_ref, gfpost_ref, y_ref):
    m = _mm(at_ref[...].astype(BF16), wo_ref[...]) + bo_ref[...]
    h1 = h_ref[...] + _rms(m, gpost_ref[...])
    f = _ffn(_rms(h1, gfpre_ref[...]).astype(BF16), wup_ref, wdn_ref)
    y_ref[...] = h1 + _rms(f, gfpost_ref[...])


def _tok_call(body, n_rows, tm, row_inputs, const_inputs, out_widths, out_dtypes, name):
    assert n_rows % tm == 0
    in_specs, args = [], []
    for arr, imap in row_inputs:
        in_specs.append(pl.BlockSpec((tm, arr.shape[1]), imap if imap is not None else (lambda i: (i, 0))))
        args.append(arr)
    for entry in const_inputs:
        if isinstance(entry, tuple):
            arr, layer = entry
            spec = pl.BlockSpec((None,) + arr.shape[1:], lambda i, layer=layer: (layer, 0, 0),
                                pipeline_mode=pl.Buffered(1))
        else:
            arr = entry
            spec = pl.BlockSpec(arr.shape, lambda i: (0, 0), pipeline_mode=pl.Buffered(1))
        in_specs.append(spec)
        args.append(arr)
    out_specs = [pl.BlockSpec((tm, w), lambda i: (i, 0)) for w in out_widths]
    out_shape = [jax.ShapeDtypeStruct((n_rows, w), dt) for w, dt in zip(out_widths, out_dtypes)]
    return pl.pallas_call(
        body,
        grid=(n_rows // tm,),
        in_specs=in_specs,
        out_specs=out_specs,
        out_shape=out_shape,
        compiler_params=pltpu.CompilerParams(dimension_semantics=("arbitrary",), vmem_limit_bytes=VMEM_LIMIT),
        name=name,
    )(*args)


def _gla_prompt_body(q_ref, k_ref, v_ref, la_ref, o_ref, sfin_ref, st_ref):
    t = pl.program_id(1)
    tg = q_ref.shape[0]
    c_len = GLA_CHUNK

    @pl.when(t == 0)
    def _():
        st_ref[...] = jnp.zeros_like(st_ref)

    row = lax.broadcasted_iota(jnp.int32, (tg, tg), 0)
    col = lax.broadcasted_iota(jnp.int32, (tg, tg), 1)
    lower = jnp.where((row // c_len == col // c_len) & (col <= row), 1.0, 0.0).astype(BF16)
    hi, mid, lo = _split3(la_ref[...])
    cum = _mm(lower, hi) + _mm(lower, mid) + _mm(lower, lo)
    causal = lax.broadcasted_iota(jnp.int32, (c_len, c_len), 0) >= lax.broadcasted_iota(jnp.int32, (c_len, c_len), 1)
    for h in range(GLA_HEADS):
        kc = slice(h * GLA_DK_HEAD, (h + 1) * GLA_DK_HEAD)
        vc = slice(h * GLA_DV_HEAD, (h + 1) * GLA_DV_HEAD)
        for ci in range(tg // c_len):
            rows = slice(ci * c_len, (ci + 1) * c_len)
            cum_c = cum[rows, kc]
            tot = cum_c[c_len - 1:c_len, :]
            e_col = jnp.exp(cum_c[c_len - 8:c_len, :]).T[:, 7:8]
            q_c = q_ref[rows, kc]
            k_c = k_ref[rows, kc]
            v_c = v_ref[rows, vc]
            qd = (q_c * jnp.exp(cum_c)).astype(BF16)
            ki = (k_c * jnp.exp(-cum_c)).astype(BF16)
            ke = (k_c * jnp.exp(tot - cum_c)).astype(BF16)
            att = jnp.where(causal, _mm_nt(qd, ki), 0.0).astype(BF16)
            st = st_ref[h]
            o_ref[rows, vc] = _mm(att, v_c) + _mm(qd, st.astype(BF16))
            st_ref[h] = e_col * st + _mm_tn(ke, v_c)

    @pl.when(t == pl.num_programs(1) - 1)
    def _():
        sfin_ref[0] = st_ref[...]


def _gla_prompt(q, k, v, la, batch, seq, tg=256):
    nt = seq // tg
    qk_spec = pl.BlockSpec((tg, GLA_DK), lambda b, t: (b * nt + t, 0))
    v_spec = pl.BlockSpec((tg, GLA_DV), lambda b, t: (b * nt + t, 0))
    st_shape = (GLA_HEADS, GLA_DK_HEAD, GLA_DV_HEAD)
    return pl.pallas_call(
        _gla_prompt_body,
        grid=(batch, nt),
        in_specs=[qk_spec, qk_spec, v_spec, qk_spec],
        out_specs=[v_spec, pl.BlockSpec((1,) + st_shape, lambda b, t: (b, 0, 0, 0))],
        out_shape=[jax.ShapeDtypeStruct((batch * seq, GLA_DV), F32),
                   jax.ShapeDtypeStruct((batch,) + st_shape, F32)],
        scratch_shapes=[pltpu.VMEM(st_shape, F32)],
        compiler_params=pltpu.CompilerParams(dimension_semantics=("arbitrary", "arbitrary"), vmem_limit_bytes=VMEM_LIMIT),
        name="gla_prompt",
    )(q, k, v, la)


def _gla_sample_body(q_ref, k_ref, v_ref, la_ref, s_ref, o_ref, sn_ref):
    bt = q_ref.shape[0]
    for h in range(GLA_HEADS):
        kc = slice(h * GLA_DK_HEAD, (h + 1) * GLA_DK_HEAD)
        vc = slice(h * GLA_DV_HEAD, (h + 1) * GLA_DV_HEAD)
        a_t = jnp.exp(la_ref[:, kc]).T
        k_t = k_ref[:, kc].T
        q_t = q_ref[:, kc].T
        for j in range(bt):
            s_new = a_t[:, j:j + 1] * s_ref[j, h] + k_t[:, j:j + 1] * v_ref[j:j + 1, vc]
            sn_ref[j, h] = s_new
            o_ref[j:j + 1, vc] = jnp.sum(q_t[:, j:j + 1] * s_new, axis=0, keepdims=True)


def _gla_sample(q, k, v, la, state, bt=8):
    nb = q.shape[0]
    row = lambda w: pl.BlockSpec((bt, w), lambda i: (i, 0))
    st_spec = pl.BlockSpec((bt, GLA_HEADS, GLA_DK_HEAD, GLA_DV_HEAD), lambda i: (i, 0, 0, 0))
    return pl.pallas_call(
        _gla_sample_body,
        grid=(nb // bt,),
        in_specs=[row(GLA_DK), row(GLA_DK), row(GLA_DV), row(GLA_DK), st_spec],
        out_specs=[row(GLA_DV), st_spec],
        out_shape=[jax.ShapeDtypeStruct((nb, GLA_DV), F32), jax.ShapeDtypeStruct(state.shape, F32)],
        compiler_params=pltpu.CompilerParams(dimension_semantics=("arbitrary",), vmem_limit_bytes=VMEM_LIMIT),
        name="gla_sample",
    )(q, k, v, la, state)


def _swa_attend_tile(sink_ref, q_ref, k_full, v_full, has_prev, o_ref):
    w = SWA_WINDOW
    hd = SWA_HEAD_DIM
    tq = q_ref.shape[0]
    nkv = k_full.shape[0]
    lane_q = lax.broadcasted_iota(jnp.int32, (w, LANES), 1) < hd
    lane_kv = lax.broadcasted_iota(jnp.int32, (nkv, LANES), 1) < hd
    i = lax.broadcasted_iota(jnp.int32, (w, 2 * w), 0)
    j = lax.broadcasted_iota(jnp.int32, (w, 2 * w), 1)
    band = jnp.where(j < w, jnp.where(j >= i, 1, 0), jnp.where(j - w <= i, 1, 0))
    band_first = jnp.where(j < w, has_prev, 1) * band
    lane_2w = lax.broadcasted_iota(jnp.int32, (2 * w, LANES), 1) < hd
    ones_lo = jnp.where(lane_2w, 1.0, 0.0).astype(BF16)
    ones_hi = jnp.where(lane_2w, 0.0, 1.0).astype(BF16)
    c2 = (hd ** -0.5) * LOG2E

    k_prep, v_prep = [], []
    for p in range(SWA_KV // LANES):
        cols = slice(p * LANES, (p + 1) * LANES)
        k_p, v_p = k_full[:, cols], v_full[:, cols]
        k_prep.append((k_p.astype(BF16), pltpu.roll(k_p, hd, axis=1).astype(BF16)))
        v_r = pltpu.roll(v_p, hd, axis=1)
        v_prep.append(((jnp.where(lane_kv, v_p, 0.0).astype(BF16), jnp.where(lane_kv, 0.0, v_r).astype(BF16)),
                       (jnp.where(lane_kv, v_r, 0.0).astype(BF16), jnp.where(lane_kv, 0.0, v_p).astype(BF16))))

    def softmax_part(s, hh, mask):
        s2 = jnp.where(mask, s, NEG_BIG)
        sk2 = jnp.full((w, 1), sink_ref[hh], F32) * LOG2E
        m2 = jnp.maximum(jnp.max(s2, axis=-1, keepdims=True), sk2)
        return jnp.exp2(s2 - m2).astype(BF16), sk2 - m2

    for b in range(tq // w):
        rows = slice(b * w, (b + 1) * w)
        krows = slice(b * w, (b + 2) * w)
        mask = (band_first if b == 0 else band) > 0
        for p in range(SWA_KV // LANES):
            q_lo, q_hi = [], []
            for x in range(4):
                q_c = q_ref[rows, (4 * p + x) * LANES:(4 * p + x + 1) * LANES] * c2
                q_lo.append(jnp.where(lane_q, q_c, 0.0).astype(BF16))
                q_hi.append(jnp.where(lane_q, 0.0, q_c).astype(BF16))
            s_self = _mm_nt(jnp.concatenate([q_lo[0], q_lo[1], q_hi[2], q_hi[3]], axis=0), k_prep[p][0][krows])
            s_roll = _mm_nt(jnp.concatenate([q_hi[0], q_hi[1], q_lo[2], q_lo[3]], axis=0), k_prep[p][1][krows])
            for x in range(4):
                c = 4 * p + x
                gh = x // 2
                xr = slice(x * w, (x + 1) * w)
                s_lo, s_hi = (s_self[xr], s_roll[xr]) if gh == 0 else (s_roll[xr], s_self[xr])
                p_lo, d_lo = softmax_part(s_lo, 2 * c, mask)
                p_hi, d_hi = softmax_part(s_hi, 2 * c + 1, mask)
                v_lo, v_hi = v_prep[p][gh]
                rhs = jnp.concatenate([jnp.concatenate([v_lo[krows], ones_lo], axis=1),
                                       jnp.concatenate([v_hi[krows], ones_hi], axis=1)], axis=0)
                ext = _mm(jnp.concatenate([p_lo, p_hi], axis=1), rhs)
                den = ext[:, LANES:] + jnp.exp2(jnp.where(lane_q, d_lo, d_hi))
                o_ref[rows, c * LANES:(c + 1) * LANES] = ext[:, :LANES] / den


def _swa_prompt_body(sink_ref, q_ref, kc_ref, kp_ref, vc_ref, vp_ref, o_ref):
    has_prev = jnp.minimum(pl.program_id(1), 1)
    k_full = jnp.concatenate([kp_ref[...], kc_ref[...]], axis=0)
    v_full = jnp.concatenate([vp_ref[...], vc_ref[...]], axis=0)
    _swa_attend_tile(sink_ref, q_ref, k_full, v_full, has_prev, o_ref)


def _swa_prompt(sinks, q, k, v, batch, seq, tq=512):
    w = SWA_WINDOW
    nt = seq // tq
    cur = lambda width: pl.BlockSpec((tq, width), lambda b, t: (b * nt + t, 0))
    prev = lambda width: pl.BlockSpec((w, width), lambda b, t: (b * (seq // w) + jnp.maximum(t * (tq // w) - 1, 0), 0))
    return pl.pallas_call(
        _swa_prompt_body,
        grid=(batch, nt),
        in_specs=[pl.BlockSpec(memory_space=pltpu.SMEM), cur(SWA_Q), cur(SWA_KV), prev(SWA_KV), cur(SWA_KV), prev(SWA_KV)],
        out_specs=cur(SWA_Q),
        out_shape=jax.ShapeDtypeStruct((batch * seq, SWA_Q), F32),
        compiler_params=pltpu.CompilerParams(dimension_semantics=("arbitrary", "arbitrary"), vmem_limit_bytes=VMEM_LIMIT),
        name="swa_prompt",
    )(sinks, q, k, k, v, v)


def _swa_sample_body(sk_ref, q_ref, kn_ref, vn_ref, ck_ref, cv_ref, o_ref, nk_ref, nv_ref):
    bt = q_ref.shape[0]
    w = ck_ref.shape[2]
    hd = SWA_HEAD_DIM
    hgroup = lax.broadcasted_iota(jnp.int32, (SWA_HEADS, 1), 0) // SWA_GROUP
    newest = lax.broadcasted_iota(jnp.int32, (SWA_KV, w), 1) == w - 1
    kn_t = kn_ref[...].T
    vn_t = vn_ref[...].T
    scale = hd ** -0.5
    sk = sk_ref[...]
    for j in range(bt):
        k_t = ck_ref[j]
        v_t = cv_ref[j]
        nk_ref[j] = jnp.where(newest, kn_t[:, j:j + 1], pltpu.roll(k_t, w - 1, axis=1))
        nv_ref[j] = jnp.where(newest, vn_t[:, j:j + 1], pltpu.roll(v_t, w - 1, axis=1))
        q = q_ref[j]
        qm = jnp.concatenate([jnp.where(hgroup == g, q, 0.0) for g in range(SWA_KV_HEADS)], axis=1)
        s_old = _mm(qm.astype(BF16), k_t.astype(BF16)) * scale
        s_new = jnp.sum(qm * kn_ref[j:j + 1, :], axis=-1, keepdims=True) * scale
        m = jnp.maximum(jnp.maximum(jnp.max(s_old, axis=-1, keepdims=True), s_new), sk)
        p_old = jnp.exp(s_old - m)
        p_new = jnp.exp(s_new - m)
        den = jnp.sum(p_old, axis=-1, keepdims=True) + p_new + jnp.exp(sk - m)
        o_full = _mm_nt(p_old.astype(BF16), v_t.astype(BF16)) + p_new * vn_ref[j:j + 1, :]
        o = jnp.zeros((SWA_HEADS, hd), F32)
        for g in range(SWA_KV_HEADS):
            o = jnp.where(hgroup == g, o_full[:, g * hd:(g + 1) * hd], o)
        o_ref[j] = o / den


def _swa_sample(sinks, q3, k_new, v_new, cache_k, cache_v, bt=8):
    nb, _, w = cache_k.shape
    assert w == LANES
    row = lambda width: pl.BlockSpec((bt, width), lambda i: (i, 0))
    q_spec = pl.BlockSpec((bt, SWA_HEADS, SWA_HEAD_DIM), lambda i: (i, 0, 0))
    c_spec = pl.BlockSpec((bt, SWA_KV, w), lambda i: (i, 0, 0))
    return pl.pallas_call(
        _swa_sample_body,
        grid=(nb // bt,),
        in_specs=[pl.BlockSpec((SWA_HEADS, 1), lambda i: (0, 0)), q_spec, row(SWA_KV), row(SWA_KV), c_spec, c_spec],
        out_specs=[q_spec, c_spec, c_spec],
        out_shape=[jax.ShapeDtypeStruct((nb, SWA_HEADS, SWA_HEAD_DIM), F32),
                   jax.ShapeDtypeStruct(cache_k.shape, F32), jax.ShapeDtypeStruct(cache_v.shape, F32)],
        compiler_params=pltpu.CompilerParams(dimension_semantics=("arbitrary",)),
        name="swa_sample",
    )(sinks, q3, k_new, v_new, cache_k, cache_v)


def _rope_tables(pos):
    inv = jnp.power(ROPE_THETA, -jnp.arange(ROPE_HALF, dtype=F32) * 2.0 / ROPE_DIM)
    ang = pos.astype(F32)[:, None] * inv[None, :]
    cos, sin = jnp.cos(ang), jnp.sin(ang)
    n = pos.shape[0]
    rest = SWA_HEAD_DIM - ROPE_DIM
    rc = jnp.concatenate([cos, cos, jnp.ones((n, rest), F32)], axis=-1)
    ra = jnp.concatenate([-sin, jnp.zeros((n, ROPE_HALF + rest), F32)], axis=-1)
    rb = jnp.concatenate([jnp.zeros((n, ROPE_HALF), F32), sin, jnp.zeros((n, rest), F32)], axis=-1)
    reps = LANES // SWA_HEAD_DIM
    return tuple(jnp.tile(t, (1, reps)) for t in (rc, ra, rb))


def kernel(x_prompt, x_sample, state_gla, cache_swa_k, cache_swa_v, gla_w_in, gla_w_gate2, gla_b_gate, gla_g_head, gla_w_out, swa_w_qkv, swa_b_qkv, swa_sinks, swa_w_out, swa_b_out, norm_mix_pre, norm_mix_post, norm_ffn_pre, norm_ffn_post, ffn_w_up, ffn_w_down):
    batch, seq, _ = x_prompt.shape
    dec_batch, dec_seq, _ = x_sample.shape
    assert dec_seq == 1 and seq % SWA_WINDOW == 0
    past_len = seq
    n_p, n_s = batch * seq, dec_batch * dec_seq
    xp = x_prompt.reshape(n_p, D_MODEL)
    xs = x_sample.reshape(n_s, D_MODEL)

    w_in = jnp.pad(gla_w_in[0].astype(BF16), ((0, 0), (0, LANES - GLA_GATE_RANK)))
    w_g2 = jnp.pad(gla_w_gate2[0], ((0, LANES - GLA_GATE_RANK), (0, 0))).astype(BF16)
    b_g = gla_b_gate[0][None, :]
    g_head = gla_g_head[0][None, :]
    w_gout = gla_w_out[0].astype(BF16)
    w_qkv = swa_w_qkv[0].astype(BF16)
    b_qkv = swa_b_qkv[0][None, :]
    w_sout = swa_w_out[0].astype(BF16)
    b_sout = swa_b_out[0][None, :]
    w_up = ffn_w_up.astype(BF16)
    w_dn = ffn_w_down.astype(BF16)
    row = lambda t, i: t[i][None, :]

    in0_consts = [row(norm_mix_pre, 0), w_in, w_g2, b_g]
    mid_consts = [g_head, w_gout, row(norm_mix_post, 0), row(norm_ffn_pre, 0), (w_up, 0), (w_dn, 0), row(norm_ffn_post, 0),
                  row(norm_mix_pre, 1), w_qkv, b_qkv]
    out_consts = [w_sout, b_sout, row(norm_mix_post, 1), row(norm_ffn_pre, 1), (w_up, 1), (w_dn, 1), row(norm_ffn_post, 1)]
    in0_widths = [GLA_DK, GLA_DK, GLA_DV, GLA_DV, GLA_DK]
    mid_widths = [D_MODEL, SWA_Q, SWA_KV, SWA_KV]

    tm = 512
    q, k, v, r, la = _tok_call(_in0_body, n_p, tm, [(xp, None)], in0_consts, in0_widths,
                               [F32, F32, BF16, F32, F32], "in0_prompt")
    o, s_fin_p = _gla_prompt(q, k, v, la, batch, seq)
    tabs = _rope_tables(jnp.arange(seq))
    tab_map = lambda i: (i % (seq // tm), 0)
    h2, q1, k1, v1 = _tok_call(_mid_body, n_p, tm,
                               [(o, None), (r, None), (xp, None)] + [(t, tab_map) for t in tabs],
                               mid_consts, mid_widths, [F32] * 4, "mid_prompt")
    attn = _swa_prompt(swa_sinks[0], q1, k1, v1, batch, seq)
    (y_p,) = _tok_call(_out_body, n_p, tm, [(attn, None), (h2, None)], out_consts, [D_MODEL], [F32], "out_prompt")
    wp = min(SWA_WINDOW, seq)
    tail = lambda t: t.reshape(batch, seq, SWA_KV)[:, seq - wp:].reshape(batch, wp, SWA_KV_HEADS, SWA_HEAD_DIM)
    k_tail, v_tail = tail(k1), tail(v1)

    ts = n_s
    qs, ks, vs, rs, las = _tok_call(_in0_body, n_s, ts, [(xs, None)], in0_consts, in0_widths, [F32] * 5, "in0_sample")
    o_s, s_new = _gla_sample(qs, ks, vs, las, state_gla[0])
    tabs_s = _rope_tables(jnp.full((n_s,), past_len, jnp.int32))
    h2s, q1s, k1s, v1s = _tok_call(_mid_body, n_s, ts,
                                   [(o_s, None), (rs, None), (xs, None)] + [(t, None) for t in tabs_s],
                                   mid_consts, mid_widths, [F32] * 4, "mid_sample")
    win = cache_swa_k.shape[2]
    to_t = lambda c: jnp.transpose(c[0].reshape(dec_batch, win, SWA_KV), (0, 2, 1))
    from_t = lambda c: jnp.transpose(c, (0, 2, 1)).reshape(1, dec_batch, win, SWA_KV_HEADS, SWA_HEAD_DIM)
    attn_s, nk, nv = _swa_sample(swa_sinks[0][:, None], q1s.reshape(n_s, SWA_HEADS, SWA_HEAD_DIM), k1s, v1s,
                                 to_t(cache_swa_k), to_t(cache_swa_v))
    (y_s,) = _tok_call(_out_body, n_s, ts, [(attn_s.reshape(n_s, SWA_Q), None), (h2s, None)], out_consts,
                       [D_MODEL], [F32], "out_sample")

    return (y_p.reshape(batch, seq, D_MODEL), y_s.reshape(dec_batch, dec_seq, D_MODEL),
            s_fin_p[None], s_new[None], k_tail[None], v_tail[None], from_t(nk), from_t(nv))
```

```python
import functools

import jax
import jax.numpy as jnp
from jax import lax
from jax.experimental import pallas as pl
from jax.experimental.pallas import tpu as pltpu

F32 = jnp.float32
BF16 = jnp.bfloat16

D_MODEL = 1024
D_FF = 4 * D_MODEL
NORM_EPS = 1e-6

GLA_HEADS = 4
GLA_DK = D_MODEL // 2
GLA_DV = D_MODEL
GLA_DK_HEAD = GLA_DK // GLA_HEADS
GLA_DV_HEAD = GLA_DV // GLA_HEADS
GLA_GATE_RANK = 16
GLA_TAU = 16.0
GLA_CHUNK = 64
GLA_MAIN = 2 * GLA_DK + 2 * GLA_DV

SWA_HEAD_DIM = 64
SWA_HEADS = D_MODEL // SWA_HEAD_DIM
SWA_KV_HEADS = 4
SWA_GROUP = SWA_HEADS // SWA_KV_HEADS
SWA_WINDOW = 128
SWA_Q = SWA_HEADS * SWA_HEAD_DIM
SWA_KV = SWA_KV_HEADS * SWA_HEAD_DIM
SWA_QKV = SWA_Q + 2 * SWA_KV
ROPE_THETA = 500000.0
ROPE_DIM = SWA_HEAD_DIM // 4
ROPE_HALF = ROPE_DIM // 2

LANES = 128
FFN_CHUNK = 512
VMEM_LIMIT = 56 * 1024 * 1024
NEG_BIG = -1e30
LOG2E = 1.4426950408889634


def _mm(a, b):
    return jnp.dot(a, b, preferred_element_type=F32)


def _mm_nt(a, b):
    return lax.dot_general(a, b, (((1,), (1,)), ((), ())), preferred_element_type=F32)


def _mm_tn(a, b):
    return lax.dot_general(a, b, (((0,), (0,)), ((), ())), preferred_element_type=F32)


def _rms(x, g):
    ms = jnp.mean(x * x, axis=-1, keepdims=True)
    return x * lax.rsqrt(ms + NORM_EPS) * g


def _split3(x):
    hi = x.astype(BF16)
    r1 = x - hi.astype(F32)
    mid = r1.astype(BF16)
    lo = (r1 - mid.astype(F32)).astype(BF16)
    return hi, mid, lo


def _ffn(a_bf16, wup_ref, wdn_ref):
    acc = None
    for c in range(D_FF // FFN_CHUNK):
        cols = slice(c * FFN_CHUNK, (c + 1) * FFN_CHUNK)
        u = _mm(a_bf16, wup_ref[:, cols])
        u = jnp.square(jnp.maximum(u, 0.0)).astype(BF16)
        p = _mm(u, wdn_ref[cols, :])
        acc = p if acc is None else acc + p
    return acc


def _in0_body(x_ref, g_ref, w_ref, wg_ref, bg_ref, q_ref, k_ref, v_ref, r_ref, la_ref):
    a = _rms(x_ref[...], g_ref[...]).astype(BF16)
    q_ref[...] = _mm(a, w_ref[:, 0:GLA_DK]) * (GLA_DK_HEAD ** -0.5)
    k_ref[...] = _mm(a, w_ref[:, GLA_DK:2 * GLA_DK])
    for c in range(GLA_DV // 512):
        cols = slice(c * 512, (c + 1) * 512)
        v_ref[:, cols] = _mm(a, w_ref[:, 2 * GLA_DK + c * 512:2 * GLA_DK + (c + 1) * 512]).astype(v_ref.dtype)
        r_ref[:, cols] = _mm(a, w_ref[:, 2 * GLA_DK + GLA_DV + c * 512:2 * GLA_DK + GLA_DV + (c + 1) * 512])
    z = _mm(a, w_ref[:, GLA_MAIN:GLA_MAIN + LANES]).astype(BF16)
    zg = _mm(z, wg_ref[...]) + bg_ref[...]
    la_ref[...] = (jnp.minimum(zg, 0.0) - jnp.log1p(jnp.exp(-jnp.abs(zg)))) * (1.0 / GLA_TAU)


def _mid_body(o_ref, r_ref, h_ref, rc_ref, ra_ref, rb_ref,
              gh_ref, wo_ref, gpost_ref, gfpre_ref, wup_ref, wdn_ref, gfpost_ref,
              gpre1_ref, wqkv_ref, bqkv_ref,
              h2_ref, q1_ref, k1_ref, v1_ref):
    m = None
    for hh in range(GLA_HEADS):
        cols = slice(hh * GLA_DV_HEAD, (hh + 1) * GLA_DV_HEAD)
        on = _rms(o_ref[:, cols], gh_ref[...])
        r = r_ref[:, cols]
        u = (on * (r * (1.0 / (1.0 + jnp.exp(-r))))).astype(BF16)
        p = _mm(u, wo_ref[cols, :])
        m = p if m is None else m + p
    h1 = h_ref[...] + _rms(m, gpost_ref[...])
    f = _ffn(_rms(h1, gfpre_ref[...]).astype(BF16), wup_ref, wdn_ref)
    h2 = h1 + _rms(f, gfpost_ref[...])
    h2_ref[...] = h2
    a3 = _rms(h2, gpre1_ref[...]).astype(BF16)
    rc, ra, rb = rc_ref[...], ra_ref[...], rb_ref[...]
    wide = 2 * LANES
    for c2 in range((SWA_Q + SWA_KV) // wide):
        x2 = _mm(a3, wqkv_ref[:, c2 * wide:(c2 + 1) * wide]) + bqkv_ref[:, c2 * wide:(c2 + 1) * wide]
        for half in range(2):
            c = 2 * c2 + half
            x = x2[:, half * LANES:(half + 1) * LANES]
            y = x * rc + pltpu.roll(x, LANES - ROPE_HALF, axis=1) * ra + pltpu.roll(x, ROPE_HALF, axis=1) * rb
            if c < SWA_Q // LANES:
                q1_ref[:, c * LANES:(c + 1) * LANES] = y
            else:
                k1_ref[:, c * LANES - SWA_Q:(c + 1) * LANES - SWA_Q] = y
    v1_ref[...] = _mm(a3, wqkv_ref[:, SWA_Q + SWA_KV:SWA_QKV]) + bqkv_ref[:, SWA_Q + SWA_KV:SWA_QKV]


def _out_body(at_ref, h_ref, wo_ref, bo_ref, gpost_ref, gfpre_ref, wup_ref, wdn_ref, gfpost_ref, y_ref):
    m = _mm(at_ref[...].astype(BF16), wo_ref[...]) + bo_ref[...]
    h1 = h_ref[...] + _rms(m, gpost_ref[...])
    f = _ffn(_rms(h1, gfpre_ref[...]).astype(BF16), wup_ref, wdn_ref)
    y_ref[...] = h1 + _rms(f, gfpost_ref[...])


def _tok_call(body, n_rows, tm, row_inputs, const_inputs, out_widths, out_dtypes, name):
    assert n_rows % tm == 0
    in_specs, args = [], []
    for arr, imap in row_inputs:
        in_specs.append(pl.BlockSpec((tm, arr.shape[1]), imap if imap is not None else (lambda i: (i, 0))))
        args.append(arr)
    for entry in const_inputs:
        if isinstance(entry, tuple):
            arr, layer = entry
            spec = pl.BlockSpec((None,) + arr.shape[1:], lambda i, layer=layer: (layer, 0, 0),
                                pipeline_mode=pl.Buffered(1))
        else:
            arr = entry
            spec = pl.BlockSpec(arr.shape, lambda i: (0, 0), pipeline_mode=pl.Buffered(1))
        in_specs.append(spec)
        args.append(arr)
    out_specs = [pl.BlockSpec((tm, w), lambda i: (i, 0)) for w in out_widths]
    out_shape = [jax.ShapeDtypeStruct((n_rows, w), dt) for w, dt in zip(out_widths, out_dtypes)]
    return pl.pallas_call(
        body,
        grid=(n_rows // tm,),
        in_specs=in_specs,
        out_specs=out_specs,
        out_shape=out_shape,
        compiler_params=pltpu.CompilerParams(dimension_semantics=("arbitrary",), vmem_limit_bytes=VMEM_LIMIT),
        name=name,
    )(*args)


def _gla_prompt_body(q_ref, k_ref, v_ref, la_ref, o_ref, sfin_ref, st_ref):
    t = pl.program_id(1)
    tg = q_ref.shape[0]
    c_len = GLA_CHUNK

    @pl.when(t == 0)
    def _():
        st_ref[...] = jnp.zeros_like(st_ref)

    n_chunks = tg // c_len
    chunk_rows = [slice(ci * c_len, (ci + 1) * c_len) for ci in range(n_chunks)]
    row = lax.broadcasted_iota(jnp.int32, (tg, tg), 0)
    col = lax.broadcasted_iota(jnp.int32, (tg, tg), 1)
    lower_b = (row // c_len == col // c_len) & (col <= row)
    lower = jnp.where(lower_b, 1.0, 0.0).astype(BF16)
    hi, mid, lo = _split3(la_ref[...])
    cum = _mm(lower, hi) + _mm(lower, mid) + _mm(lower, lo)
    tot = jnp.concatenate([jnp.broadcast_to(cum[r.stop - 1:r.stop, :], (c_len, cum.shape[1])) for r in chunk_rows], axis=0)
    k_all = k_ref[...]
    qd = (q_ref[...] * jnp.exp(cum)).astype(BF16)
    ki = (k_all * jnp.exp(-cum)).astype(BF16)
    ke = k_all * jnp.exp(tot - cum)
    lane_chunk = lax.broadcasted_iota(jnp.int32, (GLA_DK_HEAD, tg), 1) // c_len

    att, kv = [], []
    for h in range(GLA_HEADS):
        kc = slice(h * GLA_DK_HEAD, (h + 1) * GLA_DK_HEAD)
        att.append(jnp.where(lower_b, _mm_nt(qd[:, kc], ki[:, kc]), 0.0).astype(BF16))
        ke_t = ke[:, kc].T
        v_h = v_ref[:, h * GLA_DV_HEAD:(h + 1) * GLA_DV_HEAD]
        kv.append([_mm(jnp.where(lane_chunk == ci, ke_t, 0.0).astype(BF16), v_h) for ci in range(n_chunks)])

    s_before = []
    for h in range(GLA_HEADS):
        kc = slice(h * GLA_DK_HEAD, (h + 1) * GLA_DK_HEAD)
        st = st_ref[h]
        starts = []
        for ci, r in enumerate(chunk_rows):
            starts.append(st.astype(BF16))
            e_col = jnp.exp(cum[r.stop - 8:r.stop, kc]).T[:, 7:8]
            st = e_col * st + kv[h][ci]
        st_ref[h] = st
        s_before.append(starts)

    for h in range(GLA_HEADS):
        kc = slice(h * GLA_DK_HEAD, (h + 1) * GLA_DK_HEAD)
        vc = slice(h * GLA_DV_HEAD, (h + 1) * GLA_DV_HEAD)
        o_intra = _mm(att[h], v_ref[:, vc])
        for ci, r in enumerate(chunk_rows):
            o_ref[r, vc] = o_intra[r] + _mm(qd[r, kc], s_before[h][ci])

    @pl.when(t == pl.num_programs(1) - 1)
    def _():
        sfin_ref[0] = st_ref[...]


def _gla_prompt(q, k, v, la, batch, seq, tg=256):
    nt = seq // tg
    qk_spec = pl.BlockSpec((tg, GLA_DK), lambda b, t: (b * nt + t, 0))
    v_spec = pl.BlockSpec((tg, GLA_DV), lambda b, t: (b * nt + t, 0))
    st_shape = (GLA_HEADS, GLA_DK_HEAD, GLA_DV_HEAD)
    return pl.pallas_call(
        _gla_prompt_body,
        grid=(batch, nt),
        in_specs=[qk_spec, qk_spec, v_spec, qk_spec],
        out_specs=[v_spec, pl.BlockSpec((1,) + st_shape, lambda b, t: (b, 0, 0, 0))],
        out_shape=[jax.ShapeDtypeStruct((batch * seq, GLA_DV), F32),
                   jax.ShapeDtypeStruct((batch,) + st_shape, F32)],
        scratch_shapes=[pltpu.VMEM(st_shape, F32)],
        compiler_params=pltpu.CompilerParams(dimension_semantics=("arbitrary", "arbitrary"), vmem_limit_bytes=VMEM_LIMIT),
        name="gla_prompt",
    )(q, k, v, la)


def _gla_sample_body(q_ref, k_ref, v_ref, la_ref, s_ref, o_ref, sn_ref):
    bt = q_ref.shape[0]
    for h in range(GLA_HEADS):
        kc = slice(h * GLA_DK_HEAD, (h + 1) * GLA_DK_HEAD)
        vc = slice(h * GLA_DV_HEAD, (h + 1) * GLA_DV_HEAD)
        a_t = jnp.exp(la_ref[:, kc]).T
        k_t = k_ref[:, kc].T
        q_t = q_ref[:, kc].T
        for j in range(bt):
            s_new = a_t[:, j:j + 1] * s_ref[j, h] + k_t[:, j:j + 1] * v_ref[j:j + 1, vc]
            sn_ref[j, h] = s_new
            o_ref[j:j + 1, vc] = jnp.sum(q_t[:, j:j + 1] * s_new, axis=0, keepdims=True)


def _gla_sample(q, k, v, la, state, bt=8):
    nb = q.shape[0]
    row = lambda w: pl.BlockSpec((bt, w), lambda i: (i, 0))
    st_spec = pl.BlockSpec((bt, GLA_HEADS, GLA_DK_HEAD, GLA_DV_HEAD), lambda i: (i, 0, 0, 0))
    return pl.pallas_call(
        _gla_sample_body,
        grid=(nb // bt,),
        in_specs=[row(GLA_DK), row(GLA_DK), row(GLA_DV), row(GLA_DK), st_spec],
        out_specs=[row(GLA_DV), st_spec],
        out_shape=[jax.ShapeDtypeStruct((nb, GLA_DV), F32), jax.ShapeDtypeStruct(state.shape, F32)],
        compiler_params=pltpu.CompilerParams(dimension_semantics=("arbitrary",), vmem_limit_bytes=VMEM_LIMIT),
        name="gla_sample",
    )(q, k, v, la, state)


def _swa_attend_tile(sink_ref, q_ref, k_full, v_full, has_prev, o_ref):
    w = SWA_WINDOW
    hd = SWA_HEAD_DIM
    tq = q_ref.shape[0]
    nkv = k_full.shape[0]
    lane_q = lax.broadcasted_iota(jnp.int32, (w, LANES), 1) < hd
    lane_kv = lax.broadcasted_iota(jnp.int32, (nkv, LANES), 1) < hd
    i = lax.broadcasted_iota(jnp.int32, (w, 2 * w), 0)
    j = lax.broadcasted_iota(jnp.int32, (w, 2 * w), 1)
    band = jnp.where(j < w, jnp.where(j >= i, 1, 0), jnp.where(j - w <= i, 1, 0))
    band_first = jnp.where(j < w, has_prev, 1) * band
    lane_2w = lax.broadcasted_iota(jnp.int32, (2 * w, LANES), 1) < hd
    ones_lo = jnp.where(lane_2w, 1.0, 0.0).astype(BF16)
    ones_hi = jnp.where(lane_2w, 0.0, 1.0).astype(BF16)
    c2 = (hd ** -0.5) * LOG2E

    k_prep, v_prep = [], []
    for p in range(SWA_KV // LANES):
        cols = slice(p * LANES, (p + 1) * LANES)
        k_p, v_p = k_full[:, cols], v_full[:, cols]
        k_prep.append((k_p.astype(BF16), pltpu.roll(k_p, hd, axis=1).astype(BF16)))
        v_r = pltpu.roll(v_p, hd, axis=1)
        v_prep.append(((jnp.where(lane_kv, v_p, 0.0).astype(BF16), jnp.where(lane_kv, 0.0, v_r).astype(BF16)),
                       (jnp.where(lane_kv, v_r, 0.0).astype(BF16), jnp.where(lane_kv, 0.0, v_p).astype(BF16))))

    def softmax_part(s, hh, mask):
        s2 = jnp.where(mask, s, NEG_BIG)
        sk2 = jnp.full((w, 1), sink_ref[hh], F32) * LOG2E
        m2 = jnp.maximum(jnp.max(s2, axis=-1, keepdims=True), sk2)
        return jnp.exp2(s2 - m2).astype(BF16), sk2 - m2

    for b in range(tq // w):
        rows = slice(b * w, (b + 1) * w)
        krows = slice(b * w, (b + 2) * w)
        mask = (band_first if b == 0 else band) > 0
        for p in range(SWA_KV // LANES):
            q_lo, q_hi = [], []
            for x in range(4):
                q_c = q_ref[rows, (4 * p + x) * LANES:(4 * p + x + 1) * LANES] * c2
                q_lo.append(jnp.where(lane_q, q_c, 0.0).astype(BF16))
                q_hi.append(jnp.where(lane_q, 0.0, q_c).astype(BF16))
            s_self = _mm_nt(jnp.concatenate([q_lo[0], q_lo[1], q_hi[2], q_hi[3]], axis=0), k_prep[p][0][krows])
            s_roll = _mm_nt(jnp.concatenate([q_hi[0], q_hi[1], q_lo[2], q_lo[3]], axis=0), k_prep[p][1][krows])
            for x in range(4):
                c = 4 * p + x
                gh = x // 2
                xr = slice(x * w, (x + 1) * w)
                s_lo, s_hi = (s_self[xr], s_roll[xr]) if gh == 0 else (s_roll[xr], s_self[xr])
                p_lo, d_lo = softmax_part(s_lo, 2 * c, mask)
                p_hi, d_hi = softmax_part(s_hi, 2 * c + 1, mask)
                v_lo, v_hi = v_prep[p][gh]
                rhs = jnp.concatenate([jnp.concatenate([v_lo[krows], ones_lo], axis=1),
                                       jnp.concatenate([v_hi[krows], ones_hi], axis=1)], axis=0)
                ext = _mm(jnp.concatenate([p_lo, p_hi], axis=1), rhs)
                den = ext[:, LANES:] + jnp.exp2(jnp.where(lane_q, d_lo, d_hi))
                o_ref[rows, c * LANES:(c + 1) * LANES] = ext[:, :LANES] / den


def _swa_prompt_body(sink_ref, q_ref, kc_ref, kp_ref, vc_ref, vp_ref, o_ref):
    has_prev = jnp.minimum(pl.program_id(1), 1)
    k_full = jnp.concatenate([kp_ref[...], kc_ref[...]], axis=0)
    v_full = jnp.concatenate([vp_ref[...], vc_ref[...]], axis=0)
    _swa_attend_tile(sink_ref, q_ref, k_full, v_full, has_prev, o_ref)


def _swa_prompt(sinks, q, k, v, batch, seq, tq=512):
    w = SWA_WINDOW
    nt = seq // tq
    cur = lambda width: pl.BlockSpec((tq, width), lambda b, t: (b * nt + t, 0))
    prev = lambda width: pl.BlockSpec((w, width), lambda b, t: (b * (seq // w) + jnp.maximum(t * (tq // w) - 1, 0), 0))
    return pl.pallas_call(
        _swa_prompt_body,
        grid=(batch, nt),
        in_specs=[pl.BlockSpec(memory_space=pltpu.SMEM), cur(SWA_Q), cur(SWA_KV), prev(SWA_KV), cur(SWA_KV), prev(SWA_KV)],
        out_specs=cur(SWA_Q),
        out_shape=jax.ShapeDtypeStruct((batch * seq, SWA_Q), F32),
        compiler_params=pltpu.CompilerParams(dimension_semantics=("arbitrary", "arbitrary"), vmem_limit_bytes=VMEM_LIMIT),
        name="swa_prompt",
    )(sinks, q, k, k, v, v)


def _swa_sample_body(sk_ref, q_ref, kn_ref, vn_ref, ck_ref, cv_ref, o_ref, nk_ref, nv_ref):
    bt = q_ref.shape[0]
    w = ck_ref.shape[2]
    hd = SWA_HEAD_DIM
    hgroup = lax.broadcasted_iota(jnp.int32, (SWA_HEADS, 1), 0) // SWA_GROUP
    newest = lax.broadcasted_iota(jnp.int32, (SWA_KV, w), 1) == w - 1
    kn_t = kn_ref[...].T
    vn_t = vn_ref[...].T
    scale = hd ** -0.5
    sk = sk_ref[...]
    groups = [slice(g * hd, (g + 1) * hd) for g in range(SWA_KV_HEADS)]

    def per_head(pieces):
        out = pieces[0]
        for g in range(1, SWA_KV_HEADS):
            out = jnp.where(hgroup == g, pieces[g], out)
        return out

    for j in range(bt):
        nk_ref[j] = jnp.where(newest, kn_t[:, j:j + 1], pltpu.roll(ck_ref[j], w - 1, axis=1))
        nv_ref[j] = jnp.where(newest, vn_t[:, j:j + 1], pltpu.roll(cv_ref[j], w - 1, axis=1))

    s_old, s_new, v_sel = [], [], []
    for j in range(bt):
        q = q_ref[j]
        qb = q.astype(BF16)
        s_old.append(per_head([_mm(qb, ck_ref[j, rows, :].astype(BF16)) for rows in groups]))
        k_sel = per_head([kn_ref[j:j + 1, cols] for cols in groups])
        v_sel.append(per_head([vn_ref[j:j + 1, cols] for cols in groups]))
        s_new.append(jnp.sum(q * k_sel, axis=-1, keepdims=True))
    s_old = jnp.stack(s_old, axis=0) * scale
    s_new = jnp.stack(s_new, axis=0) * scale
    m = jnp.maximum(jnp.maximum(jnp.max(s_old, axis=-1, keepdims=True), s_new), sk)
    p_old = jnp.exp(s_old - m)
    p_new = jnp.exp(s_new - m)
    inv = 1.0 / (jnp.sum(p_old, axis=-1, keepdims=True) + p_new + jnp.exp(sk - m))
    p_old = p_old.astype(BF16)
    for j in range(bt):
        o = per_head([_mm_nt(p_old[j], cv_ref[j, rows, :].astype(BF16)) for rows in groups])
        o_ref[j] = (o + p_new[j] * v_sel[j]) * inv[j]


def _swa_sample(sinks, q3, k_new, v_new, cache_k, cache_v, bt=8):
    nb, _, w = cache_k.shape
    assert w == LANES
    row = lambda width: pl.BlockSpec((bt, width), lambda i: (i, 0))
    q_spec = pl.BlockSpec((bt, SWA_HEADS, SWA_HEAD_DIM), lambda i: (i, 0, 0))
    c_spec = pl.BlockSpec((bt, SWA_KV, w), lambda i: (i, 0, 0))
    return pl.pallas_call(
        _swa_sample_body,
        grid=(nb // bt,),
        in_specs=[pl.BlockSpec((SWA_HEADS, 1), lambda i: (0, 0)), q_spec, row(SWA_KV), row(SWA_KV), c_spec, c_spec],
        out_specs=[q_spec, c_spec, c_spec],
        out_shape=[jax.ShapeDtypeStruct((nb, SWA_HEADS, SWA_HEAD_DIM), F32),
                   jax.ShapeDtypeStruct(cache_k.shape, F32), jax.ShapeDtypeStruct(cache_v.shape, F32)],
        compiler_params=pltpu.CompilerParams(dimension_semantics=("arbitrary",)),
        name="swa_sample",
    )(sinks, q3, k_new, v_new, cache_k, cache_v)


def _rope_tables(pos):
    inv = jnp.power(ROPE_THETA, -jnp.arange(ROPE_HALF, dtype=F32) * 2.0 / ROPE_DIM)
    ang = pos.astype(F32)[:, None] * inv[None, :]
    cos, sin = jnp.cos(ang), jnp.sin(ang)
    n = pos.shape[0]
    rest = SWA_HEAD_DIM - ROPE_DIM
    rc = jnp.concatenate([cos, cos, jnp.ones((n, rest), F32)], axis=-1)
    ra = jnp.concatenate([-sin, jnp.zeros((n, ROPE_HALF + rest), F32)], axis=-1)
    rb = jnp.concatenate([jnp.zeros((n, ROPE_HALF), F32), sin, jnp.zeros((n, rest), F32)], axis=-1)
    reps = LANES // SWA_HEAD_DIM
    return tuple(jnp.tile(t, (1, reps)) for t in (rc, ra, rb))


def kernel(x_prompt, x_sample, state_gla, cache_swa_k, cache_swa_v, gla_w_in, gla_w_gate2, gla_b_gate, gla_g_head, gla_w_out, swa_w_qkv, swa_b_qkv, swa_sinks, swa_w_out, swa_b_out, norm_mix_pre, norm_mix_post, norm_ffn_pre, norm_ffn_post, ffn_w_up, ffn_w_down):
    batch, seq, _ = x_prompt.shape
    dec_batch, dec_seq, _ = x_sample.shape
    assert dec_seq == 1 and seq % SWA_WINDOW == 0
    past_len = seq
    n_p, n_s = batch * seq, dec_batch * dec_seq
    xp = x_prompt.reshape(n_p, D_MODEL)
    xs = x_sample.reshape(n_s, D_MODEL)

    w_in = jnp.pad(gla_w_in[0].astype(BF16), ((0, 0), (0, LANES - GLA_GATE_RANK)))
    w_g2 = jnp.pad(gla_w_gate2[0], ((0, LANES - GLA_GATE_RANK), (0, 0))).astype(BF16)
    b_g = gla_b_gate[0][None, :]
    g_head = gla_g_head[0][None, :]
    w_gout = gla_w_out[0].astype(BF16)
    w_qkv = swa_w_qkv[0].astype(BF16)
    b_qkv = swa_b_qkv[0][None, :]
    w_sout = swa_w_out[0].astype(BF16)
    b_sout = swa_b_out[0][None, :]
    w_up = ffn_w_up.astype(BF16)
    w_dn = ffn_w_down.astype(BF16)
    row = lambda t, i: t[i][None, :]

    in0_consts = [row(norm_mix_pre, 0), w_in, w_g2, b_g]
    mid_consts = [g_head, w_gout, row(norm_mix_post, 0), row(norm_ffn_pre, 0), (w_up, 0), (w_dn, 0), row(norm_ffn_post, 0),
                  row(norm_mix_pre, 1), w_qkv, b_qkv]
    out_consts = [w_sout, b_sout, row(norm_mix_post, 1), row(norm_ffn_pre, 1), (w_up, 1), (w_dn, 1), row(norm_ffn_post, 1)]
    in0_widths = [GLA_DK, GLA_DK, GLA_DV, GLA_DV, GLA_DK]
    mid_widths = [D_MODEL, SWA_Q, SWA_KV, SWA_KV]

    tm = 512
    q, k, v, r, la = _tok_call(_in0_body, n_p, tm, [(xp, None)], in0_consts, in0_widths,
                               [F32, F32, BF16, F32, F32], "in0_prompt")
    o, s_fin_p = _gla_prompt(q, k, v, la, batch, seq)
    tabs = _rope_tables(jnp.arange(seq))
    tab_map = lambda i: (i % (seq // tm), 0)
    h2, q1, k1, v1 = _tok_call(_mid_body, n_p, tm,
                               [(o, None), (r, None), (xp, None)] + [(t, tab_map) for t in tabs],
                               mid_consts, mid_widths, [F32] * 4, "mid_prompt")
    attn = _swa_prompt(swa_sinks[0], q1, k1, v1, batch, seq)
    (y_p,) = _tok_call(_out_body, n_p, tm, [(attn, None), (h2, None)], out_consts, [D_MODEL], [F32], "out_prompt")
    wp = min(SWA_WINDOW, seq)
    tail = lambda t: t.reshape(batch, seq, SWA_KV)[:, seq - wp:].reshape(batch, wp, SWA_KV_HEADS, SWA_HEAD_DIM)
    k_tail, v_tail = tail(k1), tail(v1)

    ts = n_s
    qs, ks, vs, rs, las = _tok_call(_in0_body, n_s, ts, [(xs, None)], in0_consts, in0_widths, [F32] * 5, "in0_sample")
    o_s, s_new = _gla_sample(qs, ks, vs, las, state_gla[0])
    tabs_s = _rope_tables(jnp.full((n_s,), past_len, jnp.int32))
    h2s, q1s, k1s, v1s = _tok_call(_mid_body, n_s, ts,
                                   [(o_s, None), (rs, None), (xs, None)] + [(t, None) for t in tabs_s],
                                   mid_consts, mid_widths, [F32] * 4, "mid_sample")
    win = cache_swa_k.shape[2]
    to_t = lambda c: jnp.transpose(c[0].reshape(dec_batch, win, SWA_KV), (0, 2, 1))
    from_t = lambda c: jnp.transpose(c, (0, 2, 1)).reshape(1, dec_batch, win, SWA_KV_HEADS, SWA_HEAD_DIM)
    attn_s, nk, nv = _swa_sample(swa_sinks[0][:, None], q1s.reshape(n_s, SWA_HEADS, SWA_HEAD_DIM), k1s, v1s,
                                 to_t(cache_swa_k), to_t(cache_swa_v))
    (y_s,) = _tok_call(_out_body, n_s, ts, [(attn_s.reshape(n_s, SWA_Q), None), (h2s, None)], out_consts,
                       [D_MODEL], [F32], "out_sample")

    return (y_p.reshape(batch, seq, D_MODEL), y_s.reshape(dec_batch, dec_seq, D_MODEL),
            s_fin_p[None], s_new[None], k_tail[None], v_tail[None], from_t(nk), from_t(nv))
```

```python
import functools

import jax
import jax.numpy as jnp
from jax import lax
from jax.experimental import pallas as pl
from jax.experimental.pallas import tpu as pltpu

F32 = jnp.float32
BF16 = jnp.bfloat16

D_MODEL = 1024
D_FF = 4 * D_MODEL
NORM_EPS = 1e-6

GLA_HEADS = 4
GLA_DK = D_MODEL // 2
GLA_DV = D_MODEL
GLA_DK_HEAD = GLA_DK // GLA_HEADS
GLA_DV_HEAD = GLA_DV // GLA_HEADS
GLA_GATE_RANK = 16
GLA_TAU = 16.0
GLA_CHUNK = 64
GLA_MAIN = 2 * GLA_DK + 2 * GLA_DV

SWA_HEAD_DIM = 64
SWA_HEADS = D_MODEL // SWA_HEAD_DIM
SWA_KV_HEADS = 4
SWA_GROUP = SWA_HEADS // SWA_KV_HEADS
SWA_WINDOW = 128
SWA_Q = SWA_HEADS * SWA_HEAD_DIM
SWA_KV = SWA_KV_HEADS * SWA_HEAD_DIM
SWA_QKV = SWA_Q + 2 * SWA_KV
ROPE_THETA = 500000.0
ROPE_DIM = SWA_HEAD_DIM // 4
ROPE_HALF = ROPE_DIM // 2

LANES = 128
FFN_CHUNK = 512
TOKEN_TILE = 512
GLA_TILE = 256
SWA_TILE = 512
SEQ_TILE = 8
VMEM_LIMIT = 56 * 1024 * 1024
NEG_BIG = -1e30
LOG2E = 1.4426950408889634


def _mm(a, b):
    return jnp.dot(a, b, preferred_element_type=F32)


def _mm_nt(a, b):
    return lax.dot_general(a, b, (((1,), (1,)), ((), ())), preferred_element_type=F32)


def _mm_tn(a, b):
    return lax.dot_general(a, b, (((0,), (0,)), ((), ())), preferred_element_type=F32)


def _rms(x, g):
    ms = jnp.mean(x * x, axis=-1, keepdims=True)
    return x * lax.rsqrt(ms + NORM_EPS) * g


def _split3(x):
    hi = x.astype(BF16)
    r1 = x - hi.astype(F32)
    mid = r1.astype(BF16)
    lo = (r1 - mid.astype(F32)).astype(BF16)
    return hi, mid, lo


def _ffn(a_bf16, wup_ref, wdn_ref):
    acc = None
    for c in range(D_FF // FFN_CHUNK):
        cols = slice(c * FFN_CHUNK, (c + 1) * FFN_CHUNK)
        u = _mm(a_bf16, wup_ref[:, cols])
        u = jnp.square(jnp.maximum(u, 0.0)).astype(BF16)
        p = _mm(u, wdn_ref[cols, :])
        acc = p if acc is None else acc + p
    return acc


def _in0_body(x_ref, g_ref, w_ref, wg_ref, bg_ref, q_ref, k_ref, v_ref, r_ref, la_ref):
    a = _rms(x_ref[...], g_ref[...]).astype(BF16)
    q_ref[...] = _mm(a, w_ref[:, 0:GLA_DK]) * (GLA_DK_HEAD ** -0.5)
    k_ref[...] = _mm(a, w_ref[:, GLA_DK:2 * GLA_DK])
    for c in range(GLA_DV // 512):
        cols = slice(c * 512, (c + 1) * 512)
        v_ref[:, cols] = _mm(a, w_ref[:, 2 * GLA_DK + c * 512:2 * GLA_DK + (c + 1) * 512]).astype(v_ref.dtype)
        r_ref[:, cols] = _mm(a, w_ref[:, 2 * GLA_DK + GLA_DV + c * 512:2 * GLA_DK + GLA_DV + (c + 1) * 512])
    z = _mm(a, w_ref[:, GLA_MAIN:GLA_MAIN + LANES]).astype(BF16)
    zg = _mm(z, wg_ref[...]) + bg_ref[...]
    la_ref[...] = (jnp.minimum(zg, 0.0) - jnp.log1p(jnp.exp(-jnp.abs(zg)))) * (1.0 / GLA_TAU)


def _mid_body(o_ref, r_ref, h_ref, rc_ref, ra_ref, rb_ref,
              gh_ref, wo_ref, gpost_ref, gfpre_ref, wup_ref, wdn_ref, gfpost_ref,
              gpre1_ref, wqkv_ref, bqkv_ref,
              h2_ref, q1_ref, k1_ref, v1_ref):
    m = None
    for hh in range(GLA_HEADS):
        cols = slice(hh * GLA_DV_HEAD, (hh + 1) * GLA_DV_HEAD)
        on = _rms(o_ref[:, cols], gh_ref[...])
        r = r_ref[:, cols]
        u = (on * (r * (1.0 / (1.0 + jnp.exp(-r))))).astype(BF16)
        p = _mm(u, wo_ref[cols, :])
        m = p if m is None else m + p
    h1 = h_ref[...] + _rms(m, gpost_ref[...])
    f = _ffn(_rms(h1, gfpre_ref[...]).astype(BF16), wup_ref, wdn_ref)
    h2 = h1 + _rms(f, gfpost_ref[...])
    h2_ref[...] = h2
    a3 = _rms(h2, gpre1_ref[...]).astype(BF16)
    rc, ra, rb = rc_ref[...], ra_ref[...], rb_ref[...]
    wide = 2 * LANES
    for c2 in range((SWA_Q + SWA_KV) // wide):
        x2 = _mm(a3, wqkv_ref[:, c2 * wide:(c2 + 1) * wide]) + bqkv_ref[:, c2 * wide:(c2 + 1) * wide]
        for half in range(2):
            c = 2 * c2 + half
            x = x2[:, half * LANES:(half + 1) * LANES]
            y = x * rc + pltpu.roll(x, LANES - ROPE_HALF, axis=1) * ra + pltpu.roll(x, ROPE_HALF, axis=1) * rb
            if c < SWA_Q // LANES:
                q1_ref[:, c * LANES:(c + 1) * LANES] = y
            else:
                k1_ref[:, c * LANES - SWA_Q:(c + 1) * LANES - SWA_Q] = y
    v1_ref[...] = _mm(a3, wqkv_ref[:, SWA_Q + SWA_KV:SWA_QKV]) + bqkv_ref[:, SWA_Q + SWA_KV:SWA_QKV]


def _out_body(at_ref, h_ref, wo_ref, bo_ref, gpost_ref, gfpre_ref, wup_ref, wdn_ref, gfpost_ref, y_ref):
    m = _mm(at_ref[...].astype(BF16), wo_ref[...]) + bo_ref[...]
    h1 = h_ref[...] + _rms(m, gpost_ref[...])
    f = _ffn(_rms(h1, gfpre_ref[...]).astype(BF16), wup_ref, wdn_ref)
    y_ref[...] = h1 + _rms(f, gfpost_ref[...])


def _tok_call(body, groups, const_inputs, name):
    in_specs, args, out_specs, out_shape = [], [], [], []
    ranges, start = [], 0
    for n_rows, tm, row_inputs, out_widths, out_dtypes in groups:
        assert n_rows % tm == 0
        count = n_rows // tm
        local = lambda i, start=start, count=count: jnp.clip(i - start, 0, count - 1)
        for arr, imap in row_inputs:
            imap = imap if imap is not None else (lambda t: (t, 0))
            in_specs.append(pl.BlockSpec((tm, arr.shape[1]), lambda i, imap=imap, local=local: imap(local(i))))
            args.append(arr)
        for w, dt in zip(out_widths, out_dtypes):
            out_specs.append(pl.BlockSpec((tm, w), lambda i, local=local: (local(i), 0)))
            out_shape.append(jax.ShapeDtypeStruct((n_rows, w), dt))
        ranges.append((start, count, len(row_inputs), len(out_widths)))
        start += count
    n_row_refs = len(in_specs)
    n_const = len(const_inputs)

    def kern(*refs):
        row_refs, const_refs, out_refs = refs[:n_row_refs], refs[n_row_refs:n_row_refs + n_const], refs[n_row_refs + n_const:]
        i = pl.program_id(0)
        r0 = o0 = 0
        for first, count, n_in, n_out in ranges:
            ins, outs = row_refs[r0:r0 + n_in], out_refs[o0:o0 + n_out]
            r0, o0 = r0 + n_in, o0 + n_out

            @pl.when((i >= first) & (i < first + count))
            def _(ins=ins, outs=outs):
                body(*ins, *const_refs, *outs)

    for entry in const_inputs:
        if isinstance(entry, tuple):
            arr, layer = entry
            spec = pl.BlockSpec((None,) + arr.shape[1:], lambda i, layer=layer: (layer, 0, 0),
                                pipeline_mode=pl.Buffered(1))
        else:
            arr = entry
            spec = pl.BlockSpec(arr.shape, lambda i: (0, 0), pipeline_mode=pl.Buffered(1))
        in_specs.append(spec)
        args.append(arr)
    outs = pl.pallas_call(
        kern,
        grid=(start,),
        in_specs=in_specs,
        out_specs=out_specs,
        out_shape=out_shape,
        compiler_params=pltpu.CompilerParams(dimension_semantics=("arbitrary",), vmem_limit_bytes=VMEM_LIMIT),
        name=name,
    )(*args)
    grouped, o0 = [], 0
    for _, _, _, n_out in ranges:
        grouped.append(list(outs[o0:o0 + n_out]))
        o0 += n_out
    return grouped


def _gla_prompt_body(q_ref, k_ref, v_ref, la_ref, o_ref, sfin_ref, st_ref):
    t = pl.program_id(1)
    tg = q_ref.shape[0]
    c_len = GLA_CHUNK

    @pl.when(t == 0)
    def _():
        st_ref[...] = jnp.zeros_like(st_ref)

    n_chunks = tg // c_len
    chunk_rows = [slice(ci * c_len, (ci + 1) * c_len) for ci in range(n_chunks)]
    row = lax.broadcasted_iota(jnp.int32, (tg, tg), 0)
    col = lax.broadcasted_iota(jnp.int32, (tg, tg), 1)
    lower_b = (row // c_len == col // c_len) & (col <= row)
    lower = jnp.where(lower_b, 1.0, 0.0).astype(BF16)
    hi, mid, lo = _split3(la_ref[...])
    cum = _mm(lower, hi) + _mm(lower, mid) + _mm(lower, lo)
    tot = jnp.concatenate([jnp.broadcast_to(cum[r.stop - 1:r.stop, :], (c_len, cum.shape[1])) for r in chunk_rows], axis=0)
    k_all = k_ref[...]
    qd = (q_ref[...] * jnp.exp(cum)).astype(BF16)
    ki = (k_all * jnp.exp(-cum)).astype(BF16)
    ke = k_all * jnp.exp(tot - cum)
    lane_chunk = lax.broadcasted_iota(jnp.int32, (GLA_DK_HEAD, tg), 1) // c_len

    att, kv = [], []
    for h in range(GLA_HEADS):
        kc = slice(h * GLA_DK_HEAD, (h + 1) * GLA_DK_HEAD)
        att.append(jnp.where(lower_b, _mm_nt(qd[:, kc], ki[:, kc]), 0.0).astype(BF16))
        ke_t = ke[:, kc].T
        v_h = v_ref[:, h * GLA_DV_HEAD:(h + 1) * GLA_DV_HEAD]
        kv.append([_mm(jnp.where(lane_chunk == ci, ke_t, 0.0).astype(BF16), v_h) for ci in range(n_chunks)])

    s_before = []
    for h in range(GLA_HEADS):
        kc = slice(h * GLA_DK_HEAD, (h + 1) * GLA_DK_HEAD)
        st = st_ref[h]
        starts = []
        for ci, r in enumerate(chunk_rows):
            starts.append(st.astype(BF16))
            e_col = jnp.exp(cum[r.stop - 8:r.stop, kc]).T[:, 7:8]
            st = e_col * st + kv[h][ci]
        st_ref[h] = st
        s_before.append(starts)

    for h in range(GLA_HEADS):
        kc = slice(h * GLA_DK_HEAD, (h + 1) * GLA_DK_HEAD)
        vc = slice(h * GLA_DV_HEAD, (h + 1) * GLA_DV_HEAD)
        o_intra = _mm(att[h], v_ref[:, vc])
        for ci, r in enumerate(chunk_rows):
            o_ref[r, vc] = o_intra[r] + _mm(qd[r, kc], s_before[h][ci])

    @pl.when(t == pl.num_programs(1) - 1)
    def _():
        sfin_ref[0] = st_ref[...]


def _gla_prompt(q, k, v, la, batch, seq):
    tg = GLA_TILE
    nt = seq // tg
    qk_spec = pl.BlockSpec((tg, GLA_DK), lambda b, t: (b * nt + t, 0))
    v_spec = pl.BlockSpec((tg, GLA_DV), lambda b, t: (b * nt + t, 0))
    st_shape = (GLA_HEADS, GLA_DK_HEAD, GLA_DV_HEAD)
    return pl.pallas_call(
        _gla_prompt_body,
        grid=(batch, nt),
        in_specs=[qk_spec, qk_spec, v_spec, qk_spec],
        out_specs=[v_spec, pl.BlockSpec((1,) + st_shape, lambda b, t: (b, 0, 0, 0))],
        out_shape=[jax.ShapeDtypeStruct((batch * seq, GLA_DV), F32),
                   jax.ShapeDtypeStruct((batch,) + st_shape, F32)],
        scratch_shapes=[pltpu.VMEM(st_shape, F32)],
        compiler_params=pltpu.CompilerParams(dimension_semantics=("arbitrary", "arbitrary"), vmem_limit_bytes=VMEM_LIMIT),
        name="gla_prompt",
    )(q, k, v, la)


def _gla_sample_body(q_ref, k_ref, v_ref, la_ref, s_ref, o_ref, sn_ref):
    bt = q_ref.shape[0]
    for h in range(GLA_HEADS):
        kc = slice(h * GLA_DK_HEAD, (h + 1) * GLA_DK_HEAD)
        vc = slice(h * GLA_DV_HEAD, (h + 1) * GLA_DV_HEAD)
        a_t = jnp.exp(la_ref[:, kc]).T
        k_t = k_ref[:, kc].T
        q_t = q_ref[:, kc].T
        for j in range(bt):
            s_new = a_t[:, j:j + 1] * s_ref[j, h] + k_t[:, j:j + 1] * v_ref[j:j + 1, vc]
            sn_ref[j, h] = s_new
            o_ref[j:j + 1, vc] = jnp.sum(q_t[:, j:j + 1] * s_new, axis=0, keepdims=True)


def _gla_sample(q, k, v, la, state):
    bt = SEQ_TILE
    nb = q.shape[0]
    row = lambda w: pl.BlockSpec((bt, w), lambda i: (i, 0))
    st_spec = pl.BlockSpec((bt, GLA_HEADS, GLA_DK_HEAD, GLA_DV_HEAD), lambda i: (i, 0, 0, 0))
    return pl.pallas_call(
        _gla_sample_body,
        grid=(nb // bt,),
        in_specs=[row(GLA_DK), row(GLA_DK), row(GLA_DV), row(GLA_DK), st_spec],
        out_specs=[row(GLA_DV), st_spec],
        out_shape=[jax.ShapeDtypeStruct((nb, GLA_DV), F32), jax.ShapeDtypeStruct(state.shape, F32)],
        compiler_params=pltpu.CompilerParams(dimension_semantics=("arbitrary",), vmem_limit_bytes=VMEM_LIMIT),
        name="gla_sample",
    )(q, k, v, la, state)


def _swa_attend_tile(sink_ref, q_ref, k_full, v_full, has_prev, o_ref):
    w = SWA_WINDOW
    hd = SWA_HEAD_DIM
    tq = q_ref.shape[0]
    nkv = k_full.shape[0]
    lane_q = lax.broadcasted_iota(jnp.int32, (w, LANES), 1) < hd
    lane_kv = lax.broadcasted_iota(jnp.int32, (nkv, LANES), 1) < hd
    i = lax.broadcasted_iota(jnp.int32, (w, 2 * w), 0)
    j = lax.broadcasted_iota(jnp.int32, (w, 2 * w), 1)
    band = jnp.where(j < w, jnp.where(j >= i, 1, 0), jnp.where(j - w <= i, 1, 0))
    band_first = jnp.where(j < w, has_prev, 1) * band
    lane_2w = lax.broadcasted_iota(jnp.int32, (2 * w, LANES), 1) < hd
    ones_lo = jnp.where(lane_2w, 1.0, 0.0).astype(BF16)
    ones_hi = jnp.where(lane_2w, 0.0, 1.0).astype(BF16)
    c2 = (hd ** -0.5) * LOG2E

    k_prep, v_prep = [], []
    for p in range(SWA_KV // LANES):
        cols = slice(p * LANES, (p + 1) * LANES)
        k_p, v_p = k_full[:, cols], v_full[:, cols]
        k_prep.append((k_p.astype(BF16), pltpu.roll(k_p, hd, axis=1).astype(BF16)))
        v_r = pltpu.roll(v_p, hd, axis=1)
        v_prep.append(((jnp.where(lane_kv, v_p, 0.0).astype(BF16), jnp.where(lane_kv, 0.0, v_r).astype(BF16)),
                       (jnp.where(lane_kv, v_r, 0.0).astype(BF16), jnp.where(lane_kv, 0.0, v_p).astype(BF16))))

    def softmax_part(s, hh, mask):
        s2 = jnp.where(mask, s, NEG_BIG)
        sk2 = jnp.full((w, 1), sink_ref[hh], F32) * LOG2E
        m2 = jnp.maximum(jnp.max(s2, axis=-1, keepdims=True), sk2)
        return jnp.exp2(s2 - m2).astype(BF16), sk2 - m2

    for b in range(tq // w):
        rows = slice(b * w, (b + 1) * w)
        krows = slice(b * w, (b + 2) * w)
        mask = (band_first if b == 0 else band) > 0
        for p in range(SWA_KV // LANES):
            q_lo, q_hi = [], []
            for x in range(4):
                q_c = q_ref[rows, (4 * p + x) * LANES:(4 * p + x + 1) * LANES] * c2
                q_lo.append(jnp.where(lane_q, q_c, 0.0).astype(BF16))
                q_hi.append(jnp.where(lane_q, 0.0, q_c).astype(BF16))
            s_self = _mm_nt(jnp.concatenate([q_lo[0], q_lo[1], q_hi[2], q_hi[3]], axis=0), k_prep[p][0][krows])
            s_roll = _mm_nt(jnp.concatenate([q_hi[0], q_hi[1], q_lo[2], q_lo[3]], axis=0), k_prep[p][1][krows])
            for x in range(4):
                c = 4 * p + x
                gh = x // 2
                xr = slice(x * w, (x + 1) * w)
                s_lo, s_hi = (s_self[xr], s_roll[xr]) if gh == 0 else (s_roll[xr], s_self[xr])
                p_lo, d_lo = softmax_part(s_lo, 2 * c, mask)
                p_hi, d_hi = softmax_part(s_hi, 2 * c + 1, mask)
                v_lo, v_hi = v_prep[p][gh]
                rhs = jnp.concatenate([jnp.concatenate([v_lo[krows], ones_lo], axis=1),
                                       jnp.concatenate([v_hi[krows], ones_hi], axis=1)], axis=0)
                ext = _mm(jnp.concatenate([p_lo, p_hi], axis=1), rhs)
                den = ext[:, LANES:] + jnp.exp2(jnp.where(lane_q, d_lo, d_hi))
                o_ref[rows, c * LANES:(c + 1) * LANES] = ext[:, :LANES] / den


def _swa_prompt_body(sink_ref, q_ref, kc_ref, kp_ref, vc_ref, vp_ref, o_ref):
    has_prev = jnp.minimum(pl.program_id(1), 1)
    k_full = jnp.concatenate([kp_ref[...], kc_ref[...]], axis=0)
    v_full = jnp.concatenate([vp_ref[...], vc_ref[...]], axis=0)
    _swa_attend_tile(sink_ref, q_ref, k_full, v_full, has_prev, o_ref)


def _swa_prompt(sinks, q, k, v, batch, seq):
    tq = SWA_TILE
    w = SWA_WINDOW
    nt = seq // tq
    cur = lambda width: pl.BlockSpec((tq, width), lambda b, t: (b * nt + t, 0))
    prev = lambda width: pl.BlockSpec((w, width), lambda b, t: (b * (seq // w) + jnp.maximum(t * (tq // w) - 1, 0), 0))
    return pl.pallas_call(
        _swa_prompt_body,
        grid=(batch, nt),
        in_specs=[pl.BlockSpec(memory_space=pltpu.SMEM), cur(SWA_Q), cur(SWA_KV), prev(SWA_KV), cur(SWA_KV), prev(SWA_KV)],
        out_specs=cur(SWA_Q),
        out_shape=jax.ShapeDtypeStruct((batch * seq, SWA_Q), F32),
        compiler_params=pltpu.CompilerParams(dimension_semantics=("arbitrary", "arbitrary"), vmem_limit_bytes=VMEM_LIMIT),
        name="swa_prompt",
    )(sinks, q, k, k, v, v)


def _swa_sample_body(sk_ref, q_ref, kn_ref, vn_ref, ck_ref, cv_ref, o_ref, nk_ref, nv_ref):
    bt = q_ref.shape[0]
    w = ck_ref.shape[2]
    hd = SWA_HEAD_DIM
    hgroup = lax.broadcasted_iota(jnp.int32, (SWA_HEADS, 1), 0) // SWA_GROUP
    newest = lax.broadcasted_iota(jnp.int32, (SWA_KV, w), 1) == w - 1
    kn_t = kn_ref[...].T
    vn_t = vn_ref[...].T
    scale = hd ** -0.5
    sk = sk_ref[...]
    groups = [slice(g * hd, (g + 1) * hd) for g in range(SWA_KV_HEADS)]

    def per_head(pieces):
        out = pieces[0]
        for g in range(1, SWA_KV_HEADS):
            out = jnp.where(hgroup == g, pieces[g], out)
        return out

    for j in range(bt):
        nk_ref[j] = jnp.where(newest, kn_t[:, j:j + 1], pltpu.roll(ck_ref[j], w - 1, axis=1))
        nv_ref[j] = jnp.where(newest, vn_t[:, j:j + 1], pltpu.roll(cv_ref[j], w - 1, axis=1))

    s_old, s_new, v_sel = [], [], []
    for j in range(bt):
        q = q_ref[j]
        qb = q.astype(BF16)
        s_old.append(per_head([_mm(qb, ck_ref[j, rows, :].astype(BF16)) for rows in groups]))
        k_sel = per_head([kn_ref[j:j + 1, cols] for cols in groups])
        v_sel.append(per_head([vn_ref[j:j + 1, cols] for cols in groups]))
        s_new.append(jnp.sum(q * k_sel, axis=-1, keepdims=True))
    s_old = jnp.stack(s_old, axis=0) * scale
    s_new = jnp.stack(s_new, axis=0) * scale
    m = jnp.maximum(jnp.maximum(jnp.max(s_old, axis=-1, keepdims=True), s_new), sk)
    p_old = jnp.exp(s_old - m)
    p_new = jnp.exp(s_new - m)
    inv = 1.0 / (jnp.sum(p_old, axis=-1, keepdims=True) + p_new + jnp.exp(sk - m))
    p_old = p_old.astype(BF16)
    for j in range(bt):
        o = per_head([_mm_nt(p_old[j], cv_ref[j, rows, :].astype(BF16)) for rows in groups])
        o_ref[j] = (o + p_new[j] * v_sel[j]) * inv[j]


def _swa_sample(sinks, q3, k_new, v_new, cache_k, cache_v):
    bt = SEQ_TILE
    nb, _, w = cache_k.shape
    assert w == LANES
    row = lambda width: pl.BlockSpec((bt, width), lambda i: (i, 0))
    q_spec = pl.BlockSpec((bt, SWA_HEADS, SWA_HEAD_DIM), lambda i: (i, 0, 0))
    c_spec = pl.BlockSpec((bt, SWA_KV, w), lambda i: (i, 0, 0))
    return pl.pallas_call(
        _swa_sample_body,
        grid=(nb // bt,),
        in_specs=[pl.BlockSpec((SWA_HEADS, 1), lambda i: (0, 0)), q_spec, row(SWA_KV), row(SWA_KV), c_spec, c_spec],
        out_specs=[q_spec, c_spec, c_spec],
        out_shape=[jax.ShapeDtypeStruct((nb, SWA_HEADS, SWA_HEAD_DIM), F32),
                   jax.ShapeDtypeStruct(cache_k.shape, F32), jax.ShapeDtypeStruct(cache_v.shape, F32)],
        compiler_params=pltpu.CompilerParams(dimension_semantics=("arbitrary",)),
        name="swa_sample",
    )(sinks, q3, k_new, v_new, cache_k, cache_v)


def _rope_tables(pos):
    inv = jnp.power(ROPE_THETA, -jnp.arange(ROPE_HALF, dtype=F32) * 2.0 / ROPE_DIM)
    ang = pos.astype(F32)[:, None] * inv[None, :]
    cos, sin = jnp.cos(ang), jnp.sin(ang)
    n = pos.shape[0]
    rest = SWA_HEAD_DIM - ROPE_DIM
    rc = jnp.concatenate([cos, cos, jnp.ones((n, rest), F32)], axis=-1)
    ra = jnp.concatenate([-sin, jnp.zeros((n, ROPE_HALF + rest), F32)], axis=-1)
    rb = jnp.concatenate([jnp.zeros((n, ROPE_HALF), F32), sin, jnp.zeros((n, rest), F32)], axis=-1)
    reps = LANES // SWA_HEAD_DIM
    return tuple(jnp.tile(t, (1, reps)) for t in (rc, ra, rb))


def kernel(x_prompt, x_sample, state_gla, cache_swa_k, cache_swa_v, gla_w_in, gla_w_gate2, gla_b_gate, gla_g_head, gla_w_out, swa_w_qkv, swa_b_qkv, swa_sinks, swa_w_out, swa_b_out, norm_mix_pre, norm_mix_post, norm_ffn_pre, norm_ffn_post, ffn_w_up, ffn_w_down):
    batch, seq, _ = x_prompt.shape
    dec_batch, dec_seq, _ = x_sample.shape
    assert dec_seq == 1 and seq % SWA_WINDOW == 0
    past_len = seq
    n_p, n_s = batch * seq, dec_batch * dec_seq
    xp = x_prompt.reshape(n_p, D_MODEL)
    xs = x_sample.reshape(n_s, D_MODEL)

    w_in = jnp.pad(gla_w_in[0].astype(BF16), ((0, 0), (0, LANES - GLA_GATE_RANK)))
    w_g2 = jnp.pad(gla_w_gate2[0], ((0, LANES - GLA_GATE_RANK), (0, 0))).astype(BF16)
    b_g = gla_b_gate[0][None, :]
    g_head = gla_g_head[0][None, :]
    w_gout = gla_w_out[0].astype(BF16)
    w_qkv = swa_w_qkv[0].astype(BF16)
    b_qkv = swa_b_qkv[0][None, :]
    w_sout = swa_w_out[0].astype(BF16)
    b_sout = swa_b_out[0][None, :]
    w_up = ffn_w_up.astype(BF16)
    w_dn = ffn_w_down.astype(BF16)
    row = lambda t, i: t[i][None, :]

    in0_consts = [row(norm_mix_pre, 0), w_in, w_g2, b_g]
    mid_consts = [g_head, w_gout, row(norm_mix_post, 0), row(norm_ffn_pre, 0), (w_up, 0), (w_dn, 0), row(norm_ffn_post, 0),
                  row(norm_mix_pre, 1), w_qkv, b_qkv]
    out_consts = [w_sout, b_sout, row(norm_mix_post, 1), row(norm_ffn_pre, 1), (w_up, 1), (w_dn, 1), row(norm_ffn_post, 1)]
    in0_widths = [GLA_DK, GLA_DK, GLA_DV, GLA_DV, GLA_DK]
    mid_widths = [D_MODEL, SWA_Q, SWA_KV, SWA_KV]

    tm, ts = TOKEN_TILE, n_s
    (q, k, v, r, la), (qs, ks, vs, rs, las) = _tok_call(
        _in0_body,
        [(n_p, tm, [(xp, None)], in0_widths, [F32, F32, BF16, F32, F32]),
         (n_s, ts, [(xs, None)], in0_widths, [F32] * 5)],
        in0_consts, "in0")
    o, s_fin_p = _gla_prompt(q, k, v, la, batch, seq)
    o_s, s_new = _gla_sample(qs, ks, vs, las, state_gla[0])
    tabs = _rope_tables(jnp.arange(seq))
    tabs_s = _rope_tables(jnp.full((n_s,), past_len, jnp.int32))
    tab_map = lambda t: (t % (seq // tm), 0)
    (h2, q1, k1, v1), (h2s, q1s, k1s, v1s) = _tok_call(
        _mid_body,
        [(n_p, tm, [(o, None), (r, None), (xp, None)] + [(t, tab_map) for t in tabs], mid_widths, [F32] * 4),
         (n_s, ts, [(o_s, None), (rs, None), (xs, None)] + [(t, None) for t in tabs_s], mid_widths, [F32] * 4)],
        mid_consts, "mid")
    attn = _swa_prompt(swa_sinks[0], q1, k1, v1, batch, seq)
    win = cache_swa_k.shape[2]
    to_t = lambda c: jnp.transpose(c[0].reshape(dec_batch, win, SWA_KV), (0, 2, 1))
    from_t = lambda c: jnp.transpose(c, (0, 2, 1)).reshape(1, dec_batch, win, SWA_KV_HEADS, SWA_HEAD_DIM)
    attn_s, nk, nv = _swa_sample(swa_sinks[0][:, None], q1s.reshape(n_s, SWA_HEADS, SWA_HEAD_DIM), k1s, v1s,
                                 to_t(cache_swa_k), to_t(cache_swa_v))
    (y_p,), (y_s,) = _tok_call(
        _out_body,
        [(n_p, tm, [(attn, None), (h2, None)], [D_MODEL], [F32]),
         (n_s, ts, [(attn_s.reshape(n_s, SWA_Q), None), (h2s, None)], [D_MODEL], [F32])],
        out_consts, "out")
    wp = min(SWA_WINDOW, seq)
    tail = lambda t: t.reshape(batch, seq, SWA_KV)[:, seq - wp:].reshape(batch, wp, SWA_KV_HEADS, SWA_HEAD_DIM)
    k_tail, v_tail = tail(k1), tail(v1)

    return (y_p.reshape(batch, seq, D_MODEL), y_s.reshape(dec_batch, dec_seq, D_MODEL),
            s_fin_p[None], s_new[None], k_tail[None], v_tail[None], from_t(nk), from_t(nv))
```

```python
import functools

import jax
import jax.numpy as jnp
from jax import lax
from jax.experimental import pallas as pl
from jax.experimental.pallas import tpu as pltpu

F32 = jnp.float32
BF16 = jnp.bfloat16

D_MODEL = 1024
D_FF = 4 * D_MODEL
NORM_EPS = 1e-6

GLA_HEADS = 4
GLA_DK = D_MODEL // 2
GLA_DV = D_MODEL
GLA_DK_HEAD = GLA_DK // GLA_HEADS
GLA_DV_HEAD = GLA_DV // GLA_HEADS
GLA_GATE_RANK = 16
GLA_TAU = 16.0
GLA_CHUNK = 64
GLA_MAIN = 2 * GLA_DK + 2 * GLA_DV

SWA_HEAD_DIM = 64
SWA_HEADS = D_MODEL // SWA_HEAD_DIM
SWA_KV_HEADS = 4
SWA_GROUP = SWA_HEADS // SWA_KV_HEADS
SWA_WINDOW = 128
SWA_Q = SWA_HEADS * SWA_HEAD_DIM
SWA_KV = SWA_KV_HEADS * SWA_HEAD_DIM
SWA_QKV = SWA_Q + 2 * SWA_KV
ROPE_THETA = 500000.0
ROPE_DIM = SWA_HEAD_DIM // 4
ROPE_HALF = ROPE_DIM // 2

LANES = 128
FFN_CHUNK = 512
TOKEN_TILE = 512
GLA_TILE = 256
SWA_TILE = 512
SEQ_TILE = 8
VMEM_LIMIT = 56 * 1024 * 1024
NEG_BIG = -1e30
LOG2E = 1.4426950408889634


def _mm(a, b):
    return jnp.dot(a, b, preferred_element_type=F32)


def _mm_nt(a, b):
    return lax.dot_general(a, b, (((1,), (1,)), ((), ())), preferred_element_type=F32)


def _mm_tn(a, b):
    return lax.dot_general(a, b, (((0,), (0,)), ((), ())), preferred_element_type=F32)


def _rms(x, g):
    ms = jnp.mean(x * x, axis=-1, keepdims=True)
    return x * lax.rsqrt(ms + NORM_EPS) * g


def _split3(x):
    hi = x.astype(BF16)
    r1 = x - hi.astype(F32)
    mid = r1.astype(BF16)
    lo = (r1 - mid.astype(F32)).astype(BF16)
    return hi, mid, lo


def _ffn(a_bf16, wup_ref, wdn_ref, between=None):
    acc = None
    for c in range(D_FF // FFN_CHUNK):
        cols = slice(c * FFN_CHUNK, (c + 1) * FFN_CHUNK)
        u = _mm(a_bf16, wup_ref[:, cols])
        u = jnp.square(jnp.maximum(u, 0.0)).astype(BF16)
        p = _mm(u, wdn_ref[cols, :])
        acc = p if acc is None else acc + p
        if between is not None:
            between()
    return acc


def _in0_body(x_ref, g_ref, w_ref, wg_ref, bg_ref, q_ref, k_ref, v_ref, r_ref, la_ref):
    a = _rms(x_ref[...], g_ref[...]).astype(BF16)
    q_ref[...] = _mm(a, w_ref[:, 0:GLA_DK]) * (GLA_DK_HEAD ** -0.5)
    k_ref[...] = _mm(a, w_ref[:, GLA_DK:2 * GLA_DK])
    for c in range(GLA_DV // 512):
        cols = slice(c * 512, (c + 1) * 512)
        v_ref[:, cols] = _mm(a, w_ref[:, 2 * GLA_DK + c * 512:2 * GLA_DK + (c + 1) * 512]).astype(v_ref.dtype)
        r_ref[:, cols] = _mm(a, w_ref[:, 2 * GLA_DK + GLA_DV + c * 512:2 * GLA_DK + GLA_DV + (c + 1) * 512])
    z = _mm(a, w_ref[:, GLA_MAIN:GLA_MAIN + LANES]).astype(BF16)
    zg = _mm(z, wg_ref[...]) + bg_ref[...]
    la_ref[...] = (jnp.minimum(zg, 0.0) - jnp.log1p(jnp.exp(-jnp.abs(zg)))) * (1.0 / GLA_TAU)


def _mid_body(o_ref, r_ref, h_ref, rc_ref, ra_ref, rb_ref,
              gh_ref, wo_ref, gpost_ref, gfpre_ref, wup_ref, wdn_ref, gfpost_ref,
              gpre1_ref, wqkv_ref, bqkv_ref,
              h2_ref, q1_ref, k1_ref, v1_ref):
    m = None
    for hh in range(GLA_HEADS):
        cols = slice(hh * GLA_DV_HEAD, (hh + 1) * GLA_DV_HEAD)
        on = _rms(o_ref[:, cols], gh_ref[...])
        r = r_ref[:, cols]
        u = (on * (r * (1.0 / (1.0 + jnp.exp(-r))))).astype(BF16)
        p = _mm(u, wo_ref[cols, :])
        m = p if m is None else m + p
    h1 = h_ref[...] + _rms(m, gpost_ref[...])
    f = _ffn(_rms(h1, gfpre_ref[...]).astype(BF16), wup_ref, wdn_ref)
    h2 = h1 + _rms(f, gfpost_ref[...])
    h2_ref[...] = h2
    a3 = _rms(h2, gpre1_ref[...]).astype(BF16)
    rc, ra, rb = rc_ref[...], ra_ref[...], rb_ref[...]
    wide = 2 * LANES
    for c2 in range((SWA_Q + SWA_KV) // wide):
        x2 = _mm(a3, wqkv_ref[:, c2 * wide:(c2 + 1) * wide]) + bqkv_ref[:, c2 * wide:(c2 + 1) * wide]
        for half in range(2):
            c = 2 * c2 + half
            x = x2[:, half * LANES:(half + 1) * LANES]
            y = x * rc + pltpu.roll(x, LANES - ROPE_HALF, axis=1) * ra + pltpu.roll(x, ROPE_HALF, axis=1) * rb
            if c < SWA_Q // LANES:
                q1_ref[:, c * LANES:(c + 1) * LANES] = y
            else:
                k1_ref[:, c * LANES - SWA_Q:(c + 1) * LANES - SWA_Q] = y
    v1_ref[...] = _mm(a3, wqkv_ref[:, SWA_Q + SWA_KV:SWA_QKV]) + bqkv_ref[:, SWA_Q + SWA_KV:SWA_QKV]


def _tok_call(body, groups, const_inputs, name):
    in_specs, args, out_specs, out_shape = [], [], [], []
    ranges, start = [], 0
    for n_rows, tm, row_inputs, out_widths, out_dtypes in groups:
        assert n_rows % tm == 0
        count = n_rows // tm
        local = lambda i, start=start, count=count: jnp.clip(i - start, 0, count - 1)
        for arr, imap in row_inputs:
            imap = imap if imap is not None else (lambda t: (t, 0))
            in_specs.append(pl.BlockSpec((tm, arr.shape[1]), lambda i, imap=imap, local=local: imap(local(i))))
            args.append(arr)
        for w, dt in zip(out_widths, out_dtypes):
            out_specs.append(pl.BlockSpec((tm, w), lambda i, local=local: (local(i), 0)))
            out_shape.append(jax.ShapeDtypeStruct((n_rows, w), dt))
        ranges.append((start, count, len(row_inputs), len(out_widths)))
        start += count
    n_row_refs = len(in_specs)
    n_const = len(const_inputs)

    def kern(*refs):
        row_refs, const_refs, out_refs = refs[:n_row_refs], refs[n_row_refs:n_row_refs + n_const], refs[n_row_refs + n_const:]
        i = pl.program_id(0)
        r0 = o0 = 0
        for first, count, n_in, n_out in ranges:
            ins, outs = row_refs[r0:r0 + n_in], out_refs[o0:o0 + n_out]
            r0, o0 = r0 + n_in, o0 + n_out

            @pl.when((i >= first) & (i < first + count))
            def _(ins=ins, outs=outs):
                body(*ins, *const_refs, *outs)

    for entry in const_inputs:
        if isinstance(entry, tuple):
            arr, layer = entry
            spec = pl.BlockSpec((None,) + arr.shape[1:], lambda i, layer=layer: (layer, 0, 0),
                                pipeline_mode=pl.Buffered(1))
        else:
            arr = entry
            spec = pl.BlockSpec(arr.shape, lambda i: (0, 0), pipeline_mode=pl.Buffered(1))
        in_specs.append(spec)
        args.append(arr)
    outs = pl.pallas_call(
        kern,
        grid=(start,),
        in_specs=in_specs,
        out_specs=out_specs,
        out_shape=out_shape,
        compiler_params=pltpu.CompilerParams(dimension_semantics=("arbitrary",), vmem_limit_bytes=VMEM_LIMIT),
        name=name,
    )(*args)
    grouped, o0 = [], 0
    for _, _, _, n_out in ranges:
        grouped.append(list(outs[o0:o0 + n_out]))
        o0 += n_out
    return grouped


def _gla_prompt_body(q_ref, k_ref, v_ref, la_ref, o_ref, sfin_ref, st_ref):
    t = pl.program_id(1)
    tg = q_ref.shape[0]
    c_len = GLA_CHUNK

    @pl.when(t == 0)
    def _():
        st_ref[...] = jnp.zeros_like(st_ref)

    n_chunks = tg // c_len
    chunk_rows = [slice(ci * c_len, (ci + 1) * c_len) for ci in range(n_chunks)]
    row = lax.broadcasted_iota(jnp.int32, (tg, tg), 0)
    col = lax.broadcasted_iota(jnp.int32, (tg, tg), 1)
    lower_b = (row // c_len == col // c_len) & (col <= row)
    lower = jnp.where(lower_b, 1.0, 0.0).astype(BF16)
    hi, mid, lo = _split3(la_ref[...])
    cum = _mm(lower, hi) + _mm(lower, mid) + _mm(lower, lo)
    tot = jnp.concatenate([jnp.broadcast_to(cum[r.stop - 1:r.stop, :], (c_len, cum.shape[1])) for r in chunk_rows], axis=0)
    k_all = k_ref[...]
    qd = (q_ref[...] * jnp.exp(cum)).astype(BF16)
    ki = (k_all * jnp.exp(-cum)).astype(BF16)
    ke = k_all * jnp.exp(tot - cum)
    lane_chunk = lax.broadcasted_iota(jnp.int32, (GLA_DK_HEAD, tg), 1) // c_len

    att, kv = [], []
    for h in range(GLA_HEADS):
        kc = slice(h * GLA_DK_HEAD, (h + 1) * GLA_DK_HEAD)
        att.append(jnp.where(lower_b, _mm_nt(qd[:, kc], ki[:, kc]), 0.0).astype(BF16))
        ke_t = ke[:, kc].T
        v_h = v_ref[:, h * GLA_DV_HEAD:(h + 1) * GLA_DV_HEAD]
        kv.append([_mm(jnp.where(lane_chunk == ci, ke_t, 0.0).astype(BF16), v_h) for ci in range(n_chunks)])

    s_before = []
    for h in range(GLA_HEADS):
        kc = slice(h * GLA_DK_HEAD, (h + 1) * GLA_DK_HEAD)
        st = st_ref[h]
        starts = []
        for ci, r in enumerate(chunk_rows):
            starts.append(st.astype(BF16))
            e_col = jnp.exp(cum[r.stop - 8:r.stop, kc]).T[:, 7:8]
            st = e_col * st + kv[h][ci]
        st_ref[h] = st
        s_before.append(starts)

    for h in range(GLA_HEADS):
        kc = slice(h * GLA_DK_HEAD, (h + 1) * GLA_DK_HEAD)
        vc = slice(h * GLA_DV_HEAD, (h + 1) * GLA_DV_HEAD)
        o_intra = _mm(att[h], v_ref[:, vc])
        for ci, r in enumerate(chunk_rows):
            o_ref[r, vc] = o_intra[r] + _mm(qd[r, kc], s_before[h][ci])

    @pl.when(t == pl.num_programs(1) - 1)
    def _():
        sfin_ref[0] = st_ref[...]


def _gla_prompt(q, k, v, la, batch, seq):
    tg = GLA_TILE
    nt = seq // tg
    qk_spec = pl.BlockSpec((tg, GLA_DK), lambda b, t: (b * nt + t, 0))
    v_spec = pl.BlockSpec((tg, GLA_DV), lambda b, t: (b * nt + t, 0))
    st_shape = (GLA_HEADS, GLA_DK_HEAD, GLA_DV_HEAD)
    return pl.pallas_call(
        _gla_prompt_body,
        grid=(batch, nt),
        in_specs=[qk_spec, qk_spec, v_spec, qk_spec],
        out_specs=[v_spec, pl.BlockSpec((1,) + st_shape, lambda b, t: (b, 0, 0, 0))],
        out_shape=[jax.ShapeDtypeStruct((batch * seq, GLA_DV), F32),
                   jax.ShapeDtypeStruct((batch,) + st_shape, F32)],
        scratch_shapes=[pltpu.VMEM(st_shape, F32)],
        compiler_params=pltpu.CompilerParams(dimension_semantics=("arbitrary", "arbitrary"), vmem_limit_bytes=VMEM_LIMIT),
        name="gla_prompt",
    )(q, k, v, la)


def _gla_sample_body(q_ref, k_ref, v_ref, la_ref, s_ref, o_ref, sn_ref):
    bt = q_ref.shape[0]
    for h in range(GLA_HEADS):
        kc = slice(h * GLA_DK_HEAD, (h + 1) * GLA_DK_HEAD)
        vc = slice(h * GLA_DV_HEAD, (h + 1) * GLA_DV_HEAD)
        a_t = jnp.exp(la_ref[:, kc]).T
        k_t = k_ref[:, kc].T
        q_t = q_ref[:, kc].T
        for j in range(bt):
            s_new = a_t[:, j:j + 1] * s_ref[j, h] + k_t[:, j:j + 1] * v_ref[j:j + 1, vc]
            sn_ref[j, h] = s_new
            o_ref[j:j + 1, vc] = jnp.sum(q_t[:, j:j + 1] * s_new, axis=0, keepdims=True)


def _gla_sample(q, k, v, la, state):
    bt = SEQ_TILE
    nb = q.shape[0]
    row = lambda w: pl.BlockSpec((bt, w), lambda i: (i, 0))
    st_spec = pl.BlockSpec((bt, GLA_HEADS, GLA_DK_HEAD, GLA_DV_HEAD), lambda i: (i, 0, 0, 0))
    return pl.pallas_call(
        _gla_sample_body,
        grid=(nb // bt,),
        in_specs=[row(GLA_DK), row(GLA_DK), row(GLA_DV), row(GLA_DK), st_spec],
        out_specs=[row(GLA_DV), st_spec],
        out_shape=[jax.ShapeDtypeStruct((nb, GLA_DV), F32), jax.ShapeDtypeStruct(state.shape, F32)],
        compiler_params=pltpu.CompilerParams(dimension_semantics=("arbitrary",), vmem_limit_bytes=VMEM_LIMIT),
        name="gla_sample",
    )(q, k, v, la, state)


def _swa_attend_units(sink_ref, q_ref, k_full, v_full, has_prev, o_ref):
    w = SWA_WINDOW
    hd = SWA_HEAD_DIM
    tq = q_ref.shape[0]
    nkv = k_full.shape[0]
    lane_q = lax.broadcasted_iota(jnp.int32, (w, LANES), 1) < hd
    lane_kv = lax.broadcasted_iota(jnp.int32, (nkv, LANES), 1) < hd
    i = lax.broadcasted_iota(jnp.int32, (w, 2 * w), 0)
    j = lax.broadcasted_iota(jnp.int32, (w, 2 * w), 1)
    band = jnp.where(j < w, jnp.where(j >= i, 1, 0), jnp.where(j - w <= i, 1, 0))
    band_first = jnp.where(j < w, has_prev, 1) * band
    lane_2w = lax.broadcasted_iota(jnp.int32, (2 * w, LANES), 1) < hd
    ones_lo = jnp.where(lane_2w, 1.0, 0.0).astype(BF16)
    ones_hi = jnp.where(lane_2w, 0.0, 1.0).astype(BF16)
    c2 = (hd ** -0.5) * LOG2E

    k_prep, v_prep = [], []
    for p in range(SWA_KV // LANES):
        cols = slice(p * LANES, (p + 1) * LANES)
        k_p, v_p = k_full[:, cols], v_full[:, cols]
        k_prep.append((k_p.astype(BF16), pltpu.roll(k_p, hd, axis=1).astype(BF16)))
        v_r = pltpu.roll(v_p, hd, axis=1)
        v_prep.append(((jnp.where(lane_kv, v_p, 0.0).astype(BF16), jnp.where(lane_kv, 0.0, v_r).astype(BF16)),
                       (jnp.where(lane_kv, v_r, 0.0).astype(BF16), jnp.where(lane_kv, 0.0, v_p).astype(BF16))))

    def softmax_part(s, hh, mask):
        s2 = jnp.where(mask, s, NEG_BIG)
        sk2 = jnp.full((w, 1), sink_ref[hh], F32) * LOG2E
        m2 = jnp.maximum(jnp.max(s2, axis=-1, keepdims=True), sk2)
        return jnp.exp2(s2 - m2).astype(BF16), sk2 - m2

    for b in range(tq // w):
        rows = slice(b * w, (b + 1) * w)
        krows = slice(b * w, (b + 2) * w)
        mask = (band_first if b == 0 else band) > 0
        for p in range(SWA_KV // LANES):
            q_lo, q_hi = [], []
            for x in range(4):
                q_c = q_ref[rows, (4 * p + x) * LANES:(4 * p + x + 1) * LANES] * c2
                q_lo.append(jnp.where(lane_q, q_c, 0.0).astype(BF16))
                q_hi.append(jnp.where(lane_q, 0.0, q_c).astype(BF16))
            s_self = _mm_nt(jnp.concatenate([q_lo[0], q_lo[1], q_hi[2], q_hi[3]], axis=0), k_prep[p][0][krows])
            s_roll = _mm_nt(jnp.concatenate([q_hi[0], q_hi[1], q_lo[2], q_lo[3]], axis=0), k_prep[p][1][krows])
            for x in range(4):
                c = 4 * p + x
                gh = x // 2
                xr = slice(x * w, (x + 1) * w)
                s_lo, s_hi = (s_self[xr], s_roll[xr]) if gh == 0 else (s_roll[xr], s_self[xr])
                p_lo, d_lo = softmax_part(s_lo, 2 * c, mask)
                p_hi, d_hi = softmax_part(s_hi, 2 * c + 1, mask)
                v_lo, v_hi = v_prep[p][gh]
                rhs = jnp.concatenate([jnp.concatenate([v_lo[krows], ones_lo], axis=1),
                                       jnp.concatenate([v_hi[krows], ones_hi], axis=1)], axis=0)
                ext = _mm(jnp.concatenate([p_lo, p_hi], axis=1), rhs)
                den = ext[:, LANES:] + jnp.exp2(jnp.where(lane_q, d_lo, d_hi))
                o_ref[rows, c * LANES:(c + 1) * LANES] = (ext[:, :LANES] / den).astype(o_ref.dtype)
            yield


def _out_stage(at_bf16, h, wo_ref, bo_ref, gpost_ref, gfpre_ref, wup_ref, wdn_ref, gfpost_ref, between=None):
    m = _mm(at_bf16, wo_ref[...]) + bo_ref[...]
    h1 = h + _rms(m, gpost_ref[...])
    f = _ffn(_rms(h1, gfpre_ref[...]).astype(BF16), wup_ref, wdn_ref, between)
    return h1 + _rms(f, gfpost_ref[...])


def _swa_out_body(batch, nt, sink_ref, q_ref, kc_ref, kp_ref, vc_ref, vp_ref, h_ref, ats_ref, hs_ref,
                  wo_ref, bo_ref, gpost_ref, gfpre_ref, wup_ref, wdn_ref, gfpost_ref,
                  y_ref, ys_ref, attn_scr):
    i = pl.program_id(0)
    per_batch = nt + 1
    t = lax.rem(i, per_batch)
    is_prompt = i < batch * per_batch
    consts = (wo_ref, bo_ref, gpost_ref, gfpre_ref, wup_ref, wdn_ref, gfpost_ref)

    @pl.when(i == 0)
    def _():
        attn_scr[...] = jnp.zeros_like(attn_scr)

    @pl.when(is_prompt & (t < nt))
    def _():
        at_prev = attn_scr[...]
        k_full = jnp.concatenate([kp_ref[...], kc_ref[...]], axis=0)
        v_full = jnp.concatenate([vp_ref[...], vc_ref[...]], axis=0)
        units = _swa_attend_units(sink_ref, q_ref, k_full, v_full, jnp.minimum(t, 1), attn_scr)
        y_ref[...] = _out_stage(at_prev, h_ref[...], *consts, between=lambda: next(units, None))
        for _ in units:
            pass

    @pl.when(is_prompt & (t == nt))
    def _():
        y_ref[...] = _out_stage(attn_scr[...], h_ref[...], *consts)

    @pl.when(i == batch * per_batch)
    def _():
        ys_ref[...] = _out_stage(ats_ref[...].astype(BF16), hs_ref[...], *consts)


def _swa_out(sinks, q, k, v, h, attn_s, h_s, consts, batch, seq):
    tq = SWA_TILE
    w = SWA_WINDOW
    nt = seq // tq
    per_batch = nt + 1
    n_s = h_s.shape[0]

    def pos(i):
        ii = jnp.minimum(i, batch * per_batch - 1)
        return ii // per_batch, lax.rem(ii, per_batch)

    def att_tile(i):
        b, t = pos(i)
        return (b * nt + jnp.minimum(t, nt - 1), 0)

    def prev_block(i):
        b, t = pos(i)
        return (b * (seq // w) + jnp.maximum(jnp.minimum(t, nt - 1) * (tq // w) - 1, 0), 0)

    def out_tile(i):
        b, t = pos(i)
        return (b * nt + jnp.maximum(t - 1, 0), 0)

    whole = lambda arr: pl.BlockSpec(arr.shape, lambda i: (0, 0))
    const_specs = []
    for entry in consts:
        if isinstance(entry, tuple):
            arr, layer = entry
            const_specs.append(pl.BlockSpec((None,) + arr.shape[1:], lambda i, layer=layer: (layer, 0, 0),
                                            pipeline_mode=pl.Buffered(1)))
        else:
            const_specs.append(pl.BlockSpec(entry.shape, lambda i: (0, 0), pipeline_mode=pl.Buffered(1)))
    const_args = [e[0] if isinstance(e, tuple) else e for e in consts]
    return pl.pallas_call(
        functools.partial(_swa_out_body, batch, nt),
        grid=(batch * per_batch + 1,),
        in_specs=[pl.BlockSpec(memory_space=pltpu.SMEM),
                  pl.BlockSpec((tq, SWA_Q), att_tile), pl.BlockSpec((tq, SWA_KV), att_tile),
                  pl.BlockSpec((w, SWA_KV), prev_block), pl.BlockSpec((tq, SWA_KV), att_tile),
                  pl.BlockSpec((w, SWA_KV), prev_block), pl.BlockSpec((tq, D_MODEL), out_tile),
                  whole(attn_s), whole(h_s)] + const_specs,
        out_specs=[pl.BlockSpec((tq, D_MODEL), out_tile), whole(h_s)],
        out_shape=[jax.ShapeDtypeStruct((batch * seq, D_MODEL), F32), jax.ShapeDtypeStruct((n_s, D_MODEL), F32)],
        scratch_shapes=[pltpu.VMEM((tq, SWA_Q), BF16)],
        compiler_params=pltpu.CompilerParams(dimension_semantics=("arbitrary",), vmem_limit_bytes=VMEM_LIMIT),
        name="swa_out",
    )(sinks, q, k, k, v, v, h, attn_s, h_s, *const_args)


def _swa_sample_body(sk_ref, q_ref, kn_ref, vn_ref, ck_ref, cv_ref, o_ref, nk_ref, nv_ref):
    bt = q_ref.shape[0]
    w = ck_ref.shape[2]
    hd = SWA_HEAD_DIM
    hgroup = lax.broadcasted_iota(jnp.int32, (SWA_HEADS, 1), 0) // SWA_GROUP
    newest = lax.broadcasted_iota(jnp.int32, (SWA_KV, w), 1) == w - 1
    kn_t = kn_ref[...].T
    vn_t = vn_ref[...].T
    scale = hd ** -0.5
    sk = sk_ref[...]
    groups = [slice(g * hd, (g + 1) * hd) for g in range(SWA_KV_HEADS)]

    def per_head(pieces):
        out = pieces[0]
        for g in range(1, SWA_KV_HEADS):
            out = jnp.where(hgroup == g, pieces[g], out)
        return out

    for j in range(bt):
        nk_ref[j] = jnp.where(newest, kn_t[:, j:j + 1], pltpu.roll(ck_ref[j], w - 1, axis=1))
        nv_ref[j] = jnp.where(newest, vn_t[:, j:j + 1], pltpu.roll(cv_ref[j], w - 1, axis=1))

    s_old, s_new, v_sel = [], [], []
    for j in range(bt):
        q = q_ref[j]
        qb = q.astype(BF16)
        s_old.append(per_head([_mm(qb, ck_ref[j, rows, :].astype(BF16)) for rows in groups]))
        k_sel = per_head([kn_ref[j:j + 1, cols] for cols in groups])
        v_sel.append(per_head([vn_ref[j:j + 1, cols] for cols in groups]))
        s_new.append(jnp.sum(q * k_sel, axis=-1, keepdims=True))
    s_old = jnp.stack(s_old, axis=0) * scale
    s_new = jnp.stack(s_new, axis=0) * scale
    m = jnp.maximum(jnp.maximum(jnp.max(s_old, axis=-1, keepdims=True), s_new), sk)
    p_old = jnp.exp(s_old - m)
    p_new = jnp.exp(s_new - m)
    inv = 1.0 / (jnp.sum(p_old, axis=-1, keepdims=True) + p_new + jnp.exp(sk - m))
    p_old = p_old.astype(BF16)
    for j in range(bt):
        o = per_head([_mm_nt(p_old[j], cv_ref[j, rows, :].astype(BF16)) for rows in groups])
        o_ref[j] = (o + p_new[j] * v_sel[j]) * inv[j]


def _swa_sample(sinks, q3, k_new, v_new, cache_k, cache_v):
    bt = SEQ_TILE
    nb, _, w = cache_k.shape
    assert w == LANES
    row = lambda width: pl.BlockSpec((bt, width), lambda i: (i, 0))
    q_spec = pl.BlockSpec((bt, SWA_HEADS, SWA_HEAD_DIM), lambda i: (i, 0, 0))
    c_spec = pl.BlockSpec((bt, SWA_KV, w), lambda i: (i, 0, 0))
    return pl.pallas_call(
        _swa_sample_body,
        grid=(nb // bt,),
        in_specs=[pl.BlockSpec((SWA_HEADS, 1), lambda i: (0, 0)), q_spec, row(SWA_KV), row(SWA_KV), c_spec, c_spec],
        out_specs=[q_spec, c_spec, c_spec],
        out_shape=[jax.ShapeDtypeStruct((nb, SWA_HEADS, SWA_HEAD_DIM), F32),
                   jax.ShapeDtypeStruct(cache_k.shape, F32), jax.ShapeDtypeStruct(cache_v.shape, F32)],
        compiler_params=pltpu.CompilerParams(dimension_semantics=("arbitrary",)),
        name="swa_sample",
    )(sinks, q3, k_new, v_new, cache_k, cache_v)


def _rope_tables(pos):
    inv = jnp.power(ROPE_THETA, -jnp.arange(ROPE_HALF, dtype=F32) * 2.0 / ROPE_DIM)
    ang = pos.astype(F32)[:, None] * inv[None, :]
    cos, sin = jnp.cos(ang), jnp.sin(ang)
    n = pos.shape[0]
    rest = SWA_HEAD_DIM - ROPE_DIM
    rc = jnp.concatenate([cos, cos, jnp.ones((n, rest), F32)], axis=-1)
    ra = jnp.concatenate([-sin, jnp.zeros((n, ROPE_HALF + rest), F32)], axis=-1)
    rb = jnp.concatenate([jnp.zeros((n, ROPE_HALF), F32), sin, jnp.zeros((n, rest), F32)], axis=-1)
    reps = LANES // SWA_HEAD_DIM
    return tuple(jnp.tile(t, (1, reps)) for t in (rc, ra, rb))


def kernel(x_prompt, x_sample, state_gla, cache_swa_k, cache_swa_v, gla_w_in, gla_w_gate2, gla_b_gate, gla_g_head, gla_w_out, swa_w_qkv, swa_b_qkv, swa_sinks, swa_w_out, swa_b_out, norm_mix_pre, norm_mix_post, norm_ffn_pre, norm_ffn_post, ffn_w_up, ffn_w_down):
    batch, seq, _ = x_prompt.shape
    dec_batch, dec_seq, _ = x_sample.shape
    assert dec_seq == 1 and seq % SWA_WINDOW == 0
    past_len = seq
    n_p, n_s = batch * seq, dec_batch * dec_seq
    xp = x_prompt.reshape(n_p, D_MODEL)
    xs = x_sample.reshape(n_s, D_MODEL)

    w_in = jnp.pad(gla_w_in[0].astype(BF16), ((0, 0), (0, LANES - GLA_GATE_RANK)))
    w_g2 = jnp.pad(gla_w_gate2[0], ((0, LANES - GLA_GATE_RANK), (0, 0))).astype(BF16)
    b_g = gla_b_gate[0][None, :]
    g_head = gla_g_head[0][None, :]
    w_gout = gla_w_out[0].astype(BF16)
    w_qkv = swa_w_qkv[0].astype(BF16)
    b_qkv = swa_b_qkv[0][None, :]
    w_sout = swa_w_out[0].astype(BF16)
    b_sout = swa_b_out[0][None, :]
    w_up = ffn_w_up.astype(BF16)
    w_dn = ffn_w_down.astype(BF16)
    row = lambda t, i: t[i][None, :]

    in0_consts = [row(norm_mix_pre, 0), w_in, w_g2, b_g]
    mid_consts = [g_head, w_gout, row(norm_mix_post, 0), row(norm_ffn_pre, 0), (w_up, 0), (w_dn, 0), row(norm_ffn_post, 0),
                  row(norm_mix_pre, 1), w_qkv, b_qkv]
    out_consts = [w_sout, b_sout, row(norm_mix_post, 1), row(norm_ffn_pre, 1), (w_up, 1), (w_dn, 1), row(norm_ffn_post, 1)]
    in0_widths = [GLA_DK, GLA_DK, GLA_DV, GLA_DV, GLA_DK]
    mid_widths = [D_MODEL, SWA_Q, SWA_KV, SWA_KV]

    tm, ts = TOKEN_TILE, n_s
    (q, k, v, r, la), (qs, ks, vs, rs, las) = _tok_call(
        _in0_body,
        [(n_p, tm, [(xp, None)], in0_widths, [F32, F32, BF16, F32, F32]),
         (n_s, ts, [(xs, None)], in0_widths, [F32] * 5)],
        in0_consts, "in0")
    o, s_fin_p = _gla_prompt(q, k, v, la, batch, seq)
    o_s, s_new = _gla_sample(qs, ks, vs, las, state_gla[0])
    tabs = _rope_tables(jnp.arange(seq))
    tabs_s = _rope_tables(jnp.full((n_s,), past_len, jnp.int32))
    tab_map = lambda t: (t % (seq // tm), 0)
    (h2, q1, k1, v1), (h2s, q1s, k1s, v1s) = _tok_call(
        _mid_body,
        [(n_p, tm, [(o, None), (r, None), (xp, None)] + [(t, tab_map) for t in tabs], mid_widths, [F32] * 4),
         (n_s, ts, [(o_s, None), (rs, None), (xs, None)] + [(t, None) for t in tabs_s], mid_widths, [F32] * 4)],
        mid_consts, "mid")
    win = cache_swa_k.shape[2]
    to_t = lambda c: jnp.transpose(c[0].reshape(dec_batch, win, SWA_KV), (0, 2, 1))
    from_t = lambda c: jnp.transpose(c, (0, 2, 1)).reshape(1, dec_batch, win, SWA_KV_HEADS, SWA_HEAD_DIM)
    attn_s, nk, nv = _swa_sample(swa_sinks[0][:, None], q1s.reshape(n_s, SWA_HEADS, SWA_HEAD_DIM), k1s, v1s,
                                 to_t(cache_swa_k), to_t(cache_swa_v))
    y_p, y_s = _swa_out(swa_sinks[0], q1, k1, v1, h2, attn_s.reshape(n_s, SWA_Q), h2s, out_consts, batch, seq)
    wp = min(SWA_WINDOW, seq)
    tail = lambda t: t.reshape(batch, seq, SWA_KV)[:, seq - wp:].reshape(batch, wp, SWA_KV_HEADS, SWA_HEAD_DIM)
    k_tail, v_tail = tail(k1), tail(v1)

    return (y_p.reshape(batch, seq, D_MODEL), y_s.reshape(dec_batch, dec_seq, D_MODEL),
            s_fin_p[None], s_new[None], k_tail[None], v_tail[None], from_t(nk), from_t(nv))
```

```python
import functools

import jax
import jax.numpy as jnp
from jax import lax
from jax.experimental import pallas as pl
from jax.experimental.pallas import tpu as pltpu

F32 = jnp.float32
BF16 = jnp.bfloat16

D_MODEL = 1024
D_FF = 4 * D_MODEL
NORM_EPS = 1e-6

GLA_HEADS = 4
GLA_DK = D_MODEL // 2
GLA_DV = D_MODEL
GLA_DK_HEAD = GLA_DK // GLA_HEADS
GLA_DV_HEAD = GLA_DV // GLA_HEADS
GLA_GATE_RANK = 16
GLA_TAU = 16.0
GLA_CHUNK = 64
GLA_MAIN = 2 * GLA_DK + 2 * GLA_DV

SWA_HEAD_DIM = 64
SWA_HEADS = D_MODEL // SWA_HEAD_DIM
SWA_KV_HEADS = 4
SWA_GROUP = SWA_HEADS // SWA_KV_HEADS
SWA_WINDOW = 128
SWA_Q = SWA_HEADS * SWA_HEAD_DIM
SWA_KV = SWA_KV_HEADS * SWA_HEAD_DIM
SWA_QKV = SWA_Q + 2 * SWA_KV
ROPE_THETA = 500000.0
ROPE_DIM = SWA_HEAD_DIM // 4
ROPE_HALF = ROPE_DIM // 2

LANES = 128
FFN_CHUNK = 512
TOKEN_TILE = 512
GLA_TILE = 256
GLA_STEP = 512
SWA_TILE = 512
SEQ_TILE = 8
VMEM_LIMIT = 56 * 1024 * 1024
NEG_BIG = -1e30
LOG2E = 1.4426950408889634


def _mm(a, b):
    return jnp.dot(a, b, preferred_element_type=F32)


def _mm_nt(a, b):
    return lax.dot_general(a, b, (((1,), (1,)), ((), ())), preferred_element_type=F32)


def _mm_tn(a, b):
    return lax.dot_general(a, b, (((0,), (0,)), ((), ())), preferred_element_type=F32)


def _rms(x, g):
    ms = jnp.mean(x * x, axis=-1, keepdims=True)
    return x * lax.rsqrt(ms + NORM_EPS) * g


def _split3(x):
    hi = x.astype(BF16)
    r1 = x - hi.astype(F32)
    mid = r1.astype(BF16)
    lo = (r1 - mid.astype(F32)).astype(BF16)
    return hi, mid, lo


def _ffn(a_bf16, wup_ref, wdn_ref, between=None):
    acc = None
    for c in range(D_FF // FFN_CHUNK):
        cols = slice(c * FFN_CHUNK, (c + 1) * FFN_CHUNK)
        u = _mm(a_bf16, wup_ref[:, cols])
        u = jnp.square(jnp.maximum(u, 0.0)).astype(BF16)
        p = _mm(u, wdn_ref[cols, :])
        acc = p if acc is None else acc + p
        if between is not None:
            between()
    return acc


def _in0_body(x_ref, g_ref, w_ref, wg_ref, bg_ref, q_ref, k_ref, v_ref, r_ref, la_ref):
    a = _rms(x_ref[...], g_ref[...]).astype(BF16)
    q_ref[...] = _mm(a, w_ref[:, 0:GLA_DK]) * (GLA_DK_HEAD ** -0.5)
    k_ref[...] = _mm(a, w_ref[:, GLA_DK:2 * GLA_DK])
    for c in range(GLA_DV // 512):
        cols = slice(c * 512, (c + 1) * 512)
        v_ref[:, cols] = _mm(a, w_ref[:, 2 * GLA_DK + c * 512:2 * GLA_DK + (c + 1) * 512]).astype(v_ref.dtype)
        r_ref[:, cols] = _mm(a, w_ref[:, 2 * GLA_DK + GLA_DV + c * 512:2 * GLA_DK + GLA_DV + (c + 1) * 512])
    z = _mm(a, w_ref[:, GLA_MAIN:GLA_MAIN + LANES]).astype(BF16)
    zg = _mm(z, wg_ref[...]) + bg_ref[...]
    la_ref[...] = (jnp.minimum(zg, 0.0) - jnp.log1p(jnp.exp(-jnp.abs(zg)))) * (1.0 / GLA_TAU)


def _mid_body(o_ref, r_ref, h_ref, rc_ref, ra_ref, rb_ref,
              gh_ref, wo_ref, gpost_ref, gfpre_ref, wup_ref, wdn_ref, gfpost_ref,
              gpre1_ref, wqkv_ref, bqkv_ref,
              h2_ref, q1_ref, k1_ref, v1_ref):
    m = None
    for hh in range(GLA_HEADS):
        cols = slice(hh * GLA_DV_HEAD, (hh + 1) * GLA_DV_HEAD)
        on = _rms(o_ref[:, cols], gh_ref[...])
        r = r_ref[:, cols]
        u = (on * (r * (1.0 / (1.0 + jnp.exp(-r))))).astype(BF16)
        p = _mm(u, wo_ref[cols, :])
        m = p if m is None else m + p
    h1 = h_ref[...] + _rms(m, gpost_ref[...])
    f = _ffn(_rms(h1, gfpre_ref[...]).astype(BF16), wup_ref, wdn_ref)
    h2 = h1 + _rms(f, gfpost_ref[...])
    h2_ref[...] = h2
    a3 = _rms(h2, gpre1_ref[...]).astype(BF16)
    rc, ra, rb = rc_ref[...], ra_ref[...], rb_ref[...]
    wide = 2 * LANES
    for c2 in range((SWA_Q + SWA_KV) // wide):
        x2 = _mm(a3, wqkv_ref[:, c2 * wide:(c2 + 1) * wide]) + bqkv_ref[:, c2 * wide:(c2 + 1) * wide]
        for half in range(2):
            c = 2 * c2 + half
            x = x2[:, half * LANES:(half + 1) * LANES]
            y = x * rc + pltpu.roll(x, LANES - ROPE_HALF, axis=1) * ra + pltpu.roll(x, ROPE_HALF, axis=1) * rb
            if c < SWA_Q // LANES:
                q1_ref[:, c * LANES:(c + 1) * LANES] = y
            else:
                k1_ref[:, c * LANES - SWA_Q:(c + 1) * LANES - SWA_Q] = y
    v1_ref[...] = _mm(a3, wqkv_ref[:, SWA_Q + SWA_KV:SWA_QKV]) + bqkv_ref[:, SWA_Q + SWA_KV:SWA_QKV]


def _tok_call(body, groups, const_inputs, name):
    in_specs, args, out_specs, out_shape = [], [], [], []
    ranges, start = [], 0
    for n_rows, tm, row_inputs, out_widths, out_dtypes in groups:
        assert n_rows % tm == 0
        count = n_rows // tm
        local = lambda i, start=start, count=count: jnp.clip(i - start, 0, count - 1)
        for arr, imap in row_inputs:
            imap = imap if imap is not None else (lambda t: (t, 0))
            in_specs.append(pl.BlockSpec((tm, arr.shape[1]), lambda i, imap=imap, local=local: imap(local(i))))
            args.append(arr)
        for w, dt in zip(out_widths, out_dtypes):
            out_specs.append(pl.BlockSpec((tm, w), lambda i, local=local: (local(i), 0)))
            out_shape.append(jax.ShapeDtypeStruct((n_rows, w), dt))
        ranges.append((start, count, len(row_inputs), len(out_widths)))
        start += count
    n_row_refs = len(in_specs)
    n_const = len(const_inputs)

    def kern(*refs):
        row_refs, const_refs, out_refs = refs[:n_row_refs], refs[n_row_refs:n_row_refs + n_const], refs[n_row_refs + n_const:]
        i = pl.program_id(0)
        r0 = o0 = 0
        for first, count, n_in, n_out in ranges:
            ins, outs = row_refs[r0:r0 + n_in], out_refs[o0:o0 + n_out]
            r0, o0 = r0 + n_in, o0 + n_out

            @pl.when((i >= first) & (i < first + count))
            def _(ins=ins, outs=outs):
                body(*ins, *const_refs, *outs)

    for entry in const_inputs:
        if isinstance(entry, tuple):
            arr, layer = entry
            spec = pl.BlockSpec((None,) + arr.shape[1:], lambda i, layer=layer: (layer, 0, 0),
                                pipeline_mode=pl.Buffered(1))
        else:
            arr = entry
            spec = pl.BlockSpec(arr.shape, lambda i: (0, 0), pipeline_mode=pl.Buffered(1))
        in_specs.append(spec)
        args.append(arr)
    outs = pl.pallas_call(
        kern,
        grid=(start,),
        in_specs=in_specs,
        out_specs=out_specs,
        out_shape=out_shape,
        compiler_params=pltpu.CompilerParams(dimension_semantics=("arbitrary",), vmem_limit_bytes=VMEM_LIMIT),
        name=name,
    )(*args)
    grouped, o0 = [], 0
    for _, _, _, n_out in ranges:
        grouped.append(list(outs[o0:o0 + n_out]))
        o0 += n_out
    return grouped


def _gla_prompt_body(q_ref, k_ref, v_ref, la_ref, o_ref, sfin_ref, st_ref):
    t = pl.program_id(1)

    @pl.when(t == 0)
    def _():
        st_ref[...] = jnp.zeros_like(st_ref)

    for base in range(0, q_ref.shape[0], GLA_TILE):
        _gla_tile(slice(base, base + GLA_TILE), q_ref, k_ref, v_ref, la_ref, o_ref, st_ref)

    @pl.when(t == pl.num_programs(1) - 1)
    def _():
        sfin_ref[0] = st_ref[...]


def _gla_tile(tile, q_ref, k_ref, v_ref, la_ref, o_ref, st_ref):
    tg = GLA_TILE
    c_len = GLA_CHUNK
    n_chunks = tg // c_len
    chunk_rows = [slice(ci * c_len, (ci + 1) * c_len) for ci in range(n_chunks)]
    row = lax.broadcasted_iota(jnp.int32, (tg, tg), 0)
    col = lax.broadcasted_iota(jnp.int32, (tg, tg), 1)
    lower_b = (row // c_len == col // c_len) & (col <= row)
    lower = jnp.where(lower_b, 1.0, 0.0).astype(BF16)
    hi, mid, lo = _split3(la_ref[tile, :])
    cum = _mm(lower, hi) + _mm(lower, mid) + _mm(lower, lo)
    tot = jnp.concatenate([jnp.broadcast_to(cum[r.stop - 1:r.stop, :], (c_len, cum.shape[1])) for r in chunk_rows], axis=0)
    k_all = k_ref[tile, :]
    qd = (q_ref[tile, :] * jnp.exp(cum)).astype(BF16)
    ki = (k_all * jnp.exp(-cum)).astype(BF16)
    ke = k_all * jnp.exp(tot - cum)
    lane_chunk = lax.broadcasted_iota(jnp.int32, (GLA_DK_HEAD, tg), 1) // c_len

    att, kv = [], []
    for h in range(GLA_HEADS):
        kc = slice(h * GLA_DK_HEAD, (h + 1) * GLA_DK_HEAD)
        att.append(jnp.where(lower_b, _mm_nt(qd[:, kc], ki[:, kc]), 0.0).astype(BF16))
        ke_t = ke[:, kc].T
        v_h = v_ref[tile, h * GLA_DV_HEAD:(h + 1) * GLA_DV_HEAD]
        kv.append([_mm(jnp.where(lane_chunk == ci, ke_t, 0.0).astype(BF16), v_h) for ci in range(n_chunks)])

    s_before = []
    for h in range(GLA_HEADS):
        kc = slice(h * GLA_DK_HEAD, (h + 1) * GLA_DK_HEAD)
        st = st_ref[h]
        starts = []
        for ci, r in enumerate(chunk_rows):
            starts.append(st.astype(BF16))
            e_col = jnp.exp(cum[r.stop - 8:r.stop, kc]).T[:, 7:8]
            st = e_col * st + kv[h][ci]
        st_ref[h] = st
        s_before.append(starts)

    for h in range(GLA_HEADS):
        kc = slice(h * GLA_DK_HEAD, (h + 1) * GLA_DK_HEAD)
        vc = slice(h * GLA_DV_HEAD, (h + 1) * GLA_DV_HEAD)
        o_intra = _mm(att[h], v_ref[tile, vc])
        for ci, r in enumerate(chunk_rows):
            o_ref[tile.start + r.start:tile.start + r.stop, vc] = o_intra[r] + _mm(qd[r, kc], s_before[h][ci])


def _gla_prompt(q, k, v, la, batch, seq):
    tg = GLA_STEP
    nt = seq // tg
    qk_spec = pl.BlockSpec((tg, GLA_DK), lambda b, t: (b * nt + t, 0))
    v_spec = pl.BlockSpec((tg, GLA_DV), lambda b, t: (b * nt + t, 0))
    st_shape = (GLA_HEADS, GLA_DK_HEAD, GLA_DV_HEAD)
    return pl.pallas_call(
        _gla_prompt_body,
        grid=(batch, nt),
        in_specs=[qk_spec, qk_spec, v_spec, qk_spec],
        out_specs=[v_spec, pl.BlockSpec((1,) + st_shape, lambda b, t: (b, 0, 0, 0))],
        out_shape=[jax.ShapeDtypeStruct((batch * seq, GLA_DV), F32),
                   jax.ShapeDtypeStruct((batch,) + st_shape, F32)],
        scratch_shapes=[pltpu.VMEM(st_shape, F32)],
        compiler_params=pltpu.CompilerParams(dimension_semantics=("arbitrary", "arbitrary"), vmem_limit_bytes=VMEM_LIMIT),
        name="gla_prompt",
    )(q, k, v, la)


def _gla_sample_body(q_ref, k_ref, v_ref, la_ref, s_ref, o_ref, sn_ref):
    bt = q_ref.shape[0]
    for h in range(GLA_HEADS):
        kc = slice(h * GLA_DK_HEAD, (h + 1) * GLA_DK_HEAD)
        vc = slice(h * GLA_DV_HEAD, (h + 1) * GLA_DV_HEAD)
        a_t = jnp.exp(la_ref[:, kc]).T
        k_t = k_ref[:, kc].T
        q_t = q_ref[:, kc].T
        for j in range(bt):
            s_new = a_t[:, j:j + 1] * s_ref[j, h] + k_t[:, j:j + 1] * v_ref[j:j + 1, vc]
            sn_ref[j, h] = s_new
            o_ref[j:j + 1, vc] = jnp.sum(q_t[:, j:j + 1] * s_new, axis=0, keepdims=True)


def _gla_sample(q, k, v, la, state):
    bt = SEQ_TILE
    nb = q.shape[0]
    row = lambda w: pl.BlockSpec((bt, w), lambda i: (i, 0))
    st_spec = pl.BlockSpec((bt, GLA_HEADS, GLA_DK_HEAD, GLA_DV_HEAD), lambda i: (i, 0, 0, 0))
    return pl.pallas_call(
        _gla_sample_body,
        grid=(nb // bt,),
        in_specs=[row(GLA_DK), row(GLA_DK), row(GLA_DV), row(GLA_DK), st_spec],
        out_specs=[row(GLA_DV), st_spec],
        out_shape=[jax.ShapeDtypeStruct((nb, GLA_DV), F32), jax.ShapeDtypeStruct(state.shape, F32)],
        compiler_params=pltpu.CompilerParams(dimension_semantics=("arbitrary",), vmem_limit_bytes=VMEM_LIMIT),
        name="gla_sample",
    )(q, k, v, la, state)


def _swa_attend_units(sink_ref, q_ref, k_full, v_full, has_prev, o_ref):
    w = SWA_WINDOW
    hd = SWA_HEAD_DIM
    tq = q_ref.shape[0]
    nkv = k_full.shape[0]
    lane_q = lax.broadcasted_iota(jnp.int32, (w, LANES), 1) < hd
    lane_kv = lax.broadcasted_iota(jnp.int32, (nkv, LANES), 1) < hd
    i = lax.broadcasted_iota(jnp.int32, (w, 2 * w), 0)
    j = lax.broadcasted_iota(jnp.int32, (w, 2 * w), 1)
    band = jnp.where(j < w, jnp.where(j >= i, 1, 0), jnp.where(j - w <= i, 1, 0))
    band_first = jnp.where(j < w, has_prev, 1) * band
    lane_2w = lax.broadcasted_iota(jnp.int32, (2 * w, LANES), 1) < hd
    ones_lo = jnp.where(lane_2w, 1.0, 0.0).astype(BF16)
    ones_hi = jnp.where(lane_2w, 0.0, 1.0).astype(BF16)
    c2 = (hd ** -0.5) * LOG2E

    k_prep, v_prep = [], []
    for p in range(SWA_KV // LANES):
        cols = slice(p * LANES, (p + 1) * LANES)
        k_p, v_p = k_full[:, cols], v_full[:, cols]
        k_prep.append((k_p.astype(BF16), pltpu.roll(k_p, hd, axis=1).astype(BF16)))
        v_r = pltpu.roll(v_p, hd, axis=1)
        v_prep.append(((jnp.where(lane_kv, v_p, 0.0).astype(BF16), jnp.where(lane_kv, 0.0, v_r).astype(BF16)),
                       (jnp.where(lane_kv, v_r, 0.0).astype(BF16), jnp.where(lane_kv, 0.0, v_p).astype(BF16))))

    def softmax_part(s, hh, mask):
        s2 = jnp.where(mask, s, NEG_BIG)
        sk2 = jnp.full((w, 1), sink_ref[hh], F32) * LOG2E
        m2 = jnp.maximum(jnp.max(s2, axis=-1, keepdims=True), sk2)
        return jnp.exp2(s2 - m2).astype(BF16), sk2 - m2

    for b in range(tq // w):
        rows = slice(b * w, (b + 1) * w)
        krows = slice(b * w, (b + 2) * w)
        mask = (band_first if b == 0 else band) > 0
        for p in range(SWA_KV // LANES):
            q_lo, q_hi = [], []
            for x in range(4):
                q_c = q_ref[rows, (4 * p + x) * LANES:(4 * p + x + 1) * LANES] * c2
                q_lo.append(jnp.where(lane_q, q_c, 0.0).astype(BF16))
                q_hi.append(jnp.where(lane_q, 0.0, q_c).astype(BF16))
            s_self = _mm_nt(jnp.concatenate([q_lo[0], q_lo[1], q_hi[2], q_hi[3]], axis=0), k_prep[p][0][krows])
            s_roll = _mm_nt(jnp.concatenate([q_hi[0], q_hi[1], q_lo[2], q_lo[3]], axis=0), k_prep[p][1][krows])
            for x in range(4):
                c = 4 * p + x
                gh = x // 2
                xr = slice(x * w, (x + 1) * w)
                s_lo, s_hi = (s_self[xr], s_roll[xr]) if gh == 0 else (s_roll[xr], s_self[xr])
                p_lo, d_lo = softmax_part(s_lo, 2 * c, mask)
                p_hi, d_hi = softmax_part(s_hi, 2 * c + 1, mask)
                v_lo, v_hi = v_prep[p][gh]
                rhs = jnp.concatenate([jnp.concatenate([v_lo[krows], ones_lo], axis=1),
                                       jnp.concatenate([v_hi[krows], ones_hi], axis=1)], axis=0)
                ext = _mm(jnp.concatenate([p_lo, p_hi], axis=1), rhs)
                den = ext[:, LANES:] + jnp.exp2(jnp.where(lane_q, d_lo, d_hi))
                o_ref[rows, c * LANES:(c + 1) * LANES] = (ext[:, :LANES] / den).astype(o_ref.dtype)
            yield


def _out_stage(at_bf16, h, wo_ref, bo_ref, gpost_ref, gfpre_ref, wup_ref, wdn_ref, gfpost_ref, between=None):
    m = _mm(at_bf16, wo_ref[...]) + bo_ref[...]
    h1 = h + _rms(m, gpost_ref[...])
    f = _ffn(_rms(h1, gfpre_ref[...]).astype(BF16), wup_ref, wdn_ref, between)
    return h1 + _rms(f, gfpost_ref[...])


def _swa_out_body(batch, nt, sink_ref, q_ref, kc_ref, kp_ref, vc_ref, vp_ref, h_ref, ats_ref, hs_ref,
                  wo_ref, bo_ref, gpost_ref, gfpre_ref, wup_ref, wdn_ref, gfpost_ref,
                  y_ref, ys_ref, attn_scr):
    i = pl.program_id(0)
    per_batch = nt + 1
    t = lax.rem(i, per_batch)
    is_prompt = i < batch * per_batch
    consts = (wo_ref, bo_ref, gpost_ref, gfpre_ref, wup_ref, wdn_ref, gfpost_ref)

    @pl.when(i == 0)
    def _():
        attn_scr[...] = jnp.zeros_like(attn_scr)

    @pl.when(is_prompt & (t < nt))
    def _():
        at_prev = attn_scr[...]
        k_full = jnp.concatenate([kp_ref[...], kc_ref[...]], axis=0)
        v_full = jnp.concatenate([vp_ref[...], vc_ref[...]], axis=0)
        units = _swa_attend_units(sink_ref, q_ref, k_full, v_full, jnp.minimum(t, 1), attn_scr)
        y_ref[...] = _out_stage(at_prev, h_ref[...], *consts, between=lambda: next(units, None))
        for _ in units:
            pass

    @pl.when(is_prompt & (t == nt))
    def _():
        y_ref[...] = _out_stage(attn_scr[...], h_ref[...], *consts)

    @pl.when(i == batch * per_batch)
    def _():
        ys_ref[...] = _out_stage(ats_ref[...].astype(BF16), hs_ref[...], *consts)


def _swa_out(sinks, q, k, v, h, attn_s, h_s, consts, batch, seq):
    tq = SWA_TILE
    w = SWA_WINDOW
    nt = seq // tq
    per_batch = nt + 1
    n_s = h_s.shape[0]

    def pos(i):
        ii = jnp.minimum(i, batch * per_batch - 1)
        return ii // per_batch, lax.rem(ii, per_batch)

    def att_tile(i):
        b, t = pos(i)
        return (b * nt + jnp.minimum(t, nt - 1), 0)

    def prev_block(i):
        b, t = pos(i)
        return (b * (seq // w) + jnp.maximum(jnp.minimum(t, nt - 1) * (tq // w) - 1, 0), 0)

    def out_tile(i):
        b, t = pos(i)
        return (b * nt + jnp.maximum(t - 1, 0), 0)

    whole = lambda arr: pl.BlockSpec(arr.shape, lambda i: (0, 0))
    const_specs = []
    for entry in consts:
        if isinstance(entry, tuple):
            arr, layer = entry
            const_specs.append(pl.BlockSpec((None,) + arr.shape[1:], lambda i, layer=layer: (layer, 0, 0),
                                            pipeline_mode=pl.Buffered(1)))
        else:
            const_specs.append(pl.BlockSpec(entry.shape, lambda i: (0, 0), pipeline_mode=pl.Buffered(1)))
    const_args = [e[0] if isinstance(e, tuple) else e for e in consts]
    return pl.pallas_call(
        functools.partial(_swa_out_body, batch, nt),
        grid=(batch * per_batch + 1,),
        in_specs=[pl.BlockSpec(memory_space=pltpu.SMEM),
                  pl.BlockSpec((tq, SWA_Q), att_tile), pl.BlockSpec((tq, SWA_KV), att_tile),
                  pl.BlockSpec((w, SWA_KV), prev_block), pl.BlockSpec((tq, SWA_KV), att_tile),
                  pl.BlockSpec((w, SWA_KV), prev_block), pl.BlockSpec((tq, D_MODEL), out_tile),
                  whole(attn_s), whole(h_s)] + const_specs,
        out_specs=[pl.BlockSpec((tq, D_MODEL), out_tile), whole(h_s)],
        out_shape=[jax.ShapeDtypeStruct((batch * seq, D_MODEL), F32), jax.ShapeDtypeStruct((n_s, D_MODEL), F32)],
        scratch_shapes=[pltpu.VMEM((tq, SWA_Q), BF16)],
        compiler_params=pltpu.CompilerParams(dimension_semantics=("arbitrary",), vmem_limit_bytes=VMEM_LIMIT),
        name="swa_out",
    )(sinks, q, k, k, v, v, h, attn_s, h_s, *const_args)


def _swa_sample_body(sk_ref, q_ref, kn_ref, vn_ref, ck_ref, cv_ref, o_ref, nk_ref, nv_ref):
    bt = q_ref.shape[0]
    w = ck_ref.shape[2]
    hd = SWA_HEAD_DIM
    hgroup = lax.broadcasted_iota(jnp.int32, (SWA_HEADS, 1), 0) // SWA_GROUP
    newest = lax.broadcasted_iota(jnp.int32, (SWA_KV, w), 1) == w - 1
    kn_t = kn_ref[...].T
    vn_t = vn_ref[...].T
    scale = hd ** -0.5
    sk = sk_ref[...]
    groups = [slice(g * hd, (g + 1) * hd) for g in range(SWA_KV_HEADS)]

    def per_head(pieces):
        out = pieces[0]
        for g in range(1, SWA_KV_HEADS):
            out = jnp.where(hgroup == g, pieces[g], out)
        return out

    for j in range(bt):
        nk_ref[j] = jnp.where(newest, kn_t[:, j:j + 1], pltpu.roll(ck_ref[j], w - 1, axis=1))
        nv_ref[j] = jnp.where(newest, vn_t[:, j:j + 1], pltpu.roll(cv_ref[j], w - 1, axis=1))

    s_old, s_new, v_sel = [], [], []
    for j in range(bt):
        q = q_ref[j]
        qb = q.astype(BF16)
        s_old.append(per_head([_mm(qb, ck_ref[j, rows, :].astype(BF16)) for rows in groups]))
        k_sel = per_head([kn_ref[j:j + 1, cols] for cols in groups])
        v_sel.append(per_head([vn_ref[j:j + 1, cols] for cols in groups]))
        s_new.append(jnp.sum(q * k_sel, axis=-1, keepdims=True))
    s_old = jnp.stack(s_old, axis=0) * scale
    s_new = jnp.stack(s_new, axis=0) * scale
    m = jnp.maximum(jnp.maximum(jnp.max(s_old, axis=-1, keepdims=True), s_new), sk)
    p_old = jnp.exp(s_old - m)
    p_new = jnp.exp(s_new - m)
    inv = 1.0 / (jnp.sum(p_old, axis=-1, keepdims=True) + p_new + jnp.exp(sk - m))
    p_old = p_old.astype(BF16)
    for j in range(bt):
        o = per_head([_mm_nt(p_old[j], cv_ref[j, rows, :].astype(BF16)) for rows in groups])
        o_ref[j] = (o + p_new[j] * v_sel[j]) * inv[j]


def _swa_sample(sinks, q3, k_new, v_new, cache_k, cache_v):
    bt = SEQ_TILE
    nb, _, w = cache_k.shape
    assert w == LANES
    row = lambda width: pl.BlockSpec((bt, width), lambda i: (i, 0))
    q_spec = pl.BlockSpec((bt, SWA_HEADS, SWA_HEAD_DIM), lambda i: (i, 0, 0))
    c_spec = pl.BlockSpec((bt, SWA_KV, w), lambda i: (i, 0, 0))
    return pl.pallas_call(
        _swa_sample_body,
        grid=(nb // bt,),
        in_specs=[pl.BlockSpec((SWA_HEADS, 1), lambda i: (0, 0)), q_spec, row(SWA_KV), row(SWA_KV), c_spec, c_spec],
        out_specs=[q_spec, c_spec, c_spec],
        out_shape=[jax.ShapeDtypeStruct((nb, SWA_HEADS, SWA_HEAD_DIM), F32),
                   jax.ShapeDtypeStruct(cache_k.shape, F32), jax.ShapeDtypeStruct(cache_v.shape, F32)],
        compiler_params=pltpu.CompilerParams(dimension_semantics=("arbitrary",)),
        name="swa_sample",
    )(sinks, q3, k_new, v_new, cache_k, cache_v)


def _rope_tables(pos):
    d = jnp.arange(LANES) % SWA_HEAD_DIM
    inv = jnp.power(ROPE_THETA, -(d % ROPE_HALF).astype(F32) * 2.0 / ROPE_DIM)
    ang = pos.astype(F32)[:, None] * inv[None, :]
    cos, sin = jnp.cos(ang), jnp.sin(ang)
    rc = jnp.where(d < ROPE_DIM, cos, 1.0)
    ra = jnp.where(d < ROPE_HALF, -sin, 0.0)
    rb = jnp.where((d >= ROPE_HALF) & (d < ROPE_DIM), sin, 0.0)
    return rc, ra, rb


def kernel(x_prompt, x_sample, state_gla, cache_swa_k, cache_swa_v, gla_w_in, gla_w_gate2, gla_b_gate, gla_g_head, gla_w_out, swa_w_qkv, swa_b_qkv, swa_sinks, swa_w_out, swa_b_out, norm_mix_pre, norm_mix_post, norm_ffn_pre, norm_ffn_post, ffn_w_up, ffn_w_down):
    batch, seq, _ = x_prompt.shape
    dec_batch, dec_seq, _ = x_sample.shape
    assert dec_seq == 1 and seq % SWA_WINDOW == 0
    past_len = seq
    n_p, n_s = batch * seq, dec_batch * dec_seq
    xp = x_prompt.reshape(n_p, D_MODEL)
    xs = x_sample.reshape(n_s, D_MODEL)

    w_in = jnp.pad(gla_w_in[0], ((0, 0), (0, LANES - GLA_GATE_RANK))).astype(BF16)
    w_g2 = jnp.pad(gla_w_gate2[0], ((0, LANES - GLA_GATE_RANK), (0, 0))).astype(BF16)
    b_g = gla_b_gate[0][None, :]
    g_head = gla_g_head[0][None, :]
    w_gout = gla_w_out[0].astype(BF16)
    w_qkv = swa_w_qkv[0].astype(BF16)
    b_qkv = swa_b_qkv[0][None, :]
    w_sout = swa_w_out[0].astype(BF16)
    b_sout = swa_b_out[0][None, :]
    w_up = ffn_w_up.astype(BF16)
    w_dn = ffn_w_down.astype(BF16)
    row = lambda t, i: t[i][None, :]

    in0_consts = [row(norm_mix_pre, 0), w_in, w_g2, b_g]
    mid_consts = [g_head, w_gout, row(norm_mix_post, 0), row(norm_ffn_pre, 0), (w_up, 0), (w_dn, 0), row(norm_ffn_post, 0),
                  row(norm_mix_pre, 1), w_qkv, b_qkv]
    out_consts = [w_sout, b_sout, row(norm_mix_post, 1), row(norm_ffn_pre, 1), (w_up, 1), (w_dn, 1), row(norm_ffn_post, 1)]
    in0_widths = [GLA_DK, GLA_DK, GLA_DV, GLA_DV, GLA_DK]
    mid_widths = [D_MODEL, SWA_Q, SWA_KV, SWA_KV]

    tm, ts = TOKEN_TILE, n_s
    (q, k, v, r, la), (qs, ks, vs, rs, las) = _tok_call(
        _in0_body,
        [(n_p, tm, [(xp, None)], in0_widths, [F32, F32, BF16, F32, F32]),
         (n_s, ts, [(xs, None)], in0_widths, [F32] * 5)],
        in0_consts, "in0")
    o, s_fin_p = _gla_prompt(q, k, v, la, batch, seq)
    o_s, s_new = _gla_sample(qs, ks, vs, las, state_gla[0])
    tabs = _rope_tables(jnp.arange(seq))
    tabs_s = _rope_tables(jnp.full((n_s,), past_len, jnp.int32))
    tab_map = lambda t: (t % (seq // tm), 0)
    (h2, q1, k1, v1), (h2s, q1s, k1s, v1s) = _tok_call(
        _mid_body,
        [(n_p, tm, [(o, None), (r, None), (xp, None)] + [(t, tab_map) for t in tabs], mid_widths, [F32] * 4),
         (n_s, ts, [(o_s, None), (rs, None), (xs, None)] + [(t, None) for t in tabs_s], mid_widths, [F32] * 4)],
        mid_consts, "mid")
    win = cache_swa_k.shape[2]
    to_t = lambda c: jnp.transpose(c[0].reshape(dec_batch, win, SWA_KV), (0, 2, 1))
    from_t = lambda c: jnp.transpose(c, (0, 2, 1)).reshape(1, dec_batch, win, SWA_KV_HEADS, SWA_HEAD_DIM)
    attn_s, nk, nv = _swa_sample(swa_sinks[0][:, None], q1s.reshape(n_s, SWA_HEADS, SWA_HEAD_DIM), k1s, v1s,
                                 to_t(cache_swa_k), to_t(cache_swa_v))
    y_p, y_s = _swa_out(swa_sinks[0], q1, k1, v1, h2, attn_s.reshape(n_s, SWA_Q), h2s, out_consts, batch, seq)
    wp = min(SWA_WINDOW, seq)
    tail = lambda t: t.reshape(batch, seq, SWA_KV)[:, seq - wp:].reshape(batch, wp, SWA_KV_HEADS, SWA_HEAD_DIM)
    k_tail, v_tail = tail(k1), tail(v1)

    return (y_p.reshape(batch, seq, D_MODEL), y_s.reshape(dec_batch, dec_seq, D_MODEL),
            s_fin_p[None], s_new[None], k_tail[None], v_tail[None], from_t(nk), from_t(nv))
```

```python
import functools

import jax
import jax.numpy as jnp
from jax import lax
from jax.experimental import pallas as pl
from jax.experimental.pallas import tpu as pltpu

F32 = jnp.float32
BF16 = jnp.bfloat16

D_MODEL = 1024
D_FF = 4 * D_MODEL
NORM_EPS = 1e-6

GLA_HEADS = 4
GLA_DK = D_MODEL // 2
GLA_DV = D_MODEL
GLA_DK_HEAD = GLA_DK // GLA_HEADS
GLA_DV_HEAD = GLA_DV // GLA_HEADS
GLA_GATE_RANK = 16
GLA_TAU = 16.0
GLA_CHUNK = 64
GLA_MAIN = 2 * GLA_DK + 2 * GLA_DV

SWA_HEAD_DIM = 64
SWA_HEADS = D_MODEL // SWA_HEAD_DIM
SWA_KV_HEADS = 4
SWA_GROUP = SWA_HEADS // SWA_KV_HEADS
SWA_WINDOW = 128
SWA_Q = SWA_HEADS * SWA_HEAD_DIM
SWA_KV = SWA_KV_HEADS * SWA_HEAD_DIM
SWA_QKV = SWA_Q + 2 * SWA_KV
ROPE_THETA = 500000.0
ROPE_DIM = SWA_HEAD_DIM // 4
ROPE_HALF = ROPE_DIM // 2

LANES = 128
FFN_CHUNK = 512
TOKEN_TILE = 512
GLA_TILE = 256
GLA_STEP = 512
SWA_TILE = 512
SEQ_TILE = 8
CAST_CHUNK_ELEMS = 64 * 1024
VMEM_LIMIT = 56 * 1024 * 1024
NEG_BIG = -1e30
LOG2E = 1.4426950408889634


def _mm(a, b):
    return jnp.dot(a, b, preferred_element_type=F32)


def _mm_nt(a, b):
    return lax.dot_general(a, b, (((1,), (1,)), ((), ())), preferred_element_type=F32)


def _mm_tn(a, b):
    return lax.dot_general(a, b, (((0,), (0,)), ((), ())), preferred_element_type=F32)


def _rms(x, g):
    ms = jnp.mean(x * x, axis=-1, keepdims=True)
    return x * lax.rsqrt(ms + NORM_EPS) * g


def _split3(x):
    hi = x.astype(BF16)
    r1 = x - hi.astype(F32)
    mid = r1.astype(BF16)
    lo = (r1 - mid.astype(F32)).astype(BF16)
    return hi, mid, lo


def _ffn(a_bf16, wup_ref, wdn_ref, between=None):
    acc = None
    for c in range(D_FF // FFN_CHUNK):
        cols = slice(c * FFN_CHUNK, (c + 1) * FFN_CHUNK)
        u = _mm(a_bf16, wup_ref[:, cols])
        u = jnp.square(jnp.maximum(u, 0.0)).astype(BF16)
        p = _mm(u, wdn_ref[cols, :])
        acc = p if acc is None else acc + p
        if between is not None:
            between()
    return acc


def _in0_body(x_ref, g_ref, w_ref, wg_ref, bg_ref, q_ref, k_ref, v_ref, r_ref, la_ref):
    a = _rms(x_ref[...], g_ref[...]).astype(BF16)
    q_ref[...] = _mm(a, w_ref[:, 0:GLA_DK]) * (GLA_DK_HEAD ** -0.5)
    k_ref[...] = _mm(a, w_ref[:, GLA_DK:2 * GLA_DK])
    for c in range(GLA_DV // 512):
        cols = slice(c * 512, (c + 1) * 512)
        v_ref[:, cols] = _mm(a, w_ref[:, 2 * GLA_DK + c * 512:2 * GLA_DK + (c + 1) * 512]).astype(v_ref.dtype)
        r_ref[:, cols] = _mm(a, w_ref[:, 2 * GLA_DK + GLA_DV + c * 512:2 * GLA_DK + GLA_DV + (c + 1) * 512])
    z = _mm(a, w_ref[:, GLA_MAIN:GLA_MAIN + LANES]).astype(BF16)
    zg = _mm(z, wg_ref[...]) + bg_ref[...]
    la_ref[...] = (jnp.minimum(zg, 0.0) - jnp.log1p(jnp.exp(-jnp.abs(zg)))) * (1.0 / GLA_TAU)


def _mid_body(o_ref, r_ref, h_ref, rc_ref, ra_ref, rb_ref,
              gh_ref, wo_ref, gpost_ref, gfpre_ref, wup_ref, wdn_ref, gfpost_ref,
              gpre1_ref, wqkv_ref, bqkv_ref,
              h2_ref, q1_ref, k1_ref, v1_ref):
    m = None
    for hh in range(GLA_HEADS):
        cols = slice(hh * GLA_DV_HEAD, (hh + 1) * GLA_DV_HEAD)
        on = _rms(o_ref[:, cols], gh_ref[...])
        r = r_ref[:, cols]
        u = (on * (r * (1.0 / (1.0 + jnp.exp(-r))))).astype(BF16)
        p = _mm(u, wo_ref[cols, :])
        m = p if m is None else m + p
    h1 = h_ref[...] + _rms(m, gpost_ref[...])
    f = _ffn(_rms(h1, gfpre_ref[...]).astype(BF16), wup_ref, wdn_ref)
    h2 = h1 + _rms(f, gfpost_ref[...])
    h2_ref[...] = h2
    a3 = _rms(h2, gpre1_ref[...]).astype(BF16)
    rc, ra, rb = rc_ref[...], ra_ref[...], rb_ref[...]
    wide = 2 * LANES
    for c2 in range((SWA_Q + SWA_KV) // wide):
        x2 = _mm(a3, wqkv_ref[:, c2 * wide:(c2 + 1) * wide]) + bqkv_ref[:, c2 * wide:(c2 + 1) * wide]
        for half in range(2):
            c = 2 * c2 + half
            x = x2[:, half * LANES:(half + 1) * LANES]
            y = x * rc + pltpu.roll(x, LANES - ROPE_HALF, axis=1) * ra + pltpu.roll(x, ROPE_HALF, axis=1) * rb
            if c < SWA_Q // LANES:
                q1_ref[:, c * LANES:(c + 1) * LANES] = y
            else:
                k1_ref[:, c * LANES - SWA_Q:(c + 1) * LANES - SWA_Q] = y
    v1_ref[...] = _mm(a3, wqkv_ref[:, SWA_Q + SWA_KV:SWA_QKV]) + bqkv_ref[:, SWA_Q + SWA_KV:SWA_QKV]


def _cast_load(src, dst_ref, stage_ref, sem_ref, rows):
    k_dim, n = src.shape
    n_chunks = k_dim // rows

    def copy(c):
        return pltpu.make_async_copy(src.at[pl.ds(c * rows, rows), :], stage_ref.at[c % 2], sem_ref.at[c % 2])

    if dst_ref.shape[1] > n:
        pad_from = (n // LANES) * LANES
        dst_ref[:, pad_from:] = jnp.zeros((k_dim, dst_ref.shape[1] - pad_from), BF16)
    copy(0).start()
    for c in range(n_chunks):
        if c + 1 < n_chunks:
            copy(c + 1).start()
        copy(c).wait()
        dst_ref[c * rows:(c + 1) * rows, 0:n] = stage_ref[c % 2].astype(BF16)


def _const_plan(const_inputs):
    in_specs, args, scratch, plan = [], [], [], []
    for entry in const_inputs:
        if isinstance(entry, tuple):
            arr, layer, n_pad = entry
            _, k_dim, n = arr.shape
            rows = 8
            while rows * 2 * n <= CAST_CHUNK_ELEMS and k_dim % (rows * 2) == 0:
                rows *= 2
            in_specs.append(pl.BlockSpec(memory_space=pl.ANY))
            args.append(arr)
            plan.append((len(scratch), layer, rows))
            scratch += [pltpu.VMEM((k_dim, n_pad), BF16), pltpu.VMEM((2, rows, n), F32), pltpu.SemaphoreType.DMA((2,))]
        else:
            in_specs.append(pl.BlockSpec(entry.shape, lambda i: (0, 0), pipeline_mode=pl.Buffered(1)))
            args.append(entry)
            plan.append(None)

    def bind(const_refs, scratch_refs):
        body_refs, loads = [], []
        for ref, p in zip(const_refs, plan):
            if p is None:
                body_refs.append(ref)
            else:
                s0, layer, rows = p
                dst, stage, sem = scratch_refs[s0:s0 + 3]
                body_refs.append(dst)
                loads.append(functools.partial(_cast_load, ref.at[layer], dst, stage, sem, rows))

        def load():
            for f in loads:
                f()

        return body_refs, load

    return in_specs, args, scratch, bind


def _tok_call(body, groups, const_inputs, name):
    in_specs, args, out_specs, out_shape = [], [], [], []
    ranges, start = [], 0
    for n_rows, tm, row_inputs, out_widths, out_dtypes in groups:
        assert n_rows % tm == 0
        count = n_rows // tm
        local = lambda i, start=start, count=count: jnp.clip(i - start, 0, count - 1)
        mode = dict(pipeline_mode=pl.Buffered(1)) if count == 1 else {}
        for arr, imap in row_inputs:
            imap = imap if imap is not None else (lambda t: (t, 0))
            in_specs.append(pl.BlockSpec((tm, arr.shape[1]), lambda i, imap=imap, local=local: imap(local(i)), **mode))
            args.append(arr)
        for w, dt in zip(out_widths, out_dtypes):
            out_specs.append(pl.BlockSpec((tm, w), lambda i, local=local: (local(i), 0)))
            out_shape.append(jax.ShapeDtypeStruct((n_rows, w), dt))
        ranges.append((start, count, len(row_inputs), len(out_widths)))
        start += count
    n_row_refs = len(in_specs)
    n_const = len(const_inputs)
    n_outs = len(out_specs)
    const_specs, const_args, scratch_shapes, bind = _const_plan(const_inputs)

    def kern(*refs):
        row_refs, const_refs = refs[:n_row_refs], refs[n_row_refs:n_row_refs + n_const]
        out_refs = refs[n_row_refs + n_const:n_row_refs + n_const + n_outs]
        body_consts, load_weights = bind(const_refs, refs[n_row_refs + n_const + n_outs:])
        i = pl.program_id(0)
        pl.when(i == 0)(load_weights)
        r0 = o0 = 0
        for first, count, n_in, n_out in ranges:
            ins, outs = row_refs[r0:r0 + n_in], out_refs[o0:o0 + n_out]
            r0, o0 = r0 + n_in, o0 + n_out

            @pl.when((i >= first) & (i < first + count))
            def _(ins=ins, outs=outs):
                body(*ins, *body_consts, *outs)

    outs = pl.pallas_call(
        kern,
        grid=(start,),
        in_specs=in_specs + const_specs,
        out_specs=out_specs,
        out_shape=out_shape,
        scratch_shapes=scratch_shapes,
        compiler_params=pltpu.CompilerParams(dimension_semantics=("arbitrary",), vmem_limit_bytes=VMEM_LIMIT),
        name=name,
    )(*args, *const_args)
    grouped, o0 = [], 0
    for _, _, _, n_out in ranges:
        grouped.append(list(outs[o0:o0 + n_out]))
        o0 += n_out
    return grouped


def _gla_prompt_body(q_ref, k_ref, v_ref, la_ref, o_ref, sfin_ref, st_ref):
    t = pl.program_id(1)

    @pl.when(t == 0)
    def _():
        st_ref[...] = jnp.zeros_like(st_ref)

    for base in range(0, q_ref.shape[0], GLA_TILE):
        _gla_tile(slice(base, base + GLA_TILE), q_ref, k_ref, v_ref, la_ref, o_ref, st_ref)

    @pl.when(t == pl.num_programs(1) - 1)
    def _():
        sfin_ref[0] = st_ref[...]


def _gla_tile(tile, q_ref, k_ref, v_ref, la_ref, o_ref, st_ref):
    tg = GLA_TILE
    c_len = GLA_CHUNK
    n_chunks = tg // c_len
    chunk_rows = [slice(ci * c_len, (ci + 1) * c_len) for ci in range(n_chunks)]
    row = lax.broadcasted_iota(jnp.int32, (tg, tg), 0)
    col = lax.broadcasted_iota(jnp.int32, (tg, tg), 1)
    lower_b = (row // c_len == col // c_len) & (col <= row)
    lower = jnp.where(lower_b, 1.0, 0.0).astype(BF16)
    hi, mid, lo = _split3(la_ref[tile, :])
    cum = _mm(lower, hi) + _mm(lower, mid) + _mm(lower, lo)
    tot = jnp.concatenate([jnp.broadcast_to(cum[r.stop - 1:r.stop, :], (c_len, cum.shape[1])) for r in chunk_rows], axis=0)
    k_all = k_ref[tile, :]
    qd = (q_ref[tile, :] * jnp.exp(cum)).astype(BF16)
    ki = (k_all * jnp.exp(-cum)).astype(BF16)
    ke = k_all * jnp.exp(tot - cum)
    lane_chunk = lax.broadcasted_iota(jnp.int32, (GLA_DK_HEAD, tg), 1) // c_len

    att, kv = [], []
    for h in range(GLA_HEADS):
        kc = slice(h * GLA_DK_HEAD, (h + 1) * GLA_DK_HEAD)
        att.append(jnp.where(lower_b, _mm_nt(qd[:, kc], ki[:, kc]), 0.0).astype(BF16))
        ke_t = ke[:, kc].T
        v_h = v_ref[tile, h * GLA_DV_HEAD:(h + 1) * GLA_DV_HEAD]
        kv.append([_mm(jnp.where(lane_chunk == ci, ke_t, 0.0).astype(BF16), v_h) for ci in range(n_chunks)])

    s_before = []
    for h in range(GLA_HEADS):
        kc = slice(h * GLA_DK_HEAD, (h + 1) * GLA_DK_HEAD)
        st = st_ref[h]
        starts = []
        for ci, r in enumerate(chunk_rows):
            starts.append(st.astype(BF16))
            e_col = jnp.exp(cum[r.stop - 8:r.stop, kc]).T[:, 7:8]
            st = e_col * st + kv[h][ci]
        st_ref[h] = st
        s_before.append(starts)

    for h in range(GLA_HEADS):
        kc = slice(h * GLA_DK_HEAD, (h + 1) * GLA_DK_HEAD)
        vc = slice(h * GLA_DV_HEAD, (h + 1) * GLA_DV_HEAD)
        o_intra = _mm(att[h], v_ref[tile, vc])
        for ci, r in enumerate(chunk_rows):
            o_ref[tile.start + r.start:tile.start + r.stop, vc] = o_intra[r] + _mm(qd[r, kc], s_before[h][ci])


def _gla_prompt(q, k, v, la, batch, seq):
    tg = GLA_STEP
    nt = seq // tg
    qk_spec = pl.BlockSpec((tg, GLA_DK), lambda b, t: (b * nt + t, 0))
    v_spec = pl.BlockSpec((tg, GLA_DV), lambda b, t: (b * nt + t, 0))
    st_shape = (GLA_HEADS, GLA_DK_HEAD, GLA_DV_HEAD)
    return pl.pallas_call(
        _gla_prompt_body,
        grid=(batch, nt),
        in_specs=[qk_spec, qk_spec, v_spec, qk_spec],
        out_specs=[v_spec, pl.BlockSpec((1,) + st_shape, lambda b, t: (b, 0, 0, 0))],
        out_shape=[jax.ShapeDtypeStruct((batch * seq, GLA_DV), F32),
                   jax.ShapeDtypeStruct((batch,) + st_shape, F32)],
        scratch_shapes=[pltpu.VMEM(st_shape, F32)],
        compiler_params=pltpu.CompilerParams(dimension_semantics=("arbitrary", "arbitrary"), vmem_limit_bytes=VMEM_LIMIT),
        name="gla_prompt",
    )(q, k, v, la)


def _gla_sample_body(q_ref, k_ref, v_ref, la_ref, s_ref, o_ref, sn_ref):
    bt = q_ref.shape[0]
    for h in range(GLA_HEADS):
        kc = slice(h * GLA_DK_HEAD, (h + 1) * GLA_DK_HEAD)
        vc = slice(h * GLA_DV_HEAD, (h + 1) * GLA_DV_HEAD)
        a_t = jnp.exp(la_ref[:, kc]).T
        k_t = k_ref[:, kc].T
        q_t = q_ref[:, kc].T
        for j in range(bt):
            s_new = a_t[:, j:j + 1] * s_ref[j, h] + k_t[:, j:j + 1] * v_ref[j:j + 1, vc]
            sn_ref[j, h] = s_new
            o_ref[j:j + 1, vc] = jnp.sum(q_t[:, j:j + 1] * s_new, axis=0, keepdims=True)


def _gla_sample(q, k, v, la, state):
    bt = SEQ_TILE
    nb = q.shape[0]
    row = lambda w: pl.BlockSpec((bt, w), lambda i: (i, 0))
    st_spec = pl.BlockSpec((bt, GLA_HEADS, GLA_DK_HEAD, GLA_DV_HEAD), lambda i: (i, 0, 0, 0))
    return pl.pallas_call(
        _gla_sample_body,
        grid=(nb // bt,),
        in_specs=[row(GLA_DK), row(GLA_DK), row(GLA_DV), row(GLA_DK), st_spec],
        out_specs=[row(GLA_DV), st_spec],
        out_shape=[jax.ShapeDtypeStruct((nb, GLA_DV), F32), jax.ShapeDtypeStruct(state.shape, F32)],
        compiler_params=pltpu.CompilerParams(dimension_semantics=("arbitrary",), vmem_limit_bytes=VMEM_LIMIT),
        name="gla_sample",
    )(q, k, v, la, state)


def _swa_attend_units(sink_ref, q_ref, k_full, v_full, has_prev, o_ref):
    w = SWA_WINDOW
    hd = SWA_HEAD_DIM
    tq = q_ref.shape[0]
    nkv = k_full.shape[0]
    lane_q = lax.broadcasted_iota(jnp.int32, (w, LANES), 1) < hd
    lane_kv = lax.broadcasted_iota(jnp.int32, (nkv, LANES), 1) < hd
    i = lax.broadcasted_iota(jnp.int32, (w, 2 * w), 0)
    j = lax.broadcasted_iota(jnp.int32, (w, 2 * w), 1)
    band = jnp.where(j < w, jnp.where(j >= i, 1, 0), jnp.where(j - w <= i, 1, 0))
    band_first = jnp.where(j < w, has_prev, 1) * band
    lane_2w = lax.broadcasted_iota(jnp.int32, (2 * w, LANES), 1) < hd
    ones_lo = jnp.where(lane_2w, 1.0, 0.0).astype(BF16)
    ones_hi = jnp.where(lane_2w, 0.0, 1.0).astype(BF16)
    c2 = (hd ** -0.5) * LOG2E

    k_prep, v_prep = [], []
    for p in range(SWA_KV // LANES):
        cols = slice(p * LANES, (p + 1) * LANES)
        k_p, v_p = k_full[:, cols], v_full[:, cols]
        k_prep.append((k_p.astype(BF16), pltpu.roll(k_p, hd, axis=1).astype(BF16)))
        v_r = pltpu.roll(v_p, hd, axis=1)
        v_prep.append(((jnp.where(lane_kv, v_p, 0.0).astype(BF16), jnp.where(lane_kv, 0.0, v_r).astype(BF16)),
                       (jnp.where(lane_kv, v_r, 0.0).astype(BF16), jnp.where(lane_kv, 0.0, v_p).astype(BF16))))

    def softmax_part(s, hh, mask):
        s2 = jnp.where(mask, s, NEG_BIG)
        sk2 = jnp.full((w, 1), sink_ref[hh], F32) * LOG2E
        m2 = jnp.maximum(jnp.max(s2, axis=-1, keepdims=True), sk2)
        return jnp.exp2(s2 - m2).astype(BF16), sk2 - m2

    for b in range(tq // w):
        rows = slice(b * w, (b + 1) * w)
        krows = slice(b * w, (b + 2) * w)
        mask = (band_first if b == 0 else band) > 0
        for p in range(SWA_KV // LANES):
            q_lo, q_hi = [], []
            for x in range(4):
                q_c = q_ref[rows, (4 * p + x) * LANES:(4 * p + x + 1) * LANES] * c2
                q_lo.append(jnp.where(lane_q, q_c, 0.0).astype(BF16))
                q_hi.append(jnp.where(lane_q, 0.0, q_c).astype(BF16))
            s_self = _mm_nt(jnp.concatenate([q_lo[0], q_lo[1], q_hi[2], q_hi[3]], axis=0), k_prep[p][0][krows])
            s_roll = _mm_nt(jnp.concatenate([q_hi[0], q_hi[1], q_lo[2], q_lo[3]], axis=0), k_prep[p][1][krows])
            for x in range(4):
                c = 4 * p + x
                gh = x // 2
                xr = slice(x * w, (x + 1) * w)
                s_lo, s_hi = (s_self[xr], s_roll[xr]) if gh == 0 else (s_roll[xr], s_self[xr])
                p_lo, d_lo = softmax_part(s_lo, 2 * c, mask)
                p_hi, d_hi = softmax_part(s_hi, 2 * c + 1, mask)
                v_lo, v_hi = v_prep[p][gh]
                rhs = jnp.concatenate([jnp.concatenate([v_lo[krows], ones_lo], axis=1),
                                       jnp.concatenate([v_hi[krows], ones_hi], axis=1)], axis=0)
                ext = _mm(jnp.concatenate([p_lo, p_hi], axis=1), rhs)
                den = ext[:, LANES:] + jnp.exp2(jnp.where(lane_q, d_lo, d_hi))
                o_ref[rows, c * LANES:(c + 1) * LANES] = (ext[:, :LANES] / den).astype(o_ref.dtype)
            yield


def _out_stage(at_bf16, h, wo_ref, bo_ref, gpost_ref, gfpre_ref, wup_ref, wdn_ref, gfpost_ref, between=None):
    m = _mm(at_bf16, wo_ref[...]) + bo_ref[...]
    h1 = h + _rms(m, gpost_ref[...])
    f = _ffn(_rms(h1, gfpre_ref[...]).astype(BF16), wup_ref, wdn_ref, between)
    return h1 + _rms(f, gfpost_ref[...])


N_OUT_CONSTS = 7


def _swa_out_body(batch, nt, bind, sink_ref, q_ref, kc_ref, kp_ref, vc_ref, vp_ref, h_ref, ats_ref, hs_ref, *rest):
    i = pl.program_id(0)
    per_batch = nt + 1
    t = lax.rem(i, per_batch)
    is_prompt = i < batch * per_batch
    y_ref, ys_ref, attn_scr = rest[N_OUT_CONSTS:N_OUT_CONSTS + 3]
    consts, load_weights = bind(rest[:N_OUT_CONSTS], rest[N_OUT_CONSTS + 3:])

    @pl.when(i == 0)
    def _():
        load_weights()
        attn_scr[...] = jnp.zeros_like(attn_scr)

    @pl.when(is_prompt & (t < nt))
    def _():
        at_prev = attn_scr[...]
        k_full = jnp.concatenate([kp_ref[...], kc_ref[...]], axis=0)
        v_full = jnp.concatenate([vp_ref[...], vc_ref[...]], axis=0)
        units = _swa_attend_units(sink_ref, q_ref, k_full, v_full, jnp.minimum(t, 1), attn_scr)
        y_ref[...] = _out_stage(at_prev, h_ref[...], *consts, between=lambda: next(units, None))
        for _ in units:
            pass

    @pl.when(is_prompt & (t == nt))
    def _():
        y_ref[...] = _out_stage(attn_scr[...], h_ref[...], *consts)

    @pl.when(i == batch * per_batch)
    def _():
        ys_ref[...] = _out_stage(ats_ref[...].astype(BF16), hs_ref[...], *consts)


def _swa_out(sinks, q, k, v, h, attn_s, h_s, consts, batch, seq):
    tq = SWA_TILE
    w = SWA_WINDOW
    nt = seq // tq
    per_batch = nt + 1
    n_s = h_s.shape[0]

    def pos(i):
        ii = jnp.minimum(i, batch * per_batch - 1)
        return ii // per_batch, lax.rem(ii, per_batch)

    def att_tile(i):
        b, t = pos(i)
        return (b * nt + jnp.minimum(t, nt - 1), 0)

    def prev_block(i):
        b, t = pos(i)
        return (b * (seq // w) + jnp.maximum(jnp.minimum(t, nt - 1) * (tq // w) - 1, 0), 0)

    def out_tile(i):
        b, t = pos(i)
        return (b * nt + jnp.maximum(t - 1, 0), 0)

    whole = lambda arr: pl.BlockSpec(arr.shape, lambda i: (0, 0))
    assert len(consts) == N_OUT_CONSTS
    const_specs, const_args, weight_scratch, bind = _const_plan(consts)
    return pl.pallas_call(
        functools.partial(_swa_out_body, batch, nt, bind),
        grid=(batch * per_batch + 1,),
        in_specs=[pl.BlockSpec(memory_space=pltpu.SMEM),
                  pl.BlockSpec((tq, SWA_Q), att_tile), pl.BlockSpec((tq, SWA_KV), att_tile),
                  pl.BlockSpec((w, SWA_KV), prev_block), pl.BlockSpec((tq, SWA_KV), att_tile),
                  pl.BlockSpec((w, SWA_KV), prev_block), pl.BlockSpec((tq, D_MODEL), out_tile),
                  whole(attn_s), whole(h_s)] + const_specs,
        out_specs=[pl.BlockSpec((tq, D_MODEL), out_tile), whole(h_s)],
        out_shape=[jax.ShapeDtypeStruct((batch * seq, D_MODEL), F32), jax.ShapeDtypeStruct((n_s, D_MODEL), F32)],
        scratch_shapes=[pltpu.VMEM((tq, SWA_Q), BF16)] + weight_scratch,
        compiler_params=pltpu.CompilerParams(dimension_semantics=("arbitrary",), vmem_limit_bytes=VMEM_LIMIT),
        name="swa_out",
    )(sinks, q, k, k, v, v, h, attn_s, h_s, *const_args)


def _swa_sample_body(sk_ref, q_ref, kn_ref, vn_ref, ck_ref, cv_ref, o_ref, nk_ref, nv_ref):
    bt = q_ref.shape[0]
    w = ck_ref.shape[2]
    hd = SWA_HEAD_DIM
    hgroup = lax.broadcasted_iota(jnp.int32, (SWA_HEADS, 1), 0) // SWA_GROUP
    newest = lax.broadcasted_iota(jnp.int32, (SWA_KV, w), 1) == w - 1
    kn_t = kn_ref[...].T
    vn_t = vn_ref[...].T
    scale = hd ** -0.5
    sk = sk_ref[...]
    groups = [slice(g * hd, (g + 1) * hd) for g in range(SWA_KV_HEADS)]

    def per_head(pieces):
        out = pieces[0]
        for g in range(1, SWA_KV_HEADS):
            out = jnp.where(hgroup == g, pieces[g], out)
        return out

    for j in range(bt):
        nk_ref[j] = jnp.where(newest, kn_t[:, j:j + 1], pltpu.roll(ck_ref[j], w - 1, axis=1))
        nv_ref[j] = jnp.where(newest, vn_t[:, j:j + 1], pltpu.roll(cv_ref[j], w - 1, axis=1))

    s_old, s_new, v_sel = [], [], []
    for j in range(bt):
        q = q_ref[j]
        qb = q.astype(BF16)
        s_old.append(per_head([_mm(qb, ck_ref[j, rows, :].astype(BF16)) for rows in groups]))
        k_sel = per_head([kn_ref[j:j + 1, cols] for cols in groups])
        v_sel.append(per_head([vn_ref[j:j + 1, cols] for cols in groups]))
        s_new.append(jnp.sum(q * k_sel, axis=-1, keepdims=True))
    s_old = jnp.stack(s_old, axis=0) * scale
    s_new = jnp.stack(s_new, axis=0) * scale
    m = jnp.maximum(jnp.maximum(jnp.max(s_old, axis=-1, keepdims=True), s_new), sk)
    p_old = jnp.exp(s_old - m)
    p_new = jnp.exp(s_new - m)
    inv = 1.0 / (jnp.sum(p_old, axis=-1, keepdims=True) + p_new + jnp.exp(sk - m))
    p_old = p_old.astype(BF16)
    for j in range(bt):
        o = per_head([_mm_nt(p_old[j], cv_ref[j, rows, :].astype(BF16)) for rows in groups])
        o_ref[j] = (o + p_new[j] * v_sel[j]) * inv[j]


def _swa_sample(sinks, q3, k_new, v_new, cache_k, cache_v):
    bt = SEQ_TILE
    nb, _, w = cache_k.shape
    assert w == LANES
    row = lambda width: pl.BlockSpec((bt, width), lambda i: (i, 0))
    q_spec = pl.BlockSpec((bt, SWA_HEADS, SWA_HEAD_DIM), lambda i: (i, 0, 0))
    c_spec = pl.BlockSpec((bt, SWA_KV, w), lambda i: (i, 0, 0))
    return pl.pallas_call(
        _swa_sample_body,
        grid=(nb // bt,),
        in_specs=[pl.BlockSpec((SWA_HEADS, 1), lambda i: (0, 0)), q_spec, row(SWA_KV), row(SWA_KV), c_spec, c_spec],
        out_specs=[q_spec, c_spec, c_spec],
        out_shape=[jax.ShapeDtypeStruct((nb, SWA_HEADS, SWA_HEAD_DIM), F32),
                   jax.ShapeDtypeStruct(cache_k.shape, F32), jax.ShapeDtypeStruct(cache_v.shape, F32)],
        compiler_params=pltpu.CompilerParams(dimension_semantics=("arbitrary",)),
        name="swa_sample",
    )(sinks, q3, k_new, v_new, cache_k, cache_v)


def _rope_tables(pos):
    inv = jnp.power(ROPE_THETA, -jnp.arange(ROPE_HALF, dtype=F32) * 2.0 / ROPE_DIM)
    ang = pos.astype(F32)[:, None] * inv[None, :]
    cos, sin = jnp.cos(ang), jnp.sin(ang)
    d = jnp.arange(LANES)[None, :] % SWA_HEAD_DIM
    f = jnp.arange(ROPE_HALF)[:, None]
    first = ((d < ROPE_HALF) & (d == f)).astype(F32)
    second = ((d >= ROPE_HALF) & (d < ROPE_DIM) & (d - ROPE_HALF == f)).astype(F32)
    spread = lambda t, sel: jnp.dot(t, sel, precision=lax.Precision.HIGHEST)
    rc = spread(cos, first + second) + (d >= ROPE_DIM).astype(F32)
    ra = -spread(sin, first)
    rb = spread(sin, second)
    return rc, ra, rb


def kernel(x_prompt, x_sample, state_gla, cache_swa_k, cache_swa_v, gla_w_in, gla_w_gate2, gla_b_gate, gla_g_head, gla_w_out, swa_w_qkv, swa_b_qkv, swa_sinks, swa_w_out, swa_b_out, norm_mix_pre, norm_mix_post, norm_ffn_pre, norm_ffn_post, ffn_w_up, ffn_w_down):
    batch, seq, _ = x_prompt.shape
    dec_batch, dec_seq, _ = x_sample.shape
    assert dec_seq == 1 and seq % SWA_WINDOW == 0
    past_len = seq
    n_p, n_s = batch * seq, dec_batch * dec_seq
    xp = x_prompt.reshape(n_p, D_MODEL)
    xs = x_sample.reshape(n_s, D_MODEL)

    weight = lambda w, layer=0: (w, layer, -(-w.shape[2] // LANES) * LANES)
    w_g2 = jnp.pad(gla_w_gate2[0], ((0, LANES - GLA_GATE_RANK), (0, 0))).astype(BF16)
    b_g = gla_b_gate[0][None, :]
    g_head = gla_g_head[0][None, :]
    b_qkv = swa_b_qkv[0][None, :]
    b_sout = swa_b_out[0][None, :]
    row = lambda t, i: t[i][None, :]

    in0_consts = [row(norm_mix_pre, 0), weight(gla_w_in), w_g2, b_g]
    mid_consts = [g_head, weight(gla_w_out), row(norm_mix_post, 0), row(norm_ffn_pre, 0), weight(ffn_w_up, 0),
                  weight(ffn_w_down, 0), row(norm_ffn_post, 0), row(norm_mix_pre, 1), weight(swa_w_qkv), b_qkv]
    out_consts = [weight(swa_w_out), b_sout, row(norm_mix_post, 1), row(norm_ffn_pre, 1), weight(ffn_w_up, 1),
                  weight(ffn_w_down, 1), row(norm_ffn_post, 1)]
    in0_widths = [GLA_DK, GLA_DK, GLA_DV, GLA_DV, GLA_DK]
    mid_widths = [D_MODEL, SWA_Q, SWA_KV, SWA_KV]

    tm, ts = TOKEN_TILE, n_s
    (q, k, v, r, la), (qs, ks, vs, rs, las) = _tok_call(
        _in0_body,
        [(n_p, tm, [(xp, None)], in0_widths, [F32, F32, BF16, F32, F32]),
         (n_s, ts, [(xs, None)], in0_widths, [F32] * 5)],
        in0_consts, "in0")
    o, s_fin_p = _gla_prompt(q, k, v, la, batch, seq)
    o_s, s_new = _gla_sample(qs, ks, vs, las, state_gla[0])
    tabs = _rope_tables(jnp.arange(seq))
    tabs_s = _rope_tables(jnp.full((n_s,), past_len, jnp.int32))
    tab_map = lambda t: (t % (seq // tm), 0)
    (h2, q1, k1, v1), (h2s, q1s, k1s, v1s) = _tok_call(
        _mid_body,
        [(n_p, tm, [(o, None), (r, None), (xp, None)] + [(t, tab_map) for t in tabs], mid_widths, [F32] * 4),
         (n_s, ts, [(o_s, None), (rs, None), (xs, None)] + [(t, None) for t in tabs_s], mid_widths, [F32] * 4)],
        mid_consts, "mid")
    win = cache_swa_k.shape[2]
    to_t = lambda c: jnp.transpose(c[0].reshape(dec_batch, win, SWA_KV), (0, 2, 1))
    from_t = lambda c: jnp.transpose(c, (0, 2, 1)).reshape(1, dec_batch, win, SWA_KV_HEADS, SWA_HEAD_DIM)
    attn_s, nk, nv = _swa_sample(swa_sinks[0][:, None], q1s.reshape(n_s, SWA_HEADS, SWA_HEAD_DIM), k1s, v1s,
                                 to_t(cache_swa_k), to_t(cache_swa_v))
    y_p, y_s = _swa_out(swa_sinks[0], q1, k1, v1, h2, attn_s.reshape(n_s, SWA_Q), h2s, out_consts, batch, seq)
    wp = min(SWA_WINDOW, seq)
    tail = lambda t: t.reshape(batch, seq, SWA_KV)[:, seq - wp:].reshape(batch, wp, SWA_KV_HEADS, SWA_HEAD_DIM)
    k_tail, v_tail = tail(k1), tail(v1)

    return (y_p.reshape(batch, seq, D_MODEL), y_s.reshape(dec_batch, dec_seq, D_MODEL),
            s_fin_p[None], s_new[None], k_tail[None], v_tail[None], from_t(nk), from_t(nv))
```

```python
import functools

import jax
import jax.numpy as jnp
from jax import lax
from jax.experimental import pallas as pl
from jax.experimental.pallas import tpu as pltpu

F32 = jnp.float32
BF16 = jnp.bfloat16

D_MODEL = 1024
D_FF = 4 * D_MODEL
NORM_EPS = 1e-6

GLA_HEADS = 4
GLA_DK = D_MODEL // 2
GLA_DV = D_MODEL
GLA_DK_HEAD = GLA_DK // GLA_HEADS
GLA_DV_HEAD = GLA_DV // GLA_HEADS
GLA_GATE_RANK = 16
GLA_TAU = 16.0
GLA_CHUNK = 64
GLA_MAIN = 2 * GLA_DK + 2 * GLA_DV

SWA_HEAD_DIM = 64
SWA_HEADS = D_MODEL // SWA_HEAD_DIM
SWA_KV_HEADS = 4
SWA_GROUP = SWA_HEADS // SWA_KV_HEADS
SWA_WINDOW = 128
SWA_Q = SWA_HEADS * SWA_HEAD_DIM
SWA_KV = SWA_KV_HEADS * SWA_HEAD_DIM
SWA_QKV = SWA_Q + 2 * SWA_KV
ROPE_THETA = 500000.0
ROPE_DIM = SWA_HEAD_DIM // 4
ROPE_HALF = ROPE_DIM // 2

LANES = 128
FFN_CHUNK = 512
TOKEN_TILE = 512
GLA_TILE = 256
GLA_STEP = 512
SWA_TILE = 512
SEQ_TILE = 8
VMEM_LIMIT = 56 * 1024 * 1024
NEG_BIG = -1e30
LOG2E = 1.4426950408889634


def _mm(a, b):
    return jnp.dot(a, b, preferred_element_type=F32)


def _mm_nt(a, b):
    return lax.dot_general(a, b, (((1,), (1,)), ((), ())), preferred_element_type=F32)


def _mm_tn(a, b):
    return lax.dot_general(a, b, (((0,), (0,)), ((), ())), preferred_element_type=F32)


def _rms(x, g):
    ms = jnp.mean(x * x, axis=-1, keepdims=True)
    return x * lax.rsqrt(ms + NORM_EPS) * g


def _split3(x):
    hi = x.astype(BF16)
    r1 = x - hi.astype(F32)
    mid = r1.astype(BF16)
    lo = (r1 - mid.astype(F32)).astype(BF16)
    return hi, mid, lo


def _ffn(a_bf16, wup_ref, wdn_ref, between=None):
    acc = None
    for c in range(D_FF // FFN_CHUNK):
        cols = slice(c * FFN_CHUNK, (c + 1) * FFN_CHUNK)
        u = _mm(a_bf16, wup_ref[:, cols])
        u = jnp.square(jnp.maximum(u, 0.0)).astype(BF16)
        p = _mm(u, wdn_ref[cols, :])
        acc = p if acc is None else acc + p
        if between is not None:
            between()
    return acc


def _in0_body(x_ref, g_ref, w_ref, wg_ref, bg_ref, q_ref, k_ref, v_ref, r_ref, la_ref):
    a = _rms(x_ref[...], g_ref[...]).astype(BF16)
    q_ref[...] = _mm(a, w_ref[:, 0:GLA_DK]) * (GLA_DK_HEAD ** -0.5)
    k_ref[...] = _mm(a, w_ref[:, GLA_DK:2 * GLA_DK])
    for c in range(GLA_DV // 512):
        cols = slice(c * 512, (c + 1) * 512)
        v_ref[:, cols] = _mm(a, w_ref[:, 2 * GLA_DK + c * 512:2 * GLA_DK + (c + 1) * 512]).astype(v_ref.dtype)
        r_ref[:, cols] = _mm(a, w_ref[:, 2 * GLA_DK + GLA_DV + c * 512:2 * GLA_DK + GLA_DV + (c + 1) * 512])
    z = _mm(a, w_ref[:, GLA_MAIN:GLA_MAIN + GLA_GATE_RANK]).astype(BF16)
    zg = _mm(z, wg_ref[...]) + bg_ref[...]
    la_ref[...] = (jnp.minimum(zg, 0.0) - jnp.log1p(jnp.exp(-jnp.abs(zg)))) * (1.0 / GLA_TAU)


def _mid_body(o_ref, r_ref, h_ref, rc_ref, ra_ref, rb_ref,
              gh_ref, wo_ref, gpost_ref, gfpre_ref, wup_ref, wdn_ref, gfpost_ref,
              gpre1_ref, wqkv_ref, bqkv_ref,
              h2_ref, q1_ref, k1_ref, v1_ref):
    m = None
    for hh in range(GLA_HEADS):
        cols = slice(hh * GLA_DV_HEAD, (hh + 1) * GLA_DV_HEAD)
        on = _rms(o_ref[:, cols], gh_ref[...])
        r = r_ref[:, cols]
        u = (on * (r * (1.0 / (1.0 + jnp.exp(-r))))).astype(BF16)
        p = _mm(u, wo_ref[cols, :])
        m = p if m is None else m + p
    h1 = h_ref[...] + _rms(m, gpost_ref[...])
    f = _ffn(_rms(h1, gfpre_ref[...]).astype(BF16), wup_ref, wdn_ref)
    h2 = h1 + _rms(f, gfpost_ref[...])
    h2_ref[...] = h2
    a3 = _rms(h2, gpre1_ref[...]).astype(BF16)
    rc, ra, rb = rc_ref[...], ra_ref[...], rb_ref[...]
    wide = 2 * LANES
    for c2 in range((SWA_Q + SWA_KV) // wide):
        x2 = _mm(a3, wqkv_ref[:, c2 * wide:(c2 + 1) * wide]) + bqkv_ref[:, c2 * wide:(c2 + 1) * wide]
        for half in range(2):
            c = 2 * c2 + half
            x = x2[:, half * LANES:(half + 1) * LANES]
            y = x * rc + pltpu.roll(x, LANES - ROPE_HALF, axis=1) * ra + pltpu.roll(x, ROPE_HALF, axis=1) * rb
            if c < SWA_Q // LANES:
                q1_ref[:, c * LANES:(c + 1) * LANES] = y
            else:
                k1_ref[:, c * LANES - SWA_Q:(c + 1) * LANES - SWA_Q] = y
    v1_ref[...] = _mm(a3, wqkv_ref[:, SWA_Q + SWA_KV:SWA_QKV]) + bqkv_ref[:, SWA_Q + SWA_KV:SWA_QKV]


def _const_plan(const_inputs):
    in_specs, args = [], []
    for entry in const_inputs:
        if isinstance(entry, tuple):
            arr, layer = entry
            spec = pl.BlockSpec((None,) + arr.shape[1:], lambda i, layer=layer: (layer, 0, 0), pipeline_mode=pl.Buffered(1))
        else:
            arr = entry
            spec = pl.BlockSpec(arr.shape, lambda i: (0, 0), pipeline_mode=pl.Buffered(1))
        in_specs.append(spec)
        args.append(arr)
    return in_specs, args


def _tok_call(body, groups, const_inputs, name):
    in_specs, args, out_specs, out_shape = [], [], [], []
    ranges, start = [], 0
    for n_rows, tm, row_inputs, out_widths, out_dtypes in groups:
        assert n_rows % tm == 0
        count = n_rows // tm
        local = lambda i, start=start, count=count: jnp.clip(i - start, 0, count - 1)
        mode = dict(pipeline_mode=pl.Buffered(1)) if count == 1 else {}
        for arr, imap in row_inputs:
            imap = imap if imap is not None else (lambda t: (t, 0))
            in_specs.append(pl.BlockSpec((tm, arr.shape[1]), lambda i, imap=imap, local=local: imap(local(i)), **mode))
            args.append(arr)
        for w, dt in zip(out_widths, out_dtypes):
            out_specs.append(pl.BlockSpec((tm, w), lambda i, local=local: (local(i), 0)))
            out_shape.append(jax.ShapeDtypeStruct((n_rows, w), dt))
        ranges.append((start, count, len(row_inputs), len(out_widths)))
        start += count
    n_row_refs = len(in_specs)
    n_const = len(const_inputs)
    const_specs, const_args = _const_plan(const_inputs)

    def kern(*refs):
        row_refs, const_refs = refs[:n_row_refs], refs[n_row_refs:n_row_refs + n_const]
        out_refs = refs[n_row_refs + n_const:]
        i = pl.program_id(0)
        r0 = o0 = 0
        for first, count, n_in, n_out in ranges:
            ins, outs = row_refs[r0:r0 + n_in], out_refs[o0:o0 + n_out]
            r0, o0 = r0 + n_in, o0 + n_out

            @pl.when((i >= first) & (i < first + count))
            def _(ins=ins, outs=outs):
                body(*ins, *const_refs, *outs)

    outs = pl.pallas_call(
        kern,
        grid=(start,),
        in_specs=in_specs + const_specs,
        out_specs=out_specs,
        out_shape=out_shape,
        compiler_params=pltpu.CompilerParams(dimension_semantics=("arbitrary",), vmem_limit_bytes=VMEM_LIMIT),
        name=name,
    )(*args, *const_args)
    grouped, o0 = [], 0
    for _, _, _, n_out in ranges:
        grouped.append(list(outs[o0:o0 + n_out]))
        o0 += n_out
    return grouped


def _gla_prompt_body(q_ref, k_ref, v_ref, la_ref, o_ref, sfin_ref, st_ref):
    t = pl.program_id(1)

    @pl.when(t == 0)
    def _():
        st_ref[...] = jnp.zeros_like(st_ref)

    for base in range(0, q_ref.shape[0], GLA_TILE):
        _gla_tile(slice(base, base + GLA_TILE), q_ref, k_ref, v_ref, la_ref, o_ref, st_ref)

    @pl.when(t == pl.num_programs(1) - 1)
    def _():
        sfin_ref[0] = st_ref[...]


def _gla_tile(tile, q_ref, k_ref, v_ref, la_ref, o_ref, st_ref):
    tg = GLA_TILE
    c_len = GLA_CHUNK
    n_chunks = tg // c_len
    chunk_rows = [slice(ci * c_len, (ci + 1) * c_len) for ci in range(n_chunks)]
    row = lax.broadcasted_iota(jnp.int32, (tg, tg), 0)
    col = lax.broadcasted_iota(jnp.int32, (tg, tg), 1)
    lower_b = (row // c_len == col // c_len) & (col <= row)
    lower = jnp.where(lower_b, 1.0, 0.0).astype(BF16)
    hi, mid, lo = _split3(la_ref[tile, :])
    cum = _mm(lower, hi) + _mm(lower, mid) + _mm(lower, lo)
    tot = jnp.concatenate([jnp.broadcast_to(cum[r.stop - 1:r.stop, :], (c_len, cum.shape[1])) for r in chunk_rows], axis=0)
    k_all = k_ref[tile, :]
    qd = (q_ref[tile, :] * jnp.exp(cum)).astype(BF16)
    ki = (k_all * jnp.exp(-cum)).astype(BF16)
    ke = k_all * jnp.exp(tot - cum)
    lane_chunk = lax.broadcasted_iota(jnp.int32, (GLA_DK_HEAD, tg), 1) // c_len

    att, kv = [], []
    for h in range(GLA_HEADS):
        kc = slice(h * GLA_DK_HEAD, (h + 1) * GLA_DK_HEAD)
        att.append(jnp.where(lower_b, _mm_nt(qd[:, kc], ki[:, kc]), 0.0).astype(BF16))
        ke_t = ke[:, kc].T
        v_h = v_ref[tile, h * GLA_DV_HEAD:(h + 1) * GLA_DV_HEAD]
        kv.append([_mm(jnp.where(lane_chunk == ci, ke_t, 0.0).astype(BF16), v_h) for ci in range(n_chunks)])

    s_before = []
    for h in range(GLA_HEADS):
        kc = slice(h * GLA_DK_HEAD, (h + 1) * GLA_DK_HEAD)
        st = st_ref[h]
        starts = []
        for ci, r in enumerate(chunk_rows):
            starts.append(st.astype(BF16))
            e_col = jnp.exp(cum[r.stop - 8:r.stop, kc]).T[:, 7:8]
            st = e_col * st + kv[h][ci]
        st_ref[h] = st
        s_before.append(starts)

    for h in range(GLA_HEADS):
        kc = slice(h * GLA_DK_HEAD, (h + 1) * GLA_DK_HEAD)
        vc = slice(h * GLA_DV_HEAD, (h + 1) * GLA_DV_HEAD)
        o_intra = _mm(att[h], v_ref[tile, vc])
        for ci, r in enumerate(chunk_rows):
            o_ref[tile.start + r.start:tile.start + r.stop, vc] = o_intra[r] + _mm(qd[r, kc], s_before[h][ci])


def _gla_prompt(q, k, v, la, batch, seq):
    tg = GLA_STEP
    nt = seq // tg
    qk_spec = pl.BlockSpec((tg, GLA_DK), lambda b, t: (b * nt + t, 0))
    v_spec = pl.BlockSpec((tg, GLA_DV), lambda b, t: (b * nt + t, 0))
    st_shape = (GLA_HEADS, GLA_DK_HEAD, GLA_DV_HEAD)
    return pl.pallas_call(
        _gla_prompt_body,
        grid=(batch, nt),
        in_specs=[qk_spec, qk_spec, v_spec, qk_spec],
        out_specs=[v_spec, pl.BlockSpec((1,) + st_shape, lambda b, t: (b, 0, 0, 0))],
        out_shape=[jax.ShapeDtypeStruct((batch * seq, GLA_DV), F32),
                   jax.ShapeDtypeStruct((batch,) + st_shape, F32)],
        scratch_shapes=[pltpu.VMEM(st_shape, F32)],
        compiler_params=pltpu.CompilerParams(dimension_semantics=("arbitrary", "arbitrary"), vmem_limit_bytes=VMEM_LIMIT),
        name="gla_prompt",
    )(q, k, v, la)


def _gla_sample_body(q_ref, k_ref, v_ref, la_ref, s_ref, o_ref, sn_ref):
    bt = q_ref.shape[0]
    for h in range(GLA_HEADS):
        kc = slice(h * GLA_DK_HEAD, (h + 1) * GLA_DK_HEAD)
        vc = slice(h * GLA_DV_HEAD, (h + 1) * GLA_DV_HEAD)
        a_t = jnp.exp(la_ref[:, kc]).T
        k_t = k_ref[:, kc].T
        q_t = q_ref[:, kc].T
        for j in range(bt):
            s_new = a_t[:, j:j + 1] * s_ref[j, h] + k_t[:, j:j + 1] * v_ref[j:j + 1, vc]
            sn_ref[j, h] = s_new
            o_ref[j:j + 1, vc] = jnp.sum(q_t[:, j:j + 1] * s_new, axis=0, keepdims=True)


def _gla_sample(q, k, v, la, state):
    bt = SEQ_TILE
    nb = q.shape[0]
    row = lambda w: pl.BlockSpec((bt, w), lambda i: (i, 0))
    st_spec = pl.BlockSpec((bt, GLA_HEADS, GLA_DK_HEAD, GLA_DV_HEAD), lambda i: (i, 0, 0, 0))
    return pl.pallas_call(
        _gla_sample_body,
        grid=(nb // bt,),
        in_specs=[row(GLA_DK), row(GLA_DK), row(GLA_DV), row(GLA_DK), st_spec],
        out_specs=[row(GLA_DV), st_spec],
        out_shape=[jax.ShapeDtypeStruct((nb, GLA_DV), F32), jax.ShapeDtypeStruct(state.shape, F32)],
        compiler_params=pltpu.CompilerParams(dimension_semantics=("arbitrary",), vmem_limit_bytes=VMEM_LIMIT),
        name="gla_sample",
    )(q, k, v, la, state)


def _swa_attend_units(sink_ref, q_ref, k_full, v_full, has_prev, o_ref):
    w = SWA_WINDOW
    hd = SWA_HEAD_DIM
    tq = q_ref.shape[0]
    nkv = k_full.shape[0]
    lane_q = lax.broadcasted_iota(jnp.int32, (w, LANES), 1) < hd
    lane_kv = lax.broadcasted_iota(jnp.int32, (nkv, LANES), 1) < hd
    i = lax.broadcasted_iota(jnp.int32, (w, 2 * w), 0)
    j = lax.broadcasted_iota(jnp.int32, (w, 2 * w), 1)
    band = jnp.where(j < w, jnp.where(j >= i, 1, 0), jnp.where(j - w <= i, 1, 0))
    band_first = jnp.where(j < w, has_prev, 1) * band
    lane_2w = lax.broadcasted_iota(jnp.int32, (2 * w, LANES), 1) < hd
    ones_lo = jnp.where(lane_2w, 1.0, 0.0).astype(BF16)
    ones_hi = jnp.where(lane_2w, 0.0, 1.0).astype(BF16)
    c2 = (hd ** -0.5) * LOG2E

    k_prep, v_prep = [], []
    for p in range(SWA_KV // LANES):
        cols = slice(p * LANES, (p + 1) * LANES)
        k_p, v_p = k_full[:, cols], v_full[:, cols]
        k_prep.append((k_p.astype(BF16), pltpu.roll(k_p, hd, axis=1).astype(BF16)))
        v_r = pltpu.roll(v_p, hd, axis=1)
        v_prep.append(((jnp.where(lane_kv, v_p, 0.0).astype(BF16), jnp.where(lane_kv, 0.0, v_r).astype(BF16)),
                       (jnp.where(lane_kv, v_r, 0.0).astype(BF16), jnp.where(lane_kv, 0.0, v_p).astype(BF16))))

    def softmax_part(s, hh, mask):
        s2 = jnp.where(mask, s, NEG_BIG)
        sk2 = jnp.full((w, 1), sink_ref[hh], F32) * LOG2E
        m2 = jnp.maximum(jnp.max(s2, axis=-1, keepdims=True), sk2)
        return jnp.exp2(s2 - m2).astype(BF16), sk2 - m2

    for b in range(tq // w):
        rows = slice(b * w, (b + 1) * w)
        krows = slice(b * w, (b + 2) * w)
        mask = (band_first if b == 0 else band) > 0
        for p in range(SWA_KV // LANES):
            q_lo, q_hi = [], []
            for x in range(4):
                q_c = q_ref[rows, (4 * p + x) * LANES:(4 * p + x + 1) * LANES] * c2
                q_lo.append(jnp.where(lane_q, q_c, 0.0).astype(BF16))
                q_hi.append(jnp.where(lane_q, 0.0, q_c).astype(BF16))
            s_self = _mm_nt(jnp.concatenate([q_lo[0], q_lo[1], q_hi[2], q_hi[3]], axis=0), k_prep[p][0][krows])
            s_roll = _mm_nt(jnp.concatenate([q_hi[0], q_hi[1], q_lo[2], q_lo[3]], axis=0), k_prep[p][1][krows])
            for x in range(4):
                c = 4 * p + x
                gh = x // 2
                xr = slice(x * w, (x + 1) * w)
                s_lo, s_hi = (s_self[xr], s_roll[xr]) if gh == 0 else (s_roll[xr], s_self[xr])
                p_lo, d_lo = softmax_part(s_lo, 2 * c, mask)
                p_hi, d_hi = softmax_part(s_hi, 2 * c + 1, mask)
                v_lo, v_hi = v_prep[p][gh]
                rhs = jnp.concatenate([jnp.concatenate([v_lo[krows], ones_lo], axis=1),
                                       jnp.concatenate([v_hi[krows], ones_hi], axis=1)], axis=0)
                ext = _mm(jnp.concatenate([p_lo, p_hi], axis=1), rhs)
                den = ext[:, LANES:] + jnp.exp2(jnp.where(lane_q, d_lo, d_hi))
                o_ref[rows, c * LANES:(c + 1) * LANES] = (ext[:, :LANES] / den).astype(o_ref.dtype)
            yield


def _out_stage(at_bf16, h, wo_ref, bo_ref, gpost_ref, gfpre_ref, wup_ref, wdn_ref, gfpost_ref, between=None):
    m = _mm(at_bf16, wo_ref[...]) + bo_ref[...]
    h1 = h + _rms(m, gpost_ref[...])
    f = _ffn(_rms(h1, gfpre_ref[...]).astype(BF16), wup_ref, wdn_ref, between)
    return h1 + _rms(f, gfpost_ref[...])


def _swa_out_body(batch, nt, sink_ref, q_ref, kc_ref, kp_ref, vc_ref, vp_ref, h_ref, ats_ref, hs_ref,
                  wo_ref, bo_ref, gpost_ref, gfpre_ref, wup_ref, wdn_ref, gfpost_ref,
                  y_ref, ys_ref, attn_scr):
    i = pl.program_id(0)
    per_batch = nt + 1
    t = lax.rem(i, per_batch)
    is_prompt = i < batch * per_batch
    consts = (wo_ref, bo_ref, gpost_ref, gfpre_ref, wup_ref, wdn_ref, gfpost_ref)

    @pl.when(i == 0)
    def _():
        attn_scr[...] = jnp.zeros_like(attn_scr)

    @pl.when(is_prompt & (t < nt))
    def _():
        at_prev = attn_scr[...]
        k_full = jnp.concatenate([kp_ref[...], kc_ref[...]], axis=0)
        v_full = jnp.concatenate([vp_ref[...], vc_ref[...]], axis=0)
        units = _swa_attend_units(sink_ref, q_ref, k_full, v_full, jnp.minimum(t, 1), attn_scr)
        y_ref[...] = _out_stage(at_prev, h_ref[...], *consts, between=lambda: next(units, None))
        for _ in units:
            pass

    @pl.when(is_prompt & (t == nt))
    def _():
        y_ref[...] = _out_stage(attn_scr[...], h_ref[...], *consts)

    @pl.when(i == batch * per_batch)
    def _():
        ys_ref[...] = _out_stage(ats_ref[...].astype(BF16), hs_ref[...], *consts)


def _swa_out(sinks, q, k, v, h, attn_s, h_s, consts, batch, seq):
    tq = SWA_TILE
    w = SWA_WINDOW
    nt = seq // tq
    per_batch = nt + 1
    n_s = h_s.shape[0]

    def pos(i):
        ii = jnp.minimum(i, batch * per_batch - 1)
        return ii // per_batch, lax.rem(ii, per_batch)

    def att_tile(i):
        b, t = pos(i)
        return (b * nt + jnp.minimum(t, nt - 1), 0)

    def prev_block(i):
        b, t = pos(i)
        return (b * (seq // w) + jnp.maximum(jnp.minimum(t, nt - 1) * (tq // w) - 1, 0), 0)

    def out_tile(i):
        b, t = pos(i)
        return (b * nt + jnp.maximum(t - 1, 0), 0)

    whole = lambda arr: pl.BlockSpec(arr.shape, lambda i: (0, 0))
    const_specs, const_args = _const_plan(consts)
    return pl.pallas_call(
        functools.partial(_swa_out_body, batch, nt),
        grid=(batch * per_batch + 1,),
        in_specs=[pl.BlockSpec(memory_space=pltpu.SMEM),
                  pl.BlockSpec((tq, SWA_Q), att_tile), pl.BlockSpec((tq, SWA_KV), att_tile),
                  pl.BlockSpec((w, SWA_KV), prev_block), pl.BlockSpec((tq, SWA_KV), att_tile),
                  pl.BlockSpec((w, SWA_KV), prev_block), pl.BlockSpec((tq, D_MODEL), out_tile),
                  whole(attn_s), whole(h_s)] + const_specs,
        out_specs=[pl.BlockSpec((tq, D_MODEL), out_tile), whole(h_s)],
        out_shape=[jax.ShapeDtypeStruct((batch * seq, D_MODEL), F32), jax.ShapeDtypeStruct((n_s, D_MODEL), F32)],
        scratch_shapes=[pltpu.VMEM((tq, SWA_Q), BF16)],
        compiler_params=pltpu.CompilerParams(dimension_semantics=("arbitrary",), vmem_limit_bytes=VMEM_LIMIT),
        name="swa_out",
    )(sinks, q, k, k, v, v, h, attn_s, h_s, *const_args)


def _swa_sample_body(sk_ref, q_ref, kn_ref, vn_ref, ck_ref, cv_ref, o_ref, nk_ref, nv_ref):
    bt = q_ref.shape[0]
    w = ck_ref.shape[2]
    hd = SWA_HEAD_DIM
    hgroup = lax.broadcasted_iota(jnp.int32, (SWA_HEADS, 1), 0) // SWA_GROUP
    newest = lax.broadcasted_iota(jnp.int32, (SWA_KV, w), 1) == w - 1
    kn_t = kn_ref[...].T
    vn_t = vn_ref[...].T
    scale = hd ** -0.5
    sk = sk_ref[...]
    groups = [slice(g * hd, (g + 1) * hd) for g in range(SWA_KV_HEADS)]

    def per_head(pieces):
        out = pieces[0]
        for g in range(1, SWA_KV_HEADS):
            out = jnp.where(hgroup == g, pieces[g], out)
        return out

    for j in range(bt):
        nk_ref[j] = jnp.where(newest, kn_t[:, j:j + 1], pltpu.roll(ck_ref[j], w - 1, axis=1))
        nv_ref[j] = jnp.where(newest, vn_t[:, j:j + 1], pltpu.roll(cv_ref[j], w - 1, axis=1))

    s_old, s_new, v_sel = [], [], []
    for j in range(bt):
        q = q_ref[j]
        qb = q.astype(BF16)
        s_old.append(per_head([_mm(qb, ck_ref[j, rows, :].astype(BF16)) for rows in groups]))
        k_sel = per_head([kn_ref[j:j + 1, cols] for cols in groups])
        v_sel.append(per_head([vn_ref[j:j + 1, cols] for cols in groups]))
        s_new.append(jnp.sum(q * k_sel, axis=-1, keepdims=True))
    s_old = jnp.stack(s_old, axis=0) * scale
    s_new = jnp.stack(s_new, axis=0) * scale
    m = jnp.maximum(jnp.maximum(jnp.max(s_old, axis=-1, keepdims=True), s_new), sk)
    p_old = jnp.exp(s_old - m)
    p_new = jnp.exp(s_new - m)
    inv = 1.0 / (jnp.sum(p_old, axis=-1, keepdims=True) + p_new + jnp.exp(sk - m))
    p_old = p_old.astype(BF16)
    for j in range(bt):
        o = per_head([_mm_nt(p_old[j], cv_ref[j, rows, :].astype(BF16)) for rows in groups])
        o_ref[j] = (o + p_new[j] * v_sel[j]) * inv[j]


def _swa_sample(sinks, q3, k_new, v_new, cache_k, cache_v):
    bt = SEQ_TILE
    nb, _, w = cache_k.shape
    assert w == LANES
    row = lambda width: pl.BlockSpec((bt, width), lambda i: (i, 0))
    q_spec = pl.BlockSpec((bt, SWA_HEADS, SWA_HEAD_DIM), lambda i: (i, 0, 0))
    c_spec = pl.BlockSpec((bt, SWA_KV, w), lambda i: (i, 0, 0))
    return pl.pallas_call(
        _swa_sample_body,
        grid=(nb // bt,),
        in_specs=[pl.BlockSpec((SWA_HEADS, 1), lambda i: (0, 0)), q_spec, row(SWA_KV), row(SWA_KV), c_spec, c_spec],
        out_specs=[q_spec, c_spec, c_spec],
        out_shape=[jax.ShapeDtypeStruct((nb, SWA_HEADS, SWA_HEAD_DIM), F32),
                   jax.ShapeDtypeStruct(cache_k.shape, F32), jax.ShapeDtypeStruct(cache_v.shape, F32)],
        compiler_params=pltpu.CompilerParams(dimension_semantics=("arbitrary",)),
        name="swa_sample",
    )(sinks, q3, k_new, v_new, cache_k, cache_v)


def _rope_tables(pos):
    inv = jnp.power(ROPE_THETA, -jnp.arange(ROPE_HALF, dtype=F32) * 2.0 / ROPE_DIM)
    ang = pos.astype(F32)[:, None] * inv[None, :]
    cos, sin = jnp.cos(ang), jnp.sin(ang)
    n = pos.shape[0]
    rest = SWA_HEAD_DIM - ROPE_DIM
    rc = jnp.concatenate([cos, cos, jnp.ones((n, rest), F32)], axis=-1)
    ra = jnp.concatenate([-sin, jnp.zeros((n, ROPE_HALF + rest), F32)], axis=-1)
    rb = jnp.concatenate([jnp.zeros((n, ROPE_HALF), F32), sin, jnp.zeros((n, rest), F32)], axis=-1)
    reps = LANES // SWA_HEAD_DIM
    return tuple(jnp.tile(t, (1, reps)) for t in (rc, ra, rb))


def kernel(x_prompt, x_sample, state_gla, cache_swa_k, cache_swa_v, gla_w_in, gla_w_gate2, gla_b_gate, gla_g_head, gla_w_out, swa_w_qkv, swa_b_qkv, swa_sinks, swa_w_out, swa_b_out, norm_mix_pre, norm_mix_post, norm_ffn_pre, norm_ffn_post, ffn_w_up, ffn_w_down):
    batch, seq, _ = x_prompt.shape
    dec_batch, dec_seq, _ = x_sample.shape
    assert dec_seq == 1 and seq % SWA_WINDOW == 0
    past_len = seq
    n_p, n_s = batch * seq, dec_batch * dec_seq
    xp = x_prompt.reshape(n_p, D_MODEL)
    xs = x_sample.reshape(n_s, D_MODEL)

    w_in = gla_w_in[0].astype(BF16)
    w_g2 = gla_w_gate2[0].astype(BF16)
    b_g = gla_b_gate[0][None, :]
    g_head = gla_g_head[0][None, :]
    w_gout = gla_w_out[0].astype(BF16)
    w_qkv = swa_w_qkv[0].astype(BF16)
    b_qkv = swa_b_qkv[0][None, :]
    w_sout = swa_w_out[0].astype(BF16)
    b_sout = swa_b_out[0][None, :]
    w_up = ffn_w_up.astype(BF16)
    w_dn = ffn_w_down.astype(BF16)
    row = lambda t, i: t[i][None, :]

    in0_consts = [row(norm_mix_pre, 0), w_in, w_g2, b_g]
    mid_consts = [g_head, w_gout, row(norm_mix_post, 0), row(norm_ffn_pre, 0), (w_up, 0), (w_dn, 0), row(norm_ffn_post, 0),
                  row(norm_mix_pre, 1), w_qkv, b_qkv]
    out_consts = [w_sout, b_sout, row(norm_mix_post, 1), row(norm_ffn_pre, 1), (w_up, 1), (w_dn, 1), row(norm_ffn_post, 1)]
    in0_widths = [GLA_DK, GLA_DK, GLA_DV, GLA_DV, GLA_DK]
    mid_widths = [D_MODEL, SWA_Q, SWA_KV, SWA_KV]

    tm, ts = TOKEN_TILE, n_s
    (q, k, v, r, la), (qs, ks, vs, rs, las) = _tok_call(
        _in0_body,
        [(n_p, tm, [(xp, None)], in0_widths, [F32, F32, BF16, F32, F32]),
         (n_s, ts, [(xs, None)], in0_widths, [F32] * 5)],
        in0_consts, "in0")
    o, s_fin_p = _gla_prompt(q, k, v, la, batch, seq)
    o_s, s_new = _gla_sample(qs, ks, vs, las, state_gla[0])
    tabs = _rope_tables(jnp.arange(seq))
    tabs_s = _rope_tables(jnp.full((n_s,), past_len, jnp.int32))
    tab_map = lambda t: (t % (seq // tm), 0)
    (h2, q1, k1, v1), (h2s, q1s, k1s, v1s) = _tok_call(
        _mid_body,
        [(n_p, tm, [(o, None), (r, None), (xp, None)] + [(t, tab_map) for t in tabs], mid_widths, [F32] * 4),
         (n_s, ts, [(o_s, None), (rs, None), (xs, None)] + [(t, None) for t in tabs_s], mid_widths, [F32] * 4)],
        mid_consts, "mid")
    win = cache_swa_k.shape[2]
    to_t = lambda c: jnp.transpose(c[0].reshape(dec_batch, win, SWA_KV), (0, 2, 1))
    from_t = lambda c: jnp.transpose(c, (0, 2, 1)).reshape(1, dec_batch, win, SWA_KV_HEADS, SWA_HEAD_DIM)
    attn_s, nk, nv = _swa_sample(swa_sinks[0][:, None], q1s.reshape(n_s, SWA_HEADS, SWA_HEAD_DIM), k1s, v1s,
                                 to_t(cache_swa_k), to_t(cache_swa_v))
    y_p, y_s = _swa_out(swa_sinks[0], q1, k1, v1, h2, attn_s.reshape(n_s, SWA_Q), h2s, out_consts, batch, seq)
    wp = min(SWA_WINDOW, seq)
    tail = lambda t: t.reshape(batch, seq, SWA_KV)[:, seq - wp:].reshape(batch, wp, SWA_KV_HEADS, SWA_HEAD_DIM)
    k_tail, v_tail = tail(k1), tail(v1)

    return (y_p.reshape(batch, seq, D_MODEL), y_s.reshape(dec_batch, dec_seq, D_MODEL),
            s_fin_p[None], s_new[None], k_tail[None], v_tail[None], from_t(nk), from_t(nv))
```

```python
import functools

import jax
import jax.numpy as jnp
from jax import lax
from jax.experimental import pallas as pl
from jax.experimental.pallas import tpu as pltpu

F32 = jnp.float32
BF16 = jnp.bfloat16

D_MODEL = 1024
D_FF = 4 * D_MODEL
NORM_EPS = 1e-6

GLA_HEADS = 4
GLA_DK = D_MODEL // 2
GLA_DV = D_MODEL
GLA_DK_HEAD = GLA_DK // GLA_HEADS
GLA_DV_HEAD = GLA_DV // GLA_HEADS
GLA_GATE_RANK = 16
GLA_TAU = 16.0
GLA_CHUNK = 64
GLA_MAIN = 2 * GLA_DK + 2 * GLA_DV

SWA_HEAD_DIM = 64
SWA_HEADS = D_MODEL // SWA_HEAD_DIM
SWA_KV_HEADS = 4
SWA_GROUP = SWA_HEADS // SWA_KV_HEADS
SWA_WINDOW = 128
SWA_Q = SWA_HEADS * SWA_HEAD_DIM
SWA_KV = SWA_KV_HEADS * SWA_HEAD_DIM
SWA_QKV = SWA_Q + 2 * SWA_KV
ROPE_THETA = 500000.0
ROPE_DIM = SWA_HEAD_DIM // 4
ROPE_HALF = ROPE_DIM // 2

LANES = 128
FFN_CHUNK = 512
TOKEN_TILE = 512
GLA_TILE = 256
GLA_STEP = 512
SWA_TILE = 512
SEQ_TILE = 8
STATE_TILE = 16
CAST_STEPS = 8
VMEM_LIMIT = 56 * 1024 * 1024
NEG_BIG = -1e30
LOG2E = 1.4426950408889634


def _mm(a, b):
    return jnp.dot(a, b, preferred_element_type=F32)


def _mm_nt(a, b):
    return lax.dot_general(a, b, (((1,), (1,)), ((), ())), preferred_element_type=F32)


def _mm_tn(a, b):
    return lax.dot_general(a, b, (((0,), (0,)), ((), ())), preferred_element_type=F32)


def _rms(x, g):
    ms = jnp.mean(x * x, axis=-1, keepdims=True)
    return x * lax.rsqrt(ms + NORM_EPS) * g


def _split3(x):
    hi = x.astype(BF16)
    r1 = x - hi.astype(F32)
    mid = r1.astype(BF16)
    lo = (r1 - mid.astype(F32)).astype(BF16)
    return hi, mid, lo


def _ffn(a_bf16, wup_ref, wdn_ref, between=None):
    acc = None
    for c in range(D_FF // FFN_CHUNK):
        cols = slice(c * FFN_CHUNK, (c + 1) * FFN_CHUNK)
        u = _mm(a_bf16, wup_ref[:, cols])
        u = jnp.square(jnp.maximum(u, 0.0)).astype(BF16)
        p = _mm(u, wdn_ref[cols, :])
        acc = p if acc is None else acc + p
        if between is not None:
            between()
    return acc


def _in0_body(x_ref, g_ref, w_ref, wg_ref, bg_ref, q_ref, k_ref, v_ref, r_ref, la_ref):
    a = _rms(x_ref[...], g_ref[...]).astype(BF16)
    q_ref[...] = _mm(a, w_ref[:, 0:GLA_DK]) * (GLA_DK_HEAD ** -0.5)
    k_ref[...] = _mm(a, w_ref[:, GLA_DK:2 * GLA_DK])
    for c in range(GLA_DV // 512):
        cols = slice(c * 512, (c + 1) * 512)
        v_ref[:, cols] = _mm(a, w_ref[:, 2 * GLA_DK + c * 512:2 * GLA_DK + (c + 1) * 512]).astype(v_ref.dtype)
        r_ref[:, cols] = _mm(a, w_ref[:, 2 * GLA_DK + GLA_DV + c * 512:2 * GLA_DK + GLA_DV + (c + 1) * 512])
    z = _mm(a, w_ref[:, GLA_MAIN:GLA_MAIN + GLA_GATE_RANK]).astype(BF16)
    zg = _mm(z, wg_ref[...]) + bg_ref[...]
    la_ref[...] = (jnp.minimum(zg, 0.0) - jnp.log1p(jnp.exp(-jnp.abs(zg)))) * (1.0 / GLA_TAU)


def _mid_body(o_ref, r_ref, h_ref, rc_ref, ra_ref, rb_ref,
              gh_ref, wo_ref, gpost_ref, gfpre_ref, wup_ref, wdn_ref, gfpost_ref,
              gpre1_ref, wqkv_ref, bqkv_ref,
              h2_ref, q1_ref, k1_ref, v1_ref):
    m = None
    for hh in range(GLA_HEADS):
        cols = slice(hh * GLA_DV_HEAD, (hh + 1) * GLA_DV_HEAD)
        on = _rms(o_ref[:, cols], gh_ref[...])
        r = r_ref[:, cols]
        u = (on * (r * (1.0 / (1.0 + jnp.exp(-r))))).astype(BF16)
        p = _mm(u, wo_ref[cols, :])
        m = p if m is None else m + p
    h1 = h_ref[...] + _rms(m, gpost_ref[...])
    f = _ffn(_rms(h1, gfpre_ref[...]).astype(BF16), wup_ref, wdn_ref)
    h2 = h1 + _rms(f, gfpost_ref[...])
    h2_ref[...] = h2
    a3 = _rms(h2, gpre1_ref[...]).astype(BF16)
    rc, ra, rb = rc_ref[...], ra_ref[...], rb_ref[...]
    wide = 2 * LANES
    for c2 in range((SWA_Q + SWA_KV) // wide):
        x2 = _mm(a3, wqkv_ref[:, c2 * wide:(c2 + 1) * wide]) + bqkv_ref[:, c2 * wide:(c2 + 1) * wide]
        for half in range(2):
            c = 2 * c2 + half
            x = x2[:, half * LANES:(half + 1) * LANES]
            y = x * rc + pltpu.roll(x, LANES - ROPE_HALF, axis=1) * ra + pltpu.roll(x, ROPE_HALF, axis=1) * rb
            if c < SWA_Q // LANES:
                q1_ref[:, c * LANES:(c + 1) * LANES] = y
            else:
                k1_ref[:, c * LANES - SWA_Q:(c + 1) * LANES - SWA_Q] = y
    v1_ref[...] = _mm(a3, wqkv_ref[:, SWA_Q + SWA_KV:SWA_QKV]) + bqkv_ref[:, SWA_Q + SWA_KV:SWA_QKV]


def _cast_body(*refs):
    n = len(refs) // 2
    for x_ref, o_ref in zip(refs[:n], refs[n:]):
        o_ref[...] = x_ref[...].astype(BF16)


def _cast_weights(weights):
    flat = [w.reshape(-1, w.shape[-1]) for w in weights]
    specs = [pl.BlockSpec((f.shape[0] // CAST_STEPS, f.shape[1]), lambda i: (i, 0)) for f in flat]
    outs = pl.pallas_call(
        _cast_body,
        grid=(CAST_STEPS,),
        in_specs=specs,
        out_specs=specs,
        out_shape=[jax.ShapeDtypeStruct(f.shape, BF16) for f in flat],
        compiler_params=pltpu.CompilerParams(dimension_semantics=("arbitrary",), vmem_limit_bytes=VMEM_LIMIT),
        name="cast_weights",
    )(*flat)
    return [o.reshape(w.shape) for o, w in zip(outs, weights)]


def _const_plan(const_inputs):
    in_specs, args = [], []
    for entry in const_inputs:
        if isinstance(entry, tuple):
            arr, layer = entry
            spec = pl.BlockSpec((None,) + arr.shape[1:], lambda i, layer=layer: (layer, 0, 0), pipeline_mode=pl.Buffered(1))
        else:
            arr = entry
            spec = pl.BlockSpec(arr.shape, lambda i: (0, 0), pipeline_mode=pl.Buffered(1))
        in_specs.append(spec)
        args.append(arr)
    return in_specs, args


def _tok_call(body, groups, const_inputs, name):
    in_specs, args, out_specs, out_shape = [], [], [], []
    ranges, start = [], 0
    for n_rows, tm, row_inputs, out_widths, out_dtypes in groups:
        assert n_rows % tm == 0
        count = n_rows // tm
        local = lambda i, start=start, count=count: jnp.clip(i - start, 0, count - 1)
        mode = dict(pipeline_mode=pl.Buffered(1)) if count == 1 else {}
        for arr, imap in row_inputs:
            imap = imap if imap is not None else (lambda t: (t, 0))
            in_specs.append(pl.BlockSpec((tm, arr.shape[1]), lambda i, imap=imap, local=local: imap(local(i)), **mode))
            args.append(arr)
        for w, dt in zip(out_widths, out_dtypes):
            out_specs.append(pl.BlockSpec((tm, w), lambda i, local=local: (local(i), 0)))
            out_shape.append(jax.ShapeDtypeStruct((n_rows, w), dt))
        ranges.append((start, count, len(row_inputs), len(out_widths)))
        start += count
    n_row_refs = len(in_specs)
    n_const = len(const_inputs)
    const_specs, const_args = _const_plan(const_inputs)

    def kern(*refs):
        row_refs, const_refs = refs[:n_row_refs], refs[n_row_refs:n_row_refs + n_const]
        out_refs = refs[n_row_refs + n_const:]
        i = pl.program_id(0)
        r0 = o0 = 0
        for first, count, n_in, n_out in ranges:
            ins, outs = row_refs[r0:r0 + n_in], out_refs[o0:o0 + n_out]
            r0, o0 = r0 + n_in, o0 + n_out

            @pl.when((i >= first) & (i < first + count))
            def _(ins=ins, outs=outs):
                body(*ins, *const_refs, *outs)

    outs = pl.pallas_call(
        kern,
        grid=(start,),
        in_specs=in_specs + const_specs,
        out_specs=out_specs,
        out_shape=out_shape,
        compiler_params=pltpu.CompilerParams(dimension_semantics=("arbitrary",), vmem_limit_bytes=VMEM_LIMIT),
        name=name,
    )(*args, *const_args)
    grouped, o0 = [], 0
    for _, _, _, n_out in ranges:
        grouped.append(list(outs[o0:o0 + n_out]))
        o0 += n_out
    return grouped


def _gla_prompt_body(q_ref, k_ref, v_ref, la_ref, o_ref, sfin_ref, st_ref):
    t = pl.program_id(1)

    @pl.when(t == 0)
    def _():
        st_ref[...] = jnp.zeros_like(st_ref)

    for base in range(0, q_ref.shape[0], GLA_TILE):
        _gla_tile(slice(base, base + GLA_TILE), q_ref, k_ref, v_ref, la_ref, o_ref, st_ref)

    @pl.when(t == pl.num_programs(1) - 1)
    def _():
        sfin_ref[0] = st_ref[...]


def _gla_tile(tile, q_ref, k_ref, v_ref, la_ref, o_ref, st_ref):
    tg = GLA_TILE
    c_len = GLA_CHUNK
    n_chunks = tg // c_len
    chunk_rows = [slice(ci * c_len, (ci + 1) * c_len) for ci in range(n_chunks)]
    row = lax.broadcasted_iota(jnp.int32, (tg, tg), 0)
    col = lax.broadcasted_iota(jnp.int32, (tg, tg), 1)
    lower_b = (row // c_len == col // c_len) & (col <= row)
    lower = jnp.where(lower_b, 1.0, 0.0).astype(BF16)
    hi, mid, lo = _split3(la_ref[tile, :])
    cum = _mm(lower, hi) + _mm(lower, mid) + _mm(lower, lo)
    tot = jnp.concatenate([jnp.broadcast_to(cum[r.stop - 1:r.stop, :], (c_len, cum.shape[1])) for r in chunk_rows], axis=0)
    k_all = k_ref[tile, :]
    qd = (q_ref[tile, :] * jnp.exp(cum)).astype(BF16)
    ki = (k_all * jnp.exp(-cum)).astype(BF16)
    ke = k_all * jnp.exp(tot - cum)
    lane_chunk = lax.broadcasted_iota(jnp.int32, (GLA_DK_HEAD, tg), 1) // c_len

    att, kv = [], []
    for h in range(GLA_HEADS):
        kc = slice(h * GLA_DK_HEAD, (h + 1) * GLA_DK_HEAD)
        att.append(jnp.where(lower_b, _mm_nt(qd[:, kc], ki[:, kc]), 0.0).astype(BF16))
        ke_t = ke[:, kc].T
        v_h = v_ref[tile, h * GLA_DV_HEAD:(h + 1) * GLA_DV_HEAD]
        kv.append([_mm(jnp.where(lane_chunk == ci, ke_t, 0.0).astype(BF16), v_h) for ci in range(n_chunks)])

    s_before = []
    for h in range(GLA_HEADS):
        kc = slice(h * GLA_DK_HEAD, (h + 1) * GLA_DK_HEAD)
        st = st_ref[h]
        starts = []
        for ci, r in enumerate(chunk_rows):
            starts.append(st.astype(BF16))
            e_col = jnp.exp(cum[r.stop - 8:r.stop, kc]).T[:, 7:8]
            st = e_col * st + kv[h][ci]
        st_ref[h] = st
        s_before.append(starts)

    for h in range(GLA_HEADS):
        kc = slice(h * GLA_DK_HEAD, (h + 1) * GLA_DK_HEAD)
        vc = slice(h * GLA_DV_HEAD, (h + 1) * GLA_DV_HEAD)
        o_intra = _mm(att[h], v_ref[tile, vc])
        for ci, r in enumerate(chunk_rows):
            o_ref[tile.start + r.start:tile.start + r.stop, vc] = o_intra[r] + _mm(qd[r, kc], s_before[h][ci])


def _gla_prompt(q, k, v, la, batch, seq):
    tg = GLA_STEP
    nt = seq // tg
    qk_spec = pl.BlockSpec((tg, GLA_DK), lambda b, t: (b * nt + t, 0))
    v_spec = pl.BlockSpec((tg, GLA_DV), lambda b, t: (b * nt + t, 0))
    st_shape = (GLA_HEADS, GLA_DK_HEAD, GLA_DV_HEAD)
    return pl.pallas_call(
        _gla_prompt_body,
        grid=(batch, nt),
        in_specs=[qk_spec, qk_spec, v_spec, qk_spec],
        out_specs=[v_spec, pl.BlockSpec((1,) + st_shape, lambda b, t: (b, 0, 0, 0))],
        out_shape=[jax.ShapeDtypeStruct((batch * seq, GLA_DV), F32),
                   jax.ShapeDtypeStruct((batch,) + st_shape, F32)],
        scratch_shapes=[pltpu.VMEM(st_shape, F32)],
        compiler_params=pltpu.CompilerParams(dimension_semantics=("arbitrary", "arbitrary"), vmem_limit_bytes=VMEM_LIMIT),
        name="gla_prompt",
    )(q, k, v, la)


def _gla_sample_body(q_ref, k_ref, v_ref, la_ref, s_ref, o_ref, sn_ref):
    bt = q_ref.shape[0]
    for h in range(GLA_HEADS):
        kc = slice(h * GLA_DK_HEAD, (h + 1) * GLA_DK_HEAD)
        vc = slice(h * GLA_DV_HEAD, (h + 1) * GLA_DV_HEAD)
        a_t = jnp.exp(la_ref[:, kc]).T
        k_t = k_ref[:, kc].T
        q_t = q_ref[:, kc].T
        for j in range(bt):
            s_new = a_t[:, j:j + 1] * s_ref[j, h] + k_t[:, j:j + 1] * v_ref[j:j + 1, vc]
            sn_ref[j, h] = s_new
            o_ref[j:j + 1, vc] = jnp.sum(q_t[:, j:j + 1] * s_new, axis=0, keepdims=True)


def _gla_sample(q, k, v, la, state):
    bt = STATE_TILE
    nb = q.shape[0]
    row = lambda w: pl.BlockSpec((bt, w), lambda i: (i, 0))
    st_spec = pl.BlockSpec((bt, GLA_HEADS, GLA_DK_HEAD, GLA_DV_HEAD), lambda i: (i, 0, 0, 0))
    return pl.pallas_call(
        _gla_sample_body,
        grid=(nb // bt,),
        in_specs=[row(GLA_DK), row(GLA_DK), row(GLA_DV), row(GLA_DK), st_spec],
        out_specs=[row(GLA_DV), st_spec],
        out_shape=[jax.ShapeDtypeStruct((nb, GLA_DV), F32), jax.ShapeDtypeStruct(state.shape, F32)],
        compiler_params=pltpu.CompilerParams(dimension_semantics=("arbitrary",), vmem_limit_bytes=VMEM_LIMIT),
        name="gla_sample",
    )(q, k, v, la, state)


def _swa_attend_units(sink_ref, q_ref, k_full, v_full, has_prev, o_ref):
    w = SWA_WINDOW
    hd = SWA_HEAD_DIM
    tq = q_ref.shape[0]
    nkv = k_full.shape[0]
    lane_q = lax.broadcasted_iota(jnp.int32, (w, LANES), 1) < hd
    lane_kv = lax.broadcasted_iota(jnp.int32, (nkv, LANES), 1) < hd
    i = lax.broadcasted_iota(jnp.int32, (w, 2 * w), 0)
    j = lax.broadcasted_iota(jnp.int32, (w, 2 * w), 1)
    band = jnp.where(j < w, jnp.where(j >= i, 1, 0), jnp.where(j - w <= i, 1, 0))
    band_first = jnp.where(j < w, has_prev, 1) * band
    lane_2w = lax.broadcasted_iota(jnp.int32, (2 * w, LANES), 1) < hd
    ones_lo = jnp.where(lane_2w, 1.0, 0.0).astype(BF16)
    ones_hi = jnp.where(lane_2w, 0.0, 1.0).astype(BF16)
    c2 = (hd ** -0.5) * LOG2E

    k_prep, v_prep = [], []
    for p in range(SWA_KV // LANES):
        cols = slice(p * LANES, (p + 1) * LANES)
        k_p, v_p = k_full[:, cols], v_full[:, cols]
        k_prep.append((k_p.astype(BF16), pltpu.roll(k_p, hd, axis=1).astype(BF16)))
        v_r = pltpu.roll(v_p, hd, axis=1)
        v_prep.append(((jnp.where(lane_kv, v_p, 0.0).astype(BF16), jnp.where(lane_kv, 0.0, v_r).astype(BF16)),
                       (jnp.where(lane_kv, v_r, 0.0).astype(BF16), jnp.where(lane_kv, 0.0, v_p).astype(BF16))))

    def softmax_part(s, hh, mask):
        s2 = jnp.where(mask, s, NEG_BIG)
        sk2 = jnp.full((w, 1), sink_ref[hh], F32) * LOG2E
        m2 = jnp.maximum(jnp.max(s2, axis=-1, keepdims=True), sk2)
        return jnp.exp2(s2 - m2).astype(BF16), sk2 - m2

    for b in range(tq // w):
        rows = slice(b * w, (b + 1) * w)
        krows = slice(b * w, (b + 2) * w)
        mask = (band_first if b == 0 else band) > 0
        for p in range(SWA_KV // LANES):
            q_lo, q_hi = [], []
            for x in range(4):
                q_c = q_ref[rows, (4 * p + x) * LANES:(4 * p + x + 1) * LANES] * c2
                q_lo.append(jnp.where(lane_q, q_c, 0.0).astype(BF16))
                q_hi.append(jnp.where(lane_q, 0.0, q_c).astype(BF16))
            s_self = _mm_nt(jnp.concatenate([q_lo[0], q_lo[1], q_hi[2], q_hi[3]], axis=0), k_prep[p][0][krows])
            s_roll = _mm_nt(jnp.concatenate([q_hi[0], q_hi[1], q_lo[2], q_lo[3]], axis=0), k_prep[p][1][krows])
            for x in range(4):
                c = 4 * p + x
                gh = x // 2
                xr = slice(x * w, (x + 1) * w)
                s_lo, s_hi = (s_self[xr], s_roll[xr]) if gh == 0 else (s_roll[xr], s_self[xr])
                p_lo, d_lo = softmax_part(s_lo, 2 * c, mask)
                p_hi, d_hi = softmax_part(s_hi, 2 * c + 1, mask)
                v_lo, v_hi = v_prep[p][gh]
                rhs = jnp.concatenate([jnp.concatenate([v_lo[krows], ones_lo], axis=1),
                                       jnp.concatenate([v_hi[krows], ones_hi], axis=1)], axis=0)
                ext = _mm(jnp.concatenate([p_lo, p_hi], axis=1), rhs)
                den = ext[:, LANES:] + jnp.exp2(jnp.where(lane_q, d_lo, d_hi))
                o_ref[rows, c * LANES:(c + 1) * LANES] = (ext[:, :LANES] / den).astype(o_ref.dtype)
            yield


def _out_stage(at_bf16, h, wo_ref, bo_ref, gpost_ref, gfpre_ref, wup_ref, wdn_ref, gfpost_ref, between=None):
    m = _mm(at_bf16, wo_ref[...]) + bo_ref[...]
    h1 = h + _rms(m, gpost_ref[...])
    f = _ffn(_rms(h1, gfpre_ref[...]).astype(BF16), wup_ref, wdn_ref, between)
    return h1 + _rms(f, gfpost_ref[...])


def _swa_out_body(batch, nt, sink_ref, q_ref, kc_ref, kp_ref, vc_ref, vp_ref, h_ref, ats_ref, hs_ref,
                  wo_ref, bo_ref, gpost_ref, gfpre_ref, wup_ref, wdn_ref, gfpost_ref,
                  y_ref, ys_ref, attn_scr):
    i = pl.program_id(0)
    per_batch = nt + 1
    t = lax.rem(i, per_batch)
    is_prompt = i < batch * per_batch
    consts = (wo_ref, bo_ref, gpost_ref, gfpre_ref, wup_ref, wdn_ref, gfpost_ref)

    @pl.when(i == 0)
    def _():
        attn_scr[...] = jnp.zeros_like(attn_scr)

    @pl.when(is_prompt & (t < nt))
    def _():
        at_prev = attn_scr[...]
        k_full = jnp.concatenate([kp_ref[...], kc_ref[...]], axis=0)
        v_full = jnp.concatenate([vp_ref[...], vc_ref[...]], axis=0)
        units = _swa_attend_units(sink_ref, q_ref, k_full, v_full, jnp.minimum(t, 1), attn_scr)
        y_ref[...] = _out_stage(at_prev, h_ref[...], *consts, between=lambda: next(units, None))
        for _ in units:
            pass

    @pl.when(is_prompt & (t == nt))
    def _():
        y_ref[...] = _out_stage(attn_scr[...], h_ref[...], *consts)

    @pl.when(i == batch * per_batch)
    def _():
        ys_ref[...] = _out_stage(ats_ref[...].astype(BF16), hs_ref[...], *consts)


def _swa_out(sinks, q, k, v, h, attn_s, h_s, consts, batch, seq):
    tq = SWA_TILE
    w = SWA_WINDOW
    nt = seq // tq
    per_batch = nt + 1
    n_s = h_s.shape[0]

    def pos(i):
        ii = jnp.minimum(i, batch * per_batch - 1)
        return ii // per_batch, lax.rem(ii, per_batch)

    def att_tile(i):
        b, t = pos(i)
        return (b * nt + jnp.minimum(t, nt - 1), 0)

    def prev_block(i):
        b, t = pos(i)
        return (b * (seq // w) + jnp.maximum(jnp.minimum(t, nt - 1) * (tq // w) - 1, 0), 0)

    def out_tile(i):
        b, t = pos(i)
        return (b * nt + jnp.maximum(t - 1, 0), 0)

    whole = lambda arr: pl.BlockSpec(arr.shape, lambda i: (0, 0))
    const_specs, const_args = _const_plan(consts)
    return pl.pallas_call(
        functools.partial(_swa_out_body, batch, nt),
        grid=(batch * per_batch + 1,),
        in_specs=[pl.BlockSpec(memory_space=pltpu.SMEM),
                  pl.BlockSpec((tq, SWA_Q), att_tile), pl.BlockSpec((tq, SWA_KV), att_tile),
                  pl.BlockSpec((w, SWA_KV), prev_block), pl.BlockSpec((tq, SWA_KV), att_tile),
                  pl.BlockSpec((w, SWA_KV), prev_block), pl.BlockSpec((tq, D_MODEL), out_tile),
                  whole(attn_s), whole(h_s)] + const_specs,
        out_specs=[pl.BlockSpec((tq, D_MODEL), out_tile), whole(h_s)],
        out_shape=[jax.ShapeDtypeStruct((batch * seq, D_MODEL), F32), jax.ShapeDtypeStruct((n_s, D_MODEL), F32)],
        scratch_shapes=[pltpu.VMEM((tq, SWA_Q), BF16)],
        compiler_params=pltpu.CompilerParams(dimension_semantics=("arbitrary",), vmem_limit_bytes=VMEM_LIMIT),
        name="swa_out",
    )(sinks, q, k, k, v, v, h, attn_s, h_s, *const_args)


def _swa_sample_body(sk_ref, q_ref, kn_ref, vn_ref, ck_ref, cv_ref, o_ref, nk_ref, nv_ref):
    bt = q_ref.shape[0]
    w = ck_ref.shape[2]
    hd = SWA_HEAD_DIM
    hgroup = lax.broadcasted_iota(jnp.int32, (SWA_HEADS, 1), 0) // SWA_GROUP
    newest = lax.broadcasted_iota(jnp.int32, (SWA_KV, w), 1) == w - 1
    kn_t = kn_ref[...].T
    vn_t = vn_ref[...].T
    scale = hd ** -0.5
    sk = sk_ref[...]
    groups = [slice(g * hd, (g + 1) * hd) for g in range(SWA_KV_HEADS)]

    def per_head(pieces):
        out = pieces[0]
        for g in range(1, SWA_KV_HEADS):
            out = jnp.where(hgroup == g, pieces[g], out)
        return out

    for j in range(bt):
        nk_ref[j] = jnp.where(newest, kn_t[:, j:j + 1], pltpu.roll(ck_ref[j], w - 1, axis=1))
        nv_ref[j] = jnp.where(newest, vn_t[:, j:j + 1], pltpu.roll(cv_ref[j], w - 1, axis=1))

    s_old, s_new, v_sel = [], [], []
    for j in range(bt):
        q = q_ref[j]
        qb = q.astype(BF16)
        s_old.append(per_head([_mm(qb, ck_ref[j, rows, :].astype(BF16)) for rows in groups]))
        k_sel = per_head([kn_ref[j:j + 1, cols] for cols in groups])
        v_sel.append(per_head([vn_ref[j:j + 1, cols] for cols in groups]))
        s_new.append(jnp.sum(q * k_sel, axis=-1, keepdims=True))
    s_old = jnp.stack(s_old, axis=0) * scale
    s_new = jnp.stack(s_new, axis=0) * scale
    m = jnp.maximum(jnp.maximum(jnp.max(s_old, axis=-1, keepdims=True), s_new), sk)
    p_old = jnp.exp(s_old - m)
    p_new = jnp.exp(s_new - m)
    inv = 1.0 / (jnp.sum(p_old, axis=-1, keepdims=True) + p_new + jnp.exp(sk - m))
    p_old = p_old.astype(BF16)
    for j in range(bt):
        o = per_head([_mm_nt(p_old[j], cv_ref[j, rows, :].astype(BF16)) for rows in groups])
        o_ref[j] = (o + p_new[j] * v_sel[j]) * inv[j]


def _swa_sample(sinks, q3, k_new, v_new, cache_k, cache_v):
    bt = SEQ_TILE
    nb, _, w = cache_k.shape
    assert w == LANES
    row = lambda width: pl.BlockSpec((bt, width), lambda i: (i, 0))
    q_spec = pl.BlockSpec((bt, SWA_HEADS, SWA_HEAD_DIM), lambda i: (i, 0, 0))
    c_spec = pl.BlockSpec((bt, SWA_KV, w), lambda i: (i, 0, 0))
    return pl.pallas_call(
        _swa_sample_body,
        grid=(nb // bt,),
        in_specs=[pl.BlockSpec((SWA_HEADS, 1), lambda i: (0, 0)), q_spec, row(SWA_KV), row(SWA_KV), c_spec, c_spec],
        out_specs=[q_spec, c_spec, c_spec],
        out_shape=[jax.ShapeDtypeStruct((nb, SWA_HEADS, SWA_HEAD_DIM), F32),
                   jax.ShapeDtypeStruct(cache_k.shape, F32), jax.ShapeDtypeStruct(cache_v.shape, F32)],
        compiler_params=pltpu.CompilerParams(dimension_semantics=("arbitrary",)),
        name="swa_sample",
    )(sinks, q3, k_new, v_new, cache_k, cache_v)


def _rope_tables(pos):
    inv = jnp.power(ROPE_THETA, -jnp.arange(ROPE_HALF, dtype=F32) * 2.0 / ROPE_DIM)
    ang = pos.astype(F32)[:, None] * inv[None, :]
    cos, sin = jnp.cos(ang), jnp.sin(ang)
    n = pos.shape[0]
    rest = SWA_HEAD_DIM - ROPE_DIM
    rc = jnp.concatenate([cos, cos, jnp.ones((n, rest), F32)], axis=-1)
    ra = jnp.concatenate([-sin, jnp.zeros((n, ROPE_HALF + rest), F32)], axis=-1)
    rb = jnp.concatenate([jnp.zeros((n, ROPE_HALF), F32), sin, jnp.zeros((n, rest), F32)], axis=-1)
    reps = LANES // SWA_HEAD_DIM
    return tuple(jnp.tile(t, (1, reps)) for t in (rc, ra, rb))


def kernel(x_prompt, x_sample, state_gla, cache_swa_k, cache_swa_v, gla_w_in, gla_w_gate2, gla_b_gate, gla_g_head, gla_w_out, swa_w_qkv, swa_b_qkv, swa_sinks, swa_w_out, swa_b_out, norm_mix_pre, norm_mix_post, norm_ffn_pre, norm_ffn_post, ffn_w_up, ffn_w_down):
    batch, seq, _ = x_prompt.shape
    dec_batch, dec_seq, _ = x_sample.shape
    assert dec_seq == 1 and seq % SWA_WINDOW == 0
    past_len = seq
    n_p, n_s = batch * seq, dec_batch * dec_seq
    xp = x_prompt.reshape(n_p, D_MODEL)
    xs = x_sample.reshape(n_s, D_MODEL)

    w_in, w_gout, w_qkv, w_sout, w_up, w_dn = _cast_weights(
        [gla_w_in[0], gla_w_out[0], swa_w_qkv[0], swa_w_out[0], ffn_w_up, ffn_w_down])
    w_g2 = gla_w_gate2[0].astype(BF16)
    b_g = gla_b_gate[0][None, :]
    g_head = gla_g_head[0][None, :]
    b_qkv = swa_b_qkv[0][None, :]
    b_sout = swa_b_out[0][None, :]
    row = lambda t, i: t[i][None, :]

    in0_consts = [row(norm_mix_pre, 0), w_in, w_g2, b_g]
    mid_consts = [g_head, w_gout, row(norm_mix_post, 0), row(norm_ffn_pre, 0), (w_up, 0), (w_dn, 0), row(norm_ffn_post, 0),
                  row(norm_mix_pre, 1), w_qkv, b_qkv]
    out_consts = [w_sout, b_sout, row(norm_mix_post, 1), row(norm_ffn_pre, 1), (w_up, 1), (w_dn, 1), row(norm_ffn_post, 1)]
    in0_widths = [GLA_DK, GLA_DK, GLA_DV, GLA_DV, GLA_DK]
    mid_widths = [D_MODEL, SWA_Q, SWA_KV, SWA_KV]

    tm, ts = TOKEN_TILE, n_s
    (q, k, v, r, la), (qs, ks, vs, rs, las) = _tok_call(
        _in0_body,
        [(n_p, tm, [(xp, None)], in0_widths, [F32, F32, BF16, F32, F32]),
         (n_s, ts, [(xs, None)], in0_widths, [F32] * 5)],
        in0_consts, "in0")
    o, s_fin_p = _gla_prompt(q, k, v, la, batch, seq)
    o_s, s_new = _gla_sample(qs, ks, vs, las, state_gla[0])
    tabs = _rope_tables(jnp.arange(seq))
    tabs_s = _rope_tables(jnp.full((n_s,), past_len, jnp.int32))
    tab_map = lambda t: (t % (seq // tm), 0)
    (h2, q1, k1, v1), (h2s, q1s, k1s, v1s) = _tok_call(
        _mid_body,
        [(n_p, tm, [(o, None), (r, None), (xp, None)] + [(t, tab_map) for t in tabs], mid_widths, [F32] * 4),
         (n_s, ts, [(o_s, None), (rs, None), (xs, None)] + [(t, None) for t in tabs_s], mid_widths, [F32] * 4)],
        mid_consts, "mid")
    win = cache_swa_k.shape[2]
    to_t = lambda c: jnp.transpose(c[0].reshape(dec_batch, win, SWA_KV), (0, 2, 1))
    from_t = lambda c: jnp.transpose(c, (0, 2, 1)).reshape(1, dec_batch, win, SWA_KV_HEADS, SWA_HEAD_DIM)
    attn_s, nk, nv = _swa_sample(swa_sinks[0][:, None], q1s.reshape(n_s, SWA_HEADS, SWA_HEAD_DIM), k1s, v1s,
                                 to_t(cache_swa_k), to_t(cache_swa_v))
    y_p, y_s = _swa_out(swa_sinks[0], q1, k1, v1, h2, attn_s.reshape(n_s, SWA_Q), h2s, out_consts, batch, seq)
    wp = min(SWA_WINDOW, seq)
    tail = lambda t: t.reshape(batch, seq, SWA_KV)[:, seq - wp:].reshape(batch, wp, SWA_KV_HEADS, SWA_HEAD_DIM)
    k_tail, v_tail = tail(k1), tail(v1)

    return (y_p.reshape(batch, seq, D_MODEL), y_s.reshape(dec_batch, dec_seq, D_MODEL),
            s_fin_p[None], s_new[None], k_tail[None], v_tail[None], from_t(nk), from_t(nv))
```

```python
import functools

import jax
import jax.numpy as jnp
from jax import lax
from jax.experimental import pallas as pl
from jax.experimental.pallas import tpu as pltpu

F32 = jnp.float32
BF16 = jnp.bfloat16

D_MODEL = 1024
D_FF = 4 * D_MODEL
NORM_EPS = 1e-6

GLA_HEADS = 4
GLA_DK = D_MODEL // 2
GLA_DV = D_MODEL
GLA_DK_HEAD = GLA_DK // GLA_HEADS
GLA_DV_HEAD = GLA_DV // GLA_HEADS
GLA_GATE_RANK = 16
GLA_TAU = 16.0
GLA_CHUNK = 64
GLA_MAIN = 2 * GLA_DK + 2 * GLA_DV

SWA_HEAD_DIM = 64
SWA_HEADS = D_MODEL // SWA_HEAD_DIM
SWA_KV_HEADS = 4
SWA_GROUP = SWA_HEADS // SWA_KV_HEADS
SWA_WINDOW = 128
SWA_Q = SWA_HEADS * SWA_HEAD_DIM
SWA_KV = SWA_KV_HEADS * SWA_HEAD_DIM
SWA_QKV = SWA_Q + 2 * SWA_KV
ROPE_THETA = 500000.0
ROPE_DIM = SWA_HEAD_DIM // 4
ROPE_HALF = ROPE_DIM // 2

LANES = 128
FFN_CHUNK = 512
TOKEN_TILE = 512
GLA_TILE = 256
GLA_STEP = 512
SWA_TILE = 512
SEQ_TILE = 8
STATE_TILE = 16
VMEM_LIMIT = 56 * 1024 * 1024
NEG_BIG = -1e30
LOG2E = 1.4426950408889634


def _mm(a, b):
    return jnp.dot(a, b, preferred_element_type=F32)


def _mm_nt(a, b):
    return lax.dot_general(a, b, (((1,), (1,)), ((), ())), preferred_element_type=F32)


def _mm_tn(a, b):
    return lax.dot_general(a, b, (((0,), (0,)), ((), ())), preferred_element_type=F32)


def _rms(x, g):
    ms = jnp.mean(x * x, axis=-1, keepdims=True)
    return x * lax.rsqrt(ms + NORM_EPS) * g


def _split3(x):
    hi = x.astype(BF16)
    r1 = x - hi.astype(F32)
    mid = r1.astype(BF16)
    lo = (r1 - mid.astype(F32)).astype(BF16)
    return hi, mid, lo


def _ffn(a_bf16, wup_ref, wdn_ref, between=None):
    n_slices = D_FF // FFN_CHUNK
    cols = [slice(c * FFN_CHUNK, (c + 1) * FFN_CHUNK) for c in range(n_slices)]
    acc = None
    u_next = _mm(a_bf16, wup_ref[:, cols[0]])
    for c in range(n_slices):
        u = u_next
        if c + 1 < n_slices:
            u_next = _mm(a_bf16, wup_ref[:, cols[c + 1]])
        u = jnp.square(jnp.maximum(u, 0.0)).astype(BF16)
        p = _mm(u, wdn_ref[cols[c], :])
        acc = p if acc is None else acc + p
        if between is not None:
            between()
    return acc


def _in0_body(x_ref, g_ref, w_ref, wg_ref, bg_ref, q_ref, k_ref, v_ref, r_ref, la_ref):
    a = _rms(x_ref[...], g_ref[...]).astype(BF16)
    z = _mm(a, w_ref[:, GLA_MAIN:GLA_MAIN + GLA_GATE_RANK]).astype(BF16)
    q_ref[...] = _mm(a, w_ref[:, 0:GLA_DK]) * (GLA_DK_HEAD ** -0.5)
    k_ref[...] = _mm(a, w_ref[:, GLA_DK:2 * GLA_DK])
    zg = _mm(z, wg_ref[...]) + bg_ref[...]
    la_ref[...] = (jnp.minimum(zg, 0.0) - jnp.log1p(jnp.exp(-jnp.abs(zg)))) * (1.0 / GLA_TAU)
    for c in range(GLA_DV // 512):
        cols = slice(c * 512, (c + 1) * 512)
        v_ref[:, cols] = _mm(a, w_ref[:, 2 * GLA_DK + c * 512:2 * GLA_DK + (c + 1) * 512]).astype(v_ref.dtype)
        r_ref[:, cols] = _mm(a, w_ref[:, 2 * GLA_DK + GLA_DV + c * 512:2 * GLA_DK + GLA_DV + (c + 1) * 512])


def _mid_body(o_ref, r_ref, h_ref, rc_ref, ra_ref, rb_ref,
              gh_ref, wo_ref, gpost_ref, gfpre_ref, wup_ref, wdn_ref, gfpost_ref,
              gpre1_ref, wqkv_ref, bqkv_ref,
              h2_ref, q1_ref, k1_ref, v1_ref):
    m = None
    for hh in range(GLA_HEADS):
        cols = slice(hh * GLA_DV_HEAD, (hh + 1) * GLA_DV_HEAD)
        on = _rms(o_ref[:, cols], gh_ref[...])
        r = r_ref[:, cols]
        u = (on * (r * (1.0 / (1.0 + jnp.exp(-r))))).astype(BF16)
        p = _mm(u, wo_ref[cols, :])
        m = p if m is None else m + p
    h1 = h_ref[...] + _rms(m, gpost_ref[...])
    f = _ffn(_rms(h1, gfpre_ref[...]).astype(BF16), wup_ref, wdn_ref)
    h2 = h1 + _rms(f, gfpost_ref[...])
    h2_ref[...] = h2
    a3 = _rms(h2, gpre1_ref[...]).astype(BF16)
    rc, ra, rb = rc_ref[...], ra_ref[...], rb_ref[...]
    wide = 2 * LANES
    for c2 in range((SWA_Q + SWA_KV) // wide):
        x2 = _mm(a3, wqkv_ref[:, c2 * wide:(c2 + 1) * wide]) + bqkv_ref[:, c2 * wide:(c2 + 1) * wide]
        for half in range(2):
            c = 2 * c2 + half
            x = x2[:, half * LANES:(half + 1) * LANES]
            y = x * rc + pltpu.roll(x, LANES - ROPE_HALF, axis=1) * ra + pltpu.roll(x, ROPE_HALF, axis=1) * rb
            if c < SWA_Q // LANES:
                q1_ref[:, c * LANES:(c + 1) * LANES] = y
            else:
                k1_ref[:, c * LANES - SWA_Q:(c + 1) * LANES - SWA_Q] = y
    v1_ref[...] = _mm(a3, wqkv_ref[:, SWA_Q + SWA_KV:SWA_QKV]) + bqkv_ref[:, SWA_Q + SWA_KV:SWA_QKV]


def _const_plan(const_inputs):
    in_specs, args = [], []
    for entry in const_inputs:
        if isinstance(entry, tuple):
            arr, layer = entry
            spec = pl.BlockSpec((None,) + arr.shape[1:], lambda i, layer=layer: (layer, 0, 0), pipeline_mode=pl.Buffered(1))
        else:
            arr = entry
            spec = pl.BlockSpec(arr.shape, lambda i: (0, 0), pipeline_mode=pl.Buffered(1))
        in_specs.append(spec)
        args.append(arr)
    return in_specs, args


def _tok_call(body, groups, const_inputs, name):
    in_specs, args, out_specs, out_shape = [], [], [], []
    ranges, start = [], 0
    for n_rows, tm, row_inputs, out_widths, out_dtypes in groups:
        assert n_rows % tm == 0
        count = n_rows // tm
        local = lambda i, start=start, count=count: jnp.clip(i - start, 0, count - 1)
        mode = dict(pipeline_mode=pl.Buffered(1)) if count == 1 else {}
        for arr, imap in row_inputs:
            imap = imap if imap is not None else (lambda t: (t, 0))
            in_specs.append(pl.BlockSpec((tm, arr.shape[1]), lambda i, imap=imap, local=local: imap(local(i)), **mode))
            args.append(arr)
        for w, dt in zip(out_widths, out_dtypes):
            out_specs.append(pl.BlockSpec((tm, w), lambda i, local=local: (local(i), 0)))
            out_shape.append(jax.ShapeDtypeStruct((n_rows, w), dt))
        ranges.append((start, count, len(row_inputs), len(out_widths)))
        start += count
    n_row_refs = len(in_specs)
    n_const = len(const_inputs)
    const_specs, const_args = _const_plan(const_inputs)

    def kern(*refs):
        row_refs, const_refs = refs[:n_row_refs], refs[n_row_refs:n_row_refs + n_const]
        out_refs = refs[n_row_refs + n_const:]
        i = pl.program_id(0)
        r0 = o0 = 0
        for first, count, n_in, n_out in ranges:
            ins, outs = row_refs[r0:r0 + n_in], out_refs[o0:o0 + n_out]
            r0, o0 = r0 + n_in, o0 + n_out

            @pl.when((i >= first) & (i < first + count))
            def _(ins=ins, outs=outs):
                body(*ins, *const_refs, *outs)

    outs = pl.pallas_call(
        kern,
        grid=(start,),
        in_specs=in_specs + const_specs,
        out_specs=out_specs,
        out_shape=out_shape,
        compiler_params=pltpu.CompilerParams(dimension_semantics=("arbitrary",), vmem_limit_bytes=VMEM_LIMIT),
        name=name,
    )(*args, *const_args)
    grouped, o0 = [], 0
    for _, _, _, n_out in ranges:
        grouped.append(list(outs[o0:o0 + n_out]))
        o0 += n_out
    return grouped


def _gla_prompt_body(q_ref, k_ref, v_ref, la_ref, o_ref, sfin_ref, st_ref):
    t = pl.program_id(1)

    @pl.when(t == 0)
    def _():
        st_ref[...] = jnp.zeros_like(st_ref)

    for base in range(0, q_ref.shape[0], GLA_TILE):
        _gla_tile(slice(base, base + GLA_TILE), q_ref, k_ref, v_ref, la_ref, o_ref, st_ref)

    @pl.when(t == pl.num_programs(1) - 1)
    def _():
        sfin_ref[0] = st_ref[...]


def _gla_tile(tile, q_ref, k_ref, v_ref, la_ref, o_ref, st_ref):
    tg = GLA_TILE
    c_len = GLA_CHUNK
    n_chunks = tg // c_len
    chunk_rows = [slice(ci * c_len, (ci + 1) * c_len) for ci in range(n_chunks)]
    row = lax.broadcasted_iota(jnp.int32, (tg, tg), 0)
    col = lax.broadcasted_iota(jnp.int32, (tg, tg), 1)
    lower_b = (row // c_len == col // c_len) & (col <= row)
    lower = jnp.where(lower_b, 1.0, 0.0).astype(BF16)
    hi, mid, lo = _split3(la_ref[tile, :])
    cum = _mm(lower, hi) + _mm(lower, mid) + _mm(lower, lo)
    tot = jnp.concatenate([jnp.broadcast_to(cum[r.stop - 1:r.stop, :], (c_len, cum.shape[1])) for r in chunk_rows], axis=0)
    k_all = k_ref[tile, :]
    qd = (q_ref[tile, :] * jnp.exp(cum)).astype(BF16)
    ki = (k_all * jnp.exp(-cum)).astype(BF16)
    ke = k_all * jnp.exp(tot - cum)
    lane_chunk = lax.broadcasted_iota(jnp.int32, (GLA_DK_HEAD, tg), 1) // c_len

    att, kv = [], []
    for h in range(GLA_HEADS):
        kc = slice(h * GLA_DK_HEAD, (h + 1) * GLA_DK_HEAD)
        att.append(jnp.where(lower_b, _mm_nt(qd[:, kc], ki[:, kc]), 0.0).astype(BF16))
        ke_t = ke[:, kc].T
        v_h = v_ref[tile, h * GLA_DV_HEAD:(h + 1) * GLA_DV_HEAD]
        kv.append([_mm(jnp.where(lane_chunk == ci, ke_t, 0.0).astype(BF16), v_h) for ci in range(n_chunks)])

    s_before = []
    for h in range(GLA_HEADS):
        kc = slice(h * GLA_DK_HEAD, (h + 1) * GLA_DK_HEAD)
        st = st_ref[h]
        starts = []
        for ci, r in enumerate(chunk_rows):
            starts.append(st.astype(BF16))
            e_col = jnp.exp(cum[r.stop - 8:r.stop, kc]).T[:, 7:8]
            st = e_col * st + kv[h][ci]
        st_ref[h] = st
        s_before.append(starts)

    for h in range(GLA_HEADS):
        kc = slice(h * GLA_DK_HEAD, (h + 1) * GLA_DK_HEAD)
        vc = slice(h * GLA_DV_HEAD, (h + 1) * GLA_DV_HEAD)
        o_intra = _mm(att[h], v_ref[tile, vc])
        for ci, r in enumerate(chunk_rows):
            o_ref[tile.start + r.start:tile.start + r.stop, vc] = o_intra[r] + _mm(qd[r, kc], s_before[h][ci])


def _gla_prompt(q, k, v, la, batch, seq):
    tg = GLA_STEP
    nt = seq // tg
    qk_spec = pl.BlockSpec((tg, GLA_DK), lambda b, t: (b * nt + t, 0))
    v_spec = pl.BlockSpec((tg, GLA_DV), lambda b, t: (b * nt + t, 0))
    st_shape = (GLA_HEADS, GLA_DK_HEAD, GLA_DV_HEAD)
    return pl.pallas_call(
        _gla_prompt_body,
        grid=(batch, nt),
        in_specs=[qk_spec, qk_spec, v_spec, qk_spec],
        out_specs=[v_spec, pl.BlockSpec((1,) + st_shape, lambda b, t: (b, 0, 0, 0))],
        out_shape=[jax.ShapeDtypeStruct((batch * seq, GLA_DV), F32),
                   jax.ShapeDtypeStruct((batch,) + st_shape, F32)],
        scratch_shapes=[pltpu.VMEM(st_shape, F32)],
        compiler_params=pltpu.CompilerParams(dimension_semantics=("arbitrary", "arbitrary"), vmem_limit_bytes=VMEM_LIMIT),
        name="gla_prompt",
    )(q, k, v, la)


def _gla_sample_body(q_ref, k_ref, v_ref, la_ref, s_ref, o_ref, sn_ref):
    bt = q_ref.shape[0]
    for h in range(GLA_HEADS):
        kc = slice(h * GLA_DK_HEAD, (h + 1) * GLA_DK_HEAD)
        vc = slice(h * GLA_DV_HEAD, (h + 1) * GLA_DV_HEAD)
        a_t = jnp.exp(la_ref[:, kc]).T
        k_t = k_ref[:, kc].T
        q_t = q_ref[:, kc].T
        for j in range(bt):
            s_new = a_t[:, j:j + 1] * s_ref[j, h] + k_t[:, j:j + 1] * v_ref[j:j + 1, vc]
            sn_ref[j, h] = s_new
            o_ref[j:j + 1, vc] = jnp.sum(q_t[:, j:j + 1] * s_new, axis=0, keepdims=True)


def _gla_sample(q, k, v, la, state):
    bt = STATE_TILE
    nb = q.shape[0]
    row = lambda w: pl.BlockSpec((bt, w), lambda i: (i, 0))
    st_spec = pl.BlockSpec((bt, GLA_HEADS, GLA_DK_HEAD, GLA_DV_HEAD), lambda i: (i, 0, 0, 0))
    return pl.pallas_call(
        _gla_sample_body,
        grid=(nb // bt,),
        in_specs=[row(GLA_DK), row(GLA_DK), row(GLA_DV), row(GLA_DK), st_spec],
        out_specs=[row(GLA_DV), st_spec],
        out_shape=[jax.ShapeDtypeStruct((nb, GLA_DV), F32), jax.ShapeDtypeStruct(state.shape, F32)],
        compiler_params=pltpu.CompilerParams(dimension_semantics=("arbitrary",), vmem_limit_bytes=VMEM_LIMIT),
        name="gla_sample",
    )(q, k, v, la, state)


def _swa_attend_units(sink_ref, q_ref, k_full, v_full, has_prev, o_ref):
    w = SWA_WINDOW
    hd = SWA_HEAD_DIM
    tq = q_ref.shape[0]
    nkv = k_full.shape[0]
    lane_q = lax.broadcasted_iota(jnp.int32, (w, LANES), 1) < hd
    lane_kv = lax.broadcasted_iota(jnp.int32, (nkv, LANES), 1) < hd
    i = lax.broadcasted_iota(jnp.int32, (w, 2 * w), 0)
    j = lax.broadcasted_iota(jnp.int32, (w, 2 * w), 1)
    band = jnp.where(j < w, jnp.where(j >= i, 1, 0), jnp.where(j - w <= i, 1, 0))
    band_first = jnp.where(j < w, has_prev, 1) * band
    lane_2w = lax.broadcasted_iota(jnp.int32, (2 * w, LANES), 1) < hd
    ones_lo = jnp.where(lane_2w, 1.0, 0.0).astype(BF16)
    ones_hi = jnp.where(lane_2w, 0.0, 1.0).astype(BF16)
    c2 = (hd ** -0.5) * LOG2E

    k_prep, v_prep = [], []
    for p in range(SWA_KV // LANES):
        cols = slice(p * LANES, (p + 1) * LANES)
        k_p, v_p = k_full[:, cols], v_full[:, cols]
        k_prep.append((k_p.astype(BF16), pltpu.roll(k_p, hd, axis=1).astype(BF16)))
        v_r = pltpu.roll(v_p, hd, axis=1)
        v_prep.append(((jnp.where(lane_kv, v_p, 0.0).astype(BF16), jnp.where(lane_kv, 0.0, v_r).astype(BF16)),
                       (jnp.where(lane_kv, v_r, 0.0).astype(BF16), jnp.where(lane_kv, 0.0, v_p).astype(BF16))))

    def softmax_part(s, hh, mask):
        s2 = jnp.where(mask, s, NEG_BIG)
        sk2 = jnp.full((w, 1), sink_ref[hh], F32) * LOG2E
        m2 = jnp.maximum(jnp.max(s2, axis=-1, keepdims=True), sk2)
        return jnp.exp2(s2 - m2).astype(BF16), sk2 - m2

    for b in range(tq // w):
        rows = slice(b * w, (b + 1) * w)
        krows = slice(b * w, (b + 2) * w)
        mask = (band_first if b == 0 else band) > 0
        for p in range(SWA_KV // LANES):
            q_lo, q_hi = [], []
            for x in range(4):
                q_c = q_ref[rows, (4 * p + x) * LANES:(4 * p + x + 1) * LANES] * c2
                q_lo.append(jnp.where(lane_q, q_c, 0.0).astype(BF16))
                q_hi.append(jnp.where(lane_q, 0.0, q_c).astype(BF16))
            s_self = _mm_nt(jnp.concatenate([q_lo[0], q_lo[1], q_hi[2], q_hi[3]], axis=0), k_prep[p][0][krows])
            s_roll = _mm_nt(jnp.concatenate([q_hi[0], q_hi[1], q_lo[2], q_lo[3]], axis=0), k_prep[p][1][krows])
            for x in range(4):
                c = 4 * p + x
                gh = x // 2
                xr = slice(x * w, (x + 1) * w)
                s_lo, s_hi = (s_self[xr], s_roll[xr]) if gh == 0 else (s_roll[xr], s_self[xr])
                p_lo, d_lo = softmax_part(s_lo, 2 * c, mask)
                p_hi, d_hi = softmax_part(s_hi, 2 * c + 1, mask)
                v_lo, v_hi = v_prep[p][gh]
                rhs = jnp.concatenate([jnp.concatenate([v_lo[krows], ones_lo], axis=1),
                                       jnp.concatenate([v_hi[krows], ones_hi], axis=1)], axis=0)
                ext = _mm(jnp.concatenate([p_lo, p_hi], axis=1), rhs)
                den = ext[:, LANES:] + jnp.exp2(jnp.where(lane_q, d_lo, d_hi))
                o_ref[rows, c * LANES:(c + 1) * LANES] = (ext[:, :LANES] / den).astype(o_ref.dtype)
            yield


def _out_stage(at_bf16, h, wo_ref, bo_ref, gpost_ref, gfpre_ref, wup_ref, wdn_ref, gfpost_ref, between=None):
    m = _mm(at_bf16, wo_ref[...]) + bo_ref[...]
    h1 = h + _rms(m, gpost_ref[...])
    f = _ffn(_rms(h1, gfpre_ref[...]).astype(BF16), wup_ref, wdn_ref, between)
    return h1 + _rms(f, gfpost_ref[...])


def _swa_out_body(batch, nt, sink_ref, q_ref, kc_ref, kp_ref, vc_ref, vp_ref, h_ref, ats_ref, hs_ref,
                  wo_ref, bo_ref, gpost_ref, gfpre_ref, wup_ref, wdn_ref, gfpost_ref,
                  y_ref, ys_ref, attn_scr):
    i = pl.program_id(0)
    per_batch = nt + 1
    t = lax.rem(i, per_batch)
    is_prompt = i < batch * per_batch
    consts = (wo_ref, bo_ref, gpost_ref, gfpre_ref, wup_ref, wdn_ref, gfpost_ref)

    @pl.when(i == 0)
    def _():
        attn_scr[...] = jnp.zeros_like(attn_scr)

    @pl.when(is_prompt & (t < nt))
    def _():
        at_prev = attn_scr[...]
        k_full = jnp.concatenate([kp_ref[...], kc_ref[...]], axis=0)
        v_full = jnp.concatenate([vp_ref[...], vc_ref[...]], axis=0)
        units = _swa_attend_units(sink_ref, q_ref, k_full, v_full, jnp.minimum(t, 1), attn_scr)
        y_ref[...] = _out_stage(at_prev, h_ref[...], *consts, between=lambda: next(units, None))
        for _ in units:
            pass

    @pl.when(is_prompt & (t == nt))
    def _():
        y_ref[...] = _out_stage(attn_scr[...], h_ref[...], *consts)

    @pl.when(i == batch * per_batch)
    def _():
        ys_ref[...] = _out_stage(ats_ref[...].astype(BF16), hs_ref[...], *consts)


def _swa_out(sinks, q, k, v, h, attn_s, h_s, consts, batch, seq):
    tq = SWA_TILE
    w = SWA_WINDOW
    nt = seq // tq
    per_batch = nt + 1
    n_s = h_s.shape[0]

    def pos(i):
        ii = jnp.minimum(i, batch * per_batch - 1)
        return ii // per_batch, lax.rem(ii, per_batch)

    def att_tile(i):
        b, t = pos(i)
        return (b * nt + jnp.minimum(t, nt - 1), 0)

    def prev_block(i):
        b, t = pos(i)
        return (b * (seq // w) + jnp.maximum(jnp.minimum(t, nt - 1) * (tq // w) - 1, 0), 0)

    def out_tile(i):
        b, t = pos(i)
        return (b * nt + jnp.maximum(t - 1, 0), 0)

    whole = lambda arr: pl.BlockSpec(arr.shape, lambda i: (0, 0))
    const_specs, const_args = _const_plan(consts)
    return pl.pallas_call(
        functools.partial(_swa_out_body, batch, nt),
        grid=(batch * per_batch + 1,),
        in_specs=[pl.BlockSpec(memory_space=pltpu.SMEM),
                  pl.BlockSpec((tq, SWA_Q), att_tile), pl.BlockSpec((tq, SWA_KV), att_tile),
                  pl.BlockSpec((w, SWA_KV), prev_block), pl.BlockSpec((tq, SWA_KV), att_tile),
                  pl.BlockSpec((w, SWA_KV), prev_block), pl.BlockSpec((tq, D_MODEL), out_tile),
                  whole(attn_s), whole(h_s)] + const_specs,
        out_specs=[pl.BlockSpec((tq, D_MODEL), out_tile), whole(h_s)],
        out_shape=[jax.ShapeDtypeStruct((batch * seq, D_MODEL), F32), jax.ShapeDtypeStruct((n_s, D_MODEL), F32)],
        scratch_shapes=[pltpu.VMEM((tq, SWA_Q), BF16)],
        compiler_params=pltpu.CompilerParams(dimension_semantics=("arbitrary",), vmem_limit_bytes=VMEM_LIMIT),
        name="swa_out",
    )(sinks, q, k, k, v, v, h, attn_s, h_s, *const_args)


def _swa_sample_body(sk_ref, q_ref, kn_ref, vn_ref, ck_ref, cv_ref, o_ref, nk_ref, nv_ref):
    bt = q_ref.shape[0]
    w = ck_ref.shape[2]
    hd = SWA_HEAD_DIM
    hgroup = lax.broadcasted_iota(jnp.int32, (SWA_HEADS, 1), 0) // SWA_GROUP
    newest = lax.broadcasted_iota(jnp.int32, (SWA_KV, w), 1) == w - 1
    kn_t = kn_ref[...].T
    vn_t = vn_ref[...].T
    scale = hd ** -0.5
    sk = sk_ref[...]
    groups = [slice(g * hd, (g + 1) * hd) for g in range(SWA_KV_HEADS)]

    def per_head(pieces):
        out = pieces[0]
        for g in range(1, SWA_KV_HEADS):
            out = jnp.where(hgroup == g, pieces[g], out)
        return out

    for j in range(bt):
        nk_ref[j] = jnp.where(newest, kn_t[:, j:j + 1], pltpu.roll(ck_ref[j], w - 1, axis=1))
        nv_ref[j] = jnp.where(newest, vn_t[:, j:j + 1], pltpu.roll(cv_ref[j], w - 1, axis=1))

    s_old, s_new, v_sel = [], [], []
    for j in range(bt):
        q = q_ref[j]
        qb = q.astype(BF16)
        s_old.append(per_head([_mm(qb, ck_ref[j, rows, :].astype(BF16)) for rows in groups]))
        k_sel = per_head([kn_ref[j:j + 1, cols] for cols in groups])
        v_sel.append(per_head([vn_ref[j:j + 1, cols] for cols in groups]))
        s_new.append(jnp.sum(q * k_sel, axis=-1, keepdims=True))
    s_old = jnp.stack(s_old, axis=0) * scale
    s_new = jnp.stack(s_new, axis=0) * scale
    m = jnp.maximum(jnp.maximum(jnp.max(s_old, axis=-1, keepdims=True), s_new), sk)
    p_old = jnp.exp(s_old - m)
    p_new = jnp.exp(s_new - m)
    inv = 1.0 / (jnp.sum(p_old, axis=-1, keepdims=True) + p_new + jnp.exp(sk - m))
    p_old = p_old.astype(BF16)
    for j in range(bt):
        o = per_head([_mm_nt(p_old[j], cv_ref[j, rows, :].astype(BF16)) for rows in groups])
        o_ref[j] = (o + p_new[j] * v_sel[j]) * inv[j]


def _swa_sample(sinks, q3, k_new, v_new, cache_k, cache_v):
    bt = SEQ_TILE
    nb, _, w = cache_k.shape
    assert w == LANES
    row = lambda width: pl.BlockSpec((bt, width), lambda i: (i, 0))
    q_spec = pl.BlockSpec((bt, SWA_HEADS, SWA_HEAD_DIM), lambda i: (i, 0, 0))
    c_spec = pl.BlockSpec((bt, SWA_KV, w), lambda i: (i, 0, 0))
    return pl.pallas_call(
        _swa_sample_body,
        grid=(nb // bt,),
        in_specs=[pl.BlockSpec((SWA_HEADS, 1), lambda i: (0, 0)), q_spec, row(SWA_KV), row(SWA_KV), c_spec, c_spec],
        out_specs=[q_spec, c_spec, c_spec],
        out_shape=[jax.ShapeDtypeStruct((nb, SWA_HEADS, SWA_HEAD_DIM), F32),
                   jax.ShapeDtypeStruct(cache_k.shape, F32), jax.ShapeDtypeStruct(cache_v.shape, F32)],
        compiler_params=pltpu.CompilerParams(dimension_semantics=("arbitrary",)),
        name="swa_sample",
    )(sinks, q3, k_new, v_new, cache_k, cache_v)


def _rope_tables(pos):
    inv = jnp.power(ROPE_THETA, -jnp.arange(ROPE_HALF, dtype=F32) * 2.0 / ROPE_DIM)
    ang = pos.astype(F32)[:, None] * inv[None, :]
    cos, sin = jnp.cos(ang), jnp.sin(ang)
    n = pos.shape[0]
    rest = SWA_HEAD_DIM - ROPE_DIM
    rc = jnp.concatenate([cos, cos, jnp.ones((n, rest), F32)], axis=-1)
    ra = jnp.concatenate([-sin, jnp.zeros((n, ROPE_HALF + rest), F32)], axis=-1)
    rb = jnp.concatenate([jnp.zeros((n, ROPE_HALF), F32), sin, jnp.zeros((n, rest), F32)], axis=-1)
    reps = LANES // SWA_HEAD_DIM
    return tuple(jnp.tile(t, (1, reps)) for t in (rc, ra, rb))


def kernel(x_prompt, x_sample, state_gla, cache_swa_k, cache_swa_v, gla_w_in, gla_w_gate2, gla_b_gate, gla_g_head, gla_w_out, swa_w_qkv, swa_b_qkv, swa_sinks, swa_w_out, swa_b_out, norm_mix_pre, norm_mix_post, norm_ffn_pre, norm_ffn_post, ffn_w_up, ffn_w_down):
    batch, seq, _ = x_prompt.shape
    dec_batch, dec_seq, _ = x_sample.shape
    assert dec_seq == 1 and seq % SWA_WINDOW == 0
    past_len = seq
    n_p, n_s = batch * seq, dec_batch * dec_seq
    xp = x_prompt.reshape(n_p, D_MODEL)
    xs = x_sample.reshape(n_s, D_MODEL)

    w_in = gla_w_in[0].astype(BF16)
    w_g2 = gla_w_gate2[0].astype(BF16)
    b_g = gla_b_gate[0][None, :]
    g_head = gla_g_head[0][None, :]
    w_gout = gla_w_out[0].astype(BF16)
    w_qkv = swa_w_qkv[0].astype(BF16)
    b_qkv = swa_b_qkv[0][None, :]
    w_sout = swa_w_out[0].astype(BF16)
    b_sout = swa_b_out[0][None, :]
    w_up = ffn_w_up.astype(BF16)
    w_dn = ffn_w_down.astype(BF16)
    row = lambda t, i: t[i][None, :]

    in0_consts = [row(norm_mix_pre, 0), w_in, w_g2, b_g]
    mid_consts = [g_head, w_gout, row(norm_mix_post, 0), row(norm_ffn_pre, 0), (w_up, 0), (w_dn, 0), row(norm_ffn_post, 0),
                  row(norm_mix_pre, 1), w_qkv, b_qkv]
    out_consts = [w_sout, b_sout, row(norm_mix_post, 1), row(norm_ffn_pre, 1), (w_up, 1), (w_dn, 1), row(norm_ffn_post, 1)]
    in0_widths = [GLA_DK, GLA_DK, GLA_DV, GLA_DV, GLA_DK]
    mid_widths = [D_MODEL, SWA_Q, SWA_KV, SWA_KV]

    tm, ts = TOKEN_TILE, n_s
    (q, k, v, r, la), (qs, ks, vs, rs, las) = _tok_call(
        _in0_body,
        [(n_p, tm, [(xp, None)], in0_widths, [F32, F32, BF16, F32, F32]),
         (n_s, ts, [(xs, None)], in0_widths, [F32] * 5)],
        in0_consts, "in0")
    o, s_fin_p = _gla_prompt(q, k, v, la, batch, seq)
    o_s, s_new = _gla_sample(qs, ks, vs, las, state_gla[0])
    tabs = _rope_tables(jnp.arange(seq))
    tabs_s = _rope_tables(jnp.full((n_s,), past_len, jnp.int32))
    tab_map = lambda t: (t % (seq // tm), 0)
    (h2, q1, k1, v1), (h2s, q1s, k1s, v1s) = _tok_call(
        _mid_body,
        [(n_p, tm, [(o, None), (r, None), (xp, None)] + [(t, tab_map) for t in tabs], mid_widths, [F32] * 4),
         (n_s, ts, [(o_s, None), (rs, None), (xs, None)] + [(t, None) for t in tabs_s], mid_widths, [F32] * 4)],
        mid_consts, "mid")
    win = cache_swa_k.shape[2]
    to_t = lambda c: jnp.transpose(c[0].reshape(dec_batch, win, SWA_KV), (0, 2, 1))
    from_t = lambda c: jnp.transpose(c, (0, 2, 1)).reshape(1, dec_batch, win, SWA_KV_HEADS, SWA_HEAD_DIM)
    attn_s, nk, nv = _swa_sample(swa_sinks[0][:, None], q1s.reshape(n_s, SWA_HEADS, SWA_HEAD_DIM), k1s, v1s,
                                 to_t(cache_swa_k), to_t(cache_swa_v))
    y_p, y_s = _swa_out(swa_sinks[0], q1, k1, v1, h2, attn_s.reshape(n_s, SWA_Q), h2s, out_consts, batch, seq)
    wp = min(SWA_WINDOW, seq)
    tail = lambda t: t.reshape(batch, seq, SWA_KV)[:, seq - wp:].reshape(batch, wp, SWA_KV_HEADS, SWA_HEAD_DIM)
    k_tail, v_tail = tail(k1), tail(v1)

    return (y_p.reshape(batch, seq, D_MODEL), y_s.reshape(dec_batch, dec_seq, D_MODEL),
            s_fin_p[None], s_new[None], k_tail[None], v_tail[None], from_t(nk), from_t(nv))
```

```python
import functools

import jax
import jax.numpy as jnp
from jax import lax
from jax.experimental import pallas as pl
from jax.experimental.pallas import tpu as pltpu

F32 = jnp.float32
BF16 = jnp.bfloat16

D_MODEL = 1024
D_FF = 4 * D_MODEL
NORM_EPS = 1e-6

GLA_HEADS = 4
GLA_DK = D_MODEL // 2
GLA_DV = D_MODEL
GLA_DK_HEAD = GLA_DK // GLA_HEADS
GLA_DV_HEAD = GLA_DV // GLA_HEADS
GLA_GATE_RANK = 16
GLA_TAU = 16.0
GLA_CHUNK = 64
GLA_MAIN = 2 * GLA_DK + 2 * GLA_DV

SWA_HEAD_DIM = 64
SWA_HEADS = D_MODEL // SWA_HEAD_DIM
SWA_KV_HEADS = 4
SWA_GROUP = SWA_HEADS // SWA_KV_HEADS
SWA_WINDOW = 128
SWA_Q = SWA_HEADS * SWA_HEAD_DIM
SWA_KV = SWA_KV_HEADS * SWA_HEAD_DIM
SWA_QKV = SWA_Q + 2 * SWA_KV
ROPE_THETA = 500000.0
ROPE_DIM = SWA_HEAD_DIM // 4
ROPE_HALF = ROPE_DIM // 2

LANES = 128
FFN_CHUNK = 512
TOKEN_TILE = 512
GLA_TILE = 256
GLA_STEP = 512
SWA_TILE = 512
SEQ_TILE = 8
STATE_TILE = 16
VMEM_LIMIT = 58 * 1024 * 1024
NEG_BIG = -1e30
LOG2E = 1.4426950408889634


def _mm(a, b):
    return jnp.dot(a, b, preferred_element_type=F32)


def _mm_nt(a, b):
    return lax.dot_general(a, b, (((1,), (1,)), ((), ())), preferred_element_type=F32)


def _mm_tn(a, b):
    return lax.dot_general(a, b, (((0,), (0,)), ((), ())), preferred_element_type=F32)


def _rms(x, g):
    ms = jnp.mean(x * x, axis=-1, keepdims=True)
    return x * lax.rsqrt(ms + NORM_EPS) * g


def _split3(x):
    hi = x.astype(BF16)
    r1 = x - hi.astype(F32)
    mid = r1.astype(BF16)
    lo = (r1 - mid.astype(F32)).astype(BF16)
    return hi, mid, lo


def _ffn(a_bf16, wup_ref, wdn_ref, between=None):
    n_slices = D_FF // FFN_CHUNK
    cols = [slice(c * FFN_CHUNK, (c + 1) * FFN_CHUNK) for c in range(n_slices)]
    acc = None
    u_next = _mm(a_bf16, wup_ref[:, cols[0]])
    for c in range(n_slices):
        u = u_next
        if c + 1 < n_slices:
            u_next = _mm(a_bf16, wup_ref[:, cols[c + 1]])
        u = jnp.square(jnp.maximum(u, 0.0)).astype(BF16)
        p = _mm(u, wdn_ref[cols[c], :])
        acc = p if acc is None else acc + p
        if between is not None:
            between()
    return acc


def _in0_body(x_ref, g_ref, w_ref, wg_ref, bg_ref, q_ref, k_ref, v_ref, r_ref, la_ref):
    a = _rms(x_ref[...], g_ref[...]).astype(BF16)
    z = _mm(a, w_ref[:, GLA_MAIN:GLA_MAIN + GLA_GATE_RANK]).astype(BF16)
    q_ref[...] = _mm(a, w_ref[:, 0:GLA_DK]) * (GLA_DK_HEAD ** -0.5)
    k_ref[...] = _mm(a, w_ref[:, GLA_DK:2 * GLA_DK])
    zg = _mm(z, wg_ref[...]) + bg_ref[...]
    la_ref[...] = (jnp.minimum(zg, 0.0) - jnp.log1p(jnp.exp(-jnp.abs(zg)))) * (1.0 / GLA_TAU)
    for c in range(GLA_DV // 512):
        cols = slice(c * 512, (c + 1) * 512)
        v_ref[:, cols] = _mm(a, w_ref[:, 2 * GLA_DK + c * 512:2 * GLA_DK + (c + 1) * 512]).astype(v_ref.dtype)
        r_ref[:, cols] = _mm(a, w_ref[:, 2 * GLA_DK + GLA_DV + c * 512:2 * GLA_DK + GLA_DV + (c + 1) * 512])


def _mid_body(o_ref, r_ref, h_ref, rc_ref, ra_ref, rb_ref,
              gh_ref, wo_ref, gpost_ref, gfpre_ref, wup_ref, wdn_ref, gfpost_ref,
              gpre1_ref, wqkv_ref, bqkv_ref,
              h2_ref, q1_ref, k1_ref, v1_ref):
    m = None
    for hh in range(GLA_HEADS):
        cols = slice(hh * GLA_DV_HEAD, (hh + 1) * GLA_DV_HEAD)
        on = _rms(o_ref[:, cols], gh_ref[...])
        r = r_ref[:, cols]
        u = (on * (r * (1.0 / (1.0 + jnp.exp(-r))))).astype(BF16)
        p = _mm(u, wo_ref[cols, :])
        m = p if m is None else m + p
    h1 = h_ref[...] + _rms(m, gpost_ref[...])
    f = _ffn(_rms(h1, gfpre_ref[...]).astype(BF16), wup_ref, wdn_ref)
    h2 = h1 + _rms(f, gfpost_ref[...])
    h2_ref[...] = h2
    a3 = _rms(h2, gpre1_ref[...]).astype(BF16)
    rc, ra, rb = rc_ref[...], ra_ref[...], rb_ref[...]
    wide = 2 * LANES
    for c2 in range((SWA_Q + SWA_KV) // wide):
        x2 = _mm(a3, wqkv_ref[:, c2 * wide:(c2 + 1) * wide]) + bqkv_ref[:, c2 * wide:(c2 + 1) * wide]
        for half in range(2):
            c = 2 * c2 + half
            x = x2[:, half * LANES:(half + 1) * LANES]
            y = x * rc + pltpu.roll(x, LANES - ROPE_HALF, axis=1) * ra + pltpu.roll(x, ROPE_HALF, axis=1) * rb
            if c < SWA_Q // LANES:
                q1_ref[:, c * LANES:(c + 1) * LANES] = y
            else:
                k1_ref[:, c * LANES - SWA_Q:(c + 1) * LANES - SWA_Q] = y
    v1_ref[...] = _mm(a3, wqkv_ref[:, SWA_Q + SWA_KV:SWA_QKV]) + bqkv_ref[:, SWA_Q + SWA_KV:SWA_QKV]


def _const_plan(const_inputs):
    in_specs, args = [], []
    for entry in const_inputs:
        if isinstance(entry, tuple):
            arr, layer = entry
            spec = pl.BlockSpec((None,) + arr.shape[1:], lambda i, layer=layer: (layer, 0, 0), pipeline_mode=pl.Buffered(1))
        else:
            arr = entry
            spec = pl.BlockSpec(arr.shape, lambda i: (0, 0), pipeline_mode=pl.Buffered(1))
        in_specs.append(spec)
        args.append(arr)
    return in_specs, args


def _tok_call(body, groups, const_inputs, name, casts=()):
    in_specs, args, out_specs, out_shape = [], [], [], []
    ranges, start = [], 0
    for n_rows, tm, row_inputs, out_widths, out_dtypes in groups:
        assert n_rows % tm == 0
        count = n_rows // tm
        local = lambda i, start=start, count=count: jnp.clip(i - start, 0, count - 1)
        mode = dict(pipeline_mode=pl.Buffered(1)) if count == 1 else {}
        for arr, imap in row_inputs:
            imap = imap if imap is not None else (lambda t: (t, 0))
            in_specs.append(pl.BlockSpec((tm, arr.shape[1]), lambda i, imap=imap, local=local: imap(local(i)), **mode))
            args.append(arr)
        for w, dt in zip(out_widths, out_dtypes):
            out_specs.append(pl.BlockSpec((tm, w), lambda i, local=local: (local(i), 0)))
            out_shape.append(jax.ShapeDtypeStruct((n_rows, w), dt))
        ranges.append((start, count, len(row_inputs), len(out_widths)))
        start += count
    n_row_refs = len(in_specs)
    n_const = len(const_inputs)
    n_group_outs = len(out_specs)
    const_specs, const_args = _const_plan(const_inputs)
    cast_steps = ranges[0][1]
    cast_specs, cast_args = [], []
    for arr, layer in casts:
        _, k_dim, n = arr.shape
        assert k_dim % (cast_steps * 16) == 0
        rows = k_dim // cast_steps
        block = lambda i: jnp.minimum(i, cast_steps - 1)
        cast_specs.append(pl.BlockSpec((None, rows, n), lambda i, layer=layer, block=block: (layer, block(i), 0)))
        cast_args.append(arr)
        out_specs.append(pl.BlockSpec((rows, n), lambda i, block=block: (block(i), 0)))
        out_shape.append(jax.ShapeDtypeStruct((k_dim, n), BF16))

    def kern(*refs):
        row_refs, const_refs = refs[:n_row_refs], refs[n_row_refs:n_row_refs + n_const]
        n_in_refs = n_row_refs + n_const + len(casts)
        cast_in = refs[n_row_refs + n_const:n_in_refs]
        out_refs = refs[n_in_refs:n_in_refs + n_group_outs]
        cast_out = refs[n_in_refs + n_group_outs:]
        i = pl.program_id(0)
        r0 = o0 = 0
        for g, (first, count, n_in, n_out) in enumerate(ranges):
            ins, outs = row_refs[r0:r0 + n_in], out_refs[o0:o0 + n_out]
            r0, o0 = r0 + n_in, o0 + n_out

            @pl.when((i >= first) & (i < first + count))
            def _(ins=ins, outs=outs, g=g):
                body(*ins, *const_refs, *outs)
                if g == 0:
                    for x_ref, o_ref in zip(cast_in, cast_out):
                        o_ref[...] = x_ref[...].astype(BF16)

    outs = pl.pallas_call(
        kern,
        grid=(start,),
        in_specs=in_specs + const_specs + cast_specs,
        out_specs=out_specs,
        out_shape=out_shape,
        compiler_params=pltpu.CompilerParams(dimension_semantics=("arbitrary",), vmem_limit_bytes=VMEM_LIMIT),
        name=name,
    )(*args, *const_args, *cast_args)
    grouped, o0 = [], 0
    for _, _, _, n_out in ranges:
        grouped.append(list(outs[o0:o0 + n_out]))
        o0 += n_out
    return grouped, list(outs[n_group_outs:])


def _gla_prompt_body(q_ref, k_ref, v_ref, la_ref, o_ref, sfin_ref, st_ref):
    t = pl.program_id(1)

    @pl.when(t == 0)
    def _():
        st_ref[...] = jnp.zeros_like(st_ref)

    for base in range(0, q_ref.shape[0], GLA_TILE):
        _gla_tile(slice(base, base + GLA_TILE), q_ref, k_ref, v_ref, la_ref, o_ref, st_ref)

    @pl.when(t == pl.num_programs(1) - 1)
    def _():
        sfin_ref[0] = st_ref[...]


def _gla_tile(tile, q_ref, k_ref, v_ref, la_ref, o_ref, st_ref):
    tg = GLA_TILE
    c_len = GLA_CHUNK
    n_chunks = tg // c_len
    chunk_rows = [slice(ci * c_len, (ci + 1) * c_len) for ci in range(n_chunks)]
    row = lax.broadcasted_iota(jnp.int32, (tg, tg), 0)
    col = lax.broadcasted_iota(jnp.int32, (tg, tg), 1)
    lower_b = (row // c_len == col // c_len) & (col <= row)
    lower = jnp.where(lower_b, 1.0, 0.0).astype(BF16)
    hi, mid, lo = _split3(la_ref[tile, :])
    cum = _mm(lower, hi) + _mm(lower, mid) + _mm(lower, lo)
    lane_chunk = lax.broadcasted_iota(jnp.int32, (GLA_DK_HEAD, tg), 1) // c_len

    qd, att, kv = [], [], []
    for h in range(GLA_HEADS):
        kc = slice(h * GLA_DK_HEAD, (h + 1) * GLA_DK_HEAD)
        cum_h = cum[:, kc]
        tot_h = jnp.concatenate([jnp.broadcast_to(cum_h[r.stop - 1:r.stop, :], (c_len, GLA_DK_HEAD)) for r in chunk_rows], axis=0)
        k_h = k_ref[tile, kc]
        qd_h = (q_ref[tile, kc] * jnp.exp(cum_h)).astype(BF16)
        ki_h = (k_h * jnp.exp(-cum_h)).astype(BF16)
        ke_t = (k_h * jnp.exp(tot_h - cum_h)).T
        qd.append(qd_h)
        att.append(jnp.where(lower_b, _mm_nt(qd_h, ki_h), 0.0).astype(BF16))
        v_h = v_ref[tile, h * GLA_DV_HEAD:(h + 1) * GLA_DV_HEAD]
        kv.append([_mm(jnp.where(lane_chunk == ci, ke_t, 0.0).astype(BF16), v_h) for ci in range(n_chunks)])

    s_before = []
    for h in range(GLA_HEADS):
        kc = slice(h * GLA_DK_HEAD, (h + 1) * GLA_DK_HEAD)
        st = st_ref[h]
        starts = []
        for ci, r in enumerate(chunk_rows):
            starts.append(st.astype(BF16))
            e_col = jnp.exp(cum[r.stop - 8:r.stop, kc]).T[:, 7:8]
            st = e_col * st + kv[h][ci]
        st_ref[h] = st
        s_before.append(starts)

    for h in range(GLA_HEADS):
        kc = slice(h * GLA_DK_HEAD, (h + 1) * GLA_DK_HEAD)
        vc = slice(h * GLA_DV_HEAD, (h + 1) * GLA_DV_HEAD)
        o_intra = _mm(att[h], v_ref[tile, vc])
        for ci, r in enumerate(chunk_rows):
            o_ref[tile.start + r.start:tile.start + r.stop, vc] = o_intra[r] + _mm(qd[h][r], s_before[h][ci])


def _gla_prompt(q, k, v, la, batch, seq):
    tg = GLA_STEP
    nt = seq // tg
    qk_spec = pl.BlockSpec((tg, GLA_DK), lambda b, t: (b * nt + t, 0))
    v_spec = pl.BlockSpec((tg, GLA_DV), lambda b, t: (b * nt + t, 0))
    st_shape = (GLA_HEADS, GLA_DK_HEAD, GLA_DV_HEAD)
    return pl.pallas_call(
        _gla_prompt_body,
        grid=(batch, nt),
        in_specs=[qk_spec, qk_spec, v_spec, qk_spec],
        out_specs=[v_spec, pl.BlockSpec((1,) + st_shape, lambda b, t: (b, 0, 0, 0))],
        out_shape=[jax.ShapeDtypeStruct((batch * seq, GLA_DV), F32),
                   jax.ShapeDtypeStruct((batch,) + st_shape, F32)],
        scratch_shapes=[pltpu.VMEM(st_shape, F32)],
        compiler_params=pltpu.CompilerParams(dimension_semantics=("arbitrary", "arbitrary"), vmem_limit_bytes=VMEM_LIMIT),
        name="gla_prompt",
    )(q, k, v, la)


def _gla_sample_body(q_ref, k_ref, v_ref, la_ref, s_ref, o_ref, sn_ref):
    bt = q_ref.shape[0]
    for h in range(GLA_HEADS):
        kc = slice(h * GLA_DK_HEAD, (h + 1) * GLA_DK_HEAD)
        vc = slice(h * GLA_DV_HEAD, (h + 1) * GLA_DV_HEAD)
        a_t = jnp.exp(la_ref[:, kc]).T
        k_t = k_ref[:, kc].T
        q_t = q_ref[:, kc].T
        for j in range(bt):
            s_new = a_t[:, j:j + 1] * s_ref[j, h] + k_t[:, j:j + 1] * v_ref[j:j + 1, vc]
            sn_ref[j, h] = s_new
            o_ref[j:j + 1, vc] = jnp.sum(q_t[:, j:j + 1] * s_new, axis=0, keepdims=True)


def _gla_sample(q, k, v, la, state):
    bt = STATE_TILE
    nb = q.shape[0]
    row = lambda w: pl.BlockSpec((bt, w), lambda i: (i, 0))
    st_spec = pl.BlockSpec((bt, GLA_HEADS, GLA_DK_HEAD, GLA_DV_HEAD), lambda i: (i, 0, 0, 0))
    return pl.pallas_call(
        _gla_sample_body,
        grid=(nb // bt,),
        in_specs=[row(GLA_DK), row(GLA_DK), row(GLA_DV), row(GLA_DK), st_spec],
        out_specs=[row(GLA_DV), st_spec],
        out_shape=[jax.ShapeDtypeStruct((nb, GLA_DV), F32), jax.ShapeDtypeStruct(state.shape, F32)],
        compiler_params=pltpu.CompilerParams(dimension_semantics=("arbitrary",), vmem_limit_bytes=VMEM_LIMIT),
        name="gla_sample",
    )(q, k, v, la, state)


def _swa_attend_units(sink_ref, q_ref, k_full, v_full, has_prev, o_ref):
    w = SWA_WINDOW
    hd = SWA_HEAD_DIM
    tq = q_ref.shape[0]
    nkv = k_full.shape[0]
    lane_q = lax.broadcasted_iota(jnp.int32, (w, LANES), 1) < hd
    lane_kv = lax.broadcasted_iota(jnp.int32, (nkv, LANES), 1) < hd
    i = lax.broadcasted_iota(jnp.int32, (w, 2 * w), 0)
    j = lax.broadcasted_iota(jnp.int32, (w, 2 * w), 1)
    band = jnp.where(j < w, jnp.where(j >= i, 1, 0), jnp.where(j - w <= i, 1, 0))
    band_first = jnp.where(j < w, has_prev, 1) * band
    lane_2w = lax.broadcasted_iota(jnp.int32, (2 * w, LANES), 1) < hd
    ones_lo = jnp.where(lane_2w, 1.0, 0.0).astype(BF16)
    ones_hi = jnp.where(lane_2w, 0.0, 1.0).astype(BF16)
    c2 = (hd ** -0.5) * LOG2E

    k_prep, v_prep = [], []
    for p in range(SWA_KV // LANES):
        cols = slice(p * LANES, (p + 1) * LANES)
        k_p, v_p = k_full[:, cols], v_full[:, cols]
        k_prep.append((k_p.astype(BF16), pltpu.roll(k_p, hd, axis=1).astype(BF16)))
        v_r = pltpu.roll(v_p, hd, axis=1)
        v_prep.append(((jnp.where(lane_kv, v_p, 0.0).astype(BF16), jnp.where(lane_kv, 0.0, v_r).astype(BF16)),
                       (jnp.where(lane_kv, v_r, 0.0).astype(BF16), jnp.where(lane_kv, 0.0, v_p).astype(BF16))))

    def softmax_part(s, hh, mask):
        s2 = jnp.where(mask, s, NEG_BIG)
        sk2 = jnp.full((w, 1), sink_ref[hh], F32) * LOG2E
        m2 = jnp.maximum(jnp.max(s2, axis=-1, keepdims=True), sk2)
        return jnp.exp2(s2 - m2).astype(BF16), sk2 - m2

    for b in range(tq // w):
        rows = slice(b * w, (b + 1) * w)
        krows = slice(b * w, (b + 2) * w)
        mask = (band_first if b == 0 else band) > 0
        for p in range(SWA_KV // LANES):
            q_lo, q_hi = [], []
            for x in range(4):
                q_c = q_ref[rows, (4 * p + x) * LANES:(4 * p + x + 1) * LANES] * c2
                q_lo.append(jnp.where(lane_q, q_c, 0.0).astype(BF16))
                q_hi.append(jnp.where(lane_q, 0.0, q_c).astype(BF16))
            s_self = _mm_nt(jnp.concatenate([q_lo[0], q_lo[1], q_hi[2], q_hi[3]], axis=0), k_prep[p][0][krows])
            s_roll = _mm_nt(jnp.concatenate([q_hi[0], q_hi[1], q_lo[2], q_lo[3]], axis=0), k_prep[p][1][krows])
            for x in range(4):
                c = 4 * p + x
                gh = x // 2
                xr = slice(x * w, (x + 1) * w)
                s_lo, s_hi = (s_self[xr], s_roll[xr]) if gh == 0 else (s_roll[xr], s_self[xr])
                p_lo, d_lo = softmax_part(s_lo, 2 * c, mask)
                p_hi, d_hi = softmax_part(s_hi, 2 * c + 1, mask)
                v_lo, v_hi = v_prep[p][gh]
                rhs = jnp.concatenate([jnp.concatenate([v_lo[krows], ones_lo], axis=1),
                                       jnp.concatenate([v_hi[krows], ones_hi], axis=1)], axis=0)
                ext = _mm(jnp.concatenate([p_lo, p_hi], axis=1), rhs)
                den = ext[:, LANES:] + jnp.exp2(jnp.where(lane_q, d_lo, d_hi))
                o_ref[rows, c * LANES:(c + 1) * LANES] = (ext[:, :LANES] / den).astype(o_ref.dtype)
            yield


def _out_stage(at_bf16, h, wo_ref, bo_ref, gpost_ref, gfpre_ref, wup_ref, wdn_ref, gfpost_ref, between=None):
    m = _mm(at_bf16, wo_ref[...]) + bo_ref[...]
    h1 = h + _rms(m, gpost_ref[...])
    f = _ffn(_rms(h1, gfpre_ref[...]).astype(BF16), wup_ref, wdn_ref, between)
    return h1 + _rms(f, gfpost_ref[...])


def _swa_out_body(batch, nt, sink_ref, q_ref, kc_ref, kp_ref, vc_ref, vp_ref, h_ref, ats_ref, hs_ref,
                  wo_ref, bo_ref, gpost_ref, gfpre_ref, wup_ref, wdn_ref, gfpost_ref,
                  y_ref, ys_ref, attn_scr):
    i = pl.program_id(0)
    per_batch = nt + 1
    t = lax.rem(i, per_batch)
    is_prompt = i < batch * per_batch
    consts = (wo_ref, bo_ref, gpost_ref, gfpre_ref, wup_ref, wdn_ref, gfpost_ref)

    @pl.when(i == 0)
    def _():
        attn_scr[...] = jnp.zeros_like(attn_scr)

    @pl.when(is_prompt & (t < nt))
    def _():
        at_prev = attn_scr[...]
        k_full = jnp.concatenate([kp_ref[...], kc_ref[...]], axis=0)
        v_full = jnp.concatenate([vp_ref[...], vc_ref[...]], axis=0)
        units = _swa_attend_units(sink_ref, q_ref, k_full, v_full, jnp.minimum(t, 1), attn_scr)
        y_ref[...] = _out_stage(at_prev, h_ref[...], *consts, between=lambda: next(units, None))
        for _ in units:
            pass

    @pl.when(is_prompt & (t == nt))
    def _():
        y_ref[...] = _out_stage(attn_scr[...], h_ref[...], *consts)

    @pl.when(i == batch * per_batch)
    def _():
        ys_ref[...] = _out_stage(ats_ref[...].astype(BF16), hs_ref[...], *consts)


def _swa_out(sinks, q, k, v, h, attn_s, h_s, consts, batch, seq):
    tq = SWA_TILE
    w = SWA_WINDOW
    nt = seq // tq
    per_batch = nt + 1
    n_s = h_s.shape[0]

    def pos(i):
        ii = jnp.minimum(i, batch * per_batch - 1)
        return ii // per_batch, lax.rem(ii, per_batch)

    def att_tile(i):
        b, t = pos(i)
        return (b * nt + jnp.minimum(t, nt - 1), 0)

    def prev_block(i):
        b, t = pos(i)
        return (b * (seq // w) + jnp.maximum(jnp.minimum(t, nt - 1) * (tq // w) - 1, 0), 0)

    def out_tile(i):
        b, t = pos(i)
        return (b * nt + jnp.maximum(t - 1, 0), 0)

    whole = lambda arr: pl.BlockSpec(arr.shape, lambda i: (0, 0))
    const_specs, const_args = _const_plan(consts)
    return pl.pallas_call(
        functools.partial(_swa_out_body, batch, nt),
        grid=(batch * per_batch + 1,),
        in_specs=[pl.BlockSpec(memory_space=pltpu.SMEM),
                  pl.BlockSpec((tq, SWA_Q), att_tile), pl.BlockSpec((tq, SWA_KV), att_tile),
                  pl.BlockSpec((w, SWA_KV), prev_block), pl.BlockSpec((tq, SWA_KV), att_tile),
                  pl.BlockSpec((w, SWA_KV), prev_block), pl.BlockSpec((tq, D_MODEL), out_tile),
                  whole(attn_s), whole(h_s)] + const_specs,
        out_specs=[pl.BlockSpec((tq, D_MODEL), out_tile), whole(h_s)],
        out_shape=[jax.ShapeDtypeStruct((batch * seq, D_MODEL), F32), jax.ShapeDtypeStruct((n_s, D_MODEL), F32)],
        scratch_shapes=[pltpu.VMEM((tq, SWA_Q), BF16)],
        compiler_params=pltpu.CompilerParams(dimension_semantics=("arbitrary",), vmem_limit_bytes=VMEM_LIMIT),
        name="swa_out",
    )(sinks, q, k, k, v, v, h, attn_s, h_s, *const_args)


def _swa_sample_body(sk_ref, q_ref, kn_ref, vn_ref, ck_ref, cv_ref, o_ref, nk_ref, nv_ref):
    bt = q_ref.shape[0]
    w = ck_ref.shape[2]
    hd = SWA_HEAD_DIM
    hgroup = lax.broadcasted_iota(jnp.int32, (SWA_HEADS, 1), 0) // SWA_GROUP
    newest = lax.broadcasted_iota(jnp.int32, (SWA_KV, w), 1) == w - 1
    kn_t = kn_ref[...].T
    vn_t = vn_ref[...].T
    scale = hd ** -0.5
    sk = sk_ref[...]
    groups = [slice(g * hd, (g + 1) * hd) for g in range(SWA_KV_HEADS)]

    def per_head(pieces):
        out = pieces[0]
        for g in range(1, SWA_KV_HEADS):
            out = jnp.where(hgroup == g, pieces[g], out)
        return out

    for j in range(bt):
        nk_ref[j] = jnp.where(newest, kn_t[:, j:j + 1], pltpu.roll(ck_ref[j], w - 1, axis=1))
        nv_ref[j] = jnp.where(newest, vn_t[:, j:j + 1], pltpu.roll(cv_ref[j], w - 1, axis=1))

    s_old, s_new, v_sel = [], [], []
    for j in range(bt):
        q = q_ref[j]
        qb = q.astype(BF16)
        s_old.append(per_head([_mm(qb, ck_ref[j, rows, :].astype(BF16)) for rows in groups]))
        k_sel = per_head([kn_ref[j:j + 1, cols] for cols in groups])
        v_sel.append(per_head([vn_ref[j:j + 1, cols] for cols in groups]))
        s_new.append(jnp.sum(q * k_sel, axis=-1, keepdims=True))
    s_old = jnp.stack(s_old, axis=0) * scale
    s_new = jnp.stack(s_new, axis=0) * scale
    m = jnp.maximum(jnp.maximum(jnp.max(s_old, axis=-1, keepdims=True), s_new), sk)
    p_old = jnp.exp(s_old - m)
    p_new = jnp.exp(s_new - m)
    inv = 1.0 / (jnp.sum(p_old, axis=-1, keepdims=True) + p_new + jnp.exp(sk - m))
    p_old = p_old.astype(BF16)
    for j in range(bt):
        o = per_head([_mm_nt(p_old[j], cv_ref[j, rows, :].astype(BF16)) for rows in groups])
        o_ref[j] = (o + p_new[j] * v_sel[j]) * inv[j]


def _swa_sample(sinks, q3, k_new, v_new, cache_k, cache_v):
    bt = SEQ_TILE
    nb, _, w = cache_k.shape
    assert w == LANES
    row = lambda width: pl.BlockSpec((bt, width), lambda i: (i, 0))
    q_spec = pl.BlockSpec((bt, SWA_HEADS, SWA_HEAD_DIM), lambda i: (i, 0, 0))
    c_spec = pl.BlockSpec((bt, SWA_KV, w), lambda i: (i, 0, 0))
    return pl.pallas_call(
        _swa_sample_body,
        grid=(nb // bt,),
        in_specs=[pl.BlockSpec((SWA_HEADS, 1), lambda i: (0, 0)), q_spec, row(SWA_KV), row(SWA_KV), c_spec, c_spec],
        out_specs=[q_spec, c_spec, c_spec],
        out_shape=[jax.ShapeDtypeStruct((nb, SWA_HEADS, SWA_HEAD_DIM), F32),
                   jax.ShapeDtypeStruct(cache_k.shape, F32), jax.ShapeDtypeStruct(cache_v.shape, F32)],
        compiler_params=pltpu.CompilerParams(dimension_semantics=("arbitrary",)),
        name="swa_sample",
    )(sinks, q3, k_new, v_new, cache_k, cache_v)


def _rope_tables(pos):
    inv = jnp.power(ROPE_THETA, -jnp.arange(ROPE_HALF, dtype=F32) * 2.0 / ROPE_DIM)
    ang = pos.astype(F32)[:, None] * inv[None, :]
    cos, sin = jnp.cos(ang), jnp.sin(ang)
    n = pos.shape[0]
    rest = SWA_HEAD_DIM - ROPE_DIM
    rc = jnp.concatenate([cos, cos, jnp.ones((n, rest), F32)], axis=-1)
    ra = jnp.concatenate([-sin, jnp.zeros((n, ROPE_HALF + rest), F32)], axis=-1)
    rb = jnp.concatenate([jnp.zeros((n, ROPE_HALF), F32), sin, jnp.zeros((n, rest), F32)], axis=-1)
    reps = LANES // SWA_HEAD_DIM
    return tuple(jnp.tile(t, (1, reps)) for t in (rc, ra, rb))


def kernel(x_prompt, x_sample, state_gla, cache_swa_k, cache_swa_v, gla_w_in, gla_w_gate2, gla_b_gate, gla_g_head, gla_w_out, swa_w_qkv, swa_b_qkv, swa_sinks, swa_w_out, swa_b_out, norm_mix_pre, norm_mix_post, norm_ffn_pre, norm_ffn_post, ffn_w_up, ffn_w_down):
    batch, seq, _ = x_prompt.shape
    dec_batch, dec_seq, _ = x_sample.shape
    assert dec_seq == 1 and seq % SWA_WINDOW == 0
    past_len = seq
    n_p, n_s = batch * seq, dec_batch * dec_seq
    xp = x_prompt.reshape(n_p, D_MODEL)
    xs = x_sample.reshape(n_s, D_MODEL)

    w_in = gla_w_in[0].astype(BF16)
    w_g2 = gla_w_gate2[0].astype(BF16)
    b_g = gla_b_gate[0][None, :]
    g_head = gla_g_head[0][None, :]
    b_qkv = swa_b_qkv[0][None, :]
    b_sout = swa_b_out[0][None, :]
    row = lambda t, i: t[i][None, :]
    in0_widths = [GLA_DK, GLA_DK, GLA_DV, GLA_DV, GLA_DK]
    mid_widths = [D_MODEL, SWA_Q, SWA_KV, SWA_KV]

    tm, ts = TOKEN_TILE, n_s
    ((q, k, v, r, la), (qs, ks, vs, rs, las)), (w_gout, w_up0, w_dn0, w_qkv) = _tok_call(
        _in0_body,
        [(n_p, tm, [(xp, None)], in0_widths, [F32, F32, BF16, F32, F32]),
         (n_s, ts, [(xs, None)], in0_widths, [F32] * 5)],
        [row(norm_mix_pre, 0), w_in, w_g2, b_g], "in0",
        casts=[(gla_w_out, 0), (ffn_w_up, 0), (ffn_w_down, 0), (swa_w_qkv, 0)])
    o, s_fin_p = _gla_prompt(q, k, v, la, batch, seq)
    o_s, s_new = _gla_sample(qs, ks, vs, las, state_gla[0])
    tabs = _rope_tables(jnp.arange(seq))
    tabs_s = _rope_tables(jnp.full((n_s,), past_len, jnp.int32))
    tab_map = lambda t: (t % (seq // tm), 0)
    mid_consts = [g_head, w_gout, row(norm_mix_post, 0), row(norm_ffn_pre, 0), w_up0, w_dn0, row(norm_ffn_post, 0),
                  row(norm_mix_pre, 1), w_qkv, b_qkv]
    ((h2, q1, k1, v1), (h2s, q1s, k1s, v1s)), (w_sout, w_up1, w_dn1) = _tok_call(
        _mid_body,
        [(n_p, tm, [(o, None), (r, None), (xp, None)] + [(t, tab_map) for t in tabs], mid_widths, [F32] * 4),
         (n_s, ts, [(o_s, None), (rs, None), (xs, None)] + [(t, None) for t in tabs_s], mid_widths, [F32] * 4)],
        mid_consts, "mid",
        casts=[(swa_w_out, 0), (ffn_w_up, 1), (ffn_w_down, 1)])
    out_consts = [w_sout, b_sout, row(norm_mix_post, 1), row(norm_ffn_pre, 1), w_up1, w_dn1, row(norm_ffn_post, 1)]
    win = cache_swa_k.shape[2]
    to_t = lambda c: jnp.transpose(c[0].reshape(dec_batch, win, SWA_KV), (0, 2, 1))
    from_t = lambda c: jnp.transpose(c, (0, 2, 1)).reshape(1, dec_batch, win, SWA_KV_HEADS, SWA_HEAD_DIM)
    attn_s, nk, nv = _swa_sample(swa_sinks[0][:, None], q1s.reshape(n_s, SWA_HEADS, SWA_HEAD_DIM), k1s, v1s,
                                 to_t(cache_swa_k), to_t(cache_swa_v))
    y_p, y_s = _swa_out(swa_sinks[0], q1, k1, v1, h2, attn_s.reshape(n_s, SWA_Q), h2s, out_consts, batch, seq)
    wp = min(SWA_WINDOW, seq)
    tail = lambda t: t.reshape(batch, seq, SWA_KV)[:, seq - wp:].reshape(batch, wp, SWA_KV_HEADS, SWA_HEAD_DIM)
    k_tail, v_tail = tail(k1), tail(v1)

    return (y_p.reshape(batch, seq, D_MODEL), y_s.reshape(dec_batch, dec_seq, D_MODEL),
            s_fin_p[None], s_new[None], k_tail[None], v_tail[None], from_t(nk), from_t(nv))
```

```python
import functools

import jax
import jax.numpy as jnp
from jax import lax
from jax.experimental import pallas as pl
from jax.experimental.pallas import tpu as pltpu

F32 = jnp.float32
BF16 = jnp.bfloat16

D_MODEL = 1024
D_FF = 4 * D_MODEL
NORM_EPS = 1e-6

GLA_HEADS = 4
GLA_DK = D_MODEL // 2
GLA_DV = D_MODEL
GLA_DK_HEAD = GLA_DK // GLA_HEADS
GLA_DV_HEAD = GLA_DV // GLA_HEADS
GLA_GATE_RANK = 16
GLA_TAU = 16.0
GLA_CHUNK = 64
GLA_MAIN = 2 * GLA_DK + 2 * GLA_DV

SWA_HEAD_DIM = 64
SWA_HEADS = D_MODEL // SWA_HEAD_DIM
SWA_KV_HEADS = 4
SWA_GROUP = SWA_HEADS // SWA_KV_HEADS
SWA_WINDOW = 128
SWA_Q = SWA_HEADS * SWA_HEAD_DIM
SWA_KV = SWA_KV_HEADS * SWA_HEAD_DIM
SWA_QKV = SWA_Q + 2 * SWA_KV
ROPE_THETA = 500000.0
ROPE_DIM = SWA_HEAD_DIM // 4
ROPE_HALF = ROPE_DIM // 2

LANES = 128
FFN_CHUNK = 512
TOKEN_TILE = 512
GLA_TILE = 256
GLA_STEP = 512
SWA_TILE = 512
SEQ_TILE = 8
STATE_TILE = 16
VMEM_LIMIT = 58 * 1024 * 1024
NEG_BIG = -1e30
LOG2E = 1.4426950408889634


def _mm(a, b):
    return jnp.dot(a, b, preferred_element_type=F32)


def _mm_nt(a, b):
    return lax.dot_general(a, b, (((1,), (1,)), ((), ())), preferred_element_type=F32)


def _mm_tn(a, b):
    return lax.dot_general(a, b, (((0,), (0,)), ((), ())), preferred_element_type=F32)


def _rms(x, g):
    ms = jnp.mean(x * x, axis=-1, keepdims=True)
    return x * lax.rsqrt(ms + NORM_EPS) * g


def _split3(x):
    hi = x.astype(BF16)
    r1 = x - hi.astype(F32)
    mid = r1.astype(BF16)
    lo = (r1 - mid.astype(F32)).astype(BF16)
    return hi, mid, lo


def _ffn(a_bf16, wup_ref, wdn_ref, between=None):
    n_slices = D_FF // FFN_CHUNK
    cols = [slice(c * FFN_CHUNK, (c + 1) * FFN_CHUNK) for c in range(n_slices)]
    acc = None
    u_next = _mm(a_bf16, wup_ref[:, cols[0]])
    for c in range(n_slices):
        u = u_next
        if c + 1 < n_slices:
            u_next = _mm(a_bf16, wup_ref[:, cols[c + 1]])
        u = jnp.square(jnp.maximum(u, 0.0)).astype(BF16)
        p = _mm(u, wdn_ref[cols[c], :])
        acc = p if acc is None else acc + p
        if between is not None:
            between()
    return acc


def _in0_body(x_ref, g_ref, w_ref, wg_ref, bg_ref, q_ref, k_ref, v_ref, r_ref, la_ref):
    a = _rms(x_ref[...], g_ref[...]).astype(BF16)
    z = _mm(a, w_ref[:, GLA_MAIN:GLA_MAIN + GLA_GATE_RANK]).astype(BF16)
    q_ref[...] = _mm(a, w_ref[:, 0:GLA_DK]) * (GLA_DK_HEAD ** -0.5)
    k_ref[...] = _mm(a, w_ref[:, GLA_DK:2 * GLA_DK])
    zg = _mm(z, wg_ref[...]) + bg_ref[...]
    la_ref[...] = (jnp.minimum(zg, 0.0) - jnp.log1p(jnp.exp(-jnp.abs(zg)))) * (1.0 / GLA_TAU)
    for c in range(GLA_DV // 512):
        cols = slice(c * 512, (c + 1) * 512)
        v_ref[:, cols] = _mm(a, w_ref[:, 2 * GLA_DK + c * 512:2 * GLA_DK + (c + 1) * 512]).astype(v_ref.dtype)
        r_ref[:, cols] = _mm(a, w_ref[:, 2 * GLA_DK + GLA_DV + c * 512:2 * GLA_DK + GLA_DV + (c + 1) * 512])


def _mid_body(o_ref, r_ref, h_ref, lc_ref, ls_ref, bc_ref, bs_ref,
              gh_ref, wo_ref, gpost_ref, gfpre_ref, wup_ref, wdn_ref, gfpost_ref,
              gpre1_ref, wqkv_ref, bqkv_ref,
              h2_ref, q1_ref, k1_ref, v1_ref):
    m = None
    for hh in range(GLA_HEADS):
        cols = slice(hh * GLA_DV_HEAD, (hh + 1) * GLA_DV_HEAD)
        on = _rms(o_ref[:, cols], gh_ref[...])
        r = r_ref[:, cols]
        u = (on * (r * (1.0 / (1.0 + jnp.exp(-r))))).astype(BF16)
        p = _mm(u, wo_ref[cols, :])
        m = p if m is None else m + p
    h1 = h_ref[...] + _rms(m, gpost_ref[...])
    f = _ffn(_rms(h1, gfpre_ref[...]).astype(BF16), wup_ref, wdn_ref)
    h2 = h1 + _rms(f, gfpost_ref[...])
    h2_ref[...] = h2
    a3 = _rms(h2, gpre1_ref[...]).astype(BF16)
    lc, ls, bc, bs = lc_ref[...], ls_ref[...], bc_ref[0:1, :], bs_ref[0:1, :]
    cos_t = bc * lc - bs * ls
    sin_t = bs * lc + bc * ls
    d = lax.broadcasted_iota(jnp.int32, (1, LANES), 1) % SWA_HEAD_DIM
    rc = jnp.where(d < ROPE_DIM, cos_t, 1.0)
    ra = jnp.where(d < ROPE_HALF, -sin_t, 0.0)
    rb = jnp.where(d < ROPE_HALF, 0.0, jnp.where(d < ROPE_DIM, sin_t, 0.0))
    wide = 2 * LANES
    for c2 in range((SWA_Q + SWA_KV) // wide):
        x2 = _mm(a3, wqkv_ref[:, c2 * wide:(c2 + 1) * wide]) + bqkv_ref[:, c2 * wide:(c2 + 1) * wide]
        for half in range(2):
            c = 2 * c2 + half
            x = x2[:, half * LANES:(half + 1) * LANES]
            y = x * rc + pltpu.roll(x, LANES - ROPE_HALF, axis=1) * ra + pltpu.roll(x, ROPE_HALF, axis=1) * rb
            if c < SWA_Q // LANES:
                q1_ref[:, c * LANES:(c + 1) * LANES] = y
            else:
                k1_ref[:, c * LANES - SWA_Q:(c + 1) * LANES - SWA_Q] = y
    v1_ref[...] = _mm(a3, wqkv_ref[:, SWA_Q + SWA_KV:SWA_QKV]) + bqkv_ref[:, SWA_Q + SWA_KV:SWA_QKV]


def _const_plan(const_inputs):
    in_specs, args = [], []
    for entry in const_inputs:
        if isinstance(entry, tuple):
            arr, layer = entry
            spec = pl.BlockSpec((None,) + arr.shape[1:], lambda i, layer=layer: (layer, 0, 0), pipeline_mode=pl.Buffered(1))
        else:
            arr = entry
            spec = pl.BlockSpec(arr.shape, lambda i: (0, 0), pipeline_mode=pl.Buffered(1))
        in_specs.append(spec)
        args.append(arr)
    return in_specs, args


def _tok_call(body, groups, const_inputs, name, casts=(), convert_first=()):
    in_specs, args, out_specs, out_shape = [], [], [], []
    ranges, start = [], 0
    for n_rows, tm, row_inputs, out_widths, out_dtypes in groups:
        assert n_rows % tm == 0
        count = n_rows // tm
        local = lambda i, start=start, count=count: jnp.clip(i - start, 0, count - 1)
        mode = dict(pipeline_mode=pl.Buffered(1)) if count == 1 else {}
        for arr, imap, *block_rows in row_inputs:
            imap = imap if imap is not None else (lambda t: (t, 0))
            rows = block_rows[0] if block_rows else tm
            in_specs.append(pl.BlockSpec((rows, arr.shape[1]), lambda i, imap=imap, local=local: imap(local(i)), **mode))
            args.append(arr)
        for w, dt in zip(out_widths, out_dtypes):
            out_specs.append(pl.BlockSpec((tm, w), lambda i, local=local: (local(i), 0)))
            out_shape.append(jax.ShapeDtypeStruct((n_rows, w), dt))
        ranges.append((start, count, len(row_inputs), len(out_widths)))
        start += count
    n_row_refs = len(in_specs)
    n_const = len(const_inputs)
    n_group_outs = len(out_specs)
    const_specs, const_args = _const_plan(const_inputs)
    cast_steps = ranges[0][1]
    cast_specs, cast_args = [], []
    for arr, layer in casts:
        _, k_dim, n = arr.shape
        assert k_dim % (cast_steps * 16) == 0
        rows = k_dim // cast_steps
        block = lambda i: jnp.minimum(i, cast_steps - 1)
        cast_specs.append(pl.BlockSpec((None, rows, n), lambda i, layer=layer, block=block: (layer, block(i), 0)))
        cast_args.append(arr)
        out_specs.append(pl.BlockSpec((rows, n), lambda i, block=block: (block(i), 0)))
        out_shape.append(jax.ShapeDtypeStruct((k_dim, n), BF16))

    def kern(*refs):
        row_refs, raw_consts = refs[:n_row_refs], refs[n_row_refs:n_row_refs + n_const]
        n_in_refs = n_row_refs + n_const + len(casts)
        cast_in = refs[n_row_refs + n_const:n_in_refs]
        out_refs = refs[n_in_refs:n_in_refs + n_group_outs]
        cast_out = refs[n_in_refs + n_group_outs:n_in_refs + n_group_outs + len(casts)]
        own_bf16 = refs[n_in_refs + n_group_outs + len(casts):]
        i = pl.program_id(0)

        @pl.when(i == 0)
        def _():
            for j, s_ref in zip(convert_first, own_bf16):
                s_ref[...] = raw_consts[j][...].astype(BF16)

        const_refs = list(raw_consts)
        for j, s_ref in zip(convert_first, own_bf16):
            const_refs[j] = s_ref
        r0 = o0 = 0
        for g, (first, count, n_in, n_out) in enumerate(ranges):
            ins, outs = row_refs[r0:r0 + n_in], out_refs[o0:o0 + n_out]
            r0, o0 = r0 + n_in, o0 + n_out

            @pl.when((i >= first) & (i < first + count))
            def _(ins=ins, outs=outs, g=g):
                body(*ins, *const_refs, *outs)
                if g == 0:
                    for x_ref, o_ref in zip(cast_in, cast_out):
                        o_ref[...] = x_ref[...].astype(BF16)

    outs = pl.pallas_call(
        kern,
        grid=(start,),
        in_specs=in_specs + const_specs + cast_specs,
        out_specs=out_specs,
        out_shape=out_shape,
        scratch_shapes=[pltpu.VMEM(const_inputs[j].shape, BF16) for j in convert_first],
        compiler_params=pltpu.CompilerParams(dimension_semantics=("arbitrary",), vmem_limit_bytes=VMEM_LIMIT),
        name=name,
    )(*args, *const_args, *cast_args)
    grouped, o0 = [], 0
    for _, _, _, n_out in ranges:
        grouped.append(list(outs[o0:o0 + n_out]))
        o0 += n_out
    return grouped, list(outs[n_group_outs:])


def _gla_prompt_body(q_ref, k_ref, v_ref, la_ref, o_ref, sfin_ref, st_ref):
    t = pl.program_id(1)

    @pl.when(t == 0)
    def _():
        st_ref[...] = jnp.zeros_like(st_ref)

    for base in range(0, q_ref.shape[0], GLA_TILE):
        _gla_tile(slice(base, base + GLA_TILE), q_ref, k_ref, v_ref, la_ref, o_ref, st_ref)

    @pl.when(t == pl.num_programs(1) - 1)
    def _():
        sfin_ref[0] = st_ref[...]


def _gla_tile(tile, q_ref, k_ref, v_ref, la_ref, o_ref, st_ref):
    tg = GLA_TILE
    c_len = GLA_CHUNK
    n_chunks = tg // c_len
    chunk_rows = [slice(ci * c_len, (ci + 1) * c_len) for ci in range(n_chunks)]
    row = lax.broadcasted_iota(jnp.int32, (tg, tg), 0)
    col = lax.broadcasted_iota(jnp.int32, (tg, tg), 1)
    lower_b = (row // c_len == col // c_len) & (col <= row)
    lower = jnp.where(lower_b, 1.0, 0.0).astype(BF16)
    hi, mid, lo = _split3(la_ref[tile, :])
    cum = _mm(lower, hi) + _mm(lower, mid) + _mm(lower, lo)
    lane_chunk = lax.broadcasted_iota(jnp.int32, (GLA_DK_HEAD, tg), 1) // c_len

    qd, att, kv = [], [], []
    for h in range(GLA_HEADS):
        kc = slice(h * GLA_DK_HEAD, (h + 1) * GLA_DK_HEAD)
        cum_h = cum[:, kc]
        tot_h = jnp.concatenate([jnp.broadcast_to(cum_h[r.stop - 1:r.stop, :], (c_len, GLA_DK_HEAD)) for r in chunk_rows], axis=0)
        k_h = k_ref[tile, kc]
        qd_h = (q_ref[tile, kc] * jnp.exp(cum_h)).astype(BF16)
        ki_h = (k_h * jnp.exp(-cum_h)).astype(BF16)
        ke_t = (k_h * jnp.exp(tot_h - cum_h)).T
        qd.append(qd_h)
        att.append(jnp.where(lower_b, _mm_nt(qd_h, ki_h), 0.0).astype(BF16))
        v_h = v_ref[tile, h * GLA_DV_HEAD:(h + 1) * GLA_DV_HEAD]
        kv.append([_mm(jnp.where(lane_chunk == ci, ke_t, 0.0).astype(BF16), v_h) for ci in range(n_chunks)])

    s_before = []
    for h in range(GLA_HEADS):
        kc = slice(h * GLA_DK_HEAD, (h + 1) * GLA_DK_HEAD)
        st = st_ref[h]
        starts = []
        for ci, r in enumerate(chunk_rows):
            starts.append(st.astype(BF16))
            e_col = jnp.exp(cum[r.stop - 8:r.stop, kc]).T[:, 7:8]
            st = e_col * st + kv[h][ci]
        st_ref[h] = st
        s_before.append(starts)

    for h in range(GLA_HEADS):
        kc = slice(h * GLA_DK_HEAD, (h + 1) * GLA_DK_HEAD)
        vc = slice(h * GLA_DV_HEAD, (h + 1) * GLA_DV_HEAD)
        o_intra = _mm(att[h], v_ref[tile, vc])
        for ci, r in enumerate(chunk_rows):
            o_ref[tile.start + r.start:tile.start + r.stop, vc] = o_intra[r] + _mm(qd[h][r], s_before[h][ci])


def _gla_prompt(q, k, v, la, batch, seq):
    tg = GLA_STEP
    nt = seq // tg
    qk_spec = pl.BlockSpec((tg, GLA_DK), lambda b, t: (b * nt + t, 0))
    v_spec = pl.BlockSpec((tg, GLA_DV), lambda b, t: (b * nt + t, 0))
    st_shape = (GLA_HEADS, GLA_DK_HEAD, GLA_DV_HEAD)
    return pl.pallas_call(
        _gla_prompt_body,
        grid=(batch, nt),
        in_specs=[qk_spec, qk_spec, v_spec, qk_spec],
        out_specs=[v_spec, pl.BlockSpec((1,) + st_shape, lambda b, t: (b, 0, 0, 0))],
        out_shape=[jax.ShapeDtypeStruct((batch * seq, GLA_DV), F32),
                   jax.ShapeDtypeStruct((batch,) + st_shape, F32)],
        scratch_shapes=[pltpu.VMEM(st_shape, F32)],
        compiler_params=pltpu.CompilerParams(dimension_semantics=("arbitrary", "arbitrary"), vmem_limit_bytes=VMEM_LIMIT),
        name="gla_prompt",
    )(q, k, v, la)


def _gla_sample_body(q_ref, k_ref, v_ref, la_ref, s_ref, o_ref, sn_ref):
    bt = q_ref.shape[0]
    for h in range(GLA_HEADS):
        kc = slice(h * GLA_DK_HEAD, (h + 1) * GLA_DK_HEAD)
        vc = slice(h * GLA_DV_HEAD, (h + 1) * GLA_DV_HEAD)
        a_t = jnp.exp(la_ref[:, kc]).T
        k_t = k_ref[:, kc].T
        q_t = q_ref[:, kc].T
        for j in range(bt):
            s_new = a_t[:, j:j + 1] * s_ref[j, h] + k_t[:, j:j + 1] * v_ref[j:j + 1, vc]
            sn_ref[j, h] = s_new
            o_ref[j:j + 1, vc] = jnp.sum(q_t[:, j:j + 1] * s_new, axis=0, keepdims=True)


def _gla_sample(q, k, v, la, state):
    bt = STATE_TILE
    nb = q.shape[0]
    row = lambda w: pl.BlockSpec((bt, w), lambda i: (i, 0))
    st_spec = pl.BlockSpec((bt, GLA_HEADS, GLA_DK_HEAD, GLA_DV_HEAD), lambda i: (i, 0, 0, 0))
    return pl.pallas_call(
        _gla_sample_body,
        grid=(nb // bt,),
        in_specs=[row(GLA_DK), row(GLA_DK), row(GLA_DV), row(GLA_DK), st_spec],
        out_specs=[row(GLA_DV), st_spec],
        out_shape=[jax.ShapeDtypeStruct((nb, GLA_DV), F32), jax.ShapeDtypeStruct(state.shape, F32)],
        compiler_params=pltpu.CompilerParams(dimension_semantics=("arbitrary",), vmem_limit_bytes=VMEM_LIMIT),
        name="gla_sample",
    )(q, k, v, la, state)


def _swa_attend_units(sink_ref, q_ref, k_full, v_full, has_prev, o_ref):
    w = SWA_WINDOW
    hd = SWA_HEAD_DIM
    tq = q_ref.shape[0]
    nkv = k_full.shape[0]
    lane_q = lax.broadcasted_iota(jnp.int32, (w, LANES), 1) < hd
    lane_kv = lax.broadcasted_iota(jnp.int32, (nkv, LANES), 1) < hd
    i = lax.broadcasted_iota(jnp.int32, (w, 2 * w), 0)
    j = lax.broadcasted_iota(jnp.int32, (w, 2 * w), 1)
    band = jnp.where(j < w, jnp.where(j >= i, 1, 0), jnp.where(j - w <= i, 1, 0))
    band_first = jnp.where(j < w, has_prev, 1) * band
    lane_2w = lax.broadcasted_iota(jnp.int32, (2 * w, LANES), 1) < hd
    ones_lo = jnp.where(lane_2w, 1.0, 0.0).astype(BF16)
    ones_hi = jnp.where(lane_2w, 0.0, 1.0).astype(BF16)
    c2 = (hd ** -0.5) * LOG2E

    k_prep, v_prep = [], []
    for p in range(SWA_KV // LANES):
        cols = slice(p * LANES, (p + 1) * LANES)
        k_p, v_p = k_full[:, cols], v_full[:, cols]
        k_prep.append((k_p.astype(BF16), pltpu.roll(k_p, hd, axis=1).astype(BF16)))
        v_r = pltpu.roll(v_p, hd, axis=1)
        v_prep.append(((jnp.where(lane_kv, v_p, 0.0).astype(BF16), jnp.where(lane_kv, 0.0, v_r).astype(BF16)),
                       (jnp.where(lane_kv, v_r, 0.0).astype(BF16), jnp.where(lane_kv, 0.0, v_p).astype(BF16))))

    def softmax_part(s, hh, mask):
        s2 = jnp.where(mask, s, NEG_BIG)
        sk2 = jnp.full((w, 1), sink_ref[hh], F32) * LOG2E
        m2 = jnp.maximum(jnp.max(s2, axis=-1, keepdims=True), sk2)
        return jnp.exp2(s2 - m2).astype(BF16), sk2 - m2

    for b in range(tq // w):
        rows = slice(b * w, (b + 1) * w)
        krows = slice(b * w, (b + 2) * w)
        mask = (band_first if b == 0 else band) > 0
        for p in range(SWA_KV // LANES):
            q_lo, q_hi = [], []
            for x in range(4):
                q_c = q_ref[rows, (4 * p + x) * LANES:(4 * p + x + 1) * LANES] * c2
                q_lo.append(jnp.where(lane_q, q_c, 0.0).astype(BF16))
                q_hi.append(jnp.where(lane_q, 0.0, q_c).astype(BF16))
            s_self = _mm_nt(jnp.concatenate([q_lo[0], q_lo[1], q_hi[2], q_hi[3]], axis=0), k_prep[p][0][krows])
            s_roll = _mm_nt(jnp.concatenate([q_hi[0], q_hi[1], q_lo[2], q_lo[3]], axis=0), k_prep[p][1][krows])
            for x in range(4):
                c = 4 * p + x
                gh = x // 2
                xr = slice(x * w, (x + 1) * w)
                s_lo, s_hi = (s_self[xr], s_roll[xr]) if gh == 0 else (s_roll[xr], s_self[xr])
                p_lo, d_lo = softmax_part(s_lo, 2 * c, mask)
                p_hi, d_hi = softmax_part(s_hi, 2 * c + 1, mask)
                v_lo, v_hi = v_prep[p][gh]
                rhs = jnp.concatenate([jnp.concatenate([v_lo[krows], ones_lo], axis=1),
                                       jnp.concatenate([v_hi[krows], ones_hi], axis=1)], axis=0)
                ext = _mm(jnp.concatenate([p_lo, p_hi], axis=1), rhs)
                den = ext[:, LANES:] + jnp.exp2(jnp.where(lane_q, d_lo, d_hi))
                o_ref[rows, c * LANES:(c + 1) * LANES] = (ext[:, :LANES] / den).astype(o_ref.dtype)
            yield


def _out_stage(at_bf16, h, wo_ref, bo_ref, gpost_ref, gfpre_ref, wup_ref, wdn_ref, gfpost_ref, between=None):
    m = _mm(at_bf16, wo_ref[...]) + bo_ref[...]
    h1 = h + _rms(m, gpost_ref[...])
    f = _ffn(_rms(h1, gfpre_ref[...]).astype(BF16), wup_ref, wdn_ref, between)
    return h1 + _rms(f, gfpost_ref[...])


def _swa_out_body(batch, nt, sink_ref, q_ref, kc_ref, kp_ref, vc_ref, vp_ref, h_ref, ats_ref, hs_ref,
                  wo_ref, bo_ref, gpost_ref, gfpre_ref, wup_ref, wdn_ref, gfpost_ref,
                  y_ref, ys_ref, attn_scr):
    i = pl.program_id(0)
    per_batch = nt + 1
    t = lax.rem(i, per_batch)
    is_prompt = i < batch * per_batch
    consts = (wo_ref, bo_ref, gpost_ref, gfpre_ref, wup_ref, wdn_ref, gfpost_ref)

    @pl.when(i == 0)
    def _():
        attn_scr[...] = jnp.zeros_like(attn_scr)

    @pl.when(is_prompt & (t < nt))
    def _():
        at_prev = attn_scr[...]
        k_full = jnp.concatenate([kp_ref[...], kc_ref[...]], axis=0)
        v_full = jnp.concatenate([vp_ref[...], vc_ref[...]], axis=0)
        units = _swa_attend_units(sink_ref, q_ref, k_full, v_full, jnp.minimum(t, 1), attn_scr)
        y_ref[...] = _out_stage(at_prev, h_ref[...], *consts, between=lambda: next(units, None))
        for _ in units:
            pass

    @pl.when(is_prompt & (t == nt))
    def _():
        y_ref[...] = _out_stage(attn_scr[...], h_ref[...], *consts)

    @pl.when(i == batch * per_batch)
    def _():
        ys_ref[...] = _out_stage(ats_ref[...].astype(BF16), hs_ref[...], *consts)


def _swa_out(sinks, q, k, v, h, attn_s, h_s, consts, batch, seq):
    tq = SWA_TILE
    w = SWA_WINDOW
    nt = seq // tq
    per_batch = nt + 1
    n_s = h_s.shape[0]

    def pos(i):
        ii = jnp.minimum(i, batch * per_batch - 1)
        return ii // per_batch, lax.rem(ii, per_batch)

    def att_tile(i):
        b, t = pos(i)
        return (b * nt + jnp.minimum(t, nt - 1), 0)

    def prev_block(i):
        b, t = pos(i)
        return (b * (seq // w) + jnp.maximum(jnp.minimum(t, nt - 1) * (tq // w) - 1, 0), 0)

    def out_tile(i):
        b, t = pos(i)
        return (b * nt + jnp.maximum(t - 1, 0), 0)

    whole = lambda arr: pl.BlockSpec(arr.shape, lambda i: (0, 0))
    const_specs, const_args = _const_plan(consts)
    return pl.pallas_call(
        functools.partial(_swa_out_body, batch, nt),
        grid=(batch * per_batch + 1,),
        in_specs=[pl.BlockSpec(memory_space=pltpu.SMEM),
                  pl.BlockSpec((tq, SWA_Q), att_tile), pl.BlockSpec((tq, SWA_KV), att_tile),
                  pl.BlockSpec((w, SWA_KV), prev_block), pl.BlockSpec((tq, SWA_KV), att_tile),
                  pl.BlockSpec((w, SWA_KV), prev_block), pl.BlockSpec((tq, D_MODEL), out_tile),
                  whole(attn_s), whole(h_s)] + const_specs,
        out_specs=[pl.BlockSpec((tq, D_MODEL), out_tile), whole(h_s)],
        out_shape=[jax.ShapeDtypeStruct((batch * seq, D_MODEL), F32), jax.ShapeDtypeStruct((n_s, D_MODEL), F32)],
        scratch_shapes=[pltpu.VMEM((tq, SWA_Q), BF16)],
        compiler_params=pltpu.CompilerParams(dimension_semantics=("arbitrary",), vmem_limit_bytes=VMEM_LIMIT),
        name="swa_out",
    )(sinks, q, k, k, v, v, h, attn_s, h_s, *const_args)


def _swa_sample_body(sk_ref, q_ref, kn_ref, vn_ref, ck_ref, cv_ref, o_ref, nk_ref, nv_ref):
    bt = q_ref.shape[0]
    w = ck_ref.shape[2]
    hd = SWA_HEAD_DIM
    hgroup = lax.broadcasted_iota(jnp.int32, (SWA_HEADS, 1), 0) // SWA_GROUP
    newest = lax.broadcasted_iota(jnp.int32, (SWA_KV, w), 1) == w - 1
    kn_t = kn_ref[...].T
    vn_t = vn_ref[...].T
    scale = hd ** -0.5
    sk = sk_ref[...]
    groups = [slice(g * hd, (g + 1) * hd) for g in range(SWA_KV_HEADS)]

    def per_head(pieces):
        out = pieces[0]
        for g in range(1, SWA_KV_HEADS):
            out = jnp.where(hgroup == g, pieces[g], out)
        return out

    for j in range(bt):
        nk_ref[j] = jnp.where(newest, kn_t[:, j:j + 1], pltpu.roll(ck_ref[j], w - 1, axis=1))
        nv_ref[j] = jnp.where(newest, vn_t[:, j:j + 1], pltpu.roll(cv_ref[j], w - 1, axis=1))

    s_old, s_new, v_sel = [], [], []
    for j in range(bt):
        q = q_ref[j]
        qb = q.astype(BF16)
        s_old.append(per_head([_mm(qb, ck_ref[j, rows, :].astype(BF16)) for rows in groups]))
        k_sel = per_head([kn_ref[j:j + 1, cols] for cols in groups])
        v_sel.append(per_head([vn_ref[j:j + 1, cols] for cols in groups]))
        s_new.append(jnp.sum(q * k_sel, axis=-1, keepdims=True))
    s_old = jnp.stack(s_old, axis=0) * scale
    s_new = jnp.stack(s_new, axis=0) * scale
    m = jnp.maximum(jnp.maximum(jnp.max(s_old, axis=-1, keepdims=True), s_new), sk)
    p_old = jnp.exp(s_old - m)
    p_new = jnp.exp(s_new - m)
    inv = 1.0 / (jnp.sum(p_old, axis=-1, keepdims=True) + p_new + jnp.exp(sk - m))
    p_old = p_old.astype(BF16)
    for j in range(bt):
        o = per_head([_mm_nt(p_old[j], cv_ref[j, rows, :].astype(BF16)) for rows in groups])
        o_ref[j] = (o + p_new[j] * v_sel[j]) * inv[j]


def _swa_sample(sinks, q3, k_new, v_new, cache_k, cache_v):
    bt = SEQ_TILE
    nb, _, w = cache_k.shape
    assert w == LANES
    row = lambda width: pl.BlockSpec((bt, width), lambda i: (i, 0))
    q_spec = pl.BlockSpec((bt, SWA_HEADS, SWA_HEAD_DIM), lambda i: (i, 0, 0))
    c_spec = pl.BlockSpec((bt, SWA_KV, w), lambda i: (i, 0, 0))
    return pl.pallas_call(
        _swa_sample_body,
        grid=(nb // bt,),
        in_specs=[pl.BlockSpec((SWA_HEADS, 1), lambda i: (0, 0)), q_spec, row(SWA_KV), row(SWA_KV), c_spec, c_spec],
        out_specs=[q_spec, c_spec, c_spec],
        out_shape=[jax.ShapeDtypeStruct((nb, SWA_HEADS, SWA_HEAD_DIM), F32),
                   jax.ShapeDtypeStruct(cache_k.shape, F32), jax.ShapeDtypeStruct(cache_v.shape, F32)],
        compiler_params=pltpu.CompilerParams(dimension_semantics=("arbitrary",)),
        name="swa_sample",
    )(sinks, q3, k_new, v_new, cache_k, cache_v)


def _rope_parts(tile_rows, tile_starts):
    d = jnp.arange(LANES) % SWA_HEAD_DIM
    inv = jnp.power(ROPE_THETA, -(d % ROPE_HALF).astype(F32) * 2.0 / ROPE_DIM)
    local = jnp.arange(tile_rows, dtype=F32)[:, None] * inv[None, :]
    base = jnp.repeat(tile_starts.astype(F32), 8)[:, None] * inv[None, :]
    return jnp.cos(local), jnp.sin(local), jnp.cos(base), jnp.sin(base)


def kernel(x_prompt, x_sample, state_gla, cache_swa_k, cache_swa_v, gla_w_in, gla_w_gate2, gla_b_gate, gla_g_head, gla_w_out, swa_w_qkv, swa_b_qkv, swa_sinks, swa_w_out, swa_b_out, norm_mix_pre, norm_mix_post, norm_ffn_pre, norm_ffn_post, ffn_w_up, ffn_w_down):
    batch, seq, _ = x_prompt.shape
    dec_batch, dec_seq, _ = x_sample.shape
    assert dec_seq == 1 and seq % SWA_WINDOW == 0
    past_len = seq
    n_p, n_s = batch * seq, dec_batch * dec_seq
    xp = x_prompt.reshape(n_p, D_MODEL)
    xs = x_sample.reshape(n_s, D_MODEL)

    w_g2 = gla_w_gate2[0].astype(BF16)
    b_g = gla_b_gate[0][None, :]
    g_head = gla_g_head[0][None, :]
    b_qkv = swa_b_qkv[0][None, :]
    b_sout = swa_b_out[0][None, :]
    row = lambda t, i: t[i][None, :]
    in0_widths = [GLA_DK, GLA_DK, GLA_DV, GLA_DV, GLA_DK]
    mid_widths = [D_MODEL, SWA_Q, SWA_KV, SWA_KV]

    tm, ts = TOKEN_TILE, n_s
    ((q, k, v, r, la), (qs, ks, vs, rs, las)), (w_gout, w_up0, w_dn0, w_qkv) = _tok_call(
        _in0_body,
        [(n_p, tm, [(xp, None)], in0_widths, [F32, F32, BF16, F32, F32]),
         (n_s, ts, [(xs, None)], in0_widths, [F32] * 5)],
        [row(norm_mix_pre, 0), gla_w_in[0], w_g2, b_g], "in0",
        casts=[(gla_w_out, 0), (ffn_w_up, 0), (ffn_w_down, 0), (swa_w_qkv, 0)], convert_first=(1,))
    o, s_fin_p = _gla_prompt(q, k, v, la, batch, seq)
    o_s, s_new = _gla_sample(qs, ks, vs, las, state_gla[0])
    lc, ls, bc, bs = _rope_parts(tm, jnp.arange(seq // tm) * tm)
    lc_s, ls_s, bc_s, bs_s = _rope_parts(1, jnp.full((1,), past_len))
    same = lambda t: (0, 0)
    start_map = lambda t: (t % (seq // tm), 0)
    rope_p = [(lc, same), (ls, same), (bc, start_map, 8), (bs, start_map, 8)]
    rope_s = [(jnp.broadcast_to(lc_s, (ts, LANES)), None), (jnp.broadcast_to(ls_s, (ts, LANES)), None),
              (bc_s, same, 8), (bs_s, same, 8)]
    mid_consts = [g_head, w_gout, row(norm_mix_post, 0), row(norm_ffn_pre, 0), w_up0, w_dn0, row(norm_ffn_post, 0),
                  row(norm_mix_pre, 1), w_qkv, b_qkv]
    ((h2, q1, k1, v1), (h2s, q1s, k1s, v1s)), (w_sout, w_up1, w_dn1) = _tok_call(
        _mid_body,
        [(n_p, tm, [(o, None), (r, None), (xp, None)] + rope_p, mid_widths, [F32] * 4),
         (n_s, ts, [(o_s, None), (rs, None), (xs, None)] + rope_s, mid_widths, [F32] * 4)],
        mid_consts, "mid",
        casts=[(swa_w_out, 0), (ffn_w_up, 1), (ffn_w_down, 1)])
    out_consts = [w_sout, b_sout, row(norm_mix_post, 1), row(norm_ffn_pre, 1), w_up1, w_dn1, row(norm_ffn_post, 1)]
    win = cache_swa_k.shape[2]
    to_t = lambda c: jnp.transpose(c[0].reshape(dec_batch, win, SWA_KV), (0, 2, 1))
    from_t = lambda c: jnp.transpose(c, (0, 2, 1)).reshape(1, dec_batch, win, SWA_KV_HEADS, SWA_HEAD_DIM)
    attn_s, nk, nv = _swa_sample(swa_sinks[0][:, None], q1s.reshape(n_s, SWA_HEADS, SWA_HEAD_DIM), k1s, v1s,
                                 to_t(cache_swa_k), to_t(cache_swa_v))
    y_p, y_s = _swa_out(swa_sinks[0], q1, k1, v1, h2, attn_s.reshape(n_s, SWA_Q), h2s, out_consts, batch, seq)
    wp = min(SWA_WINDOW, seq)
    tail = lambda t: t.reshape(batch, seq, SWA_KV)[:, seq - wp:].reshape(batch, wp, SWA_KV_HEADS, SWA_HEAD_DIM)
    k_tail, v_tail = tail(k1), tail(v1)

    return (y_p.reshape(batch, seq, D_MODEL), y_s.reshape(dec_batch, dec_seq, D_MODEL),
            s_fin_p[None], s_new[None], k_tail[None], v_tail[None], from_t(nk), from_t(nv))
```

```python
import functools

import jax
import jax.numpy as jnp
from jax import lax
from jax.experimental import pallas as pl
from jax.experimental.pallas import tpu as pltpu

F32 = jnp.float32
BF16 = jnp.bfloat16

D_MODEL = 1024
D_FF = 4 * D_MODEL
NORM_EPS = 1e-6

GLA_HEADS = 4
GLA_DK = D_MODEL // 2
GLA_DV = D_MODEL
GLA_DK_HEAD = GLA_DK // GLA_HEADS
GLA_DV_HEAD = GLA_DV // GLA_HEADS
GLA_GATE_RANK = 16
GLA_TAU = 16.0
GLA_CHUNK = 64
GLA_MAIN = 2 * GLA_DK + 2 * GLA_DV

SWA_HEAD_DIM = 64
SWA_HEADS = D_MODEL // SWA_HEAD_DIM
SWA_KV_HEADS = 4
SWA_GROUP = SWA_HEADS // SWA_KV_HEADS
SWA_WINDOW = 128
SWA_Q = SWA_HEADS * SWA_HEAD_DIM
SWA_KV = SWA_KV_HEADS * SWA_HEAD_DIM
SWA_QKV = SWA_Q + 2 * SWA_KV
ROPE_THETA = 500000.0
ROPE_DIM = SWA_HEAD_DIM // 4
ROPE_HALF = ROPE_DIM // 2

LANES = 128
FFN_CHUNK = 512
TOKEN_TILE = 512
GLA_TILE = 256
GLA_STEP = 512
SWA_TILE = 512
SEQ_TILE = 8
STATE_TILE = 16
VMEM_LIMIT = 58 * 1024 * 1024
NEG_BIG = -1e30
LOG2E = 1.4426950408889634


def _mm(a, b):
    return jnp.dot(a, b, preferred_element_type=F32)


def _mm_nt(a, b):
    return lax.dot_general(a, b, (((1,), (1,)), ((), ())), preferred_element_type=F32)


def _mm_tn(a, b):
    return lax.dot_general(a, b, (((0,), (0,)), ((), ())), preferred_element_type=F32)


def _rms(x, g):
    ms = jnp.mean(x * x, axis=-1, keepdims=True)
    return x * lax.rsqrt(ms + NORM_EPS) * g


def _split3(x):
    hi = x.astype(BF16)
    r1 = x - hi.astype(F32)
    mid = r1.astype(BF16)
    lo = (r1 - mid.astype(F32)).astype(BF16)
    return hi, mid, lo


def _ffn(a_bf16, wup_ref, wdn_ref, between=None):
    n_slices = D_FF // FFN_CHUNK
    cols = [slice(c * FFN_CHUNK, (c + 1) * FFN_CHUNK) for c in range(n_slices)]
    acc = None
    u_next = _mm(a_bf16, wup_ref[:, cols[0]])
    for c in range(n_slices):
        u = u_next
        if c + 1 < n_slices:
            u_next = _mm(a_bf16, wup_ref[:, cols[c + 1]])
        u = jnp.square(jnp.maximum(u, 0.0)).astype(BF16)
        p = _mm(u, wdn_ref[cols[c], :])
        acc = p if acc is None else acc + p
        if between is not None:
            between()
    return acc


def _in0_body(x_ref, g_ref, wt_ref, wg_ref, bg_ref, q_ref, k_ref, v_ref, r_ref, la_ref):
    a = _rms(x_ref[...], g_ref[...]).astype(BF16)
    proj = lambda lo, hi: _mm_nt(a, wt_ref[lo:hi, :])
    z = proj(GLA_MAIN, GLA_MAIN + GLA_GATE_RANK).astype(BF16)
    q_ref[...] = proj(0, GLA_DK) * (GLA_DK_HEAD ** -0.5)
    k_ref[...] = proj(GLA_DK, 2 * GLA_DK)
    zg = _mm(z, wg_ref[...]) + bg_ref[...]
    la_ref[...] = (jnp.minimum(zg, 0.0) - jnp.log1p(jnp.exp(-jnp.abs(zg)))) * (1.0 / GLA_TAU)
    for c in range(GLA_DV // 512):
        cols = slice(c * 512, (c + 1) * 512)
        v_ref[:, cols] = proj(2 * GLA_DK + c * 512, 2 * GLA_DK + (c + 1) * 512).astype(v_ref.dtype)
        r_ref[:, cols] = proj(2 * GLA_DK + GLA_DV + c * 512, 2 * GLA_DK + GLA_DV + (c + 1) * 512)


def _mid_body(o_ref, r_ref, h_ref, lc_ref, ls_ref, bc_ref, bs_ref,
              gh_ref, wo_ref, gpost_ref, gfpre_ref, wup_ref, wdn_ref, gfpost_ref,
              gpre1_ref, wqkv_ref, bqkv_ref,
              h2_ref, q1_ref, k1_ref, v1_ref):
    m = None
    for hh in range(GLA_HEADS):
        cols = slice(hh * GLA_DV_HEAD, (hh + 1) * GLA_DV_HEAD)
        on = _rms(o_ref[:, cols], gh_ref[...])
        r = r_ref[:, cols]
        u = (on * (r * (1.0 / (1.0 + jnp.exp(-r))))).astype(BF16)
        p = _mm(u, wo_ref[cols, :])
        m = p if m is None else m + p
    h1 = h_ref[...] + _rms(m, gpost_ref[...])
    f = _ffn(_rms(h1, gfpre_ref[...]).astype(BF16), wup_ref, wdn_ref)
    h2 = h1 + _rms(f, gfpost_ref[...])
    h2_ref[...] = h2
    a3 = _rms(h2, gpre1_ref[...]).astype(BF16)
    lc, ls, bc, bs = lc_ref[...], ls_ref[...], bc_ref[0:1, :], bs_ref[0:1, :]
    cos_t = bc * lc - bs * ls
    sin_t = bs * lc + bc * ls
    d = lax.broadcasted_iota(jnp.int32, (1, LANES), 1) % SWA_HEAD_DIM
    rc = jnp.where(d < ROPE_DIM, cos_t, 1.0)
    ra = jnp.where(d < ROPE_HALF, -sin_t, 0.0)
    rb = jnp.where(d < ROPE_HALF, 0.0, jnp.where(d < ROPE_DIM, sin_t, 0.0))
    wide = 2 * LANES
    for c2 in range((SWA_Q + SWA_KV) // wide):
        x2 = _mm(a3, wqkv_ref[:, c2 * wide:(c2 + 1) * wide]) + bqkv_ref[:, c2 * wide:(c2 + 1) * wide]
        for half in range(2):
            c = 2 * c2 + half
            x = x2[:, half * LANES:(half + 1) * LANES]
            y = x * rc + pltpu.roll(x, LANES - ROPE_HALF, axis=1) * ra + pltpu.roll(x, ROPE_HALF, axis=1) * rb
            if c < SWA_Q // LANES:
                q1_ref[:, c * LANES:(c + 1) * LANES] = y
            else:
                k1_ref[:, c * LANES - SWA_Q:(c + 1) * LANES - SWA_Q] = y
    v1_ref[...] = _mm(a3, wqkv_ref[:, SWA_Q + SWA_KV:SWA_QKV]) + bqkv_ref[:, SWA_Q + SWA_KV:SWA_QKV]


def _const_plan(const_inputs):
    in_specs, args = [], []
    for entry in const_inputs:
        if isinstance(entry, tuple):
            arr, layer = entry
            spec = pl.BlockSpec((None,) + arr.shape[1:], lambda i, layer=layer: (layer, 0, 0), pipeline_mode=pl.Buffered(1))
        else:
            arr = entry
            spec = pl.BlockSpec(arr.shape, lambda i: (0, 0), pipeline_mode=pl.Buffered(1))
        in_specs.append(spec)
        args.append(arr)
    return in_specs, args


def _tok_call(body, groups, const_inputs, name, casts=(), convert_first=()):
    in_specs, args, out_specs, out_shape = [], [], [], []
    ranges, start = [], 0
    for n_rows, tm, row_inputs, out_widths, out_dtypes in groups:
        assert n_rows % tm == 0
        count = n_rows // tm
        local = lambda i, start=start, count=count: jnp.clip(i - start, 0, count - 1)
        mode = dict(pipeline_mode=pl.Buffered(1)) if count == 1 else {}
        for arr, imap, *block_rows in row_inputs:
            imap = imap if imap is not None else (lambda t: (t, 0))
            rows = block_rows[0] if block_rows else tm
            in_specs.append(pl.BlockSpec((rows, arr.shape[1]), lambda i, imap=imap, local=local: imap(local(i)), **mode))
            args.append(arr)
        for w, dt in zip(out_widths, out_dtypes):
            out_specs.append(pl.BlockSpec((tm, w), lambda i, local=local: (local(i), 0)))
            out_shape.append(jax.ShapeDtypeStruct((n_rows, w), dt))
        ranges.append((start, count, len(row_inputs), len(out_widths)))
        start += count
    n_row_refs = len(in_specs)
    n_const = len(const_inputs)
    n_group_outs = len(out_specs)
    const_specs, const_args = _const_plan(const_inputs)
    cast_steps = ranges[0][1]
    cast_specs, cast_args = [], []
    for arr, layer in casts:
        _, k_dim, n = arr.shape
        assert k_dim % (cast_steps * 16) == 0
        rows = k_dim // cast_steps
        block = lambda i: jnp.minimum(i, cast_steps - 1)
        cast_specs.append(pl.BlockSpec((None, rows, n), lambda i, layer=layer, block=block: (layer, block(i), 0)))
        cast_args.append(arr)
        out_specs.append(pl.BlockSpec((rows, n), lambda i, block=block: (block(i), 0)))
        out_shape.append(jax.ShapeDtypeStruct((k_dim, n), BF16))

    def kern(*refs):
        row_refs, raw_consts = refs[:n_row_refs], refs[n_row_refs:n_row_refs + n_const]
        n_in_refs = n_row_refs + n_const + len(casts)
        cast_in = refs[n_row_refs + n_const:n_in_refs]
        out_refs = refs[n_in_refs:n_in_refs + n_group_outs]
        cast_out = refs[n_in_refs + n_group_outs:n_in_refs + n_group_outs + len(casts)]
        own_bf16 = refs[n_in_refs + n_group_outs + len(casts):]
        i = pl.program_id(0)

        @pl.when(i == 0)
        def _():
            for j, s_ref in zip(convert_first, own_bf16):
                s_ref[...] = raw_consts[j][...].astype(BF16)

        const_refs = list(raw_consts)
        for j, s_ref in zip(convert_first, own_bf16):
            const_refs[j] = s_ref
        r0 = o0 = 0
        for g, (first, count, n_in, n_out) in enumerate(ranges):
            ins, outs = row_refs[r0:r0 + n_in], out_refs[o0:o0 + n_out]
            r0, o0 = r0 + n_in, o0 + n_out

            @pl.when((i >= first) & (i < first + count))
            def _(ins=ins, outs=outs, g=g):
                body(*ins, *const_refs, *outs)
                if g == 0:
                    for x_ref, o_ref in zip(cast_in, cast_out):
                        o_ref[...] = x_ref[...].astype(BF16)

    outs = pl.pallas_call(
        kern,
        grid=(start,),
        in_specs=in_specs + const_specs + cast_specs,
        out_specs=out_specs,
        out_shape=out_shape,
        scratch_shapes=[pltpu.VMEM(const_inputs[j].shape, BF16) for j in convert_first],
        compiler_params=pltpu.CompilerParams(dimension_semantics=("arbitrary",), vmem_limit_bytes=VMEM_LIMIT),
        name=name,
    )(*args, *const_args, *cast_args)
    grouped, o0 = [], 0
    for _, _, _, n_out in ranges:
        grouped.append(list(outs[o0:o0 + n_out]))
        o0 += n_out
    return grouped, list(outs[n_group_outs:])


def _gla_prompt_body(q_ref, k_ref, v_ref, la_ref, o_ref, sfin_ref, st_ref):
    t = pl.program_id(1)

    @pl.when(t == 0)
    def _():
        st_ref[...] = jnp.zeros_like(st_ref)

    for base in range(0, q_ref.shape[0], GLA_TILE):
        _gla_tile(slice(base, base + GLA_TILE), q_ref, k_ref, v_ref, la_ref, o_ref, st_ref)

    @pl.when(t == pl.num_programs(1) - 1)
    def _():
        sfin_ref[0] = st_ref[...]


def _gla_tile(tile, q_ref, k_ref, v_ref, la_ref, o_ref, st_ref):
    tg = GLA_TILE
    c_len = GLA_CHUNK
    n_chunks = tg // c_len
    chunk_rows = [slice(ci * c_len, (ci + 1) * c_len) for ci in range(n_chunks)]
    row = lax.broadcasted_iota(jnp.int32, (tg, tg), 0)
    col = lax.broadcasted_iota(jnp.int32, (tg, tg), 1)
    lower_b = (row // c_len == col // c_len) & (col <= row)
    lower = jnp.where(lower_b, 1.0, 0.0).astype(BF16)
    hi, mid, lo = _split3(la_ref[tile, :])
    cum = _mm(lower, hi) + _mm(lower, mid) + _mm(lower, lo)
    lane_chunk = lax.broadcasted_iota(jnp.int32, (GLA_DK_HEAD, tg), 1) // c_len

    qd, att, kv = [], [], []
    for h in range(GLA_HEADS):
        kc = slice(h * GLA_DK_HEAD, (h + 1) * GLA_DK_HEAD)
        cum_h = cum[:, kc]
        tot_h = jnp.concatenate([jnp.broadcast_to(cum_h[r.stop - 1:r.stop, :], (c_len, GLA_DK_HEAD)) for r in chunk_rows], axis=0)
        k_h = k_ref[tile, kc]
        qd_h = (q_ref[tile, kc] * jnp.exp(cum_h)).astype(BF16)
        ki_h = (k_h * jnp.exp(-cum_h)).astype(BF16)
        ke_t = (k_h * jnp.exp(tot_h - cum_h)).T
        qd.append(qd_h)
        att.append(jnp.where(lower_b, _mm_nt(qd_h, ki_h), 0.0).astype(BF16))
        v_h = v_ref[tile, h * GLA_DV_HEAD:(h + 1) * GLA_DV_HEAD]
        kv.append([_mm(jnp.where(lane_chunk == ci, ke_t, 0.0).astype(BF16), v_h) for ci in range(n_chunks)])

    s_before = []
    for h in range(GLA_HEADS):
        kc = slice(h * GLA_DK_HEAD, (h + 1) * GLA_DK_HEAD)
        st = st_ref[h]
        starts = []
        for ci, r in enumerate(chunk_rows):
            starts.append(st.astype(BF16))
            e_col = jnp.exp(cum[r.stop - 8:r.stop, kc]).T[:, 7:8]
            st = e_col * st + kv[h][ci]
        st_ref[h] = st
        s_before.append(starts)

    for h in range(GLA_HEADS):
        kc = slice(h * GLA_DK_HEAD, (h + 1) * GLA_DK_HEAD)
        vc = slice(h * GLA_DV_HEAD, (h + 1) * GLA_DV_HEAD)
        o_intra = _mm(att[h], v_ref[tile, vc])
        for ci, r in enumerate(chunk_rows):
            o_ref[tile.start + r.start:tile.start + r.stop, vc] = o_intra[r] + _mm(qd[h][r], s_before[h][ci])


def _gla_prompt(q, k, v, la, batch, seq):
    tg = GLA_STEP
    nt = seq // tg
    qk_spec = pl.BlockSpec((tg, GLA_DK), lambda b, t: (b * nt + t, 0))
    v_spec = pl.BlockSpec((tg, GLA_DV), lambda b, t: (b * nt + t, 0))
    st_shape = (GLA_HEADS, GLA_DK_HEAD, GLA_DV_HEAD)
    return pl.pallas_call(
        _gla_prompt_body,
        grid=(batch, nt),
        in_specs=[qk_spec, qk_spec, v_spec, qk_spec],
        out_specs=[v_spec, pl.BlockSpec((1,) + st_shape, lambda b, t: (b, 0, 0, 0))],
        out_shape=[jax.ShapeDtypeStruct((batch * seq, GLA_DV), F32),
                   jax.ShapeDtypeStruct((batch,) + st_shape, F32)],
        scratch_shapes=[pltpu.VMEM(st_shape, F32)],
        compiler_params=pltpu.CompilerParams(dimension_semantics=("arbitrary", "arbitrary"), vmem_limit_bytes=VMEM_LIMIT),
        name="gla_prompt",
    )(q, k, v, la)


def _gla_sample_body(q_ref, k_ref, v_ref, la_ref, s_ref, o_ref, sn_ref):
    bt = q_ref.shape[0]
    for h in range(GLA_HEADS):
        kc = slice(h * GLA_DK_HEAD, (h + 1) * GLA_DK_HEAD)
        vc = slice(h * GLA_DV_HEAD, (h + 1) * GLA_DV_HEAD)
        a_t = jnp.exp(la_ref[:, kc]).T
        k_t = k_ref[:, kc].T
        q_t = q_ref[:, kc].T
        for j in range(bt):
            s_new = a_t[:, j:j + 1] * s_ref[j, h] + k_t[:, j:j + 1] * v_ref[j:j + 1, vc]
            sn_ref[j, h] = s_new
            o_ref[j:j + 1, vc] = jnp.sum(q_t[:, j:j + 1] * s_new, axis=0, keepdims=True)


def _gla_sample(q, k, v, la, state):
    bt = STATE_TILE
    nb = q.shape[0]
    row = lambda w: pl.BlockSpec((bt, w), lambda i: (i, 0))
    st_spec = pl.BlockSpec((bt, GLA_HEADS, GLA_DK_HEAD, GLA_DV_HEAD), lambda i: (i, 0, 0, 0))
    return pl.pallas_call(
        _gla_sample_body,
        grid=(nb // bt,),
        in_specs=[row(GLA_DK), row(GLA_DK), row(GLA_DV), row(GLA_DK), st_spec],
        out_specs=[row(GLA_DV), st_spec],
        out_shape=[jax.ShapeDtypeStruct((nb, GLA_DV), F32), jax.ShapeDtypeStruct(state.shape, F32)],
        compiler_params=pltpu.CompilerParams(dimension_semantics=("arbitrary",), vmem_limit_bytes=VMEM_LIMIT),
        name="gla_sample",
    )(q, k, v, la, state)


def _swa_attend_units(sink_ref, q_ref, k_full, v_full, has_prev, o_ref):
    w = SWA_WINDOW
    hd = SWA_HEAD_DIM
    tq = q_ref.shape[0]
    nkv = k_full.shape[0]
    lane_q = lax.broadcasted_iota(jnp.int32, (w, LANES), 1) < hd
    lane_kv = lax.broadcasted_iota(jnp.int32, (nkv, LANES), 1) < hd
    i = lax.broadcasted_iota(jnp.int32, (w, 2 * w), 0)
    j = lax.broadcasted_iota(jnp.int32, (w, 2 * w), 1)
    band = jnp.where(j < w, jnp.where(j >= i, 1, 0), jnp.where(j - w <= i, 1, 0))
    band_first = jnp.where(j < w, has_prev, 1) * band
    lane_2w = lax.broadcasted_iota(jnp.int32, (2 * w, LANES), 1) < hd
    ones_lo = jnp.where(lane_2w, 1.0, 0.0).astype(BF16)
    ones_hi = jnp.where(lane_2w, 0.0, 1.0).astype(BF16)
    c2 = (hd ** -0.5) * LOG2E

    k_prep, v_prep = [], []
    for p in range(SWA_KV // LANES):
        cols = slice(p * LANES, (p + 1) * LANES)
        k_p, v_p = k_full[:, cols], v_full[:, cols]
        k_prep.append((k_p.astype(BF16), pltpu.roll(k_p, hd, axis=1).astype(BF16)))
        v_r = pltpu.roll(v_p, hd, axis=1)
        v_prep.append(((jnp.where(lane_kv, v_p, 0.0).astype(BF16), jnp.where(lane_kv, 0.0, v_r).astype(BF16)),
                       (jnp.where(lane_kv, v_r, 0.0).astype(BF16), jnp.where(lane_kv, 0.0, v_p).astype(BF16))))

    def softmax_part(s, hh, mask):
        s2 = jnp.where(mask, s, NEG_BIG)
        sk2 = jnp.full((w, 1), sink_ref[hh], F32) * LOG2E
        m2 = jnp.maximum(jnp.max(s2, axis=-1, keepdims=True), sk2)
        return jnp.exp2(s2 - m2).astype(BF16), sk2 - m2

    for b in range(tq // w):
        rows = slice(b * w, (b + 1) * w)
        krows = slice(b * w, (b + 2) * w)
        mask = (band_first if b == 0 else band) > 0
        for p in range(SWA_KV // LANES):
            q_lo, q_hi = [], []
            for x in range(4):
                q_c = q_ref[rows, (4 * p + x) * LANES:(4 * p + x + 1) * LANES] * c2
                q_lo.append(jnp.where(lane_q, q_c, 0.0).astype(BF16))
                q_hi.append(jnp.where(lane_q, 0.0, q_c).astype(BF16))
            s_self = _mm_nt(jnp.concatenate([q_lo[0], q_lo[1], q_hi[2], q_hi[3]], axis=0), k_prep[p][0][krows])
            s_roll = _mm_nt(jnp.concatenate([q_hi[0], q_hi[1], q_lo[2], q_lo[3]], axis=0), k_prep[p][1][krows])
            for x in range(4):
                c = 4 * p + x
                gh = x // 2
                xr = slice(x * w, (x + 1) * w)
                s_lo, s_hi = (s_self[xr], s_roll[xr]) if gh == 0 else (s_roll[xr], s_self[xr])
                p_lo, d_lo = softmax_part(s_lo, 2 * c, mask)
                p_hi, d_hi = softmax_part(s_hi, 2 * c + 1, mask)
                v_lo, v_hi = v_prep[p][gh]
                rhs = jnp.concatenate([jnp.concatenate([v_lo[krows], ones_lo], axis=1),
                                       jnp.concatenate([v_hi[krows], ones_hi], axis=1)], axis=0)
                ext = _mm(jnp.concatenate([p_lo, p_hi], axis=1), rhs)
                den = ext[:, LANES:] + jnp.exp2(jnp.where(lane_q, d_lo, d_hi))
                o_ref[rows, c * LANES:(c + 1) * LANES] = (ext[:, :LANES] / den).astype(o_ref.dtype)
            yield


def _out_stage(at_bf16, h, wo_ref, bo_ref, gpost_ref, gfpre_ref, wup_ref, wdn_ref, gfpost_ref, between=None):
    m = _mm(at_bf16, wo_ref[...]) + bo_ref[...]
    h1 = h + _rms(m, gpost_ref[...])
    f = _ffn(_rms(h1, gfpre_ref[...]).astype(BF16), wup_ref, wdn_ref, between)
    return h1 + _rms(f, gfpost_ref[...])


def _swa_out_body(batch, nt, sink_ref, q_ref, kc_ref, kp_ref, vc_ref, vp_ref, h_ref, ats_ref, hs_ref,
                  wo_ref, bo_ref, gpost_ref, gfpre_ref, wup_ref, wdn_ref, gfpost_ref,
                  y_ref, ys_ref, attn_scr):
    i = pl.program_id(0)
    per_batch = nt + 1
    t = lax.rem(i, per_batch)
    is_prompt = i < batch * per_batch
    consts = (wo_ref, bo_ref, gpost_ref, gfpre_ref, wup_ref, wdn_ref, gfpost_ref)

    @pl.when(i == 0)
    def _():
        attn_scr[...] = jnp.zeros_like(attn_scr)

    @pl.when(is_prompt & (t < nt))
    def _():
        at_prev = attn_scr[...]
        k_full = jnp.concatenate([kp_ref[...], kc_ref[...]], axis=0)
        v_full = jnp.concatenate([vp_ref[...], vc_ref[...]], axis=0)
        units = _swa_attend_units(sink_ref, q_ref, k_full, v_full, jnp.minimum(t, 1), attn_scr)
        y_ref[...] = _out_stage(at_prev, h_ref[...], *consts, between=lambda: next(units, None))
        for _ in units:
            pass

    @pl.when(is_prompt & (t == nt))
    def _():
        y_ref[...] = _out_stage(attn_scr[...], h_ref[...], *consts)

    @pl.when(i == batch * per_batch)
    def _():
        ys_ref[...] = _out_stage(ats_ref[...].astype(BF16), hs_ref[...], *consts)


def _swa_out(sinks, q, k, v, h, attn_s, h_s, consts, batch, seq):
    tq = SWA_TILE
    w = SWA_WINDOW
    nt = seq // tq
    per_batch = nt + 1
    n_s = h_s.shape[0]

    def pos(i):
        ii = jnp.minimum(i, batch * per_batch - 1)
        return ii // per_batch, lax.rem(ii, per_batch)

    def att_tile(i):
        b, t = pos(i)
        return (b * nt + jnp.minimum(t, nt - 1), 0)

    def prev_block(i):
        b, t = pos(i)
        return (b * (seq // w) + jnp.maximum(jnp.minimum(t, nt - 1) * (tq // w) - 1, 0), 0)

    def out_tile(i):
        b, t = pos(i)
        return (b * nt + jnp.maximum(t - 1, 0), 0)

    whole = lambda arr: pl.BlockSpec(arr.shape, lambda i: (0, 0))
    const_specs, const_args = _const_plan(consts)
    return pl.pallas_call(
        functools.partial(_swa_out_body, batch, nt),
        grid=(batch * per_batch + 1,),
        in_specs=[pl.BlockSpec(memory_space=pltpu.SMEM),
                  pl.BlockSpec((tq, SWA_Q), att_tile), pl.BlockSpec((tq, SWA_KV), att_tile),
                  pl.BlockSpec((w, SWA_KV), prev_block), pl.BlockSpec((tq, SWA_KV), att_tile),
                  pl.BlockSpec((w, SWA_KV), prev_block), pl.BlockSpec((tq, D_MODEL), out_tile),
                  whole(attn_s), whole(h_s)] + const_specs,
        out_specs=[pl.BlockSpec((tq, D_MODEL), out_tile), whole(h_s)],
        out_shape=[jax.ShapeDtypeStruct((batch * seq, D_MODEL), F32), jax.ShapeDtypeStruct((n_s, D_MODEL), F32)],
        scratch_shapes=[pltpu.VMEM((tq, SWA_Q), BF16)],
        compiler_params=pltpu.CompilerParams(dimension_semantics=("arbitrary",), vmem_limit_bytes=VMEM_LIMIT),
        name="swa_out",
    )(sinks, q, k, k, v, v, h, attn_s, h_s, *const_args)


def _swa_sample_body(sk_ref, q_ref, kn_ref, vn_ref, ck_ref, cv_ref, o_ref, nk_ref, nv_ref):
    bt = q_ref.shape[0]
    w = ck_ref.shape[2]
    hd = SWA_HEAD_DIM
    hgroup = lax.broadcasted_iota(jnp.int32, (SWA_HEADS, 1), 0) // SWA_GROUP
    newest = lax.broadcasted_iota(jnp.int32, (SWA_KV, w), 1) == w - 1
    kn_t = kn_ref[...].T
    vn_t = vn_ref[...].T
    scale = hd ** -0.5
    sk = sk_ref[...]
    groups = [slice(g * hd, (g + 1) * hd) for g in range(SWA_KV_HEADS)]

    def per_head(pieces):
        out = pieces[0]
        for g in range(1, SWA_KV_HEADS):
            out = jnp.where(hgroup == g, pieces[g], out)
        return out

    for j in range(bt):
        nk_ref[j] = jnp.where(newest, kn_t[:, j:j + 1], pltpu.roll(ck_ref[j], w - 1, axis=1))
        nv_ref[j] = jnp.where(newest, vn_t[:, j:j + 1], pltpu.roll(cv_ref[j], w - 1, axis=1))

    s_old, s_new, v_sel = [], [], []
    for j in range(bt):
        q = q_ref[j]
        qb = q.astype(BF16)
        s_old.append(per_head([_mm(qb, ck_ref[j, rows, :].astype(BF16)) for rows in groups]))
        k_sel = per_head([kn_ref[j:j + 1, cols] for cols in groups])
        v_sel.append(per_head([vn_ref[j:j + 1, cols] for cols in groups]))
        s_new.append(jnp.sum(q * k_sel, axis=-1, keepdims=True))
    s_old = jnp.stack(s_old, axis=0) * scale
    s_new = jnp.stack(s_new, axis=0) * scale
    m = jnp.maximum(jnp.maximum(jnp.max(s_old, axis=-1, keepdims=True), s_new), sk)
    p_old = jnp.exp(s_old - m)
    p_new = jnp.exp(s_new - m)
    inv = 1.0 / (jnp.sum(p_old, axis=-1, keepdims=True) + p_new + jnp.exp(sk - m))
    p_old = p_old.astype(BF16)
    for j in range(bt):
        o = per_head([_mm_nt(p_old[j], cv_ref[j, rows, :].astype(BF16)) for rows in groups])
        o_ref[j] = (o + p_new[j] * v_sel[j]) * inv[j]


def _swa_sample(sinks, q3, k_new, v_new, cache_k, cache_v):
    bt = SEQ_TILE
    nb, _, w = cache_k.shape
    assert w == LANES
    row = lambda width: pl.BlockSpec((bt, width), lambda i: (i, 0))
    q_spec = pl.BlockSpec((bt, SWA_HEADS, SWA_HEAD_DIM), lambda i: (i, 0, 0))
    c_spec = pl.BlockSpec((bt, SWA_KV, w), lambda i: (i, 0, 0))
    return pl.pallas_call(
        _swa_sample_body,
        grid=(nb // bt,),
        in_specs=[pl.BlockSpec((SWA_HEADS, 1), lambda i: (0, 0)), q_spec, row(SWA_KV), row(SWA_KV), c_spec, c_spec],
        out_specs=[q_spec, c_spec, c_spec],
        out_shape=[jax.ShapeDtypeStruct((nb, SWA_HEADS, SWA_HEAD_DIM), F32),
                   jax.ShapeDtypeStruct(cache_k.shape, F32), jax.ShapeDtypeStruct(cache_v.shape, F32)],
        compiler_params=pltpu.CompilerParams(dimension_semantics=("arbitrary",)),
        name="swa_sample",
    )(sinks, q3, k_new, v_new, cache_k, cache_v)


def _rope_parts(tile_rows, tile_starts):
    d = jnp.arange(LANES) % SWA_HEAD_DIM
    inv = jnp.power(ROPE_THETA, -(d % ROPE_HALF).astype(F32) * 2.0 / ROPE_DIM)
    local = jnp.arange(tile_rows, dtype=F32)[:, None] * inv[None, :]
    base = jnp.repeat(tile_starts.astype(F32), 8)[:, None] * inv[None, :]
    return jnp.cos(local), jnp.sin(local), jnp.cos(base), jnp.sin(base)


def kernel(x_prompt, x_sample, state_gla, cache_swa_k, cache_swa_v, gla_w_in, gla_w_gate2, gla_b_gate, gla_g_head, gla_w_out, swa_w_qkv, swa_b_qkv, swa_sinks, swa_w_out, swa_b_out, norm_mix_pre, norm_mix_post, norm_ffn_pre, norm_ffn_post, ffn_w_up, ffn_w_down):
    batch, seq, _ = x_prompt.shape
    dec_batch, dec_seq, _ = x_sample.shape
    assert dec_seq == 1 and seq % SWA_WINDOW == 0
    past_len = seq
    n_p, n_s = batch * seq, dec_batch * dec_seq
    xp = x_prompt.reshape(n_p, D_MODEL)
    xs = x_sample.reshape(n_s, D_MODEL)

    w_g2 = gla_w_gate2[0].astype(BF16)
    b_g = gla_b_gate[0][None, :]
    g_head = gla_g_head[0][None, :]
    b_qkv = swa_b_qkv[0][None, :]
    b_sout = swa_b_out[0][None, :]
    row = lambda t, i: t[i][None, :]
    in0_widths = [GLA_DK, GLA_DK, GLA_DV, GLA_DV, GLA_DK]
    mid_widths = [D_MODEL, SWA_Q, SWA_KV, SWA_KV]

    tm, ts = TOKEN_TILE, n_s
    ((q, k, v, r, la), (qs, ks, vs, rs, las)), (w_gout, w_up0, w_dn0, w_qkv) = _tok_call(
        _in0_body,
        [(n_p, tm, [(xp, None)], in0_widths, [F32, F32, BF16, F32, F32]),
         (n_s, ts, [(xs, None)], in0_widths, [F32] * 5)],
        [row(norm_mix_pre, 0), gla_w_in[0].T, w_g2, b_g], "in0",
        casts=[(gla_w_out, 0), (ffn_w_up, 0), (ffn_w_down, 0), (swa_w_qkv, 0)], convert_first=(1,))
    o, s_fin_p = _gla_prompt(q, k, v, la, batch, seq)
    o_s, s_new = _gla_sample(qs, ks, vs, las, state_gla[0])
    lc, ls, bc, bs = _rope_parts(tm, jnp.arange(seq // tm) * tm)
    lc_s, ls_s, bc_s, bs_s = _rope_parts(1, jnp.full((1,), past_len))
    same = lambda t: (0, 0)
    start_map = lambda t: (t % (seq // tm), 0)
    rope_p = [(lc, same), (ls, same), (bc, start_map, 8), (bs, start_map, 8)]
    rope_s = [(jnp.broadcast_to(lc_s, (ts, LANES)), None), (jnp.broadcast_to(ls_s, (ts, LANES)), None),
              (bc_s, same, 8), (bs_s, same, 8)]
    mid_consts = [g_head, w_gout, row(norm_mix_post, 0), row(norm_ffn_pre, 0), w_up0, w_dn0, row(norm_ffn_post, 0),
                  row(norm_mix_pre, 1), w_qkv, b_qkv]
    ((h2, q1, k1, v1), (h2s, q1s, k1s, v1s)), (w_sout, w_up1, w_dn1) = _tok_call(
        _mid_body,
        [(n_p, tm, [(o, None), (r, None), (xp, None)] + rope_p, mid_widths, [F32] * 4),
         (n_s, ts, [(o_s, None), (rs, None), (xs, None)] + rope_s, mid_widths, [F32] * 4)],
        mid_consts, "mid",
        casts=[(swa_w_out, 0), (ffn_w_up, 1), (ffn_w_down, 1)])
    out_consts = [w_sout, b_sout, row(norm_mix_post, 1), row(norm_ffn_pre, 1), w_up1, w_dn1, row(norm_ffn_post, 1)]
    win = cache_swa_k.shape[2]
    to_t = lambda c: jnp.transpose(c[0].reshape(dec_batch, win, SWA_KV), (0, 2, 1))
    from_t = lambda c: jnp.transpose(c, (0, 2, 1)).reshape(1, dec_batch, win, SWA_KV_HEADS, SWA_HEAD_DIM)
    attn_s, nk, nv = _swa_sample(swa_sinks[0][:, None], q1s.reshape(n_s, SWA_HEADS, SWA_HEAD_DIM), k1s, v1s,
                                 to_t(cache_swa_k), to_t(cache_swa_v))
    y_p, y_s = _swa_out(swa_sinks[0], q1, k1, v1, h2, attn_s.reshape(n_s, SWA_Q), h2s, out_consts, batch, seq)
    wp = min(SWA_WINDOW, seq)
    tail = lambda t: t.reshape(batch, seq, SWA_KV)[:, seq - wp:].reshape(batch, wp, SWA_KV_HEADS, SWA_HEAD_DIM)
    k_tail, v_tail = tail(k1), tail(v1)

    return (y_p.reshape(batch, seq, D_MODEL), y_s.reshape(dec_batch, dec_seq, D_MODEL),
            s_fin_p[None], s_new[None], k_tail[None], v_tail[None], from_t(nk), from_t(nv))
```

```python
import functools

import jax
import jax.numpy as jnp
from jax import lax
from jax.experimental import pallas as pl
from jax.experimental.pallas import tpu as pltpu

F32 = jnp.float32
BF16 = jnp.bfloat16

D_MODEL = 1024
D_FF = 4 * D_MODEL
NORM_EPS = 1e-6

GLA_HEADS = 4
GLA_DK = D_MODEL // 2
GLA_DV = D_MODEL
GLA_DK_HEAD = GLA_DK // GLA_HEADS
GLA_DV_HEAD = GLA_DV // GLA_HEADS
GLA_GATE_RANK = 16
GLA_TAU = 16.0
GLA_CHUNK = 64
GLA_MAIN = 2 * GLA_DK + 2 * GLA_DV

SWA_HEAD_DIM = 64
SWA_HEADS = D_MODEL // SWA_HEAD_DIM
SWA_KV_HEADS = 4
SWA_GROUP = SWA_HEADS // SWA_KV_HEADS
SWA_WINDOW = 128
SWA_Q = SWA_HEADS * SWA_HEAD_DIM
SWA_KV = SWA_KV_HEADS * SWA_HEAD_DIM
SWA_QKV = SWA_Q + 2 * SWA_KV
ROPE_THETA = 500000.0
ROPE_DIM = SWA_HEAD_DIM // 4
ROPE_HALF = ROPE_DIM // 2

LANES = 128
FFN_CHUNK = 512
TOKEN_TILE = 512
GLA_TILE = 256
GLA_STEP = 512
SWA_TILE = 512
SEQ_TILE = 8
STATE_TILE = 16
VMEM_LIMIT = 58 * 1024 * 1024
NEG_BIG = -1e30
LOG2E = 1.4426950408889634


def _mm(a, b):
    return jnp.dot(a, b, preferred_element_type=F32)


def _mm_nt(a, b):
    return lax.dot_general(a, b, (((1,), (1,)), ((), ())), preferred_element_type=F32)


def _mm_tn(a, b):
    return lax.dot_general(a, b, (((0,), (0,)), ((), ())), preferred_element_type=F32)


def _rms(x, g):
    ms = jnp.mean(x * x, axis=-1, keepdims=True)
    return x * lax.rsqrt(ms + NORM_EPS) * g


def _split3(x):
    hi = x.astype(BF16)
    r1 = x - hi.astype(F32)
    mid = r1.astype(BF16)
    lo = (r1 - mid.astype(F32)).astype(BF16)
    return hi, mid, lo


def _ffn(a_bf16, wup_ref, wdn_ref, between=None):
    n_slices = D_FF // FFN_CHUNK
    cols = [slice(c * FFN_CHUNK, (c + 1) * FFN_CHUNK) for c in range(n_slices)]
    acc = None
    u_next = _mm(a_bf16, wup_ref[:, cols[0]])
    for c in range(n_slices):
        u = u_next
        if c + 1 < n_slices:
            u_next = _mm(a_bf16, wup_ref[:, cols[c + 1]])
        u = jnp.square(jnp.maximum(u, 0.0)).astype(BF16)
        p = _mm(u, wdn_ref[cols[c], :])
        acc = p if acc is None else acc + p
        if between is not None:
            between()
    return acc


def _in0_body(x_ref, g_ref, wt_ref, wg_ref, bg_ref, q_ref, k_ref, v_ref, r_ref, la_ref):
    a = _rms(x_ref[...], g_ref[...]).astype(BF16)
    proj = lambda lo, hi: _mm_nt(a, wt_ref[lo:hi, :])
    z = proj(GLA_MAIN, GLA_MAIN + GLA_GATE_RANK).astype(BF16)
    q_ref[...] = proj(0, GLA_DK) * (GLA_DK_HEAD ** -0.5)
    k_ref[...] = proj(GLA_DK, 2 * GLA_DK)
    zg = _mm(z, wg_ref[...]) + bg_ref[...]
    la_ref[...] = (jnp.minimum(zg, 0.0) - jnp.log1p(jnp.exp(-jnp.abs(zg)))) * (1.0 / GLA_TAU)
    for c in range(GLA_DV // 512):
        cols = slice(c * 512, (c + 1) * 512)
        v_ref[:, cols] = proj(2 * GLA_DK + c * 512, 2 * GLA_DK + (c + 1) * 512).astype(v_ref.dtype)
        r_ref[:, cols] = proj(2 * GLA_DK + GLA_DV + c * 512, 2 * GLA_DK + GLA_DV + (c + 1) * 512)


def _mid_body(o_ref, r_ref, h_ref, lc_ref, ls_ref, bc_ref, bs_ref,
              gh_ref, wo_ref, gpost_ref, gfpre_ref, wup_ref, wdn_ref, gfpost_ref,
              gpre1_ref, wqkv_ref, bqkv_ref,
              h2_ref, q1_ref, k1_ref, v1_ref):
    m = None
    for hh in range(GLA_HEADS):
        cols = slice(hh * GLA_DV_HEAD, (hh + 1) * GLA_DV_HEAD)
        on = _rms(o_ref[:, cols], gh_ref[...])
        r = r_ref[:, cols]
        u = (on * (r * (1.0 / (1.0 + jnp.exp(-r))))).astype(BF16)
        p = _mm(u, wo_ref[cols, :])
        m = p if m is None else m + p
    h1 = h_ref[...] + _rms(m, gpost_ref[...])
    f = _ffn(_rms(h1, gfpre_ref[...]).astype(BF16), wup_ref, wdn_ref)
    h2 = h1 + _rms(f, gfpost_ref[...])
    h2_ref[...] = h2
    a3 = _rms(h2, gpre1_ref[...]).astype(BF16)
    lc, ls, bc, bs = lc_ref[...], ls_ref[...], bc_ref[0:1, :], bs_ref[0:1, :]
    cos_t = bc * lc - bs * ls
    sin_t = bs * lc + bc * ls
    d = lax.broadcasted_iota(jnp.int32, (1, LANES), 1) % SWA_HEAD_DIM
    rc = jnp.where(d < ROPE_DIM, cos_t, 1.0)
    ra = jnp.where(d < ROPE_HALF, -sin_t, 0.0)
    rb = jnp.where(d < ROPE_HALF, 0.0, jnp.where(d < ROPE_DIM, sin_t, 0.0))
    wide = 2 * LANES
    for c2 in range((SWA_Q + SWA_KV) // wide):
        x2 = _mm(a3, wqkv_ref[:, c2 * wide:(c2 + 1) * wide]) + bqkv_ref[:, c2 * wide:(c2 + 1) * wide]
        for half in range(2):
            c = 2 * c2 + half
            x = x2[:, half * LANES:(half + 1) * LANES]
            y = x * rc + pltpu.roll(x, LANES - ROPE_HALF, axis=1) * ra + pltpu.roll(x, ROPE_HALF, axis=1) * rb
            if c < SWA_Q // LANES:
                q1_ref[:, c * LANES:(c + 1) * LANES] = y
            else:
                k1_ref[:, c * LANES - SWA_Q:(c + 1) * LANES - SWA_Q] = y
    v1_ref[...] = _mm(a3, wqkv_ref[:, SWA_Q + SWA_KV:SWA_QKV]) + bqkv_ref[:, SWA_Q + SWA_KV:SWA_QKV]


def _const_plan(const_inputs):
    in_specs, args = [], []
    for entry in const_inputs:
        if isinstance(entry, tuple):
            arr, layer = entry
            spec = pl.BlockSpec((None,) + arr.shape[1:], lambda i, layer=layer: (layer, 0, 0), pipeline_mode=pl.Buffered(1))
        else:
            arr = entry
            spec = pl.BlockSpec(arr.shape, lambda i: (0, 0), pipeline_mode=pl.Buffered(1))
        in_specs.append(spec)
        args.append(arr)
    return in_specs, args


def _tok_call(body, groups, const_inputs, name, casts=(), convert_first=()):
    in_specs, args, out_specs, out_shape = [], [], [], []
    ranges, start = [], 0
    for n_rows, tm, row_inputs, out_widths, out_dtypes in groups:
        assert n_rows % tm == 0
        count = n_rows // tm
        local = lambda i, start=start, count=count: jnp.clip(i - start, 0, count - 1)
        mode = dict(pipeline_mode=pl.Buffered(1)) if count == 1 else {}
        for arr, imap, *block_rows in row_inputs:
            imap = imap if imap is not None else (lambda t: (t, 0))
            rows = block_rows[0] if block_rows else tm
            in_specs.append(pl.BlockSpec((rows, arr.shape[1]), lambda i, imap=imap, local=local: imap(local(i)), **mode))
            args.append(arr)
        for w, dt in zip(out_widths, out_dtypes):
            out_specs.append(pl.BlockSpec((tm, w), lambda i, local=local: (local(i), 0)))
            out_shape.append(jax.ShapeDtypeStruct((n_rows, w), dt))
        ranges.append((start, count, len(row_inputs), len(out_widths)))
        start += count
    n_row_refs = len(in_specs)
    n_const = len(const_inputs)
    n_group_outs = len(out_specs)
    const_specs, const_args = _const_plan(const_inputs)
    cast_steps = ranges[0][1]
    cast_specs, cast_args = [], []
    for arr, layer in casts:
        _, k_dim, n = arr.shape
        assert k_dim % (cast_steps * 16) == 0
        rows = k_dim // cast_steps
        block = lambda i: jnp.minimum(i, cast_steps - 1)
        cast_specs.append(pl.BlockSpec((None, rows, n), lambda i, layer=layer, block=block: (layer, block(i), 0)))
        cast_args.append(arr)
        out_specs.append(pl.BlockSpec((rows, n), lambda i, block=block: (block(i), 0)))
        out_shape.append(jax.ShapeDtypeStruct((k_dim, n), BF16))

    def kern(*refs):
        row_refs, raw_consts = refs[:n_row_refs], refs[n_row_refs:n_row_refs + n_const]
        n_in_refs = n_row_refs + n_const + len(casts)
        cast_in = refs[n_row_refs + n_const:n_in_refs]
        out_refs = refs[n_in_refs:n_in_refs + n_group_outs]
        cast_out = refs[n_in_refs + n_group_outs:n_in_refs + n_group_outs + len(casts)]
        own_bf16 = refs[n_in_refs + n_group_outs + len(casts):]
        i = pl.program_id(0)

        @pl.when(i == 0)
        def _():
            for j, s_ref in zip(convert_first, own_bf16):
                s_ref[...] = raw_consts[j][...].astype(BF16)

        const_refs = list(raw_consts)
        for j, s_ref in zip(convert_first, own_bf16):
            const_refs[j] = s_ref
        r0 = o0 = 0
        for g, (first, count, n_in, n_out) in enumerate(ranges):
            ins, outs = row_refs[r0:r0 + n_in], out_refs[o0:o0 + n_out]
            r0, o0 = r0 + n_in, o0 + n_out

            @pl.when((i >= first) & (i < first + count))
            def _(ins=ins, outs=outs, g=g):
                body(*ins, *const_refs, *outs)
                if g == 0:
                    for x_ref, o_ref in zip(cast_in, cast_out):
                        o_ref[...] = x_ref[...].astype(BF16)

    outs = pl.pallas_call(
        kern,
        grid=(start,),
        in_specs=in_specs + const_specs + cast_specs,
        out_specs=out_specs,
        out_shape=out_shape,
        scratch_shapes=[pltpu.VMEM(const_inputs[j].shape, BF16) for j in convert_first],
        compiler_params=pltpu.CompilerParams(dimension_semantics=("arbitrary",), vmem_limit_bytes=VMEM_LIMIT),
        name=name,
    )(*args, *const_args, *cast_args)
    grouped, o0 = [], 0
    for _, _, _, n_out in ranges:
        grouped.append(list(outs[o0:o0 + n_out]))
        o0 += n_out
    return grouped, list(outs[n_group_outs:])


def _gla_prompt_body(q_ref, k_ref, v_ref, la_ref, o_ref, sfin_ref, st_ref):
    t = pl.program_id(1)

    @pl.when(t == 0)
    def _():
        st_ref[...] = jnp.zeros_like(st_ref)

    for base in range(0, q_ref.shape[0], GLA_TILE):
        _gla_tile(slice(base, base + GLA_TILE), q_ref, k_ref, v_ref, la_ref, o_ref, st_ref)

    @pl.when(t == pl.num_programs(1) - 1)
    def _():
        sfin_ref[0] = st_ref[...]


def _gla_tile(tile, q_ref, k_ref, v_ref, la_ref, o_ref, st_ref):
    tg = GLA_TILE
    c_len = GLA_CHUNK
    n_chunks = tg // c_len
    chunk_rows = [slice(ci * c_len, (ci + 1) * c_len) for ci in range(n_chunks)]
    row = lax.broadcasted_iota(jnp.int32, (tg, tg), 0)
    col = lax.broadcasted_iota(jnp.int32, (tg, tg), 1)
    lower_b = (row // c_len == col // c_len) & (col <= row)
    lower = jnp.where(lower_b, 1.0, 0.0).astype(BF16)
    hi, mid, lo = _split3(la_ref[tile, :])
    cum = _mm(lower, hi) + _mm(lower, mid) + _mm(lower, lo)
    lane_chunk = lax.broadcasted_iota(jnp.int32, (GLA_DK_HEAD, tg), 1) // c_len

    qd, att, kv = [], [], []
    for h in range(GLA_HEADS):
        kc = slice(h * GLA_DK_HEAD, (h + 1) * GLA_DK_HEAD)
        cum_h = cum[:, kc]
        tot_h = jnp.concatenate([jnp.broadcast_to(cum_h[r.stop - 1:r.stop, :], (c_len, GLA_DK_HEAD)) for r in chunk_rows], axis=0)
        k_h = k_ref[tile, kc]
        qd_h = (q_ref[tile, kc] * jnp.exp(cum_h)).astype(BF16)
        ki_h = (k_h * jnp.exp(-cum_h)).astype(BF16)
        ke_t = (k_h * jnp.exp(tot_h - cum_h)).T
        qd.append(qd_h)
        att.append(jnp.where(lower_b, _mm_nt(qd_h, ki_h), 0.0).astype(BF16))
        v_h = v_ref[tile, h * GLA_DV_HEAD:(h + 1) * GLA_DV_HEAD]
        kv.append([_mm(jnp.where(lane_chunk == ci, ke_t, 0.0).astype(BF16), v_h) for ci in range(n_chunks)])

    s_before = []
    for h in range(GLA_HEADS):
        kc = slice(h * GLA_DK_HEAD, (h + 1) * GLA_DK_HEAD)
        st = st_ref[h]
        starts = []
        for ci, r in enumerate(chunk_rows):
            starts.append(st.astype(BF16))
            e_col = jnp.exp(cum[r.stop - 8:r.stop, kc]).T[:, 7:8]
            st = e_col * st + kv[h][ci]
        st_ref[h] = st
        s_before.append(starts)

    for h in range(GLA_HEADS):
        kc = slice(h * GLA_DK_HEAD, (h + 1) * GLA_DK_HEAD)
        vc = slice(h * GLA_DV_HEAD, (h + 1) * GLA_DV_HEAD)
        o_intra = _mm(att[h], v_ref[tile, vc])
        for ci, r in enumerate(chunk_rows):
            o_ref[tile.start + r.start:tile.start + r.stop, vc] = o_intra[r] + _mm(qd[h][r], s_before[h][ci])


def _gla_prompt(q, k, v, la, batch, seq):
    tg = GLA_STEP
    nt = seq // tg
    qk_spec = pl.BlockSpec((tg, GLA_DK), lambda b, t: (b * nt + t, 0))
    v_spec = pl.BlockSpec((tg, GLA_DV), lambda b, t: (b * nt + t, 0))
    st_shape = (GLA_HEADS, GLA_DK_HEAD, GLA_DV_HEAD)
    return pl.pallas_call(
        _gla_prompt_body,
        grid=(batch, nt),
        in_specs=[qk_spec, qk_spec, v_spec, qk_spec],
        out_specs=[v_spec, pl.BlockSpec((1,) + st_shape, lambda b, t: (b, 0, 0, 0))],
        out_shape=[jax.ShapeDtypeStruct((batch * seq, GLA_DV), F32),
                   jax.ShapeDtypeStruct((batch,) + st_shape, F32)],
        scratch_shapes=[pltpu.VMEM(st_shape, F32)],
        compiler_params=pltpu.CompilerParams(dimension_semantics=("arbitrary", "arbitrary"), vmem_limit_bytes=VMEM_LIMIT),
        name="gla_prompt",
    )(q, k, v, la)


def _gla_sample_body(q_ref, k_ref, v_ref, la_ref, s_ref, o_ref, sn_ref):
    bt = q_ref.shape[0]
    for h in range(GLA_HEADS):
        kc = slice(h * GLA_DK_HEAD, (h + 1) * GLA_DK_HEAD)
        vc = slice(h * GLA_DV_HEAD, (h + 1) * GLA_DV_HEAD)
        a_t = jnp.exp(la_ref[:, kc]).T
        k_t = k_ref[:, kc].T
        q_t = q_ref[:, kc].T
        for j in range(bt):
            s_new = a_t[:, j:j + 1] * s_ref[j, h] + k_t[:, j:j + 1] * v_ref[j:j + 1, vc]
            sn_ref[j, h] = s_new
            o_ref[j:j + 1, vc] = jnp.sum(q_t[:, j:j + 1] * s_new, axis=0, keepdims=True)


def _gla_sample(q, k, v, la, state):
    bt = STATE_TILE
    nb = q.shape[0]
    row = lambda w: pl.BlockSpec((bt, w), lambda i: (i, 0))
    st_spec = pl.BlockSpec((bt, GLA_HEADS, GLA_DK_HEAD, GLA_DV_HEAD), lambda i: (i, 0, 0, 0))
    return pl.pallas_call(
        _gla_sample_body,
        grid=(nb // bt,),
        in_specs=[row(GLA_DK), row(GLA_DK), row(GLA_DV), row(GLA_DK), st_spec],
        out_specs=[row(GLA_DV), st_spec],
        out_shape=[jax.ShapeDtypeStruct((nb, GLA_DV), F32), jax.ShapeDtypeStruct(state.shape, F32)],
        compiler_params=pltpu.CompilerParams(dimension_semantics=("arbitrary",), vmem_limit_bytes=VMEM_LIMIT),
        name="gla_sample",
    )(q, k, v, la, state)


def _swa_attend_units(sink_ref, q_ref, k_full, v_full, has_prev, o_ref):
    w = SWA_WINDOW
    hd = SWA_HEAD_DIM
    tq = q_ref.shape[0]
    nkv = k_full.shape[0]
    lane_q = lax.broadcasted_iota(jnp.int32, (w, LANES), 1) < hd
    lane_kv = lax.broadcasted_iota(jnp.int32, (nkv, LANES), 1) < hd
    i = lax.broadcasted_iota(jnp.int32, (w, 2 * w), 0)
    j = lax.broadcasted_iota(jnp.int32, (w, 2 * w), 1)
    band = jnp.where(j < w, jnp.where(j >= i, 1, 0), jnp.where(j - w <= i, 1, 0))
    band_first = jnp.where(j < w, has_prev, 1) * band
    lane_2w = lax.broadcasted_iota(jnp.int32, (2 * w, LANES), 1) < hd
    ones_lo = jnp.where(lane_2w, 1.0, 0.0).astype(BF16)
    ones_hi = jnp.where(lane_2w, 0.0, 1.0).astype(BF16)
    c2 = (hd ** -0.5) * LOG2E

    k_prep, v_prep = [], []
    for p in range(SWA_KV // LANES):
        cols = slice(p * LANES, (p + 1) * LANES)
        k_p, v_p = k_full[:, cols], v_full[:, cols]
        k_prep.append((k_p.astype(BF16), pltpu.roll(k_p, hd, axis=1).astype(BF16)))
        v_r = pltpu.roll(v_p, hd, axis=1)
        v_prep.append(((jnp.where(lane_kv, v_p, 0.0).astype(BF16), jnp.where(lane_kv, 0.0, v_r).astype(BF16)),
                       (jnp.where(lane_kv, v_r, 0.0).astype(BF16), jnp.where(lane_kv, 0.0, v_p).astype(BF16))))

    def softmax_part(s, hh, mask):
        s2 = jnp.where(mask, s, NEG_BIG)
        sk2 = jnp.full((w, 1), sink_ref[hh], F32) * LOG2E
        m2 = jnp.maximum(jnp.max(s2, axis=-1, keepdims=True), sk2)
        return jnp.exp2(s2 - m2).astype(BF16), sk2 - m2

    for b in range(tq // w):
        rows = slice(b * w, (b + 1) * w)
        krows = slice(b * w, (b + 2) * w)
        mask = (band_first if b == 0 else band) > 0
        for p in range(SWA_KV // LANES):
            q_lo, q_hi = [], []
            for x in range(4):
                q_c = q_ref[rows, (4 * p + x) * LANES:(4 * p + x + 1) * LANES] * c2
                q_lo.append(jnp.where(lane_q, q_c, 0.0).astype(BF16))
                q_hi.append(jnp.where(lane_q, 0.0, q_c).astype(BF16))
            s_self = _mm_nt(jnp.concatenate([q_lo[0], q_lo[1], q_hi[2], q_hi[3]], axis=0), k_prep[p][0][krows])
            s_roll = _mm_nt(jnp.concatenate([q_hi[0], q_hi[1], q_lo[2], q_lo[3]], axis=0), k_prep[p][1][krows])
            for x in range(4):
                c = 4 * p + x
                gh = x // 2
                xr = slice(x * w, (x + 1) * w)
                s_lo, s_hi = (s_self[xr], s_roll[xr]) if gh == 0 else (s_roll[xr], s_self[xr])
                p_lo, d_lo = softmax_part(s_lo, 2 * c, mask)
                p_hi, d_hi = softmax_part(s_hi, 2 * c + 1, mask)
                v_lo, v_hi = v_prep[p][gh]
                rhs = jnp.concatenate([jnp.concatenate([v_lo[krows], ones_lo], axis=1),
                                       jnp.concatenate([v_hi[krows], ones_hi], axis=1)], axis=0)
                ext = _mm(jnp.concatenate([p_lo, p_hi], axis=1), rhs)
                den = ext[:, LANES:] + jnp.exp2(jnp.where(lane_q, d_lo, d_hi))
                o_ref[rows, c * LANES:(c + 1) * LANES] = (ext[:, :LANES] / den).astype(o_ref.dtype)
            yield


def _out_stage(at_bf16, h, wo_ref, bo_ref, gpost_ref, gfpre_ref, wup_ref, wdn_ref, gfpost_ref, between=None):
    m = _mm(at_bf16, wo_ref[...]) + bo_ref[...]
    h1 = h + _rms(m, gpost_ref[...])
    f = _ffn(_rms(h1, gfpre_ref[...]).astype(BF16), wup_ref, wdn_ref, between)
    return h1 + _rms(f, gfpost_ref[...])


def _swa_out_body(n_tiles, nt, sink_ref, q_ref, kc_ref, kp_ref, vc_ref, vp_ref, h_ref, ats_ref, hs_ref,
                  wo_ref, bo_ref, gpost_ref, gfpre_ref, wup_ref, wdn_ref, gfpost_ref,
                  y_ref, ys_ref, attn_scr):
    i = pl.program_id(0)
    consts = (wo_ref, bo_ref, gpost_ref, gfpre_ref, wup_ref, wdn_ref, gfpost_ref)

    def attention():
        has_prev = jnp.minimum(lax.rem(i, nt), 1)
        k_full = jnp.concatenate([kp_ref[...], kc_ref[...]], axis=0)
        v_full = jnp.concatenate([vp_ref[...], vc_ref[...]], axis=0)
        return _swa_attend_units(sink_ref, q_ref, k_full, v_full, has_prev, attn_scr)

    @pl.when(i == 0)
    def _():
        for _ in attention():
            pass

    @pl.when((i > 0) & (i < n_tiles))
    def _():
        at_prev = attn_scr[...]
        units = attention()
        y_ref[...] = _out_stage(at_prev, h_ref[...], *consts, between=lambda: next(units, None))
        for _ in units:
            pass

    @pl.when(i == n_tiles)
    def _():
        y_ref[...] = _out_stage(attn_scr[...], h_ref[...], *consts)

    @pl.when(i == n_tiles + 1)
    def _():
        ys_ref[...] = _out_stage(ats_ref[...].astype(BF16), hs_ref[...], *consts)


def _swa_out(sinks, q, k, v, h, attn_s, h_s, consts, batch, seq):
    tq = SWA_TILE
    w = SWA_WINDOW
    nt = seq // tq
    n_tiles = batch * nt
    n_s = h_s.shape[0]

    def att_tile(i):
        return (jnp.minimum(i, n_tiles - 1), 0)

    def prev_block(i):
        g = jnp.minimum(i, n_tiles - 1)
        return (g * (tq // w) - jnp.minimum(lax.rem(g, nt), 1), 0)

    def out_tile(i):
        return (jnp.clip(i - 1, 0, n_tiles - 1), 0)

    whole = lambda arr: pl.BlockSpec(arr.shape, lambda i: (0, 0))
    const_specs, const_args = _const_plan(consts)
    return pl.pallas_call(
        functools.partial(_swa_out_body, n_tiles, nt),
        grid=(n_tiles + 2,),
        in_specs=[pl.BlockSpec(memory_space=pltpu.SMEM),
                  pl.BlockSpec((tq, SWA_Q), att_tile), pl.BlockSpec((tq, SWA_KV), att_tile),
                  pl.BlockSpec((w, SWA_KV), prev_block), pl.BlockSpec((tq, SWA_KV), att_tile),
                  pl.BlockSpec((w, SWA_KV), prev_block), pl.BlockSpec((tq, D_MODEL), out_tile),
                  whole(attn_s), whole(h_s)] + const_specs,
        out_specs=[pl.BlockSpec((tq, D_MODEL), out_tile), whole(h_s)],
        out_shape=[jax.ShapeDtypeStruct((batch * seq, D_MODEL), F32), jax.ShapeDtypeStruct((n_s, D_MODEL), F32)],
        scratch_shapes=[pltpu.VMEM((tq, SWA_Q), BF16)],
        compiler_params=pltpu.CompilerParams(dimension_semantics=("arbitrary",), vmem_limit_bytes=VMEM_LIMIT),
        name="swa_out",
    )(sinks, q, k, k, v, v, h, attn_s, h_s, *const_args)


def _swa_sample_body(sk_ref, q_ref, kn_ref, vn_ref, ck_ref, cv_ref, o_ref, nk_ref, nv_ref):
    bt = q_ref.shape[0]
    w = ck_ref.shape[2]
    hd = SWA_HEAD_DIM
    hgroup = lax.broadcasted_iota(jnp.int32, (SWA_HEADS, 1), 0) // SWA_GROUP
    newest = lax.broadcasted_iota(jnp.int32, (SWA_KV, w), 1) == w - 1
    kn_t = kn_ref[...].T
    vn_t = vn_ref[...].T
    scale = hd ** -0.5
    sk = sk_ref[...]
    groups = [slice(g * hd, (g + 1) * hd) for g in range(SWA_KV_HEADS)]

    def per_head(pieces):
        out = pieces[0]
        for g in range(1, SWA_KV_HEADS):
            out = jnp.where(hgroup == g, pieces[g], out)
        return out

    for j in range(bt):
        nk_ref[j] = jnp.where(newest, kn_t[:, j:j + 1], pltpu.roll(ck_ref[j], w - 1, axis=1))
        nv_ref[j] = jnp.where(newest, vn_t[:, j:j + 1], pltpu.roll(cv_ref[j], w - 1, axis=1))

    s_old, s_new, v_sel = [], [], []
    for j in range(bt):
        q = q_ref[j]
        qb = q.astype(BF16)
        s_old.append(per_head([_mm(qb, ck_ref[j, rows, :].astype(BF16)) for rows in groups]))
        k_sel = per_head([kn_ref[j:j + 1, cols] for cols in groups])
        v_sel.append(per_head([vn_ref[j:j + 1, cols] for cols in groups]))
        s_new.append(jnp.sum(q * k_sel, axis=-1, keepdims=True))
    s_old = jnp.stack(s_old, axis=0) * scale
    s_new = jnp.stack(s_new, axis=0) * scale
    m = jnp.maximum(jnp.maximum(jnp.max(s_old, axis=-1, keepdims=True), s_new), sk)
    p_old = jnp.exp(s_old - m)
    p_new = jnp.exp(s_new - m)
    inv = 1.0 / (jnp.sum(p_old, axis=-1, keepdims=True) + p_new + jnp.exp(sk - m))
    p_old = p_old.astype(BF16)
    for j in range(bt):
        o = per_head([_mm_nt(p_old[j], cv_ref[j, rows, :].astype(BF16)) for rows in groups])
        o_ref[j] = (o + p_new[j] * v_sel[j]) * inv[j]


def _swa_sample(sinks, q3, k_new, v_new, cache_k, cache_v):
    bt = SEQ_TILE
    nb, _, w = cache_k.shape
    assert w == LANES
    row = lambda width: pl.BlockSpec((bt, width), lambda i: (i, 0))
    q_spec = pl.BlockSpec((bt, SWA_HEADS, SWA_HEAD_DIM), lambda i: (i, 0, 0))
    c_spec = pl.BlockSpec((bt, SWA_KV, w), lambda i: (i, 0, 0))
    return pl.pallas_call(
        _swa_sample_body,
        grid=(nb // bt,),
        in_specs=[pl.BlockSpec((SWA_HEADS, 1), lambda i: (0, 0)), q_spec, row(SWA_KV), row(SWA_KV), c_spec, c_spec],
        out_specs=[q_spec, c_spec, c_spec],
        out_shape=[jax.ShapeDtypeStruct((nb, SWA_HEADS, SWA_HEAD_DIM), F32),
                   jax.ShapeDtypeStruct(cache_k.shape, F32), jax.ShapeDtypeStruct(cache_v.shape, F32)],
        compiler_params=pltpu.CompilerParams(dimension_semantics=("arbitrary",)),
        name="swa_sample",
    )(sinks, q3, k_new, v_new, cache_k, cache_v)


def _rope_parts(tile_rows, tile_starts):
    d = jnp.arange(LANES) % SWA_HEAD_DIM
    inv = jnp.power(ROPE_THETA, -(d % ROPE_HALF).astype(F32) * 2.0 / ROPE_DIM)
    local = jnp.arange(tile_rows, dtype=F32)[:, None] * inv[None, :]
    base = jnp.repeat(tile_starts.astype(F32), 8)[:, None] * inv[None, :]
    return jnp.cos(local), jnp.sin(local), jnp.cos(base), jnp.sin(base)


def kernel(x_prompt, x_sample, state_gla, cache_swa_k, cache_swa_v, gla_w_in, gla_w_gate2, gla_b_gate, gla_g_head, gla_w_out, swa_w_qkv, swa_b_qkv, swa_sinks, swa_w_out, swa_b_out, norm_mix_pre, norm_mix_post, norm_ffn_pre, norm_ffn_post, ffn_w_up, ffn_w_down):
    batch, seq, _ = x_prompt.shape
    dec_batch, dec_seq, _ = x_sample.shape
    assert dec_seq == 1 and seq % SWA_WINDOW == 0
    past_len = seq
    n_p, n_s = batch * seq, dec_batch * dec_seq
    xp = x_prompt.reshape(n_p, D_MODEL)
    xs = x_sample.reshape(n_s, D_MODEL)

    w_g2 = gla_w_gate2[0].astype(BF16)
    b_g = gla_b_gate[0][None, :]
    g_head = gla_g_head[0][None, :]
    b_qkv = swa_b_qkv[0][None, :]
    b_sout = swa_b_out[0][None, :]
    row = lambda t, i: t[i][None, :]
    in0_widths = [GLA_DK, GLA_DK, GLA_DV, GLA_DV, GLA_DK]
    mid_widths = [D_MODEL, SWA_Q, SWA_KV, SWA_KV]

    tm, ts = TOKEN_TILE, n_s
    ((q, k, v, r, la), (qs, ks, vs, rs, las)), (w_gout, w_up0, w_dn0, w_qkv) = _tok_call(
        _in0_body,
        [(n_p, tm, [(xp, None)], in0_widths, [F32, F32, BF16, F32, F32]),
         (n_s, ts, [(xs, None)], in0_widths, [F32] * 5)],
        [row(norm_mix_pre, 0), gla_w_in[0].T, w_g2, b_g], "in0",
        casts=[(gla_w_out, 0), (ffn_w_up, 0), (ffn_w_down, 0), (swa_w_qkv, 0)], convert_first=(1,))
    o, s_fin_p = _gla_prompt(q, k, v, la, batch, seq)
    o_s, s_new = _gla_sample(qs, ks, vs, las, state_gla[0])
    lc, ls, bc, bs = _rope_parts(tm, jnp.arange(seq // tm) * tm)
    lc_s, ls_s, bc_s, bs_s = _rope_parts(1, jnp.full((1,), past_len))
    same = lambda t: (0, 0)
    start_map = lambda t: (t % (seq // tm), 0)
    rope_p = [(lc, same), (ls, same), (bc, start_map, 8), (bs, start_map, 8)]
    rope_s = [(jnp.broadcast_to(lc_s, (ts, LANES)), None), (jnp.broadcast_to(ls_s, (ts, LANES)), None),
              (bc_s, same, 8), (bs_s, same, 8)]
    mid_consts = [g_head, w_gout, row(norm_mix_post, 0), row(norm_ffn_pre, 0), w_up0, w_dn0, row(norm_ffn_post, 0),
                  row(norm_mix_pre, 1), w_qkv, b_qkv]
    ((h2, q1, k1, v1), (h2s, q1s, k1s, v1s)), (w_sout, w_up1, w_dn1) = _tok_call(
        _mid_body,
        [(n_p, tm, [(o, None), (r, None), (xp, None)] + rope_p, mid_widths, [F32] * 4),
         (n_s, ts, [(o_s, None), (rs, None), (xs, None)] + rope_s, mid_widths, [F32] * 4)],
        mid_consts, "mid",
        casts=[(swa_w_out, 0), (ffn_w_up, 1), (ffn_w_down, 1)])
    out_consts = [w_sout, b_sout, row(norm_mix_post, 1), row(norm_ffn_pre, 1), w_up1, w_dn1, row(norm_ffn_post, 1)]
    win = cache_swa_k.shape[2]
    to_t = lambda c: jnp.transpose(c[0].reshape(dec_batch, win, SWA_KV), (0, 2, 1))
    from_t = lambda c: jnp.transpose(c, (0, 2, 1)).reshape(1, dec_batch, win, SWA_KV_HEADS, SWA_HEAD_DIM)
    attn_s, nk, nv = _swa_sample(swa_sinks[0][:, None], q1s.reshape(n_s, SWA_HEADS, SWA_HEAD_DIM), k1s, v1s,
                                 to_t(cache_swa_k), to_t(cache_swa_v))
    y_p, y_s = _swa_out(swa_sinks[0], q1, k1, v1, h2, attn_s.reshape(n_s, SWA_Q), h2s, out_consts, batch, seq)
    wp = min(SWA_WINDOW, seq)
    tail = lambda t: t.reshape(batch, seq, SWA_KV)[:, seq - wp:].reshape(batch, wp, SWA_KV_HEADS, SWA_HEAD_DIM)
    k_tail, v_tail = tail(k1), tail(v1)

    return (y_p.reshape(batch, seq, D_MODEL), y_s.reshape(dec_batch, dec_seq, D_MODEL),
            s_fin_p[None], s_new[None], k_tail[None], v_tail[None], from_t(nk), from_t(nv))
```

```python
import functools

import jax
import jax.numpy as jnp
from jax import lax
from jax.experimental import pallas as pl
from jax.experimental.pallas import tpu as pltpu

F32 = jnp.float32
BF16 = jnp.bfloat16

D_MODEL = 1024
D_FF = 4 * D_MODEL
NORM_EPS = 1e-6

GLA_HEADS = 4
GLA_DK = D_MODEL // 2
GLA_DV = D_MODEL
GLA_DK_HEAD = GLA_DK // GLA_HEADS
GLA_DV_HEAD = GLA_DV // GLA_HEADS
GLA_GATE_RANK = 16
GLA_TAU = 16.0
GLA_CHUNK = 64
GLA_MAIN = 2 * GLA_DK + 2 * GLA_DV

SWA_HEAD_DIM = 64
SWA_HEADS = D_MODEL // SWA_HEAD_DIM
SWA_KV_HEADS = 4
SWA_GROUP = SWA_HEADS // SWA_KV_HEADS
SWA_WINDOW = 128
SWA_Q = SWA_HEADS * SWA_HEAD_DIM
SWA_KV = SWA_KV_HEADS * SWA_HEAD_DIM
SWA_QKV = SWA_Q + 2 * SWA_KV
ROPE_THETA = 500000.0
ROPE_DIM = SWA_HEAD_DIM // 4
ROPE_HALF = ROPE_DIM // 2

LANES = 128
FFN_CHUNK = 512
TOKEN_TILE = 512
GLA_TILE = 256
GLA_STEP = 512
SWA_TILE = 512
SEQ_TILE = 8
STATE_TILE = 16
VMEM_LIMIT = 58 * 1024 * 1024
NEG_BIG = -1e30
LOG2E = 1.4426950408889634


def _mm(a, b):
    return jnp.dot(a, b, preferred_element_type=F32)


def _mm_nt(a, b):
    return lax.dot_general(a, b, (((1,), (1,)), ((), ())), preferred_element_type=F32)


def _mm_tn(a, b):
    return lax.dot_general(a, b, (((0,), (0,)), ((), ())), preferred_element_type=F32)


def _rms(x, g):
    ms = jnp.mean(x * x, axis=-1, keepdims=True)
    return x * lax.rsqrt(ms + NORM_EPS) * g


def _split3(x):
    hi = x.astype(BF16)
    r1 = x - hi.astype(F32)
    mid = r1.astype(BF16)
    lo = (r1 - mid.astype(F32)).astype(BF16)
    return hi, mid, lo


def _ffn(a_bf16, wup_ref, wdn_ref, between=None):
    n_slices = D_FF // FFN_CHUNK
    cols = [slice(c * FFN_CHUNK, (c + 1) * FFN_CHUNK) for c in range(n_slices)]
    acc = None
    u_next = _mm(a_bf16, wup_ref[:, cols[0]])
    for c in range(n_slices):
        u = u_next
        if c + 1 < n_slices:
            u_next = _mm(a_bf16, wup_ref[:, cols[c + 1]])
        u = jnp.square(jnp.maximum(u, 0.0)).astype(BF16)
        p = _mm(u, wdn_ref[cols[c], :])
        acc = p if acc is None else acc + p
        if between is not None:
            between()
    return acc


def _in0_body(x_ref, g_ref, wt_ref, wg_ref, bg_ref, q_ref, k_ref, v_ref, r_ref, la_ref):
    a = _rms(x_ref[...], g_ref[...]).astype(BF16)
    proj = lambda lo, hi: _mm_nt(a, wt_ref[lo:hi, :])
    z = proj(GLA_MAIN, GLA_MAIN + GLA_GATE_RANK).astype(BF16)
    q_ref[...] = proj(0, GLA_DK) * (GLA_DK_HEAD ** -0.5)
    k_ref[...] = proj(GLA_DK, 2 * GLA_DK)
    zg = _mm(z, wg_ref[...]) + bg_ref[...]
    la_ref[...] = (jnp.minimum(zg, 0.0) - jnp.log1p(jnp.exp(-jnp.abs(zg)))) * (1.0 / GLA_TAU)
    for c in range(GLA_DV // 512):
        cols = slice(c * 512, (c + 1) * 512)
        v_ref[:, cols] = proj(2 * GLA_DK + c * 512, 2 * GLA_DK + (c + 1) * 512).astype(v_ref.dtype)
        r_ref[:, cols] = proj(2 * GLA_DK + GLA_DV + c * 512, 2 * GLA_DK + GLA_DV + (c + 1) * 512).astype(r_ref.dtype)


def _mid_body(o_ref, r_ref, h_ref, lc_ref, ls_ref, bc_ref, bs_ref,
              gh_ref, wo_ref, gpost_ref, gfpre_ref, wup_ref, wdn_ref, gfpost_ref,
              gpre1_ref, wqkv_ref, bqkv_ref,
              h2_ref, q1_ref, k1_ref, v1_ref):
    m = None
    for hh in range(GLA_HEADS):
        cols = slice(hh * GLA_DV_HEAD, (hh + 1) * GLA_DV_HEAD)
        on = _rms(o_ref[:, cols], gh_ref[...])
        r = r_ref[:, cols].astype(F32)
        u = (on * (r * (1.0 / (1.0 + jnp.exp(-r))))).astype(BF16)
        p = _mm(u, wo_ref[cols, :])
        m = p if m is None else m + p
    h1 = h_ref[...] + _rms(m, gpost_ref[...])
    f = _ffn(_rms(h1, gfpre_ref[...]).astype(BF16), wup_ref, wdn_ref)
    h2 = h1 + _rms(f, gfpost_ref[...])
    h2_ref[...] = h2
    a3 = _rms(h2, gpre1_ref[...]).astype(BF16)
    lc, ls, bc, bs = lc_ref[...], ls_ref[...], bc_ref[0:1, :], bs_ref[0:1, :]
    cos_t = bc * lc - bs * ls
    sin_t = bs * lc + bc * ls
    d = lax.broadcasted_iota(jnp.int32, (1, LANES), 1) % SWA_HEAD_DIM
    rc = jnp.where(d < ROPE_DIM, cos_t, 1.0)
    ra = jnp.where(d < ROPE_HALF, -sin_t, 0.0)
    rb = jnp.where(d < ROPE_HALF, 0.0, jnp.where(d < ROPE_DIM, sin_t, 0.0))
    wide = 2 * LANES
    for c2 in range((SWA_Q + SWA_KV) // wide):
        x2 = _mm(a3, wqkv_ref[:, c2 * wide:(c2 + 1) * wide]) + bqkv_ref[:, c2 * wide:(c2 + 1) * wide]
        for half in range(2):
            c = 2 * c2 + half
            x = x2[:, half * LANES:(half + 1) * LANES]
            y = x * rc + pltpu.roll(x, LANES - ROPE_HALF, axis=1) * ra + pltpu.roll(x, ROPE_HALF, axis=1) * rb
            if c < SWA_Q // LANES:
                q1_ref[:, c * LANES:(c + 1) * LANES] = y
            else:
                k1_ref[:, c * LANES - SWA_Q:(c + 1) * LANES - SWA_Q] = y
    v1_ref[...] = _mm(a3, wqkv_ref[:, SWA_Q + SWA_KV:SWA_QKV]) + bqkv_ref[:, SWA_Q + SWA_KV:SWA_QKV]


def _const_plan(const_inputs):
    in_specs, args = [], []
    for entry in const_inputs:
        if isinstance(entry, tuple):
            arr, layer = entry
            spec = pl.BlockSpec((None,) + arr.shape[1:], lambda i, layer=layer: (layer, 0, 0), pipeline_mode=pl.Buffered(1))
        else:
            arr = entry
            spec = pl.BlockSpec(arr.shape, lambda i: (0, 0), pipeline_mode=pl.Buffered(1))
        in_specs.append(spec)
        args.append(arr)
    return in_specs, args


def _tok_call(body, groups, const_inputs, name, casts=(), convert_first=()):
    in_specs, args, out_specs, out_shape = [], [], [], []
    ranges, start = [], 0
    for n_rows, tm, row_inputs, out_widths, out_dtypes in groups:
        assert n_rows % tm == 0
        count = n_rows // tm
        local = lambda i, start=start, count=count: jnp.clip(i - start, 0, count - 1)
        mode = dict(pipeline_mode=pl.Buffered(1)) if count == 1 else {}
        for arr, imap, *block_rows in row_inputs:
            imap = imap if imap is not None else (lambda t: (t, 0))
            rows = block_rows[0] if block_rows else tm
            in_specs.append(pl.BlockSpec((rows, arr.shape[1]), lambda i, imap=imap, local=local: imap(local(i)), **mode))
            args.append(arr)
        for w, dt in zip(out_widths, out_dtypes):
            out_specs.append(pl.BlockSpec((tm, w), lambda i, local=local: (local(i), 0)))
            out_shape.append(jax.ShapeDtypeStruct((n_rows, w), dt))
        ranges.append((start, count, len(row_inputs), len(out_widths)))
        start += count
    n_row_refs = len(in_specs)
    n_const = len(const_inputs)
    n_group_outs = len(out_specs)
    const_specs, const_args = _const_plan(const_inputs)
    cast_steps = ranges[0][1]
    cast_specs, cast_args = [], []
    for arr, layer in casts:
        _, k_dim, n = arr.shape
        assert k_dim % (cast_steps * 16) == 0
        rows = k_dim // cast_steps
        block = lambda i: jnp.minimum(i, cast_steps - 1)
        cast_specs.append(pl.BlockSpec((None, rows, n), lambda i, layer=layer, block=block: (layer, block(i), 0)))
        cast_args.append(arr)
        out_specs.append(pl.BlockSpec((rows, n), lambda i, block=block: (block(i), 0)))
        out_shape.append(jax.ShapeDtypeStruct((k_dim, n), BF16))

    def kern(*refs):
        row_refs, raw_consts = refs[:n_row_refs], refs[n_row_refs:n_row_refs + n_const]
        n_in_refs = n_row_refs + n_const + len(casts)
        cast_in = refs[n_row_refs + n_const:n_in_refs]
        out_refs = refs[n_in_refs:n_in_refs + n_group_outs]
        cast_out = refs[n_in_refs + n_group_outs:n_in_refs + n_group_outs + len(casts)]
        own_bf16 = refs[n_in_refs + n_group_outs + len(casts):]
        i = pl.program_id(0)

        @pl.when(i == 0)
        def _():
            for j, s_ref in zip(convert_first, own_bf16):
                s_ref[...] = raw_consts[j][...].astype(BF16)

        const_refs = list(raw_consts)
        for j, s_ref in zip(convert_first, own_bf16):
            const_refs[j] = s_ref
        r0 = o0 = 0
        for g, (first, count, n_in, n_out) in enumerate(ranges):
            ins, outs = row_refs[r0:r0 + n_in], out_refs[o0:o0 + n_out]
            r0, o0 = r0 + n_in, o0 + n_out

            @pl.when((i >= first) & (i < first + count))
            def _(ins=ins, outs=outs, g=g):
                body(*ins, *const_refs, *outs)
                if g == 0:
                    for x_ref, o_ref in zip(cast_in, cast_out):
                        o_ref[...] = x_ref[...].astype(BF16)

    outs = pl.pallas_call(
        kern,
        grid=(start,),
        in_specs=in_specs + const_specs + cast_specs,
        out_specs=out_specs,
        out_shape=out_shape,
        scratch_shapes=[pltpu.VMEM(const_inputs[j].shape, BF16) for j in convert_first],
        compiler_params=pltpu.CompilerParams(dimension_semantics=("arbitrary",), vmem_limit_bytes=VMEM_LIMIT),
        name=name,
    )(*args, *const_args, *cast_args)
    grouped, o0 = [], 0
    for _, _, _, n_out in ranges:
        grouped.append(list(outs[o0:o0 + n_out]))
        o0 += n_out
    return grouped, list(outs[n_group_outs:])


def _gla_prompt_body(q_ref, k_ref, v_ref, la_ref, o_ref, sfin_ref, st_ref):
    t = pl.program_id(1)

    @pl.when(t == 0)
    def _():
        st_ref[...] = jnp.zeros_like(st_ref)

    for base in range(0, q_ref.shape[0], GLA_TILE):
        _gla_tile(slice(base, base + GLA_TILE), q_ref, k_ref, v_ref, la_ref, o_ref, st_ref)

    @pl.when(t == pl.num_programs(1) - 1)
    def _():
        sfin_ref[0] = st_ref[...]


def _gla_tile(tile, q_ref, k_ref, v_ref, la_ref, o_ref, st_ref):
    tg = GLA_TILE
    c_len = GLA_CHUNK
    n_chunks = tg // c_len
    chunk_rows = [slice(ci * c_len, (ci + 1) * c_len) for ci in range(n_chunks)]
    row = lax.broadcasted_iota(jnp.int32, (tg, tg), 0)
    col = lax.broadcasted_iota(jnp.int32, (tg, tg), 1)
    lower_b = (row // c_len == col // c_len) & (col <= row)
    lower = jnp.where(lower_b, 1.0, 0.0).astype(BF16)
    hi, mid, lo = _split3(la_ref[tile, :])
    cum = _mm(lower, hi) + _mm(lower, mid) + _mm(lower, lo)
    lane_chunk = lax.broadcasted_iota(jnp.int32, (GLA_DK_HEAD, tg), 1) // c_len

    qd, att, kv = [], [], []
    for h in range(GLA_HEADS):
        kc = slice(h * GLA_DK_HEAD, (h + 1) * GLA_DK_HEAD)
        cum_h = cum[:, kc]
        tot_h = jnp.concatenate([jnp.broadcast_to(cum_h[r.stop - 1:r.stop, :], (c_len, GLA_DK_HEAD)) for r in chunk_rows], axis=0)
        k_h = k_ref[tile, kc]
        qd_h = (q_ref[tile, kc] * jnp.exp(cum_h)).astype(BF16)
        ki_h = (k_h * jnp.exp(-cum_h)).astype(BF16)
        ke_t = (k_h * jnp.exp(tot_h - cum_h)).T
        qd.append(qd_h)
        att.append(jnp.where(lower_b, _mm_nt(qd_h, ki_h), 0.0).astype(BF16))
        v_h = v_ref[tile, h * GLA_DV_HEAD:(h + 1) * GLA_DV_HEAD]
        kv.append([_mm(jnp.where(lane_chunk == ci, ke_t, 0.0).astype(BF16), v_h) for ci in range(n_chunks)])

    s_before = []
    for h in range(GLA_HEADS):
        kc = slice(h * GLA_DK_HEAD, (h + 1) * GLA_DK_HEAD)
        st = st_ref[h]
        starts = []
        for ci, r in enumerate(chunk_rows):
            starts.append(st.astype(BF16))
            e_col = jnp.exp(cum[r.stop - 8:r.stop, kc]).T[:, 7:8]
            st = e_col * st + kv[h][ci]
        st_ref[h] = st
        s_before.append(starts)

    for h in range(GLA_HEADS):
        kc = slice(h * GLA_DK_HEAD, (h + 1) * GLA_DK_HEAD)
        vc = slice(h * GLA_DV_HEAD, (h + 1) * GLA_DV_HEAD)
        o_intra = _mm(att[h], v_ref[tile, vc])
        for ci, r in enumerate(chunk_rows):
            o_ref[tile.start + r.start:tile.start + r.stop, vc] = o_intra[r] + _mm(qd[h][r], s_before[h][ci])


def _gla_prompt(q, k, v, la, batch, seq):
    tg = GLA_STEP
    nt = seq // tg
    qk_spec = pl.BlockSpec((tg, GLA_DK), lambda b, t: (b * nt + t, 0))
    v_spec = pl.BlockSpec((tg, GLA_DV), lambda b, t: (b * nt + t, 0))
    st_shape = (GLA_HEADS, GLA_DK_HEAD, GLA_DV_HEAD)
    return pl.pallas_call(
        _gla_prompt_body,
        grid=(batch, nt),
        in_specs=[qk_spec, qk_spec, v_spec, qk_spec],
        out_specs=[v_spec, pl.BlockSpec((1,) + st_shape, lambda b, t: (b, 0, 0, 0))],
        out_shape=[jax.ShapeDtypeStruct((batch * seq, GLA_DV), F32),
                   jax.ShapeDtypeStruct((batch,) + st_shape, F32)],
        scratch_shapes=[pltpu.VMEM(st_shape, F32)],
        compiler_params=pltpu.CompilerParams(dimension_semantics=("arbitrary", "arbitrary"), vmem_limit_bytes=VMEM_LIMIT),
        name="gla_prompt",
    )(q, k, v, la)


def _gla_sample_body(q_ref, k_ref, v_ref, la_ref, s_ref, o_ref, sn_ref):
    bt = q_ref.shape[0]
    for h in range(GLA_HEADS):
        kc = slice(h * GLA_DK_HEAD, (h + 1) * GLA_DK_HEAD)
        vc = slice(h * GLA_DV_HEAD, (h + 1) * GLA_DV_HEAD)
        a_t = jnp.exp(la_ref[:, kc]).T
        k_t = k_ref[:, kc].T
        q_t = q_ref[:, kc].T
        for j in range(bt):
            s_new = a_t[:, j:j + 1] * s_ref[j, h] + k_t[:, j:j + 1] * v_ref[j:j + 1, vc]
            sn_ref[j, h] = s_new
            o_ref[j:j + 1, vc] = jnp.sum(q_t[:, j:j + 1] * s_new, axis=0, keepdims=True)


def _gla_sample(q, k, v, la, state):
    bt = STATE_TILE
    nb = q.shape[0]
    row = lambda w: pl.BlockSpec((bt, w), lambda i: (i, 0))
    st_spec = pl.BlockSpec((bt, GLA_HEADS, GLA_DK_HEAD, GLA_DV_HEAD), lambda i: (i, 0, 0, 0))
    return pl.pallas_call(
        _gla_sample_body,
        grid=(nb // bt,),
        in_specs=[row(GLA_DK), row(GLA_DK), row(GLA_DV), row(GLA_DK), st_spec],
        out_specs=[row(GLA_DV), st_spec],
        out_shape=[jax.ShapeDtypeStruct((nb, GLA_DV), F32), jax.ShapeDtypeStruct(state.shape, F32)],
        compiler_params=pltpu.CompilerParams(dimension_semantics=("arbitrary",), vmem_limit_bytes=VMEM_LIMIT),
        name="gla_sample",
    )(q, k, v, la, state)


def _swa_attend_units(sink_ref, q_ref, k_full, v_full, has_prev, o_ref):
    w = SWA_WINDOW
    hd = SWA_HEAD_DIM
    tq = q_ref.shape[0]
    nkv = k_full.shape[0]
    lane_q = lax.broadcasted_iota(jnp.int32, (w, LANES), 1) < hd
    lane_kv = lax.broadcasted_iota(jnp.int32, (nkv, LANES), 1) < hd
    i = lax.broadcasted_iota(jnp.int32, (w, 2 * w), 0)
    j = lax.broadcasted_iota(jnp.int32, (w, 2 * w), 1)
    band = jnp.where(j < w, jnp.where(j >= i, 1, 0), jnp.where(j - w <= i, 1, 0))
    band_first = jnp.where(j < w, has_prev, 1) * band
    lane_2w = lax.broadcasted_iota(jnp.int32, (2 * w, LANES), 1) < hd
    ones_lo = jnp.where(lane_2w, 1.0, 0.0).astype(BF16)
    ones_hi = jnp.where(lane_2w, 0.0, 1.0).astype(BF16)
    c2 = (hd ** -0.5) * LOG2E

    k_prep, v_prep = [], []
    for p in range(SWA_KV // LANES):
        cols = slice(p * LANES, (p + 1) * LANES)
        k_p, v_p = k_full[:, cols], v_full[:, cols]
        k_prep.append((k_p.astype(BF16), pltpu.roll(k_p, hd, axis=1).astype(BF16)))
        v_r = pltpu.roll(v_p, hd, axis=1)
        v_prep.append(((jnp.where(lane_kv, v_p, 0.0).astype(BF16), jnp.where(lane_kv, 0.0, v_r).astype(BF16)),
                       (jnp.where(lane_kv, v_r, 0.0).astype(BF16), jnp.where(lane_kv, 0.0, v_p).astype(BF16))))

    def softmax_part(s, hh, mask):
        s2 = jnp.where(mask, s, NEG_BIG)
        sk2 = jnp.full((w, 1), sink_ref[hh], F32) * LOG2E
        m2 = jnp.maximum(jnp.max(s2, axis=-1, keepdims=True), sk2)
        return jnp.exp2(s2 - m2).astype(BF16), sk2 - m2

    for b in range(tq // w):
        rows = slice(b * w, (b + 1) * w)
        krows = slice(b * w, (b + 2) * w)
        mask = (band_first if b == 0 else band) > 0
        for p in range(SWA_KV // LANES):
            q_lo, q_hi = [], []
            for x in range(4):
                q_c = q_ref[rows, (4 * p + x) * LANES:(4 * p + x + 1) * LANES] * c2
                q_lo.append(jnp.where(lane_q, q_c, 0.0).astype(BF16))
                q_hi.append(jnp.where(lane_q, 0.0, q_c).astype(BF16))
            s_self = _mm_nt(jnp.concatenate([q_lo[0], q_lo[1], q_hi[2], q_hi[3]], axis=0), k_prep[p][0][krows])
            s_roll = _mm_nt(jnp.concatenate([q_hi[0], q_hi[1], q_lo[2], q_lo[3]], axis=0), k_prep[p][1][krows])
            for x in range(4):
                c = 4 * p + x
                gh = x // 2
                xr = slice(x * w, (x + 1) * w)
                s_lo, s_hi = (s_self[xr], s_roll[xr]) if gh == 0 else (s_roll[xr], s_self[xr])
                p_lo, d_lo = softmax_part(s_lo, 2 * c, mask)
                p_hi, d_hi = softmax_part(s_hi, 2 * c + 1, mask)
                v_lo, v_hi = v_prep[p][gh]
                rhs = jnp.concatenate([jnp.concatenate([v_lo[krows], ones_lo], axis=1),
                                       jnp.concatenate([v_hi[krows], ones_hi], axis=1)], axis=0)
                ext = _mm(jnp.concatenate([p_lo, p_hi], axis=1), rhs)
                den = ext[:, LANES:] + jnp.exp2(jnp.where(lane_q, d_lo, d_hi))
                o_ref[rows, c * LANES:(c + 1) * LANES] = (ext[:, :LANES] / den).astype(o_ref.dtype)
            yield


def _out_stage(at_bf16, h, wo_ref, bo_ref, gpost_ref, gfpre_ref, wup_ref, wdn_ref, gfpost_ref, between=None):
    m = _mm(at_bf16, wo_ref[...]) + bo_ref[...]
    h1 = h + _rms(m, gpost_ref[...])
    f = _ffn(_rms(h1, gfpre_ref[...]).astype(BF16), wup_ref, wdn_ref, between)
    return h1 + _rms(f, gfpost_ref[...])


def _swa_out_body(n_tiles, nt, sink_ref, q_ref, kc_ref, kp_ref, vc_ref, vp_ref, h_ref, ats_ref, hs_ref,
                  wo_ref, bo_ref, gpost_ref, gfpre_ref, wup_ref, wdn_ref, gfpost_ref,
                  y_ref, ys_ref, attn_scr):
    i = pl.program_id(0)
    consts = (wo_ref, bo_ref, gpost_ref, gfpre_ref, wup_ref, wdn_ref, gfpost_ref)

    def attention():
        has_prev = jnp.minimum(lax.rem(i, nt), 1)
        k_full = jnp.concatenate([kp_ref[...], kc_ref[...]], axis=0)
        v_full = jnp.concatenate([vp_ref[...], vc_ref[...]], axis=0)
        return _swa_attend_units(sink_ref, q_ref, k_full, v_full, has_prev, attn_scr)

    @pl.when(i == 0)
    def _():
        for _ in attention():
            pass

    @pl.when((i > 0) & (i < n_tiles))
    def _():
        at_prev = attn_scr[...]
        units = attention()
        y_ref[...] = _out_stage(at_prev, h_ref[...], *consts, between=lambda: next(units, None))
        for _ in units:
            pass

    @pl.when(i == n_tiles)
    def _():
        y_ref[...] = _out_stage(attn_scr[...], h_ref[...], *consts)

    @pl.when(i == n_tiles + 1)
    def _():
        ys_ref[...] = _out_stage(ats_ref[...].astype(BF16), hs_ref[...], *consts)


def _swa_out(sinks, q, k, v, h, attn_s, h_s, consts, batch, seq):
    tq = SWA_TILE
    w = SWA_WINDOW
    nt = seq // tq
    n_tiles = batch * nt
    n_s = h_s.shape[0]

    def att_tile(i):
        return (jnp.minimum(i, n_tiles - 1), 0)

    def prev_block(i):
        g = jnp.minimum(i, n_tiles - 1)
        return (g * (tq // w) - jnp.minimum(lax.rem(g, nt), 1), 0)

    def out_tile(i):
        return (jnp.clip(i - 1, 0, n_tiles - 1), 0)

    whole = lambda arr: pl.BlockSpec(arr.shape, lambda i: (0, 0))
    const_specs, const_args = _const_plan(consts)
    return pl.pallas_call(
        functools.partial(_swa_out_body, n_tiles, nt),
        grid=(n_tiles + 2,),
        in_specs=[pl.BlockSpec(memory_space=pltpu.SMEM),
                  pl.BlockSpec((tq, SWA_Q), att_tile), pl.BlockSpec((tq, SWA_KV), att_tile),
                  pl.BlockSpec((w, SWA_KV), prev_block), pl.BlockSpec((tq, SWA_KV), att_tile),
                  pl.BlockSpec((w, SWA_KV), prev_block), pl.BlockSpec((tq, D_MODEL), out_tile),
                  whole(attn_s), whole(h_s)] + const_specs,
        out_specs=[pl.BlockSpec((tq, D_MODEL), out_tile), whole(h_s)],
        out_shape=[jax.ShapeDtypeStruct((batch * seq, D_MODEL), F32), jax.ShapeDtypeStruct((n_s, D_MODEL), F32)],
        scratch_shapes=[pltpu.VMEM((tq, SWA_Q), BF16)],
        compiler_params=pltpu.CompilerParams(dimension_semantics=("arbitrary",), vmem_limit_bytes=VMEM_LIMIT),
        name="swa_out",
    )(sinks, q, k, k, v, v, h, attn_s, h_s, *const_args)


def _swa_sample_body(sk_ref, q_ref, kn_ref, vn_ref, ck_ref, cv_ref, o_ref, nk_ref, nv_ref):
    bt = q_ref.shape[0]
    w = ck_ref.shape[2]
    hd = SWA_HEAD_DIM
    hgroup = lax.broadcasted_iota(jnp.int32, (SWA_HEADS, 1), 0) // SWA_GROUP
    newest = lax.broadcasted_iota(jnp.int32, (SWA_KV, w), 1) == w - 1
    kn_t = kn_ref[...].T
    vn_t = vn_ref[...].T
    scale = hd ** -0.5
    sk = sk_ref[...]
    groups = [slice(g * hd, (g + 1) * hd) for g in range(SWA_KV_HEADS)]

    def per_head(pieces):
        out = pieces[0]
        for g in range(1, SWA_KV_HEADS):
            out = jnp.where(hgroup == g, pieces[g], out)
        return out

    for j in range(bt):
        nk_ref[j] = jnp.where(newest, kn_t[:, j:j + 1], pltpu.roll(ck_ref[j], w - 1, axis=1))
        nv_ref[j] = jnp.where(newest, vn_t[:, j:j + 1], pltpu.roll(cv_ref[j], w - 1, axis=1))

    s_old, s_new, v_sel = [], [], []
    for j in range(bt):
        q = q_ref[j]
        qb = q.astype(BF16)
        s_old.append(per_head([_mm(qb, ck_ref[j, rows, :].astype(BF16)) for rows in groups]))
        k_sel = per_head([kn_ref[j:j + 1, cols] for cols in groups])
        v_sel.append(per_head([vn_ref[j:j + 1, cols] for cols in groups]))
        s_new.append(jnp.sum(q * k_sel, axis=-1, keepdims=True))
    s_old = jnp.stack(s_old, axis=0) * scale
    s_new = jnp.stack(s_new, axis=0) * scale
    m = jnp.maximum(jnp.maximum(jnp.max(s_old, axis=-1, keepdims=True), s_new), sk)
    p_old = jnp.exp(s_old - m)
    p_new = jnp.exp(s_new - m)
    inv = 1.0 / (jnp.sum(p_old, axis=-1, keepdims=True) + p_new + jnp.exp(sk - m))
    p_old = p_old.astype(BF16)
    for j in range(bt):
        o = per_head([_mm_nt(p_old[j], cv_ref[j, rows, :].astype(BF16)) for rows in groups])
        o_ref[j] = (o + p_new[j] * v_sel[j]) * inv[j]


def _swa_sample(sinks, q3, k_new, v_new, cache_k, cache_v):
    bt = SEQ_TILE
    nb, _, w = cache_k.shape
    assert w == LANES
    row = lambda width: pl.BlockSpec((bt, width), lambda i: (i, 0))
    q_spec = pl.BlockSpec((bt, SWA_HEADS, SWA_HEAD_DIM), lambda i: (i, 0, 0))
    c_spec = pl.BlockSpec((bt, SWA_KV, w), lambda i: (i, 0, 0))
    return pl.pallas_call(
        _swa_sample_body,
        grid=(nb // bt,),
        in_specs=[pl.BlockSpec((SWA_HEADS, 1), lambda i: (0, 0)), q_spec, row(SWA_KV), row(SWA_KV), c_spec, c_spec],
        out_specs=[q_spec, c_spec, c_spec],
        out_shape=[jax.ShapeDtypeStruct((nb, SWA_HEADS, SWA_HEAD_DIM), F32),
                   jax.ShapeDtypeStruct(cache_k.shape, F32), jax.ShapeDtypeStruct(cache_v.shape, F32)],
        compiler_params=pltpu.CompilerParams(dimension_semantics=("arbitrary",)),
        name="swa_sample",
    )(sinks, q3, k_new, v_new, cache_k, cache_v)


def _rope_parts(tile_rows, tile_starts):
    d = jnp.arange(LANES) % SWA_HEAD_DIM
    inv = jnp.power(ROPE_THETA, -(d % ROPE_HALF).astype(F32) * 2.0 / ROPE_DIM)
    local = jnp.arange(tile_rows, dtype=F32)[:, None] * inv[None, :]
    base = jnp.repeat(tile_starts.astype(F32), 8)[:, None] * inv[None, :]
    return jnp.cos(local), jnp.sin(local), jnp.cos(base), jnp.sin(base)


def kernel(x_prompt, x_sample, state_gla, cache_swa_k, cache_swa_v, gla_w_in, gla_w_gate2, gla_b_gate, gla_g_head, gla_w_out, swa_w_qkv, swa_b_qkv, swa_sinks, swa_w_out, swa_b_out, norm_mix_pre, norm_mix_post, norm_ffn_pre, norm_ffn_post, ffn_w_up, ffn_w_down):
    batch, seq, _ = x_prompt.shape
    dec_batch, dec_seq, _ = x_sample.shape
    assert dec_seq == 1 and seq % SWA_WINDOW == 0
    past_len = seq
    n_p, n_s = batch * seq, dec_batch * dec_seq
    xp = x_prompt.reshape(n_p, D_MODEL)
    xs = x_sample.reshape(n_s, D_MODEL)

    w_g2 = gla_w_gate2[0].astype(BF16)
    b_g = gla_b_gate[0][None, :]
    g_head = gla_g_head[0][None, :]
    b_qkv = swa_b_qkv[0][None, :]
    b_sout = swa_b_out[0][None, :]
    row = lambda t, i: t[i][None, :]
    in0_widths = [GLA_DK, GLA_DK, GLA_DV, GLA_DV, GLA_DK]
    mid_widths = [D_MODEL, SWA_Q, SWA_KV, SWA_KV]

    tm, ts = TOKEN_TILE, n_s
    ((q, k, v, r, la), (qs, ks, vs, rs, las)), (w_gout, w_up0, w_dn0, w_qkv) = _tok_call(
        _in0_body,
        [(n_p, tm, [(xp, None)], in0_widths, [F32, F32, BF16, BF16, F32]),
         (n_s, ts, [(xs, None)], in0_widths, [F32] * 5)],
        [row(norm_mix_pre, 0), gla_w_in[0].T, w_g2, b_g], "in0",
        casts=[(gla_w_out, 0), (ffn_w_up, 0), (ffn_w_down, 0), (swa_w_qkv, 0)], convert_first=(1,))
    o, s_fin_p = _gla_prompt(q, k, v, la, batch, seq)
    o_s, s_new = _gla_sample(qs, ks, vs, las, state_gla[0])
    lc, ls, bc, bs = _rope_parts(tm, jnp.arange(seq // tm) * tm)
    lc_s, ls_s, bc_s, bs_s = _rope_parts(1, jnp.full((1,), past_len))
    same = lambda t: (0, 0)
    start_map = lambda t: (t % (seq // tm), 0)
    rope_p = [(lc, same), (ls, same), (bc, start_map, 8), (bs, start_map, 8)]
    rope_s = [(jnp.broadcast_to(lc_s, (ts, LANES)), None), (jnp.broadcast_to(ls_s, (ts, LANES)), None),
              (bc_s, same, 8), (bs_s, same, 8)]
    mid_consts = [g_head, w_gout, row(norm_mix_post, 0), row(norm_ffn_pre, 0), w_up0, w_dn0, row(norm_ffn_post, 0),
                  row(norm_mix_pre, 1), w_qkv, b_qkv]
    ((h2, q1, k1, v1), (h2s, q1s, k1s, v1s)), (w_sout, w_up1, w_dn1) = _tok_call(
        _mid_body,
        [(n_p, tm, [(o, None), (r, None), (xp, None)] + rope_p, mid_widths, [F32] * 4),
         (n_s, ts, [(o_s, None), (rs, None), (xs, None)] + rope_s, mid_widths, [F32] * 4)],
        mid_consts, "mid",
        casts=[(swa_w_out, 0), (ffn_w_up, 1), (ffn_w_down, 1)])
    out_consts = [w_sout, b_sout, row(norm_mix_post, 1), row(norm_ffn_pre, 1), w_up1, w_dn1, row(norm_ffn_post, 1)]
    win = cache_swa_k.shape[2]
    to_t = lambda c: jnp.transpose(c[0].reshape(dec_batch, win, SWA_KV), (0, 2, 1))
    from_t = lambda c: jnp.transpose(c, (0, 2, 1)).reshape(1, dec_batch, win, SWA_KV_HEADS, SWA_HEAD_DIM)
    attn_s, nk, nv = _swa_sample(swa_sinks[0][:, None], q1s.reshape(n_s, SWA_HEADS, SWA_HEAD_DIM), k1s, v1s,
                                 to_t(cache_swa_k), to_t(cache_swa_v))
    y_p, y_s = _swa_out(swa_sinks[0], q1, k1, v1, h2, attn_s.reshape(n_s, SWA_Q), h2s, out_consts, batch, seq)
    wp = min(SWA_WINDOW, seq)
    tail = lambda t: t.reshape(batch, seq, SWA_KV)[:, seq - wp:].reshape(batch, wp, SWA_KV_HEADS, SWA_HEAD_DIM)
    k_tail, v_tail = tail(k1), tail(v1)

    return (y_p.reshape(batch, seq, D_MODEL), y_s.reshape(dec_batch, dec_seq, D_MODEL),
            s_fin_p[None], s_new[None], k_tail[None], v_tail[None], from_t(nk), from_t(nv))
```

```python
import functools

import jax
import jax.numpy as jnp
from jax import lax
from jax.experimental import pallas as pl
from jax.experimental.pallas import tpu as pltpu

F32 = jnp.float32
BF16 = jnp.bfloat16

D_MODEL = 1024
D_FF = 4 * D_MODEL
NORM_EPS = 1e-6

GLA_HEADS = 4
GLA_DK = D_MODEL // 2
GLA_DV = D_MODEL
GLA_DK_HEAD = GLA_DK // GLA_HEADS
GLA_DV_HEAD = GLA_DV // GLA_HEADS
GLA_GATE_RANK = 16
GLA_TAU = 16.0
GLA_CHUNK = 64
GLA_MAIN = 2 * GLA_DK + 2 * GLA_DV

SWA_HEAD_DIM = 64
SWA_HEADS = D_MODEL // SWA_HEAD_DIM
SWA_KV_HEADS = 4
SWA_GROUP = SWA_HEADS // SWA_KV_HEADS
SWA_WINDOW = 128
SWA_Q = SWA_HEADS * SWA_HEAD_DIM
SWA_KV = SWA_KV_HEADS * SWA_HEAD_DIM
SWA_QKV = SWA_Q + 2 * SWA_KV
ROPE_THETA = 500000.0
ROPE_DIM = SWA_HEAD_DIM // 4
ROPE_HALF = ROPE_DIM // 2

LANES = 128
FFN_CHUNK = 512
TOKEN_TILE = 512
GLA_TILE = 256
GLA_STEP = 1024
SWA_TILE = 512
SEQ_TILE = 8
STATE_TILE = 16
VMEM_LIMIT = 58 * 1024 * 1024
NEG_BIG = -1e30
LOG2E = 1.4426950408889634


def _mm(a, b):
    return jnp.dot(a, b, preferred_element_type=F32)


def _mm_nt(a, b):
    return lax.dot_general(a, b, (((1,), (1,)), ((), ())), preferred_element_type=F32)


def _mm_tn(a, b):
    return lax.dot_general(a, b, (((0,), (0,)), ((), ())), preferred_element_type=F32)


def _rms(x, g):
    ms = jnp.mean(x * x, axis=-1, keepdims=True)
    return x * lax.rsqrt(ms + NORM_EPS) * g


def _split3(x):
    hi = x.astype(BF16)
    r1 = x - hi.astype(F32)
    mid = r1.astype(BF16)
    lo = (r1 - mid.astype(F32)).astype(BF16)
    return hi, mid, lo


def _ffn(a_bf16, wup_ref, wdn_ref, between=None):
    n_slices = D_FF // FFN_CHUNK
    cols = [slice(c * FFN_CHUNK, (c + 1) * FFN_CHUNK) for c in range(n_slices)]
    acc = None
    u_next = _mm(a_bf16, wup_ref[:, cols[0]])
    for c in range(n_slices):
        u = u_next
        if c + 1 < n_slices:
            u_next = _mm(a_bf16, wup_ref[:, cols[c + 1]])
        u = jnp.square(jnp.maximum(u, 0.0)).astype(BF16)
        p = _mm(u, wdn_ref[cols[c], :])
        acc = p if acc is None else acc + p
        if between is not None:
            between()
    return acc


def _in0_body(x_ref, g_ref, wt_ref, wg_ref, bg_ref, q_ref, k_ref, v_ref, r_ref, la_ref):
    a = _rms(x_ref[...], g_ref[...]).astype(BF16)
    proj = lambda lo, hi: _mm_nt(a, wt_ref[lo:hi, :])
    z = proj(GLA_MAIN, GLA_MAIN + GLA_GATE_RANK).astype(BF16)
    q_ref[...] = proj(0, GLA_DK) * (GLA_DK_HEAD ** -0.5)
    k_ref[...] = proj(GLA_DK, 2 * GLA_DK)
    zg = _mm(z, wg_ref[...]) + bg_ref[...]
    la_ref[...] = (jnp.minimum(zg, 0.0) - jnp.log1p(jnp.exp(-jnp.abs(zg)))) * (1.0 / GLA_TAU)
    for c in range(GLA_DV // 512):
        cols = slice(c * 512, (c + 1) * 512)
        v_ref[:, cols] = proj(2 * GLA_DK + c * 512, 2 * GLA_DK + (c + 1) * 512).astype(v_ref.dtype)
        r_ref[:, cols] = proj(2 * GLA_DK + GLA_DV + c * 512, 2 * GLA_DK + GLA_DV + (c + 1) * 512)


def _mid_body(o_ref, r_ref, h_ref, lc_ref, ls_ref, bc_ref, bs_ref,
              gh_ref, wo_ref, gpost_ref, gfpre_ref, wup_ref, wdn_ref, gfpost_ref,
              gpre1_ref, wqkv_ref, bqkv_ref,
              h2_ref, q1_ref, k1_ref, v1_ref):
    m = None
    for hh in range(GLA_HEADS):
        cols = slice(hh * GLA_DV_HEAD, (hh + 1) * GLA_DV_HEAD)
        on = _rms(o_ref[:, cols], gh_ref[...])
        r = r_ref[:, cols]
        u = (on * (r * (1.0 / (1.0 + jnp.exp(-r))))).astype(BF16)
        p = _mm(u, wo_ref[cols, :])
        m = p if m is None else m + p
    h1 = h_ref[...] + _rms(m, gpost_ref[...])
    f = _ffn(_rms(h1, gfpre_ref[...]).astype(BF16), wup_ref, wdn_ref)
    h2 = h1 + _rms(f, gfpost_ref[...])
    h2_ref[...] = h2
    a3 = _rms(h2, gpre1_ref[...]).astype(BF16)
    lc, ls, bc, bs = lc_ref[...], ls_ref[...], bc_ref[0:1, :], bs_ref[0:1, :]
    cos_t = bc * lc - bs * ls
    sin_t = bs * lc + bc * ls
    d = lax.broadcasted_iota(jnp.int32, (1, LANES), 1) % SWA_HEAD_DIM
    rc = jnp.where(d < ROPE_DIM, cos_t, 1.0)
    ra = jnp.where(d < ROPE_HALF, -sin_t, 0.0)
    rb = jnp.where(d < ROPE_HALF, 0.0, jnp.where(d < ROPE_DIM, sin_t, 0.0))
    wide = 2 * LANES
    for c2 in range((SWA_Q + SWA_KV) // wide):
        x2 = _mm(a3, wqkv_ref[:, c2 * wide:(c2 + 1) * wide]) + bqkv_ref[:, c2 * wide:(c2 + 1) * wide]
        for half in range(2):
            c = 2 * c2 + half
            x = x2[:, half * LANES:(half + 1) * LANES]
            y = x * rc + pltpu.roll(x, LANES - ROPE_HALF, axis=1) * ra + pltpu.roll(x, ROPE_HALF, axis=1) * rb
            if c < SWA_Q // LANES:
                q1_ref[:, c * LANES:(c + 1) * LANES] = y
            else:
                k1_ref[:, c * LANES - SWA_Q:(c + 1) * LANES - SWA_Q] = y
    v1_ref[...] = _mm(a3, wqkv_ref[:, SWA_Q + SWA_KV:SWA_QKV]) + bqkv_ref[:, SWA_Q + SWA_KV:SWA_QKV]


def _const_plan(const_inputs):
    in_specs, args = [], []
    for entry in const_inputs:
        if isinstance(entry, tuple):
            arr, layer = entry
            spec = pl.BlockSpec((None,) + arr.shape[1:], lambda i, layer=layer: (layer, 0, 0), pipeline_mode=pl.Buffered(1))
        else:
            arr = entry
            spec = pl.BlockSpec(arr.shape, lambda i: (0, 0), pipeline_mode=pl.Buffered(1))
        in_specs.append(spec)
        args.append(arr)
    return in_specs, args


def _tok_call(body, groups, const_inputs, name, casts=(), convert_first=()):
    in_specs, args, out_specs, out_shape = [], [], [], []
    ranges, start = [], 0
    for n_rows, tm, row_inputs, out_widths, out_dtypes in groups:
        assert n_rows % tm == 0
        count = n_rows // tm
        local = lambda i, start=start, count=count: jnp.clip(i - start, 0, count - 1)
        mode = dict(pipeline_mode=pl.Buffered(1)) if count == 1 else {}
        for arr, imap, *block_rows in row_inputs:
            imap = imap if imap is not None else (lambda t: (t, 0))
            rows = block_rows[0] if block_rows else tm
            in_specs.append(pl.BlockSpec((rows, arr.shape[1]), lambda i, imap=imap, local=local: imap(local(i)), **mode))
            args.append(arr)
        for w, dt in zip(out_widths, out_dtypes):
            out_specs.append(pl.BlockSpec((tm, w), lambda i, local=local: (local(i), 0)))
            out_shape.append(jax.ShapeDtypeStruct((n_rows, w), dt))
        ranges.append((start, count, len(row_inputs), len(out_widths)))
        start += count
    n_row_refs = len(in_specs)
    n_const = len(const_inputs)
    n_group_outs = len(out_specs)
    const_specs, const_args = _const_plan(const_inputs)
    cast_steps = ranges[0][1]
    cast_specs, cast_args = [], []
    for arr, layer in casts:
        _, k_dim, n = arr.shape
        assert k_dim % (cast_steps * 16) == 0
        rows = k_dim // cast_steps
        block = lambda i: jnp.minimum(i, cast_steps - 1)
        cast_specs.append(pl.BlockSpec((None, rows, n), lambda i, layer=layer, block=block: (layer, block(i), 0)))
        cast_args.append(arr)
        out_specs.append(pl.BlockSpec((rows, n), lambda i, block=block: (block(i), 0)))
        out_shape.append(jax.ShapeDtypeStruct((k_dim, n), BF16))

    def kern(*refs):
        row_refs, raw_consts = refs[:n_row_refs], refs[n_row_refs:n_row_refs + n_const]
        n_in_refs = n_row_refs + n_const + len(casts)
        cast_in = refs[n_row_refs + n_const:n_in_refs]
        out_refs = refs[n_in_refs:n_in_refs + n_group_outs]
        cast_out = refs[n_in_refs + n_group_outs:n_in_refs + n_group_outs + len(casts)]
        own_bf16 = refs[n_in_refs + n_group_outs + len(casts):]
        i = pl.program_id(0)

        @pl.when(i == 0)
        def _():
            for j, s_ref in zip(convert_first, own_bf16):
                s_ref[...] = raw_consts[j][...].astype(BF16)

        const_refs = list(raw_consts)
        for j, s_ref in zip(convert_first, own_bf16):
            const_refs[j] = s_ref
        r0 = o0 = 0
        for g, (first, count, n_in, n_out) in enumerate(ranges):
            ins, outs = row_refs[r0:r0 + n_in], out_refs[o0:o0 + n_out]
            r0, o0 = r0 + n_in, o0 + n_out

            @pl.when((i >= first) & (i < first + count))
            def _(ins=ins, outs=outs, g=g):
                body(*ins, *const_refs, *outs)
                if g == 0:
                    for x_ref, o_ref in zip(cast_in, cast_out):
                        o_ref[...] = x_ref[...].astype(BF16)

    outs = pl.pallas_call(
        kern,
        grid=(start,),
        in_specs=in_specs + const_specs + cast_specs,
        out_specs=out_specs,
        out_shape=out_shape,
        scratch_shapes=[pltpu.VMEM(const_inputs[j].shape, BF16) for j in convert_first],
        compiler_params=pltpu.CompilerParams(dimension_semantics=("arbitrary",), vmem_limit_bytes=VMEM_LIMIT),
        name=name,
    )(*args, *const_args, *cast_args)
    grouped, o0 = [], 0
    for _, _, _, n_out in ranges:
        grouped.append(list(outs[o0:o0 + n_out]))
        o0 += n_out
    return grouped, list(outs[n_group_outs:])


def _gla_prompt_body(q_ref, k_ref, v_ref, la_ref, o_ref, sfin_ref, st_ref):
    t = pl.program_id(1)

    @pl.when(t == 0)
    def _():
        st_ref[...] = jnp.zeros_like(st_ref)

    for base in range(0, q_ref.shape[0], GLA_TILE):
        _gla_tile(slice(base, base + GLA_TILE), q_ref, k_ref, v_ref, la_ref, o_ref, st_ref)

    @pl.when(t == pl.num_programs(1) - 1)
    def _():
        sfin_ref[0] = st_ref[...]


def _gla_tile(tile, q_ref, k_ref, v_ref, la_ref, o_ref, st_ref):
    tg = GLA_TILE
    c_len = GLA_CHUNK
    n_chunks = tg // c_len
    chunk_rows = [slice(ci * c_len, (ci + 1) * c_len) for ci in range(n_chunks)]
    row = lax.broadcasted_iota(jnp.int32, (tg, tg), 0)
    col = lax.broadcasted_iota(jnp.int32, (tg, tg), 1)
    lower_b = (row // c_len == col // c_len) & (col <= row)
    lower = jnp.where(lower_b, 1.0, 0.0).astype(BF16)
    hi, mid, lo = _split3(la_ref[tile, :])
    cum = _mm(lower, hi) + _mm(lower, mid) + _mm(lower, lo)
    lane_chunk = lax.broadcasted_iota(jnp.int32, (GLA_DK_HEAD, tg), 1) // c_len

    qd, att, kv = [], [], []
    for h in range(GLA_HEADS):
        kc = slice(h * GLA_DK_HEAD, (h + 1) * GLA_DK_HEAD)
        cum_h = cum[:, kc]
        tot_h = jnp.concatenate([jnp.broadcast_to(cum_h[r.stop - 1:r.stop, :], (c_len, GLA_DK_HEAD)) for r in chunk_rows], axis=0)
        k_h = k_ref[tile, kc]
        qd_h = (q_ref[tile, kc] * jnp.exp(cum_h)).astype(BF16)
        ki_h = (k_h * jnp.exp(-cum_h)).astype(BF16)
        ke_t = (k_h * jnp.exp(tot_h - cum_h)).T
        qd.append(qd_h)
        att.append(jnp.where(lower_b, _mm_nt(qd_h, ki_h), 0.0).astype(BF16))
        v_h = v_ref[tile, h * GLA_DV_HEAD:(h + 1) * GLA_DV_HEAD]
        kv.append([_mm(jnp.where(lane_chunk == ci, ke_t, 0.0).astype(BF16), v_h) for ci in range(n_chunks)])

    s_before = []
    for h in range(GLA_HEADS):
        kc = slice(h * GLA_DK_HEAD, (h + 1) * GLA_DK_HEAD)
        st = st_ref[h]
        starts = []
        for ci, r in enumerate(chunk_rows):
            starts.append(st.astype(BF16))
            e_col = jnp.exp(cum[r.stop - 8:r.stop, kc]).T[:, 7:8]
            st = e_col * st + kv[h][ci]
        st_ref[h] = st
        s_before.append(starts)

    for h in range(GLA_HEADS):
        kc = slice(h * GLA_DK_HEAD, (h + 1) * GLA_DK_HEAD)
        vc = slice(h * GLA_DV_HEAD, (h + 1) * GLA_DV_HEAD)
        o_intra = _mm(att[h], v_ref[tile, vc])
        for ci, r in enumerate(chunk_rows):
            o_ref[tile.start + r.start:tile.start + r.stop, vc] = o_intra[r] + _mm(qd[h][r], s_before[h][ci])


def _gla_prompt(q, k, v, la, batch, seq):
    tg = GLA_STEP
    nt = seq // tg
    qk_spec = pl.BlockSpec((tg, GLA_DK), lambda b, t: (b * nt + t, 0))
    v_spec = pl.BlockSpec((tg, GLA_DV), lambda b, t: (b * nt + t, 0))
    st_shape = (GLA_HEADS, GLA_DK_HEAD, GLA_DV_HEAD)
    return pl.pallas_call(
        _gla_prompt_body,
        grid=(batch, nt),
        in_specs=[qk_spec, qk_spec, v_spec, qk_spec],
        out_specs=[v_spec, pl.BlockSpec((1,) + st_shape, lambda b, t: (b, 0, 0, 0))],
        out_shape=[jax.ShapeDtypeStruct((batch * seq, GLA_DV), F32),
                   jax.ShapeDtypeStruct((batch,) + st_shape, F32)],
        scratch_shapes=[pltpu.VMEM(st_shape, F32)],
        compiler_params=pltpu.CompilerParams(dimension_semantics=("arbitrary", "arbitrary"), vmem_limit_bytes=VMEM_LIMIT),
        name="gla_prompt",
    )(q, k, v, la)


def _gla_sample_body(q_ref, k_ref, v_ref, la_ref, s_ref, o_ref, sn_ref):
    bt = q_ref.shape[0]
    for h in range(GLA_HEADS):
        kc = slice(h * GLA_DK_HEAD, (h + 1) * GLA_DK_HEAD)
        vc = slice(h * GLA_DV_HEAD, (h + 1) * GLA_DV_HEAD)
        a_t = jnp.exp(la_ref[:, kc]).T
        k_t = k_ref[:, kc].T
        q_t = q_ref[:, kc].T
        for j in range(bt):
            s_new = a_t[:, j:j + 1] * s_ref[j, h] + k_t[:, j:j + 1] * v_ref[j:j + 1, vc]
            sn_ref[j, h] = s_new
            o_ref[j:j + 1, vc] = jnp.sum(q_t[:, j:j + 1] * s_new, axis=0, keepdims=True)


def _gla_sample(q, k, v, la, state):
    bt = STATE_TILE
    nb = q.shape[0]
    row = lambda w: pl.BlockSpec((bt, w), lambda i: (i, 0))
    st_spec = pl.BlockSpec((bt, GLA_HEADS, GLA_DK_HEAD, GLA_DV_HEAD), lambda i: (i, 0, 0, 0))
    return pl.pallas_call(
        _gla_sample_body,
        grid=(nb // bt,),
        in_specs=[row(GLA_DK), row(GLA_DK), row(GLA_DV), row(GLA_DK), st_spec],
        out_specs=[row(GLA_DV), st_spec],
        out_shape=[jax.ShapeDtypeStruct((nb, GLA_DV), F32), jax.ShapeDtypeStruct(state.shape, F32)],
        compiler_params=pltpu.CompilerParams(dimension_semantics=("arbitrary",), vmem_limit_bytes=VMEM_LIMIT),
        name="gla_sample",
    )(q, k, v, la, state)


def _swa_attend_units(sink_ref, q_ref, k_full, v_full, has_prev, o_ref):
    w = SWA_WINDOW
    hd = SWA_HEAD_DIM
    tq = q_ref.shape[0]
    nkv = k_full.shape[0]
    lane_q = lax.broadcasted_iota(jnp.int32, (w, LANES), 1) < hd
    lane_kv = lax.broadcasted_iota(jnp.int32, (nkv, LANES), 1) < hd
    i = lax.broadcasted_iota(jnp.int32, (w, 2 * w), 0)
    j = lax.broadcasted_iota(jnp.int32, (w, 2 * w), 1)
    band = jnp.where(j < w, jnp.where(j >= i, 1, 0), jnp.where(j - w <= i, 1, 0))
    band_first = jnp.where(j < w, has_prev, 1) * band
    lane_2w = lax.broadcasted_iota(jnp.int32, (2 * w, LANES), 1) < hd
    ones_lo = jnp.where(lane_2w, 1.0, 0.0).astype(BF16)
    ones_hi = jnp.where(lane_2w, 0.0, 1.0).astype(BF16)
    c2 = (hd ** -0.5) * LOG2E

    k_prep, v_prep = [], []
    for p in range(SWA_KV // LANES):
        cols = slice(p * LANES, (p + 1) * LANES)
        k_p, v_p = k_full[:, cols], v_full[:, cols]
        k_prep.append((k_p.astype(BF16), pltpu.roll(k_p, hd, axis=1).astype(BF16)))
        v_r = pltpu.roll(v_p, hd, axis=1)
        v_prep.append(((jnp.where(lane_kv, v_p, 0.0).astype(BF16), jnp.where(lane_kv, 0.0, v_r).astype(BF16)),
                       (jnp.where(lane_kv, v_r, 0.0).astype(BF16), jnp.where(lane_kv, 0.0, v_p).astype(BF16))))

    def softmax_part(s, hh, mask):
        s2 = jnp.where(mask, s, NEG_BIG)
        sk2 = jnp.full((w, 1), sink_ref[hh], F32) * LOG2E
        m2 = jnp.maximum(jnp.max(s2, axis=-1, keepdims=True), sk2)
        return jnp.exp2(s2 - m2).astype(BF16), sk2 - m2

    for b in range(tq // w):
        rows = slice(b * w, (b + 1) * w)
        krows = slice(b * w, (b + 2) * w)
        mask = (band_first if b == 0 else band) > 0
        for p in range(SWA_KV // LANES):
            q_lo, q_hi = [], []
            for x in range(4):
                q_c = q_ref[rows, (4 * p + x) * LANES:(4 * p + x + 1) * LANES] * c2
                q_lo.append(jnp.where(lane_q, q_c, 0.0).astype(BF16))
                q_hi.append(jnp.where(lane_q, 0.0, q_c).astype(BF16))
            s_self = _mm_nt(jnp.concatenate([q_lo[0], q_lo[1], q_hi[2], q_hi[3]], axis=0), k_prep[p][0][krows])
            s_roll = _mm_nt(jnp.concatenate([q_hi[0], q_hi[1], q_lo[2], q_lo[3]], axis=0), k_prep[p][1][krows])
            for x in range(4):
                c = 4 * p + x
                gh = x // 2
                xr = slice(x * w, (x + 1) * w)
                s_lo, s_hi = (s_self[xr], s_roll[xr]) if gh == 0 else (s_roll[xr], s_self[xr])
                p_lo, d_lo = softmax_part(s_lo, 2 * c, mask)
                p_hi, d_hi = softmax_part(s_hi, 2 * c + 1, mask)
                v_lo, v_hi = v_prep[p][gh]
                rhs = jnp.concatenate([jnp.concatenate([v_lo[krows], ones_lo], axis=1),
                                       jnp.concatenate([v_hi[krows], ones_hi], axis=1)], axis=0)
                ext = _mm(jnp.concatenate([p_lo, p_hi], axis=1), rhs)
                den = ext[:, LANES:] + jnp.exp2(jnp.where(lane_q, d_lo, d_hi))
                o_ref[rows, c * LANES:(c + 1) * LANES] = (ext[:, :LANES] / den).astype(o_ref.dtype)
            yield


def _out_stage(at_bf16, h, wo_ref, bo_ref, gpost_ref, gfpre_ref, wup_ref, wdn_ref, gfpost_ref, between=None):
    m = _mm(at_bf16, wo_ref[...]) + bo_ref[...]
    h1 = h + _rms(m, gpost_ref[...])
    f = _ffn(_rms(h1, gfpre_ref[...]).astype(BF16), wup_ref, wdn_ref, between)
    return h1 + _rms(f, gfpost_ref[...])


def _swa_out_body(n_tiles, nt, sink_ref, q_ref, kc_ref, kp_ref, vc_ref, vp_ref, h_ref, ats_ref, hs_ref,
                  wo_ref, bo_ref, gpost_ref, gfpre_ref, wup_ref, wdn_ref, gfpost_ref,
                  y_ref, ys_ref, attn_scr):
    i = pl.program_id(0)
    consts = (wo_ref, bo_ref, gpost_ref, gfpre_ref, wup_ref, wdn_ref, gfpost_ref)

    def attention():
        has_prev = jnp.minimum(lax.rem(i, nt), 1)
        k_full = jnp.concatenate([kp_ref[...], kc_ref[...]], axis=0)
        v_full = jnp.concatenate([vp_ref[...], vc_ref[...]], axis=0)
        return _swa_attend_units(sink_ref, q_ref, k_full, v_full, has_prev, attn_scr)

    @pl.when(i == 0)
    def _():
        for _ in attention():
            pass

    @pl.when((i > 0) & (i < n_tiles))
    def _():
        at_prev = attn_scr[...]
        units = attention()
        y_ref[...] = _out_stage(at_prev, h_ref[...], *consts, between=lambda: next(units, None))
        for _ in units:
            pass

    @pl.when(i == n_tiles)
    def _():
        y_ref[...] = _out_stage(attn_scr[...], h_ref[...], *consts)

    @pl.when(i == n_tiles + 1)
    def _():
        ys_ref[...] = _out_stage(ats_ref[...].astype(BF16), hs_ref[...], *consts)


def _swa_out(sinks, q, k, v, h, attn_s, h_s, consts, batch, seq):
    tq = SWA_TILE
    w = SWA_WINDOW
    nt = seq // tq
    n_tiles = batch * nt
    n_s = h_s.shape[0]

    def att_tile(i):
        return (jnp.minimum(i, n_tiles - 1), 0)

    def prev_block(i):
        g = jnp.minimum(i, n_tiles - 1)
        return (g * (tq // w) - jnp.minimum(lax.rem(g, nt), 1), 0)

    def out_tile(i):
        return (jnp.clip(i - 1, 0, n_tiles - 1), 0)

    whole = lambda arr: pl.BlockSpec(arr.shape, lambda i: (0, 0))
    const_specs, const_args = _const_plan(consts)
    return pl.pallas_call(
        functools.partial(_swa_out_body, n_tiles, nt),
        grid=(n_tiles + 2,),
        in_specs=[pl.BlockSpec(memory_space=pltpu.SMEM),
                  pl.BlockSpec((tq, SWA_Q), att_tile), pl.BlockSpec((tq, SWA_KV), att_tile),
                  pl.BlockSpec((w, SWA_KV), prev_block), pl.BlockSpec((tq, SWA_KV), att_tile),
                  pl.BlockSpec((w, SWA_KV), prev_block), pl.BlockSpec((tq, D_MODEL), out_tile),
                  whole(attn_s), whole(h_s)] + const_specs,
        out_specs=[pl.BlockSpec((tq, D_MODEL), out_tile), whole(h_s)],
        out_shape=[jax.ShapeDtypeStruct((batch * seq, D_MODEL), F32), jax.ShapeDtypeStruct((n_s, D_MODEL), F32)],
        scratch_shapes=[pltpu.VMEM((tq, SWA_Q), BF16)],
        compiler_params=pltpu.CompilerParams(dimension_semantics=("arbitrary",), vmem_limit_bytes=VMEM_LIMIT),
        name="swa_out",
    )(sinks, q, k, k, v, v, h, attn_s, h_s, *const_args)


def _swa_sample_body(sk_ref, q_ref, kn_ref, vn_ref, ck_ref, cv_ref, o_ref, nk_ref, nv_ref):
    bt = q_ref.shape[0]
    w = ck_ref.shape[2]
    hd = SWA_HEAD_DIM
    hgroup = lax.broadcasted_iota(jnp.int32, (SWA_HEADS, 1), 0) // SWA_GROUP
    newest = lax.broadcasted_iota(jnp.int32, (SWA_KV, w), 1) == w - 1
    kn_t = kn_ref[...].T
    vn_t = vn_ref[...].T
    scale = hd ** -0.5
    sk = sk_ref[...]
    groups = [slice(g * hd, (g + 1) * hd) for g in range(SWA_KV_HEADS)]

    def per_head(pieces):
        out = pieces[0]
        for g in range(1, SWA_KV_HEADS):
            out = jnp.where(hgroup == g, pieces[g], out)
        return out

    for j in range(bt):
        nk_ref[j] = jnp.where(newest, kn_t[:, j:j + 1], pltpu.roll(ck_ref[j], w - 1, axis=1))
        nv_ref[j] = jnp.where(newest, vn_t[:, j:j + 1], pltpu.roll(cv_ref[j], w - 1, axis=1))

    s_old, s_new, v_sel = [], [], []
    for j in range(bt):
        q = q_ref[j]
        qb = q.astype(BF16)
        s_old.append(per_head([_mm(qb, ck_ref[j, rows, :].astype(BF16)) for rows in groups]))
        k_sel = per_head([kn_ref[j:j + 1, cols] for cols in groups])
        v_sel.append(per_head([vn_ref[j:j + 1, cols] for cols in groups]))
        s_new.append(jnp.sum(q * k_sel, axis=-1, keepdims=True))
    s_old = jnp.stack(s_old, axis=0) * scale
    s_new = jnp.stack(s_new, axis=0) * scale
    m = jnp.maximum(jnp.maximum(jnp.max(s_old, axis=-1, keepdims=True), s_new), sk)
    p_old = jnp.exp(s_old - m)
    p_new = jnp.exp(s_new - m)
    inv = 1.0 / (jnp.sum(p_old, axis=-1, keepdims=True) + p_new + jnp.exp(sk - m))
    p_old = p_old.astype(BF16)
    for j in range(bt):
        o = per_head([_mm_nt(p_old[j], cv_ref[j, rows, :].astype(BF16)) for rows in groups])
        o_ref[j] = (o + p_new[j] * v_sel[j]) * inv[j]


def _swa_sample(sinks, q3, k_new, v_new, cache_k, cache_v):
    bt = SEQ_TILE
    nb, _, w = cache_k.shape
    assert w == LANES
    row = lambda width: pl.BlockSpec((bt, width), lambda i: (i, 0))
    q_spec = pl.BlockSpec((bt, SWA_HEADS, SWA_HEAD_DIM), lambda i: (i, 0, 0))
    c_spec = pl.BlockSpec((bt, SWA_KV, w), lambda i: (i, 0, 0))
    return pl.pallas_call(
        _swa_sample_body,
        grid=(nb // bt,),
        in_specs=[pl.BlockSpec((SWA_HEADS, 1), lambda i: (0, 0)), q_spec, row(SWA_KV), row(SWA_KV), c_spec, c_spec],
        out_specs=[q_spec, c_spec, c_spec],
        out_shape=[jax.ShapeDtypeStruct((nb, SWA_HEADS, SWA_HEAD_DIM), F32),
                   jax.ShapeDtypeStruct(cache_k.shape, F32), jax.ShapeDtypeStruct(cache_v.shape, F32)],
        compiler_params=pltpu.CompilerParams(dimension_semantics=("arbitrary",)),
        name="swa_sample",
    )(sinks, q3, k_new, v_new, cache_k, cache_v)


def _rope_parts(tile_rows, tile_starts):
    d = jnp.arange(LANES) % SWA_HEAD_DIM
    inv = jnp.power(ROPE_THETA, -(d % ROPE_HALF).astype(F32) * 2.0 / ROPE_DIM)
    local = jnp.arange(tile_rows, dtype=F32)[:, None] * inv[None, :]
    base = jnp.repeat(tile_starts.astype(F32), 8)[:, None] * inv[None, :]
    return jnp.cos(local), jnp.sin(local), jnp.cos(base), jnp.sin(base)


def kernel(x_prompt, x_sample, state_gla, cache_swa_k, cache_swa_v, gla_w_in, gla_w_gate2, gla_b_gate, gla_g_head, gla_w_out, swa_w_qkv, swa_b_qkv, swa_sinks, swa_w_out, swa_b_out, norm_mix_pre, norm_mix_post, norm_ffn_pre, norm_ffn_post, ffn_w_up, ffn_w_down):
    batch, seq, _ = x_prompt.shape
    dec_batch, dec_seq, _ = x_sample.shape
    assert dec_seq == 1 and seq % SWA_WINDOW == 0
    past_len = seq
    n_p, n_s = batch * seq, dec_batch * dec_seq
    xp = x_prompt.reshape(n_p, D_MODEL)
    xs = x_sample.reshape(n_s, D_MODEL)

    w_g2 = gla_w_gate2[0].astype(BF16)
    b_g = gla_b_gate[0][None, :]
    g_head = gla_g_head[0][None, :]
    b_qkv = swa_b_qkv[0][None, :]
    b_sout = swa_b_out[0][None, :]
    row = lambda t, i: t[i][None, :]
    in0_widths = [GLA_DK, GLA_DK, GLA_DV, GLA_DV, GLA_DK]
    mid_widths = [D_MODEL, SWA_Q, SWA_KV, SWA_KV]

    tm, ts = TOKEN_TILE, n_s
    ((q, k, v, r, la), (qs, ks, vs, rs, las)), (w_gout, w_up0, w_dn0, w_qkv) = _tok_call(
        _in0_body,
        [(n_p, tm, [(xp, None)], in0_widths, [F32, F32, BF16, F32, F32]),
         (n_s, ts, [(xs, None)], in0_widths, [F32] * 5)],
        [row(norm_mix_pre, 0), gla_w_in[0].T, w_g2, b_g], "in0",
        casts=[(gla_w_out, 0), (ffn_w_up, 0), (ffn_w_down, 0), (swa_w_qkv, 0)], convert_first=(1,))
    o, s_fin_p = _gla_prompt(q, k, v, la, batch, seq)
    o_s, s_new = _gla_sample(qs, ks, vs, las, state_gla[0])
    lc, ls, bc, bs = _rope_parts(tm, jnp.arange(seq // tm) * tm)
    lc_s, ls_s, bc_s, bs_s = _rope_parts(1, jnp.full((1,), past_len))
    same = lambda t: (0, 0)
    start_map = lambda t: (t % (seq // tm), 0)
    rope_p = [(lc, same), (ls, same), (bc, start_map, 8), (bs, start_map, 8)]
    rope_s = [(jnp.broadcast_to(lc_s, (ts, LANES)), None), (jnp.broadcast_to(ls_s, (ts, LANES)), None),
              (bc_s, same, 8), (bs_s, same, 8)]
    mid_consts = [g_head, w_gout, row(norm_mix_post, 0), row(norm_ffn_pre, 0), w_up0, w_dn0, row(norm_ffn_post, 0),
                  row(norm_mix_pre, 1), w_qkv, b_qkv]
    ((h2, q1, k1, v1), (h2s, q1s, k1s, v1s)), (w_sout, w_up1, w_dn1) = _tok_call(
        _mid_body,
        [(n_p, tm, [(o, None), (r, None), (xp, None)] + rope_p, mid_widths, [F32] * 4),
         (n_s, ts, [(o_s, None), (rs, None), (xs, None)] + rope_s, mid_widths, [F32] * 4)],
        mid_consts, "mid",
        casts=[(swa_w_out, 0), (ffn_w_up, 1), (ffn_w_down, 1)])
    out_consts = [w_sout, b_sout, row(norm_mix_post, 1), row(norm_ffn_pre, 1), w_up1, w_dn1, row(norm_ffn_post, 1)]
    win = cache_swa_k.shape[2]
    to_t = lambda c: jnp.transpose(c[0].reshape(dec_batch, win, SWA_KV), (0, 2, 1))
    from_t = lambda c: jnp.transpose(c, (0, 2, 1)).reshape(1, dec_batch, win, SWA_KV_HEADS, SWA_HEAD_DIM)
    attn_s, nk, nv = _swa_sample(swa_sinks[0][:, None], q1s.reshape(n_s, SWA_HEADS, SWA_HEAD_DIM), k1s, v1s,
                                 to_t(cache_swa_k), to_t(cache_swa_v))
    y_p, y_s = _swa_out(swa_sinks[0], q1, k1, v1, h2, attn_s.reshape(n_s, SWA_Q), h2s, out_consts, batch, seq)
    wp = min(SWA_WINDOW, seq)
    tail = lambda t: t.reshape(batch, seq, SWA_KV)[:, seq - wp:].reshape(batch, wp, SWA_KV_HEADS, SWA_HEAD_DIM)
    k_tail, v_tail = tail(k1), tail(v1)

    return (y_p.reshape(batch, seq, D_MODEL), y_s.reshape(dec_batch, dec_seq, D_MODEL),
            s_fin_p[None], s_new[None], k_tail[None], v_tail[None], from_t(nk), from_t(nv))
```

```python
import functools

import jax
import jax.numpy as jnp
from jax import lax
from jax.experimental import pallas as pl
from jax.experimental.pallas import tpu as pltpu

F32 = jnp.float32
BF16 = jnp.bfloat16

D_MODEL = 1024
D_FF = 4 * D_MODEL
NORM_EPS = 1e-6

GLA_HEADS = 4
GLA_DK = D_MODEL // 2
GLA_DV = D_MODEL
GLA_DK_HEAD = GLA_DK // GLA_HEADS
GLA_DV_HEAD = GLA_DV // GLA_HEADS
GLA_GATE_RANK = 16
GLA_TAU = 16.0
GLA_CHUNK = 64
GLA_MAIN = 2 * GLA_DK + 2 * GLA_DV

SWA_HEAD_DIM = 64
SWA_HEADS = D_MODEL // SWA_HEAD_DIM
SWA_KV_HEADS = 4
SWA_GROUP = SWA_HEADS // SWA_KV_HEADS
SWA_WINDOW = 128
SWA_Q = SWA_HEADS * SWA_HEAD_DIM
SWA_KV = SWA_KV_HEADS * SWA_HEAD_DIM
SWA_QKV = SWA_Q + 2 * SWA_KV
ROPE_THETA = 500000.0
ROPE_DIM = SWA_HEAD_DIM // 4
ROPE_HALF = ROPE_DIM // 2

LANES = 128
SUBLANES = 8
FFN_CHUNK = 512
TOKEN_TILE = 512
GLA_TILE = 256
GLA_STEP = 1024
SWA_TILE = 512
SEQ_TILE = 8
STATE_TILE = 16
VMEM_LIMIT = 58 * 1024 * 1024
NEG_BIG = -1e30
LOG2E = 1.4426950408889634


def _mm(a, b):
    return jnp.dot(a, b, preferred_element_type=F32)


def _mm_nt(a, b):
    return lax.dot_general(a, b, (((1,), (1,)), ((), ())), preferred_element_type=F32)


def _mm_tn(a, b):
    return lax.dot_general(a, b, (((0,), (0,)), ((), ())), preferred_element_type=F32)


def _rms(x, g):
    ms = jnp.mean(x * x, axis=-1, keepdims=True)
    return x * lax.rsqrt(ms + NORM_EPS) * g


def _split3(x):
    hi = x.astype(BF16)
    r1 = x - hi.astype(F32)
    mid = r1.astype(BF16)
    lo = (r1 - mid.astype(F32)).astype(BF16)
    return hi, mid, lo


def _ffn(a_bf16, wup_ref, wdn_ref, between=None):
    n_slices = D_FF // FFN_CHUNK
    cols = [slice(c * FFN_CHUNK, (c + 1) * FFN_CHUNK) for c in range(n_slices)]
    acc = None
    u_next = _mm(a_bf16, wup_ref[:, cols[0]])
    for c in range(n_slices):
        u = u_next
        if c + 1 < n_slices:
            u_next = _mm(a_bf16, wup_ref[:, cols[c + 1]])
        u = jnp.square(jnp.maximum(u, 0.0)).astype(BF16)
        p = _mm(u, wdn_ref[cols[c], :])
        acc = p if acc is None else acc + p
        if between is not None:
            between()
    return acc


def _in0_body(x_ref, g_ref, wt_ref, wg_ref, bg_ref, q_ref, k_ref, v_ref, r_ref, la_ref):
    a = _rms(x_ref[...], g_ref[...]).astype(BF16)
    proj = lambda lo, hi: _mm_nt(a, wt_ref[lo:hi, :])
    z = proj(GLA_MAIN, GLA_MAIN + GLA_GATE_RANK).astype(BF16)
    q_ref[...] = proj(0, GLA_DK) * (GLA_DK_HEAD ** -0.5)
    k_ref[...] = proj(GLA_DK, 2 * GLA_DK)
    zg = _mm(z, wg_ref[...]) + bg_ref[...]
    la_ref[...] = (jnp.minimum(zg, 0.0) - jnp.log1p(jnp.exp(-jnp.abs(zg)))) * (1.0 / GLA_TAU)
    for c in range(GLA_DV // 512):
        cols = slice(c * 512, (c + 1) * 512)
        v_ref[:, cols] = proj(2 * GLA_DK + c * 512, 2 * GLA_DK + (c + 1) * 512).astype(v_ref.dtype)
        r_ref[:, cols] = proj(2 * GLA_DK + GLA_DV + c * 512, 2 * GLA_DK + GLA_DV + (c + 1) * 512)


def _mid_body(o_ref, r_ref, h_ref, lc_ref, ls_ref, bc_ref, bs_ref,
              gh_ref, wo_ref, gpost_ref, gfpre_ref, wup_ref, wdn_ref, gfpost_ref,
              gpre1_ref, wqkv_ref, bqkv_ref,
              h2_ref, q1_ref, k1_ref, v1_ref):
    m = None
    for hh in range(GLA_HEADS):
        cols = slice(hh * GLA_DV_HEAD, (hh + 1) * GLA_DV_HEAD)
        on = _rms(o_ref[:, cols], gh_ref[...])
        r = r_ref[:, cols]
        u = (on * (r * (1.0 / (1.0 + jnp.exp(-r))))).astype(BF16)
        p = _mm(u, wo_ref[cols, :])
        m = p if m is None else m + p
    h1 = h_ref[...] + _rms(m, gpost_ref[...])
    f = _ffn(_rms(h1, gfpre_ref[...]).astype(BF16), wup_ref, wdn_ref)
    h2 = h1 + _rms(f, gfpost_ref[...])
    h2_ref[...] = h2
    a3 = _rms(h2, gpre1_ref[...]).astype(BF16)
    lc, ls, bc, bs = lc_ref[...], ls_ref[...], bc_ref[0:1, :], bs_ref[0:1, :]
    cos_t = bc * lc - bs * ls
    sin_t = bs * lc + bc * ls
    d = lax.broadcasted_iota(jnp.int32, (1, LANES), 1) % SWA_HEAD_DIM
    rc = jnp.where(d < ROPE_DIM, cos_t, 1.0)
    ra = jnp.where(d < ROPE_HALF, -sin_t, 0.0)
    rb = jnp.where(d < ROPE_HALF, 0.0, jnp.where(d < ROPE_DIM, sin_t, 0.0))
    wide = 2 * LANES
    for c2 in range((SWA_Q + SWA_KV) // wide):
        x2 = _mm(a3, wqkv_ref[:, c2 * wide:(c2 + 1) * wide]) + bqkv_ref[:, c2 * wide:(c2 + 1) * wide]
        for half in range(2):
            c = 2 * c2 + half
            x = x2[:, half * LANES:(half + 1) * LANES]
            y = x * rc + pltpu.roll(x, LANES - ROPE_HALF, axis=1) * ra + pltpu.roll(x, ROPE_HALF, axis=1) * rb
            if c < SWA_Q // LANES:
                q1_ref[:, c * LANES:(c + 1) * LANES] = y
            else:
                k1_ref[:, c * LANES - SWA_Q:(c + 1) * LANES - SWA_Q] = y
    v1_ref[...] = _mm(a3, wqkv_ref[:, SWA_Q + SWA_KV:SWA_QKV]) + bqkv_ref[:, SWA_Q + SWA_KV:SWA_QKV]


def _const_plan(const_inputs):
    in_specs, args = [], []
    for entry in const_inputs:
        if isinstance(entry, tuple):
            arr, layer = entry
            spec = pl.BlockSpec((None,) + arr.shape[1:], lambda i, layer=layer: (layer, 0, 0), pipeline_mode=pl.Buffered(1))
        else:
            arr = entry
            spec = pl.BlockSpec(arr.shape, lambda i: (0, 0), pipeline_mode=pl.Buffered(1))
        in_specs.append(spec)
        args.append(arr)
    return in_specs, args


def _tok_call(body, groups, const_inputs, name, casts=(), convert_first=()):
    in_specs, args, out_specs, out_shape = [], [], [], []
    ranges, start = [], 0
    for n_rows, tm, row_inputs, out_widths, out_dtypes in groups:
        assert n_rows % tm == 0
        count = n_rows // tm
        local = lambda i, start=start, count=count: jnp.clip(i - start, 0, count - 1)
        mode = dict(pipeline_mode=pl.Buffered(1)) if count == 1 else {}
        for arr, imap, *block_rows in row_inputs:
            imap = imap if imap is not None else (lambda t: (t, 0))
            rows = block_rows[0] if block_rows else tm
            in_specs.append(pl.BlockSpec((rows, arr.shape[1]), lambda i, imap=imap, local=local: imap(local(i)), **mode))
            args.append(arr)
        for w, dt in zip(out_widths, out_dtypes):
            out_specs.append(pl.BlockSpec((tm, w), lambda i, local=local: (local(i), 0)))
            out_shape.append(jax.ShapeDtypeStruct((n_rows, w), dt))
        ranges.append((start, count, len(row_inputs), len(out_widths)))
        start += count
    n_row_refs = len(in_specs)
    n_const = len(const_inputs)
    n_group_outs = len(out_specs)
    const_specs, const_args = _const_plan(const_inputs)
    cast_steps = ranges[0][1]
    cast_specs, cast_args = [], []
    for arr, layer in casts:
        _, k_dim, n = arr.shape
        assert k_dim % (cast_steps * 16) == 0
        rows = k_dim // cast_steps
        block = lambda i: jnp.minimum(i, cast_steps - 1)
        cast_specs.append(pl.BlockSpec((None, rows, n), lambda i, layer=layer, block=block: (layer, block(i), 0)))
        cast_args.append(arr)
        out_specs.append(pl.BlockSpec((rows, n), lambda i, block=block: (block(i), 0)))
        out_shape.append(jax.ShapeDtypeStruct((k_dim, n), BF16))

    def kern(*refs):
        row_refs, raw_consts = refs[:n_row_refs], refs[n_row_refs:n_row_refs + n_const]
        n_in_refs = n_row_refs + n_const + len(casts)
        cast_in = refs[n_row_refs + n_const:n_in_refs]
        out_refs = refs[n_in_refs:n_in_refs + n_group_outs]
        cast_out = refs[n_in_refs + n_group_outs:n_in_refs + n_group_outs + len(casts)]
        own_bf16 = refs[n_in_refs + n_group_outs + len(casts):]
        i = pl.program_id(0)

        @pl.when(i == 0)
        def _():
            for j, s_ref in zip(convert_first, own_bf16):
                s_ref[...] = raw_consts[j][...].astype(BF16)

        const_refs = list(raw_consts)
        for j, s_ref in zip(convert_first, own_bf16):
            const_refs[j] = s_ref
        r0 = o0 = 0
        for g, (first, count, n_in, n_out) in enumerate(ranges):
            ins, outs = row_refs[r0:r0 + n_in], out_refs[o0:o0 + n_out]
            r0, o0 = r0 + n_in, o0 + n_out

            @pl.when((i >= first) & (i < first + count))
            def _(ins=ins, outs=outs, g=g):
                body(*ins, *const_refs, *outs)
                if g == 0:
                    for x_ref, o_ref in zip(cast_in, cast_out):
                        o_ref[...] = x_ref[...].astype(BF16)

    outs = pl.pallas_call(
        kern,
        grid=(start,),
        in_specs=in_specs + const_specs + cast_specs,
        out_specs=out_specs,
        out_shape=out_shape,
        scratch_shapes=[pltpu.VMEM(const_inputs[j].shape, BF16) for j in convert_first],
        compiler_params=pltpu.CompilerParams(dimension_semantics=("arbitrary",), vmem_limit_bytes=VMEM_LIMIT),
        name=name,
    )(*args, *const_args, *cast_args)
    grouped, o0 = [], 0
    for _, _, _, n_out in ranges:
        grouped.append(list(outs[o0:o0 + n_out]))
        o0 += n_out
    return grouped, list(outs[n_group_outs:])


def _gla_prompt_body(q_ref, k_ref, v_ref, la_ref, o_ref, sfin_ref, st_ref):
    t = pl.program_id(1)

    @pl.when(t == 0)
    def _():
        st_ref[...] = jnp.zeros_like(st_ref)

    for base in range(0, q_ref.shape[0], GLA_TILE):
        _gla_tile(slice(base, base + GLA_TILE), q_ref, k_ref, v_ref, la_ref, o_ref, st_ref)

    @pl.when(t == pl.num_programs(1) - 1)
    def _():
        sfin_ref[0] = st_ref[...]


def _gla_tile(tile, q_ref, k_ref, v_ref, la_ref, o_ref, st_ref):
    tg = GLA_TILE
    c_len = GLA_CHUNK
    n_chunks = tg // c_len
    chunk_rows = [slice(ci * c_len, (ci + 1) * c_len) for ci in range(n_chunks)]
    row = lax.broadcasted_iota(jnp.int32, (tg, tg), 0)
    col = lax.broadcasted_iota(jnp.int32, (tg, tg), 1)
    lower_b = (row // c_len == col // c_len) & (col <= row)
    lower = jnp.where(lower_b, 1.0, 0.0).astype(BF16)
    hi, mid, lo = _split3(la_ref[tile, :])
    cum = _mm(lower, hi) + _mm(lower, mid) + _mm(lower, lo)
    lane_chunk = lax.broadcasted_iota(jnp.int32, (GLA_DK_HEAD, tg), 1) // c_len

    qd, att, kv = [], [], []
    for h in range(GLA_HEADS):
        kc = slice(h * GLA_DK_HEAD, (h + 1) * GLA_DK_HEAD)
        cum_h = cum[:, kc]
        tot_h = jnp.concatenate([jnp.broadcast_to(cum_h[r.stop - 1:r.stop, :], (c_len, GLA_DK_HEAD)) for r in chunk_rows], axis=0)
        k_h = k_ref[tile, kc]
        qd_h = (q_ref[tile, kc] * jnp.exp(cum_h)).astype(BF16)
        ki_h = (k_h * jnp.exp(-cum_h)).astype(BF16)
        ke_t = (k_h * jnp.exp(tot_h - cum_h)).T
        qd.append(qd_h)
        att.append(jnp.where(lower_b, _mm_nt(qd_h, ki_h), 0.0).astype(BF16))
        v_h = v_ref[tile, h * GLA_DV_HEAD:(h + 1) * GLA_DV_HEAD]
        kv.append([_mm(jnp.where(lane_chunk == ci, ke_t, 0.0).astype(BF16), v_h) for ci in range(n_chunks)])

    s_before = []
    for h in range(GLA_HEADS):
        kc = slice(h * GLA_DK_HEAD, (h + 1) * GLA_DK_HEAD)
        st = st_ref[h]
        starts = []
        for ci, r in enumerate(chunk_rows):
            starts.append(st.astype(BF16))
            e_col = jnp.exp(cum[r.stop - SUBLANES:r.stop, kc]).T[:, SUBLANES - 1:SUBLANES]
            st = e_col * st + kv[h][ci]
        st_ref[h] = st
        s_before.append(starts)

    for h in range(GLA_HEADS):
        kc = slice(h * GLA_DK_HEAD, (h + 1) * GLA_DK_HEAD)
        vc = slice(h * GLA_DV_HEAD, (h + 1) * GLA_DV_HEAD)
        o_intra = _mm(att[h], v_ref[tile, vc])
        for ci, r in enumerate(chunk_rows):
            o_ref[tile.start + r.start:tile.start + r.stop, vc] = o_intra[r] + _mm(qd[h][r], s_before[h][ci])


def _gla_prompt(q, k, v, la, batch, seq):
    tg = GLA_STEP
    nt = seq // tg
    qk_spec = pl.BlockSpec((tg, GLA_DK), lambda b, t: (b * nt + t, 0))
    v_spec = pl.BlockSpec((tg, GLA_DV), lambda b, t: (b * nt + t, 0))
    st_shape = (GLA_HEADS, GLA_DK_HEAD, GLA_DV_HEAD)
    return pl.pallas_call(
        _gla_prompt_body,
        grid=(batch, nt),
        in_specs=[qk_spec, qk_spec, v_spec, qk_spec],
        out_specs=[v_spec, pl.BlockSpec((1,) + st_shape, lambda b, t: (b, 0, 0, 0))],
        out_shape=[jax.ShapeDtypeStruct((batch * seq, GLA_DV), F32),
                   jax.ShapeDtypeStruct((batch,) + st_shape, F32)],
        scratch_shapes=[pltpu.VMEM(st_shape, F32)],
        compiler_params=pltpu.CompilerParams(dimension_semantics=("arbitrary", "arbitrary"), vmem_limit_bytes=VMEM_LIMIT),
        name="gla_prompt",
    )(q, k, v, la)


def _gla_sample_body(q_ref, k_ref, v_ref, la_ref, s_ref, o_ref, sn_ref):
    bt = q_ref.shape[0]
    for h in range(GLA_HEADS):
        kc = slice(h * GLA_DK_HEAD, (h + 1) * GLA_DK_HEAD)
        vc = slice(h * GLA_DV_HEAD, (h + 1) * GLA_DV_HEAD)
        a_t = jnp.exp(la_ref[:, kc]).T
        k_t = k_ref[:, kc].T
        q_t = q_ref[:, kc].T
        for j in range(bt):
            s_new = a_t[:, j:j + 1] * s_ref[j, h] + k_t[:, j:j + 1] * v_ref[j:j + 1, vc]
            sn_ref[j, h] = s_new
            o_ref[j:j + 1, vc] = jnp.sum(q_t[:, j:j + 1] * s_new, axis=0, keepdims=True)


def _gla_sample(q, k, v, la, state):
    bt = STATE_TILE
    nb = q.shape[0]
    row = lambda w: pl.BlockSpec((bt, w), lambda i: (i, 0))
    st_spec = pl.BlockSpec((bt, GLA_HEADS, GLA_DK_HEAD, GLA_DV_HEAD), lambda i: (i, 0, 0, 0))
    return pl.pallas_call(
        _gla_sample_body,
        grid=(nb // bt,),
        in_specs=[row(GLA_DK), row(GLA_DK), row(GLA_DV), row(GLA_DK), st_spec],
        out_specs=[row(GLA_DV), st_spec],
        out_shape=[jax.ShapeDtypeStruct((nb, GLA_DV), F32), jax.ShapeDtypeStruct(state.shape, F32)],
        compiler_params=pltpu.CompilerParams(dimension_semantics=("arbitrary",), vmem_limit_bytes=VMEM_LIMIT),
        name="gla_sample",
    )(q, k, v, la, state)


def _swa_attend_units(sink_ref, q_ref, k_full, v_full, has_prev, o_ref):
    w = SWA_WINDOW
    hd = SWA_HEAD_DIM
    tq = q_ref.shape[0]
    nkv = k_full.shape[0]
    lane_q = lax.broadcasted_iota(jnp.int32, (w, LANES), 1) < hd
    lane_kv = lax.broadcasted_iota(jnp.int32, (nkv, LANES), 1) < hd
    i = lax.broadcasted_iota(jnp.int32, (w, 2 * w), 0)
    j = lax.broadcasted_iota(jnp.int32, (w, 2 * w), 1)
    band = jnp.where(j < w, jnp.where(j >= i, 1, 0), jnp.where(j - w <= i, 1, 0))
    band_first = jnp.where(j < w, has_prev, 1) * band
    lane_2w = lax.broadcasted_iota(jnp.int32, (2 * w, LANES), 1) < hd
    ones_lo = jnp.where(lane_2w, 1.0, 0.0).astype(BF16)
    ones_hi = jnp.where(lane_2w, 0.0, 1.0).astype(BF16)
    c2 = (hd ** -0.5) * LOG2E

    k_prep, v_prep = [], []
    for p in range(SWA_KV // LANES):
        cols = slice(p * LANES, (p + 1) * LANES)
        k_p, v_p = k_full[:, cols], v_full[:, cols]
        k_prep.append((k_p.astype(BF16), pltpu.roll(k_p, hd, axis=1).astype(BF16)))
        v_r = pltpu.roll(v_p, hd, axis=1)
        v_prep.append(((jnp.where(lane_kv, v_p, 0.0).astype(BF16), jnp.where(lane_kv, 0.0, v_r).astype(BF16)),
                       (jnp.where(lane_kv, v_r, 0.0).astype(BF16), jnp.where(lane_kv, 0.0, v_p).astype(BF16))))

    def softmax_part(s, hh, mask):
        s2 = jnp.where(mask, s, NEG_BIG)
        sk2 = jnp.full((w, 1), sink_ref[hh], F32) * LOG2E
        m2 = jnp.maximum(jnp.max(s2, axis=-1, keepdims=True), sk2)
        return jnp.exp2(s2 - m2).astype(BF16), sk2 - m2

    for b in range(tq // w):
        rows = slice(b * w, (b + 1) * w)
        krows = slice(b * w, (b + 2) * w)
        mask = (band_first if b == 0 else band) > 0
        for p in range(SWA_KV // LANES):
            q_lo, q_hi = [], []
            for x in range(4):
                q_c = q_ref[rows, (4 * p + x) * LANES:(4 * p + x + 1) * LANES] * c2
                q_lo.append(jnp.where(lane_q, q_c, 0.0).astype(BF16))
                q_hi.append(jnp.where(lane_q, 0.0, q_c).astype(BF16))
            s_self = _mm_nt(jnp.concatenate([q_lo[0], q_lo[1], q_hi[2], q_hi[3]], axis=0), k_prep[p][0][krows])
            s_roll = _mm_nt(jnp.concatenate([q_hi[0], q_hi[1], q_lo[2], q_lo[3]], axis=0), k_prep[p][1][krows])
            for x in range(4):
                c = 4 * p + x
                gh = x // 2
                xr = slice(x * w, (x + 1) * w)
                s_lo, s_hi = (s_self[xr], s_roll[xr]) if gh == 0 else (s_roll[xr], s_self[xr])
                p_lo, d_lo = softmax_part(s_lo, 2 * c, mask)
                p_hi, d_hi = softmax_part(s_hi, 2 * c + 1, mask)
                v_lo, v_hi = v_prep[p][gh]
                rhs = jnp.concatenate([jnp.concatenate([v_lo[krows], ones_lo], axis=1),
                                       jnp.concatenate([v_hi[krows], ones_hi], axis=1)], axis=0)
                ext = _mm(jnp.concatenate([p_lo, p_hi], axis=1), rhs)
                den = ext[:, LANES:] + jnp.exp2(jnp.where(lane_q, d_lo, d_hi))
                o_ref[rows, c * LANES:(c + 1) * LANES] = (ext[:, :LANES] / den).astype(o_ref.dtype)
            yield


def _out_stage(at_bf16, h, wo_ref, bo_ref, gpost_ref, gfpre_ref, wup_ref, wdn_ref, gfpost_ref, between=None):
    m = _mm(at_bf16, wo_ref[...]) + bo_ref[...]
    h1 = h + _rms(m, gpost_ref[...])
    f = _ffn(_rms(h1, gfpre_ref[...]).astype(BF16), wup_ref, wdn_ref, between)
    return h1 + _rms(f, gfpost_ref[...])


def _swa_out_body(n_tiles, nt, sink_ref, q_ref, kc_ref, kp_ref, vc_ref, vp_ref, h_ref, ats_ref, hs_ref,
                  wo_ref, bo_ref, gpost_ref, gfpre_ref, wup_ref, wdn_ref, gfpost_ref,
                  y_ref, ys_ref, attn_scr):
    i = pl.program_id(0)
    consts = (wo_ref, bo_ref, gpost_ref, gfpre_ref, wup_ref, wdn_ref, gfpost_ref)

    def attention():
        has_prev = jnp.minimum(lax.rem(i, nt), 1)
        k_full = jnp.concatenate([kp_ref[...], kc_ref[...]], axis=0)
        v_full = jnp.concatenate([vp_ref[...], vc_ref[...]], axis=0)
        return _swa_attend_units(sink_ref, q_ref, k_full, v_full, has_prev, attn_scr)

    @pl.when(i == 0)
    def _():
        for _ in attention():
            pass

    @pl.when((i > 0) & (i < n_tiles))
    def _():
        at_prev = attn_scr[...]
        units = attention()
        y_ref[...] = _out_stage(at_prev, h_ref[...], *consts, between=lambda: next(units, None))
        for _ in units:
            pass

    @pl.when(i == n_tiles)
    def _():
        y_ref[...] = _out_stage(attn_scr[...], h_ref[...], *consts)

    @pl.when(i == n_tiles + 1)
    def _():
        ys_ref[...] = _out_stage(ats_ref[...].astype(BF16), hs_ref[...], *consts)


def _swa_out(sinks, q, k, v, h, attn_s, h_s, consts, batch, seq):
    tq = SWA_TILE
    w = SWA_WINDOW
    nt = seq // tq
    n_tiles = batch * nt
    n_s = h_s.shape[0]

    def att_tile(i):
        return (jnp.minimum(i, n_tiles - 1), 0)

    def prev_block(i):
        g = jnp.minimum(i, n_tiles - 1)
        return (g * (tq // w) - jnp.minimum(lax.rem(g, nt), 1), 0)

    def out_tile(i):
        return (jnp.clip(i - 1, 0, n_tiles - 1), 0)

    whole = lambda arr: pl.BlockSpec(arr.shape, lambda i: (0, 0))
    const_specs, const_args = _const_plan(consts)
    return pl.pallas_call(
        functools.partial(_swa_out_body, n_tiles, nt),
        grid=(n_tiles + 2,),
        in_specs=[pl.BlockSpec(memory_space=pltpu.SMEM),
                  pl.BlockSpec((tq, SWA_Q), att_tile), pl.BlockSpec((tq, SWA_KV), att_tile),
                  pl.BlockSpec((w, SWA_KV), prev_block), pl.BlockSpec((tq, SWA_KV), att_tile),
                  pl.BlockSpec((w, SWA_KV), prev_block), pl.BlockSpec((tq, D_MODEL), out_tile),
                  whole(attn_s), whole(h_s)] + const_specs,
        out_specs=[pl.BlockSpec((tq, D_MODEL), out_tile), whole(h_s)],
        out_shape=[jax.ShapeDtypeStruct((batch * seq, D_MODEL), F32), jax.ShapeDtypeStruct((n_s, D_MODEL), F32)],
        scratch_shapes=[pltpu.VMEM((tq, SWA_Q), BF16)],
        compiler_params=pltpu.CompilerParams(dimension_semantics=("arbitrary",), vmem_limit_bytes=VMEM_LIMIT),
        name="swa_out",
    )(sinks, q, k, k, v, v, h, attn_s, h_s, *const_args)


def _swa_sample_body(sk_ref, q_ref, kn_ref, vn_ref, ck_ref, cv_ref, o_ref, nk_ref, nv_ref):
    bt = q_ref.shape[0]
    w = ck_ref.shape[2]
    hd = SWA_HEAD_DIM
    hgroup = lax.broadcasted_iota(jnp.int32, (SWA_HEADS, 1), 0) // SWA_GROUP
    newest = lax.broadcasted_iota(jnp.int32, (SWA_KV, w), 1) == w - 1
    kn_t = kn_ref[...].T
    vn_t = vn_ref[...].T
    scale = hd ** -0.5
    sk = sk_ref[...]
    groups = [slice(g * hd, (g + 1) * hd) for g in range(SWA_KV_HEADS)]

    def per_head(pieces):
        out = pieces[0]
        for g in range(1, SWA_KV_HEADS):
            out = jnp.where(hgroup == g, pieces[g], out)
        return out

    for j in range(bt):
        nk_ref[j] = jnp.where(newest, kn_t[:, j:j + 1], pltpu.roll(ck_ref[j], w - 1, axis=1))
        nv_ref[j] = jnp.where(newest, vn_t[:, j:j + 1], pltpu.roll(cv_ref[j], w - 1, axis=1))

    s_old, s_new, v_sel = [], [], []
    for j in range(bt):
        q = q_ref[j]
        qb = q.astype(BF16)
        s_old.append(per_head([_mm(qb, ck_ref[j, rows, :].astype(BF16)) for rows in groups]))
        k_sel = per_head([kn_ref[j:j + 1, cols] for cols in groups])
        v_sel.append(per_head([vn_ref[j:j + 1, cols] for cols in groups]))
        s_new.append(jnp.sum(q * k_sel, axis=-1, keepdims=True))
    s_old = jnp.stack(s_old, axis=0) * scale
    s_new = jnp.stack(s_new, axis=0) * scale
    m = jnp.maximum(jnp.maximum(jnp.max(s_old, axis=-1, keepdims=True), s_new), sk)
    p_old = jnp.exp(s_old - m)
    p_new = jnp.exp(s_new - m)
    inv = 1.0 / (jnp.sum(p_old, axis=-1, keepdims=True) + p_new + jnp.exp(sk - m))
    p_old = p_old.astype(BF16)
    for j in range(bt):
        o = per_head([_mm_nt(p_old[j], cv_ref[j, rows, :].astype(BF16)) for rows in groups])
        o_ref[j] = (o + p_new[j] * v_sel[j]) * inv[j]


def _swa_sample(sinks, q3, k_new, v_new, cache_k, cache_v):
    bt = SEQ_TILE
    nb, _, w = cache_k.shape
    assert w == LANES
    row = lambda width: pl.BlockSpec((bt, width), lambda i: (i, 0))
    q_spec = pl.BlockSpec((bt, SWA_HEADS, SWA_HEAD_DIM), lambda i: (i, 0, 0))
    c_spec = pl.BlockSpec((bt, SWA_KV, w), lambda i: (i, 0, 0))
    return pl.pallas_call(
        _swa_sample_body,
        grid=(nb // bt,),
        in_specs=[pl.BlockSpec((SWA_HEADS, 1), lambda i: (0, 0)), q_spec, row(SWA_KV), row(SWA_KV), c_spec, c_spec],
        out_specs=[q_spec, c_spec, c_spec],
        out_shape=[jax.ShapeDtypeStruct((nb, SWA_HEADS, SWA_HEAD_DIM), F32),
                   jax.ShapeDtypeStruct(cache_k.shape, F32), jax.ShapeDtypeStruct(cache_v.shape, F32)],
        compiler_params=pltpu.CompilerParams(dimension_semantics=("arbitrary",)),
        name="swa_sample",
    )(sinks, q3, k_new, v_new, cache_k, cache_v)


def _rope_parts(tile_rows, tile_starts):
    d = jnp.arange(LANES) % SWA_HEAD_DIM
    inv = jnp.power(ROPE_THETA, -(d % ROPE_HALF).astype(F32) * 2.0 / ROPE_DIM)
    local = jnp.arange(tile_rows, dtype=F32)[:, None] * inv[None, :]
    base = jnp.repeat(tile_starts.astype(F32), SUBLANES)[:, None] * inv[None, :]
    return jnp.cos(local), jnp.sin(local), jnp.cos(base), jnp.sin(base)


def kernel(x_prompt, x_sample, state_gla, cache_swa_k, cache_swa_v, gla_w_in, gla_w_gate2, gla_b_gate, gla_g_head, gla_w_out, swa_w_qkv, swa_b_qkv, swa_sinks, swa_w_out, swa_b_out, norm_mix_pre, norm_mix_post, norm_ffn_pre, norm_ffn_post, ffn_w_up, ffn_w_down):
    batch, seq, _ = x_prompt.shape
    dec_batch, dec_seq, _ = x_sample.shape
    assert dec_seq == 1 and seq % SWA_WINDOW == 0
    past_len = seq
    n_p, n_s = batch * seq, dec_batch * dec_seq
    xp = x_prompt.reshape(n_p, D_MODEL)
    xs = x_sample.reshape(n_s, D_MODEL)

    w_g2 = gla_w_gate2[0].astype(BF16)
    b_g = gla_b_gate[0][None, :]
    g_head = gla_g_head[0][None, :]
    b_qkv = swa_b_qkv[0][None, :]
    b_sout = swa_b_out[0][None, :]
    row = lambda t, i: t[i][None, :]
    in0_widths = [GLA_DK, GLA_DK, GLA_DV, GLA_DV, GLA_DK]
    mid_widths = [D_MODEL, SWA_Q, SWA_KV, SWA_KV]

    tm, ts = TOKEN_TILE, n_s
    ((q, k, v, r, la), (qs, ks, vs, rs, las)), (w_gout, w_up0, w_dn0, w_qkv) = _tok_call(
        _in0_body,
        [(n_p, tm, [(xp, None)], in0_widths, [F32, F32, BF16, F32, F32]),
         (n_s, ts, [(xs, None)], in0_widths, [F32] * 5)],
        [row(norm_mix_pre, 0), gla_w_in[0].T, w_g2, b_g], "in0",
        casts=[(gla_w_out, 0), (ffn_w_up, 0), (ffn_w_down, 0), (swa_w_qkv, 0)], convert_first=(1,))
    o, s_fin_p = _gla_prompt(q, k, v, la, batch, seq)
    o_s, s_new = _gla_sample(qs, ks, vs, las, state_gla[0])
    lc, ls, bc, bs = _rope_parts(tm, jnp.arange(seq // tm) * tm)
    lc_s, ls_s, bc_s, bs_s = _rope_parts(1, jnp.full((1,), past_len))
    same = lambda t: (0, 0)
    start_map = lambda t: (t % (seq // tm), 0)
    rope_p = [(lc, same), (ls, same), (bc, start_map, SUBLANES), (bs, start_map, SUBLANES)]
    rope_s = [(jnp.broadcast_to(lc_s, (ts, LANES)), None), (jnp.broadcast_to(ls_s, (ts, LANES)), None),
              (bc_s, same, SUBLANES), (bs_s, same, SUBLANES)]
    mid_consts = [g_head, w_gout, row(norm_mix_post, 0), row(norm_ffn_pre, 0), w_up0, w_dn0, row(norm_ffn_post, 0),
                  row(norm_mix_pre, 1), w_qkv, b_qkv]
    ((h2, q1, k1, v1), (h2s, q1s, k1s, v1s)), (w_sout, w_up1, w_dn1) = _tok_call(
        _mid_body,
        [(n_p, tm, [(o, None), (r, None), (xp, None)] + rope_p, mid_widths, [F32] * 4),
         (n_s, ts, [(o_s, None), (rs, None), (xs, None)] + rope_s, mid_widths, [F32] * 4)],
        mid_consts, "mid",
        casts=[(swa_w_out, 0), (ffn_w_up, 1), (ffn_w_down, 1)])
    out_consts = [w_sout, b_sout, row(norm_mix_post, 1), row(norm_ffn_pre, 1), w_up1, w_dn1, row(norm_ffn_post, 1)]
    win = cache_swa_k.shape[2]
    to_t = lambda c: jnp.transpose(c[0].reshape(dec_batch, win, SWA_KV), (0, 2, 1))
    from_t = lambda c: jnp.transpose(c, (0, 2, 1)).reshape(1, dec_batch, win, SWA_KV_HEADS, SWA_HEAD_DIM)
    attn_s, nk, nv = _swa_sample(swa_sinks[0][:, None], q1s.reshape(n_s, SWA_HEADS, SWA_HEAD_DIM), k1s, v1s,
                                 to_t(cache_swa_k), to_t(cache_swa_v))
    y_p, y_s = _swa_out(swa_sinks[0], q1, k1, v1, h2, attn_s.reshape(n_s, SWA_Q), h2s, out_consts, batch, seq)
    wp = min(SWA_WINDOW, seq)
    tail = lambda t: t.reshape(batch, seq, SWA_KV)[:, seq - wp:].reshape(batch, wp, SWA_KV_HEADS, SWA_HEAD_DIM)
    k_tail, v_tail = tail(k1), tail(v1)

    return (y_p.reshape(batch, seq, D_MODEL), y_s.reshape(dec_batch, dec_seq, D_MODEL),
            s_fin_p[None], s_new[None], k_tail[None], v_tail[None], from_t(nk), from_t(nv))
```

```python
import functools

import jax
import jax.numpy as jnp
from jax import lax
from jax.experimental import pallas as pl
from jax.experimental.pallas import tpu as pltpu

F32 = jnp.float32
BF16 = jnp.bfloat16

D_MODEL = 1024
D_FF = 4 * D_MODEL
NORM_EPS = 1e-6

GLA_HEADS = 4
GLA_DK = D_MODEL // 2
GLA_DV = D_MODEL
GLA_DK_HEAD = GLA_DK // GLA_HEADS
GLA_DV_HEAD = GLA_DV // GLA_HEADS
GLA_GATE_RANK = 16
GLA_TAU = 16.0
GLA_CHUNK = 64
GLA_MAIN = 2 * GLA_DK + 2 * GLA_DV

SWA_HEAD_DIM = 64
SWA_HEADS = D_MODEL // SWA_HEAD_DIM
SWA_KV_HEADS = 4
SWA_GROUP = SWA_HEADS // SWA_KV_HEADS
SWA_WINDOW = 128
SWA_Q = SWA_HEADS * SWA_HEAD_DIM
SWA_KV = SWA_KV_HEADS * SWA_HEAD_DIM
SWA_QKV = SWA_Q + 2 * SWA_KV
ROPE_THETA = 500000.0
ROPE_DIM = SWA_HEAD_DIM // 4
ROPE_HALF = ROPE_DIM // 2

LANES = 128
SUBLANES = 8
FFN_CHUNK = 512
TOKEN_TILE = 512
GLA_TILE = 256
GLA_STEP = 2048
SWA_TILE = 512
SEQ_TILE = 8
STATE_TILE = 16
VMEM_LIMIT = 58 * 1024 * 1024
NEG_BIG = -1e30
LOG2E = 1.4426950408889634


def _mm(a, b):
    return jnp.dot(a, b, preferred_element_type=F32)


def _mm_nt(a, b):
    return lax.dot_general(a, b, (((1,), (1,)), ((), ())), preferred_element_type=F32)


def _mm_tn(a, b):
    return lax.dot_general(a, b, (((0,), (0,)), ((), ())), preferred_element_type=F32)


def _rms(x, g):
    ms = jnp.mean(x * x, axis=-1, keepdims=True)
    return x * lax.rsqrt(ms + NORM_EPS) * g


def _split3(x):
    hi = x.astype(BF16)
    r1 = x - hi.astype(F32)
    mid = r1.astype(BF16)
    lo = (r1 - mid.astype(F32)).astype(BF16)
    return hi, mid, lo


def _ffn(a_bf16, wup_ref, wdn_ref, between=None):
    n_slices = D_FF // FFN_CHUNK
    cols = [slice(c * FFN_CHUNK, (c + 1) * FFN_CHUNK) for c in range(n_slices)]
    acc = None
    u_next = _mm(a_bf16, wup_ref[:, cols[0]])
    for c in range(n_slices):
        u = u_next
        if c + 1 < n_slices:
            u_next = _mm(a_bf16, wup_ref[:, cols[c + 1]])
        u = jnp.square(jnp.maximum(u, 0.0)).astype(BF16)
        p = _mm(u, wdn_ref[cols[c], :])
        acc = p if acc is None else acc + p
        if between is not None:
            between()
    return acc


def _in0_body(x_ref, g_ref, wt_ref, wg_ref, bg_ref, q_ref, k_ref, v_ref, r_ref, la_ref):
    a = _rms(x_ref[...], g_ref[...]).astype(BF16)
    proj = lambda lo, hi: _mm_nt(a, wt_ref[lo:hi, :])
    z = proj(GLA_MAIN, GLA_MAIN + GLA_GATE_RANK).astype(BF16)
    q_ref[...] = proj(0, GLA_DK) * (GLA_DK_HEAD ** -0.5)
    k_ref[...] = proj(GLA_DK, 2 * GLA_DK)
    zg = _mm(z, wg_ref[...]) + bg_ref[...]
    la_ref[...] = (jnp.minimum(zg, 0.0) - jnp.log1p(jnp.exp(-jnp.abs(zg)))) * (1.0 / GLA_TAU)
    for c in range(GLA_DV // 512):
        cols = slice(c * 512, (c + 1) * 512)
        v_ref[:, cols] = proj(2 * GLA_DK + c * 512, 2 * GLA_DK + (c + 1) * 512).astype(v_ref.dtype)
        r_ref[:, cols] = proj(2 * GLA_DK + GLA_DV + c * 512, 2 * GLA_DK + GLA_DV + (c + 1) * 512)


def _mid_body(o_ref, r_ref, h_ref, lc_ref, ls_ref, bc_ref, bs_ref,
              gh_ref, wo_ref, gpost_ref, gfpre_ref, wup_ref, wdn_ref, gfpost_ref,
              gpre1_ref, wqkv_ref, bqkv_ref,
              h2_ref, q1_ref, k1_ref, v1_ref):
    m = None
    for hh in range(GLA_HEADS):
        cols = slice(hh * GLA_DV_HEAD, (hh + 1) * GLA_DV_HEAD)
        on = _rms(o_ref[:, cols], gh_ref[...])
        r = r_ref[:, cols]
        u = (on * (r * (1.0 / (1.0 + jnp.exp(-r))))).astype(BF16)
        p = _mm(u, wo_ref[cols, :])
        m = p if m is None else m + p
    h1 = h_ref[...] + _rms(m, gpost_ref[...])
    f = _ffn(_rms(h1, gfpre_ref[...]).astype(BF16), wup_ref, wdn_ref)
    h2 = h1 + _rms(f, gfpost_ref[...])
    h2_ref[...] = h2
    a3 = _rms(h2, gpre1_ref[...]).astype(BF16)
    lc, ls, bc, bs = lc_ref[...], ls_ref[...], bc_ref[0:1, :], bs_ref[0:1, :]
    cos_t = bc * lc - bs * ls
    sin_t = bs * lc + bc * ls
    d = lax.broadcasted_iota(jnp.int32, (1, LANES), 1) % SWA_HEAD_DIM
    rc = jnp.where(d < ROPE_DIM, cos_t, 1.0)
    ra = jnp.where(d < ROPE_HALF, -sin_t, 0.0)
    rb = jnp.where(d < ROPE_HALF, 0.0, jnp.where(d < ROPE_DIM, sin_t, 0.0))
    wide = 2 * LANES
    for c2 in range((SWA_Q + SWA_KV) // wide):
        x2 = _mm(a3, wqkv_ref[:, c2 * wide:(c2 + 1) * wide]) + bqkv_ref[:, c2 * wide:(c2 + 1) * wide]
        for half in range(2):
            c = 2 * c2 + half
            x = x2[:, half * LANES:(half + 1) * LANES]
            y = x * rc + pltpu.roll(x, LANES - ROPE_HALF, axis=1) * ra + pltpu.roll(x, ROPE_HALF, axis=1) * rb
            if c < SWA_Q // LANES:
                q1_ref[:, c * LANES:(c + 1) * LANES] = y
            else:
                k1_ref[:, c * LANES - SWA_Q:(c + 1) * LANES - SWA_Q] = y
    v1_ref[...] = _mm(a3, wqkv_ref[:, SWA_Q + SWA_KV:SWA_QKV]) + bqkv_ref[:, SWA_Q + SWA_KV:SWA_QKV]


def _const_plan(const_inputs):
    in_specs, args = [], []
    for entry in const_inputs:
        if isinstance(entry, tuple):
            arr, layer = entry
            spec = pl.BlockSpec((None,) + arr.shape[1:], lambda i, layer=layer: (layer, 0, 0), pipeline_mode=pl.Buffered(1))
        else:
            arr = entry
            spec = pl.BlockSpec(arr.shape, lambda i: (0, 0), pipeline_mode=pl.Buffered(1))
        in_specs.append(spec)
        args.append(arr)
    return in_specs, args


def _tok_call(body, groups, const_inputs, name, casts=(), convert_first=()):
    in_specs, args, out_specs, out_shape = [], [], [], []
    ranges, start = [], 0
    for n_rows, tm, row_inputs, out_widths, out_dtypes in groups:
        assert n_rows % tm == 0
        count = n_rows // tm
        local = lambda i, start=start, count=count: jnp.clip(i - start, 0, count - 1)
        mode = dict(pipeline_mode=pl.Buffered(1)) if count == 1 else {}
        for arr, imap, *block_rows in row_inputs:
            imap = imap if imap is not None else (lambda t: (t, 0))
            rows = block_rows[0] if block_rows else tm
            in_specs.append(pl.BlockSpec((rows, arr.shape[1]), lambda i, imap=imap, local=local: imap(local(i)), **mode))
            args.append(arr)
        for w, dt in zip(out_widths, out_dtypes):
            out_specs.append(pl.BlockSpec((tm, w), lambda i, local=local: (local(i), 0)))
            out_shape.append(jax.ShapeDtypeStruct((n_rows, w), dt))
        ranges.append((start, count, len(row_inputs), len(out_widths)))
        start += count
    n_row_refs = len(in_specs)
    n_const = len(const_inputs)
    n_group_outs = len(out_specs)
    const_specs, const_args = _const_plan(const_inputs)
    cast_steps = ranges[0][1]
    cast_specs, cast_args = [], []
    for arr, layer in casts:
        _, k_dim, n = arr.shape
        assert k_dim % (cast_steps * 16) == 0
        rows = k_dim // cast_steps
        block = lambda i: jnp.minimum(i, cast_steps - 1)
        cast_specs.append(pl.BlockSpec((None, rows, n), lambda i, layer=layer, block=block: (layer, block(i), 0)))
        cast_args.append(arr)
        out_specs.append(pl.BlockSpec((rows, n), lambda i, block=block: (block(i), 0)))
        out_shape.append(jax.ShapeDtypeStruct((k_dim, n), BF16))

    def kern(*refs):
        row_refs, raw_consts = refs[:n_row_refs], refs[n_row_refs:n_row_refs + n_const]
        n_in_refs = n_row_refs + n_const + len(casts)
        cast_in = refs[n_row_refs + n_const:n_in_refs]
        out_refs = refs[n_in_refs:n_in_refs + n_group_outs]
        cast_out = refs[n_in_refs + n_group_outs:n_in_refs + n_group_outs + len(casts)]
        own_bf16 = refs[n_in_refs + n_group_outs + len(casts):]
        i = pl.program_id(0)

        @pl.when(i == 0)
        def _():
            for j, s_ref in zip(convert_first, own_bf16):
                s_ref[...] = raw_consts[j][...].astype(BF16)

        const_refs = list(raw_consts)
        for j, s_ref in zip(convert_first, own_bf16):
            const_refs[j] = s_ref
        r0 = o0 = 0
        for g, (first, count, n_in, n_out) in enumerate(ranges):
            ins, outs = row_refs[r0:r0 + n_in], out_refs[o0:o0 + n_out]
            r0, o0 = r0 + n_in, o0 + n_out

            @pl.when((i >= first) & (i < first + count))
            def _(ins=ins, outs=outs, g=g):
                body(*ins, *const_refs, *outs)
                if g == 0:
                    for x_ref, o_ref in zip(cast_in, cast_out):
                        o_ref[...] = x_ref[...].astype(BF16)

    outs = pl.pallas_call(
        kern,
        grid=(start,),
        in_specs=in_specs + const_specs + cast_specs,
        out_specs=out_specs,
        out_shape=out_shape,
        scratch_shapes=[pltpu.VMEM(const_inputs[j].shape, BF16) for j in convert_first],
        compiler_params=pltpu.CompilerParams(dimension_semantics=("arbitrary",), vmem_limit_bytes=VMEM_LIMIT),
        name=name,
    )(*args, *const_args, *cast_args)
    grouped, o0 = [], 0
    for _, _, _, n_out in ranges:
        grouped.append(list(outs[o0:o0 + n_out]))
        o0 += n_out
    return grouped, list(outs[n_group_outs:])


def _gla_prompt_body(q_ref, k_ref, v_ref, la_ref, o_ref, sfin_ref, st_ref):
    t = pl.program_id(1)

    @pl.when(t == 0)
    def _():
        st_ref[...] = jnp.zeros_like(st_ref)

    for base in range(0, q_ref.shape[0], GLA_TILE):
        _gla_tile(slice(base, base + GLA_TILE), q_ref, k_ref, v_ref, la_ref, o_ref, st_ref)

    @pl.when(t == pl.num_programs(1) - 1)
    def _():
        sfin_ref[0] = st_ref[...]


def _gla_tile(tile, q_ref, k_ref, v_ref, la_ref, o_ref, st_ref):
    tg = GLA_TILE
    c_len = GLA_CHUNK
    n_chunks = tg // c_len
    chunk_rows = [slice(ci * c_len, (ci + 1) * c_len) for ci in range(n_chunks)]
    row = lax.broadcasted_iota(jnp.int32, (tg, tg), 0)
    col = lax.broadcasted_iota(jnp.int32, (tg, tg), 1)
    lower_b = (row // c_len == col // c_len) & (col <= row)
    lower = jnp.where(lower_b, 1.0, 0.0).astype(BF16)
    hi, mid, lo = _split3(la_ref[tile, :])
    cum = _mm(lower, hi) + _mm(lower, mid) + _mm(lower, lo)
    lane_chunk = lax.broadcasted_iota(jnp.int32, (GLA_DK_HEAD, tg), 1) // c_len

    qd, att, kv = [], [], []
    for h in range(GLA_HEADS):
        kc = slice(h * GLA_DK_HEAD, (h + 1) * GLA_DK_HEAD)
        cum_h = cum[:, kc]
        tot_h = jnp.concatenate([jnp.broadcast_to(cum_h[r.stop - 1:r.stop, :], (c_len, GLA_DK_HEAD)) for r in chunk_rows], axis=0)
        k_h = k_ref[tile, kc]
        qd_h = (q_ref[tile, kc] * jnp.exp(cum_h)).astype(BF16)
        ki_h = (k_h * jnp.exp(-cum_h)).astype(BF16)
        ke_t = (k_h * jnp.exp(tot_h - cum_h)).T
        qd.append(qd_h)
        att.append(jnp.where(lower_b, _mm_nt(qd_h, ki_h), 0.0).astype(BF16))
        v_h = v_ref[tile, h * GLA_DV_HEAD:(h + 1) * GLA_DV_HEAD]
        kv.append([_mm(jnp.where(lane_chunk == ci, ke_t, 0.0).astype(BF16), v_h) for ci in range(n_chunks)])

    s_before = []
    for h in range(GLA_HEADS):
        kc = slice(h * GLA_DK_HEAD, (h + 1) * GLA_DK_HEAD)
        st = st_ref[h]
        starts = []
        for ci, r in enumerate(chunk_rows):
            starts.append(st.astype(BF16))
            e_col = jnp.exp(cum[r.stop - SUBLANES:r.stop, kc]).T[:, SUBLANES - 1:SUBLANES]
            st = e_col * st + kv[h][ci]
        st_ref[h] = st
        s_before.append(starts)

    for h in range(GLA_HEADS):
        kc = slice(h * GLA_DK_HEAD, (h + 1) * GLA_DK_HEAD)
        vc = slice(h * GLA_DV_HEAD, (h + 1) * GLA_DV_HEAD)
        o_intra = _mm(att[h], v_ref[tile, vc])
        for ci, r in enumerate(chunk_rows):
            o_ref[tile.start + r.start:tile.start + r.stop, vc] = o_intra[r] + _mm(qd[h][r], s_before[h][ci])


def _gla_prompt(q, k, v, la, batch, seq):
    tg = GLA_STEP
    nt = seq // tg
    qk_spec = pl.BlockSpec((tg, GLA_DK), lambda b, t: (b * nt + t, 0))
    v_spec = pl.BlockSpec((tg, GLA_DV), lambda b, t: (b * nt + t, 0))
    st_shape = (GLA_HEADS, GLA_DK_HEAD, GLA_DV_HEAD)
    return pl.pallas_call(
        _gla_prompt_body,
        grid=(batch, nt),
        in_specs=[qk_spec, qk_spec, v_spec, qk_spec],
        out_specs=[v_spec, pl.BlockSpec((1,) + st_shape, lambda b, t: (b, 0, 0, 0))],
        out_shape=[jax.ShapeDtypeStruct((batch * seq, GLA_DV), F32),
                   jax.ShapeDtypeStruct((batch,) + st_shape, F32)],
        scratch_shapes=[pltpu.VMEM(st_shape, F32)],
        compiler_params=pltpu.CompilerParams(dimension_semantics=("arbitrary", "arbitrary"), vmem_limit_bytes=VMEM_LIMIT),
        name="gla_prompt",
    )(q, k, v, la)


def _gla_sample_body(q_ref, k_ref, v_ref, la_ref, s_ref, o_ref, sn_ref):
    bt = q_ref.shape[0]
    for h in range(GLA_HEADS):
        kc = slice(h * GLA_DK_HEAD, (h + 1) * GLA_DK_HEAD)
        vc = slice(h * GLA_DV_HEAD, (h + 1) * GLA_DV_HEAD)
        a_t = jnp.exp(la_ref[:, kc]).T
        k_t = k_ref[:, kc].T
        q_t = q_ref[:, kc].T
        for j in range(bt):
            s_new = a_t[:, j:j + 1] * s_ref[j, h] + k_t[:, j:j + 1] * v_ref[j:j + 1, vc]
            sn_ref[j, h] = s_new
            o_ref[j:j + 1, vc] = jnp.sum(q_t[:, j:j + 1] * s_new, axis=0, keepdims=True)


def _gla_sample(q, k, v, la, state):
    bt = STATE_TILE
    nb = q.shape[0]
    row = lambda w: pl.BlockSpec((bt, w), lambda i: (i, 0))
    st_spec = pl.BlockSpec((bt, GLA_HEADS, GLA_DK_HEAD, GLA_DV_HEAD), lambda i: (i, 0, 0, 0))
    return pl.pallas_call(
        _gla_sample_body,
        grid=(nb // bt,),
        in_specs=[row(GLA_DK), row(GLA_DK), row(GLA_DV), row(GLA_DK), st_spec],
        out_specs=[row(GLA_DV), st_spec],
        out_shape=[jax.ShapeDtypeStruct((nb, GLA_DV), F32), jax.ShapeDtypeStruct(state.shape, F32)],
        compiler_params=pltpu.CompilerParams(dimension_semantics=("arbitrary",), vmem_limit_bytes=VMEM_LIMIT),
        name="gla_sample",
    )(q, k, v, la, state)


def _swa_attend_units(sink_ref, q_ref, k_full, v_full, has_prev, o_ref):
    w = SWA_WINDOW
    hd = SWA_HEAD_DIM
    tq = q_ref.shape[0]
    nkv = k_full.shape[0]
    lane_q = lax.broadcasted_iota(jnp.int32, (w, LANES), 1) < hd
    lane_kv = lax.broadcasted_iota(jnp.int32, (nkv, LANES), 1) < hd
    i = lax.broadcasted_iota(jnp.int32, (w, 2 * w), 0)
    j = lax.broadcasted_iota(jnp.int32, (w, 2 * w), 1)
    band = jnp.where(j < w, jnp.where(j >= i, 1, 0), jnp.where(j - w <= i, 1, 0))
    band_first = jnp.where(j < w, has_prev, 1) * band
    lane_2w = lax.broadcasted_iota(jnp.int32, (2 * w, LANES), 1) < hd
    ones_lo = jnp.where(lane_2w, 1.0, 0.0).astype(BF16)
    ones_hi = jnp.where(lane_2w, 0.0, 1.0).astype(BF16)
    c2 = (hd ** -0.5) * LOG2E

    k_prep, v_prep = [], []
    for p in range(SWA_KV // LANES):
        cols = slice(p * LANES, (p + 1) * LANES)
        k_p, v_p = k_full[:, cols], v_full[:, cols]
        k_prep.append((k_p.astype(BF16), pltpu.roll(k_p, hd, axis=1).astype(BF16)))
        v_r = pltpu.roll(v_p, hd, axis=1)
        v_prep.append(((jnp.where(lane_kv, v_p, 0.0).astype(BF16), jnp.where(lane_kv, 0.0, v_r).astype(BF16)),
                       (jnp.where(lane_kv, v_r, 0.0).astype(BF16), jnp.where(lane_kv, 0.0, v_p).astype(BF16))))

    def softmax_part(s, hh, mask):
        s2 = jnp.where(mask, s, NEG_BIG)
        sk2 = jnp.full((w, 1), sink_ref[hh], F32) * LOG2E
        m2 = jnp.maximum(jnp.max(s2, axis=-1, keepdims=True), sk2)
        return jnp.exp2(s2 - m2).astype(BF16), sk2 - m2

    for b in range(tq // w):
        rows = slice(b * w, (b + 1) * w)
        krows = slice(b * w, (b + 2) * w)
        mask = (band_first if b == 0 else band) > 0
        for p in range(SWA_KV // LANES):
            q_lo, q_hi = [], []
            for x in range(4):
                q_c = q_ref[rows, (4 * p + x) * LANES:(4 * p + x + 1) * LANES] * c2
                q_lo.append(jnp.where(lane_q, q_c, 0.0).astype(BF16))
                q_hi.append(jnp.where(lane_q, 0.0, q_c).astype(BF16))
            s_self = _mm_nt(jnp.concatenate([q_lo[0], q_lo[1], q_hi[2], q_hi[3]], axis=0), k_prep[p][0][krows])
            s_roll = _mm_nt(jnp.concatenate([q_hi[0], q_hi[1], q_lo[2], q_lo[3]], axis=0), k_prep[p][1][krows])
            for x in range(4):
                c = 4 * p + x
                gh = x // 2
                xr = slice(x * w, (x + 1) * w)
                s_lo, s_hi = (s_self[xr], s_roll[xr]) if gh == 0 else (s_roll[xr], s_self[xr])
                p_lo, d_lo = softmax_part(s_lo, 2 * c, mask)
                p_hi, d_hi = softmax_part(s_hi, 2 * c + 1, mask)
                v_lo, v_hi = v_prep[p][gh]
                rhs = jnp.concatenate([jnp.concatenate([v_lo[krows], ones_lo], axis=1),
                                       jnp.concatenate([v_hi[krows], ones_hi], axis=1)], axis=0)
                ext = _mm(jnp.concatenate([p_lo, p_hi], axis=1), rhs)
                den = ext[:, LANES:] + jnp.exp2(jnp.where(lane_q, d_lo, d_hi))
                o_ref[rows, c * LANES:(c + 1) * LANES] = (ext[:, :LANES] / den).astype(o_ref.dtype)
            yield


def _out_stage(at_bf16, h, wo_ref, bo_ref, gpost_ref, gfpre_ref, wup_ref, wdn_ref, gfpost_ref, between=None):
    m = _mm(at_bf16, wo_ref[...]) + bo_ref[...]
    h1 = h + _rms(m, gpost_ref[...])
    f = _ffn(_rms(h1, gfpre_ref[...]).astype(BF16), wup_ref, wdn_ref, between)
    return h1 + _rms(f, gfpost_ref[...])


def _swa_out_body(n_tiles, nt, sink_ref, q_ref, kc_ref, kp_ref, vc_ref, vp_ref, h_ref, ats_ref, hs_ref,
                  wo_ref, bo_ref, gpost_ref, gfpre_ref, wup_ref, wdn_ref, gfpost_ref,
                  y_ref, ys_ref, attn_scr):
    i = pl.program_id(0)
    consts = (wo_ref, bo_ref, gpost_ref, gfpre_ref, wup_ref, wdn_ref, gfpost_ref)

    def attention():
        has_prev = jnp.minimum(lax.rem(i, nt), 1)
        k_full = jnp.concatenate([kp_ref[...], kc_ref[...]], axis=0)
        v_full = jnp.concatenate([vp_ref[...], vc_ref[...]], axis=0)
        return _swa_attend_units(sink_ref, q_ref, k_full, v_full, has_prev, attn_scr)

    @pl.when(i == 0)
    def _():
        for _ in attention():
            pass

    @pl.when((i > 0) & (i < n_tiles))
    def _():
        at_prev = attn_scr[...]
        units = attention()
        y_ref[...] = _out_stage(at_prev, h_ref[...], *consts, between=lambda: next(units, None))
        for _ in units:
            pass

    @pl.when(i == n_tiles)
    def _():
        y_ref[...] = _out_stage(attn_scr[...], h_ref[...], *consts)

    @pl.when(i == n_tiles + 1)
    def _():
        ys_ref[...] = _out_stage(ats_ref[...].astype(BF16), hs_ref[...], *consts)


def _swa_out(sinks, q, k, v, h, attn_s, h_s, consts, batch, seq):
    tq = SWA_TILE
    w = SWA_WINDOW
    nt = seq // tq
    n_tiles = batch * nt
    n_s = h_s.shape[0]

    def att_tile(i):
        return (jnp.minimum(i, n_tiles - 1), 0)

    def prev_block(i):
        g = jnp.minimum(i, n_tiles - 1)
        return (g * (tq // w) - jnp.minimum(lax.rem(g, nt), 1), 0)

    def out_tile(i):
        return (jnp.clip(i - 1, 0, n_tiles - 1), 0)

    whole = lambda arr: pl.BlockSpec(arr.shape, lambda i: (0, 0))
    const_specs, const_args = _const_plan(consts)
    return pl.pallas_call(
        functools.partial(_swa_out_body, n_tiles, nt),
        grid=(n_tiles + 2,),
        in_specs=[pl.BlockSpec(memory_space=pltpu.SMEM),
                  pl.BlockSpec((tq, SWA_Q), att_tile), pl.BlockSpec((tq, SWA_KV), att_tile),
                  pl.BlockSpec((w, SWA_KV), prev_block), pl.BlockSpec((tq, SWA_KV), att_tile),
                  pl.BlockSpec((w, SWA_KV), prev_block), pl.BlockSpec((tq, D_MODEL), out_tile),
                  whole(attn_s), whole(h_s)] + const_specs,
        out_specs=[pl.BlockSpec((tq, D_MODEL), out_tile), whole(h_s)],
        out_shape=[jax.ShapeDtypeStruct((batch * seq, D_MODEL), F32), jax.ShapeDtypeStruct((n_s, D_MODEL), F32)],
        scratch_shapes=[pltpu.VMEM((tq, SWA_Q), BF16)],
        compiler_params=pltpu.CompilerParams(dimension_semantics=("arbitrary",), vmem_limit_bytes=VMEM_LIMIT),
        name="swa_out",
    )(sinks, q, k, k, v, v, h, attn_s, h_s, *const_args)


def _swa_sample_body(sk_ref, q_ref, kn_ref, vn_ref, ck_ref, cv_ref, o_ref, nk_ref, nv_ref):
    bt = q_ref.shape[0]
    w = ck_ref.shape[2]
    hd = SWA_HEAD_DIM
    hgroup = lax.broadcasted_iota(jnp.int32, (SWA_HEADS, 1), 0) // SWA_GROUP
    newest = lax.broadcasted_iota(jnp.int32, (SWA_KV, w), 1) == w - 1
    kn_t = kn_ref[...].T
    vn_t = vn_ref[...].T
    scale = hd ** -0.5
    sk = sk_ref[...]
    groups = [slice(g * hd, (g + 1) * hd) for g in range(SWA_KV_HEADS)]

    def per_head(pieces):
        out = pieces[0]
        for g in range(1, SWA_KV_HEADS):
            out = jnp.where(hgroup == g, pieces[g], out)
        return out

    for j in range(bt):
        nk_ref[j] = jnp.where(newest, kn_t[:, j:j + 1], pltpu.roll(ck_ref[j], w - 1, axis=1))
        nv_ref[j] = jnp.where(newest, vn_t[:, j:j + 1], pltpu.roll(cv_ref[j], w - 1, axis=1))

    s_old, s_new, v_sel = [], [], []
    for j in range(bt):
        q = q_ref[j]
        qb = q.astype(BF16)
        s_old.append(per_head([_mm(qb, ck_ref[j, rows, :].astype(BF16)) for rows in groups]))
        k_sel = per_head([kn_ref[j:j + 1, cols] for cols in groups])
        v_sel.append(per_head([vn_ref[j:j + 1, cols] for cols in groups]))
        s_new.append(jnp.sum(q * k_sel, axis=-1, keepdims=True))
    s_old = jnp.stack(s_old, axis=0) * scale
    s_new = jnp.stack(s_new, axis=0) * scale
    m = jnp.maximum(jnp.maximum(jnp.max(s_old, axis=-1, keepdims=True), s_new), sk)
    p_old = jnp.exp(s_old - m)
    p_new = jnp.exp(s_new - m)
    inv = 1.0 / (jnp.sum(p_old, axis=-1, keepdims=True) + p_new + jnp.exp(sk - m))
    p_old = p_old.astype(BF16)
    for j in range(bt):
        o = per_head([_mm_nt(p_old[j], cv_ref[j, rows, :].astype(BF16)) for rows in groups])
        o_ref[j] = (o + p_new[j] * v_sel[j]) * inv[j]


def _swa_sample(sinks, q3, k_new, v_new, cache_k, cache_v):
    bt = SEQ_TILE
    nb, _, w = cache_k.shape
    assert w == LANES
    row = lambda width: pl.BlockSpec((bt, width), lambda i: (i, 0))
    q_spec = pl.BlockSpec((bt, SWA_HEADS, SWA_HEAD_DIM), lambda i: (i, 0, 0))
    c_spec = pl.BlockSpec((bt, SWA_KV, w), lambda i: (i, 0, 0))
    return pl.pallas_call(
        _swa_sample_body,
        grid=(nb // bt,),
        in_specs=[pl.BlockSpec((SWA_HEADS, 1), lambda i: (0, 0)), q_spec, row(SWA_KV), row(SWA_KV), c_spec, c_spec],
        out_specs=[q_spec, c_spec, c_spec],
        out_shape=[jax.ShapeDtypeStruct((nb, SWA_HEADS, SWA_HEAD_DIM), F32),
                   jax.ShapeDtypeStruct(cache_k.shape, F32), jax.ShapeDtypeStruct(cache_v.shape, F32)],
        compiler_params=pltpu.CompilerParams(dimension_semantics=("arbitrary",)),
        name="swa_sample",
    )(sinks, q3, k_new, v_new, cache_k, cache_v)


def _rope_parts(tile_rows, tile_starts):
    d = jnp.arange(LANES) % SWA_HEAD_DIM
    inv = jnp.power(ROPE_THETA, -(d % ROPE_HALF).astype(F32) * 2.0 / ROPE_DIM)
    local = jnp.arange(tile_rows, dtype=F32)[:, None] * inv[None, :]
    base = jnp.repeat(tile_starts.astype(F32), SUBLANES)[:, None] * inv[None, :]
    return jnp.cos(local), jnp.sin(local), jnp.cos(base), jnp.sin(base)


def kernel(x_prompt, x_sample, state_gla, cache_swa_k, cache_swa_v, gla_w_in, gla_w_gate2, gla_b_gate, gla_g_head, gla_w_out, swa_w_qkv, swa_b_qkv, swa_sinks, swa_w_out, swa_b_out, norm_mix_pre, norm_mix_post, norm_ffn_pre, norm_ffn_post, ffn_w_up, ffn_w_down):
    batch, seq, _ = x_prompt.shape
    dec_batch, dec_seq, _ = x_sample.shape
    assert dec_seq == 1 and seq % SWA_WINDOW == 0
    past_len = seq
    n_p, n_s = batch * seq, dec_batch * dec_seq
    xp = x_prompt.reshape(n_p, D_MODEL)
    xs = x_sample.reshape(n_s, D_MODEL)

    w_g2 = gla_w_gate2[0].astype(BF16)
    b_g = gla_b_gate[0][None, :]
    g_head = gla_g_head[0][None, :]
    b_qkv = swa_b_qkv[0][None, :]
    b_sout = swa_b_out[0][None, :]
    row = lambda t, i: t[i][None, :]
    in0_widths = [GLA_DK, GLA_DK, GLA_DV, GLA_DV, GLA_DK]
    mid_widths = [D_MODEL, SWA_Q, SWA_KV, SWA_KV]

    tm, ts = TOKEN_TILE, n_s
    ((q, k, v, r, la), (qs, ks, vs, rs, las)), (w_gout, w_up0, w_dn0, w_qkv) = _tok_call(
        _in0_body,
        [(n_p, tm, [(xp, None)], in0_widths, [F32, F32, BF16, F32, F32]),
         (n_s, ts, [(xs, None)], in0_widths, [F32] * 5)],
        [row(norm_mix_pre, 0), gla_w_in[0].T, w_g2, b_g], "in0",
        casts=[(gla_w_out, 0), (ffn_w_up, 0), (ffn_w_down, 0), (swa_w_qkv, 0)], convert_first=(1,))
    o, s_fin_p = _gla_prompt(q, k, v, la, batch, seq)
    o_s, s_new = _gla_sample(qs, ks, vs, las, state_gla[0])
    lc, ls, bc, bs = _rope_parts(tm, jnp.arange(seq // tm) * tm)
    lc_s, ls_s, bc_s, bs_s = _rope_parts(1, jnp.full((1,), past_len))
    same = lambda t: (0, 0)
    start_map = lambda t: (t % (seq // tm), 0)
    rope_p = [(lc, same), (ls, same), (bc, start_map, SUBLANES), (bs, start_map, SUBLANES)]
    rope_s = [(jnp.broadcast_to(lc_s, (ts, LANES)), None), (jnp.broadcast_to(ls_s, (ts, LANES)), None),
              (bc_s, same, SUBLANES), (bs_s, same, SUBLANES)]
    mid_consts = [g_head, w_gout, row(norm_mix_post, 0), row(norm_ffn_pre, 0), w_up0, w_dn0, row(norm_ffn_post, 0),
                  row(norm_mix_pre, 1), w_qkv, b_qkv]
    ((h2, q1, k1, v1), (h2s, q1s, k1s, v1s)), (w_sout, w_up1, w_dn1) = _tok_call(
        _mid_body,
        [(n_p, tm, [(o, None), (r, None), (xp, None)] + rope_p, mid_widths, [F32] * 4),
         (n_s, ts, [(o_s, None), (rs, None), (xs, None)] + rope_s, mid_widths, [F32] * 4)],
        mid_consts, "mid",
        casts=[(swa_w_out, 0), (ffn_w_up, 1), (ffn_w_down, 1)])
    out_consts = [w_sout, b_sout, row(norm_mix_post, 1), row(norm_ffn_pre, 1), w_up1, w_dn1, row(norm_ffn_post, 1)]
    win = cache_swa_k.shape[2]
    to_t = lambda c: jnp.transpose(c[0].reshape(dec_batch, win, SWA_KV), (0, 2, 1))
    from_t = lambda c: jnp.transpose(c, (0, 2, 1)).reshape(1, dec_batch, win, SWA_KV_HEADS, SWA_HEAD_DIM)
    attn_s, nk, nv = _swa_sample(swa_sinks[0][:, None], q1s.reshape(n_s, SWA_HEADS, SWA_HEAD_DIM), k1s, v1s,
                                 to_t(cache_swa_k), to_t(cache_swa_v))
    y_p, y_s = _swa_out(swa_sinks[0], q1, k1, v1, h2, attn_s.reshape(n_s, SWA_Q), h2s, out_consts, batch, seq)
    wp = min(SWA_WINDOW, seq)
    tail = lambda t: t.reshape(batch, seq, SWA_KV)[:, seq - wp:].reshape(batch, wp, SWA_KV_HEADS, SWA_HEAD_DIM)
    k_tail, v_tail = tail(k1), tail(v1)

    return (y_p.reshape(batch, seq, D_MODEL), y_s.reshape(dec_batch, dec_seq, D_MODEL),
            s_fin_p[None], s_new[None], k_tail[None], v_tail[None], from_t(nk), from_t(nv))
```

```python
import functools

import jax
import jax.numpy as jnp
from jax import lax
from jax.experimental import pallas as pl
from jax.experimental.pallas import tpu as pltpu

F32 = jnp.float32
BF16 = jnp.bfloat16

D_MODEL = 1024
D_FF = 4 * D_MODEL
NORM_EPS = 1e-6

GLA_HEADS = 4
GLA_DK = D_MODEL // 2
GLA_DV = D_MODEL
GLA_DK_HEAD = GLA_DK // GLA_HEADS
GLA_DV_HEAD = GLA_DV // GLA_HEADS
GLA_GATE_RANK = 16
GLA_TAU = 16.0
GLA_CHUNK = 64
GLA_MAIN = 2 * GLA_DK + 2 * GLA_DV

SWA_HEAD_DIM = 64
SWA_HEADS = D_MODEL // SWA_HEAD_DIM
SWA_KV_HEADS = 4
SWA_GROUP = SWA_HEADS // SWA_KV_HEADS
SWA_WINDOW = 128
SWA_Q = SWA_HEADS * SWA_HEAD_DIM
SWA_KV = SWA_KV_HEADS * SWA_HEAD_DIM
SWA_QKV = SWA_Q + 2 * SWA_KV
ROPE_THETA = 500000.0
ROPE_DIM = SWA_HEAD_DIM // 4
ROPE_HALF = ROPE_DIM // 2

LANES = 128
SUBLANES = 8
FFN_CHUNK = 512
TOKEN_TILE = 512
GLA_TILE = 256
GLA_STEP = 2048
SWA_TILE = 512
SEQ_TILE = 8
STATE_TILE = 16
STATE_BUFFERS = 3
VMEM_LIMIT = 58 * 1024 * 1024
NEG_BIG = -1e30
LOG2E = 1.4426950408889634


def _mm(a, b):
    return jnp.dot(a, b, preferred_element_type=F32)


def _mm_nt(a, b):
    return lax.dot_general(a, b, (((1,), (1,)), ((), ())), preferred_element_type=F32)


def _mm_tn(a, b):
    return lax.dot_general(a, b, (((0,), (0,)), ((), ())), preferred_element_type=F32)


def _rms(x, g):
    ms = jnp.mean(x * x, axis=-1, keepdims=True)
    return x * lax.rsqrt(ms + NORM_EPS) * g


def _split3(x):
    hi = x.astype(BF16)
    r1 = x - hi.astype(F32)
    mid = r1.astype(BF16)
    lo = (r1 - mid.astype(F32)).astype(BF16)
    return hi, mid, lo


def _ffn(a_bf16, wup_ref, wdn_ref, between=None):
    n_slices = D_FF // FFN_CHUNK
    cols = [slice(c * FFN_CHUNK, (c + 1) * FFN_CHUNK) for c in range(n_slices)]
    acc = None
    u_next = _mm(a_bf16, wup_ref[:, cols[0]])
    for c in range(n_slices):
        u = u_next
        if c + 1 < n_slices:
            u_next = _mm(a_bf16, wup_ref[:, cols[c + 1]])
        u = jnp.square(jnp.maximum(u, 0.0)).astype(BF16)
        p = _mm(u, wdn_ref[cols[c], :])
        acc = p if acc is None else acc + p
        if between is not None:
            between()
    return acc


def _in0_body(x_ref, g_ref, wt_ref, wg_ref, bg_ref, q_ref, k_ref, v_ref, r_ref, la_ref):
    a = _rms(x_ref[...], g_ref[...]).astype(BF16)
    proj = lambda lo, hi: _mm_nt(a, wt_ref[lo:hi, :])
    z = proj(GLA_MAIN, GLA_MAIN + GLA_GATE_RANK).astype(BF16)
    q_ref[...] = proj(0, GLA_DK) * (GLA_DK_HEAD ** -0.5)
    k_ref[...] = proj(GLA_DK, 2 * GLA_DK)
    zg = _mm(z, wg_ref[...]) + bg_ref[...]
    la_ref[...] = (jnp.minimum(zg, 0.0) - jnp.log1p(jnp.exp(-jnp.abs(zg)))) * (1.0 / GLA_TAU)
    for c in range(GLA_DV // 512):
        cols = slice(c * 512, (c + 1) * 512)
        v_ref[:, cols] = proj(2 * GLA_DK + c * 512, 2 * GLA_DK + (c + 1) * 512).astype(v_ref.dtype)
        r_ref[:, cols] = proj(2 * GLA_DK + GLA_DV + c * 512, 2 * GLA_DK + GLA_DV + (c + 1) * 512)


def _mid_body(o_ref, r_ref, h_ref, lc_ref, ls_ref, bc_ref, bs_ref,
              gh_ref, wo_ref, gpost_ref, gfpre_ref, wup_ref, wdn_ref, gfpost_ref,
              gpre1_ref, wqkv_ref, bqkv_ref,
              h2_ref, q1_ref, k1_ref, v1_ref):
    m = None
    for hh in range(GLA_HEADS):
        cols = slice(hh * GLA_DV_HEAD, (hh + 1) * GLA_DV_HEAD)
        on = _rms(o_ref[:, cols], gh_ref[...])
        r = r_ref[:, cols]
        u = (on * (r * (1.0 / (1.0 + jnp.exp(-r))))).astype(BF16)
        p = _mm(u, wo_ref[cols, :])
        m = p if m is None else m + p
    h1 = h_ref[...] + _rms(m, gpost_ref[...])
    f = _ffn(_rms(h1, gfpre_ref[...]).astype(BF16), wup_ref, wdn_ref)
    h2 = h1 + _rms(f, gfpost_ref[...])
    h2_ref[...] = h2
    a3 = _rms(h2, gpre1_ref[...]).astype(BF16)
    lc, ls, bc, bs = lc_ref[...], ls_ref[...], bc_ref[0:1, :], bs_ref[0:1, :]
    cos_t = bc * lc - bs * ls
    sin_t = bs * lc + bc * ls
    d = lax.broadcasted_iota(jnp.int32, (1, LANES), 1) % SWA_HEAD_DIM
    rc = jnp.where(d < ROPE_DIM, cos_t, 1.0)
    ra = jnp.where(d < ROPE_HALF, -sin_t, 0.0)
    rb = jnp.where(d < ROPE_HALF, 0.0, jnp.where(d < ROPE_DIM, sin_t, 0.0))
    wide = 2 * LANES
    for c2 in range((SWA_Q + SWA_KV) // wide):
        x2 = _mm(a3, wqkv_ref[:, c2 * wide:(c2 + 1) * wide]) + bqkv_ref[:, c2 * wide:(c2 + 1) * wide]
        for half in range(2):
            c = 2 * c2 + half
            x = x2[:, half * LANES:(half + 1) * LANES]
            y = x * rc + pltpu.roll(x, LANES - ROPE_HALF, axis=1) * ra + pltpu.roll(x, ROPE_HALF, axis=1) * rb
            if c < SWA_Q // LANES:
                q1_ref[:, c * LANES:(c + 1) * LANES] = y
            else:
                k1_ref[:, c * LANES - SWA_Q:(c + 1) * LANES - SWA_Q] = y
    v1_ref[...] = _mm(a3, wqkv_ref[:, SWA_Q + SWA_KV:SWA_QKV]) + bqkv_ref[:, SWA_Q + SWA_KV:SWA_QKV]


def _const_plan(const_inputs):
    in_specs, args = [], []
    for entry in const_inputs:
        if isinstance(entry, tuple):
            arr, layer = entry
            spec = pl.BlockSpec((None,) + arr.shape[1:], lambda i, layer=layer: (layer, 0, 0), pipeline_mode=pl.Buffered(1))
        else:
            arr = entry
            spec = pl.BlockSpec(arr.shape, lambda i: (0, 0), pipeline_mode=pl.Buffered(1))
        in_specs.append(spec)
        args.append(arr)
    return in_specs, args


def _tok_call(body, groups, const_inputs, name, casts=(), convert_first=()):
    in_specs, args, out_specs, out_shape = [], [], [], []
    ranges, start = [], 0
    for n_rows, tm, row_inputs, out_widths, out_dtypes in groups:
        assert n_rows % tm == 0
        count = n_rows // tm
        local = lambda i, start=start, count=count: jnp.clip(i - start, 0, count - 1)
        mode = dict(pipeline_mode=pl.Buffered(1)) if count == 1 else {}
        for arr, imap, *block_rows in row_inputs:
            imap = imap if imap is not None else (lambda t: (t, 0))
            rows = block_rows[0] if block_rows else tm
            in_specs.append(pl.BlockSpec((rows, arr.shape[1]), lambda i, imap=imap, local=local: imap(local(i)), **mode))
            args.append(arr)
        for w, dt in zip(out_widths, out_dtypes):
            out_specs.append(pl.BlockSpec((tm, w), lambda i, local=local: (local(i), 0)))
            out_shape.append(jax.ShapeDtypeStruct((n_rows, w), dt))
        ranges.append((start, count, len(row_inputs), len(out_widths)))
        start += count
    n_row_refs = len(in_specs)
    n_const = len(const_inputs)
    n_group_outs = len(out_specs)
    const_specs, const_args = _const_plan(const_inputs)
    cast_steps = ranges[0][1]
    cast_specs, cast_args = [], []
    for arr, layer in casts:
        _, k_dim, n = arr.shape
        assert k_dim % (cast_steps * 16) == 0
        rows = k_dim // cast_steps
        block = lambda i: jnp.minimum(i, cast_steps - 1)
        cast_specs.append(pl.BlockSpec((None, rows, n), lambda i, layer=layer, block=block: (layer, block(i), 0)))
        cast_args.append(arr)
        out_specs.append(pl.BlockSpec((rows, n), lambda i, block=block: (block(i), 0)))
        out_shape.append(jax.ShapeDtypeStruct((k_dim, n), BF16))

    def kern(*refs):
        row_refs, raw_consts = refs[:n_row_refs], refs[n_row_refs:n_row_refs + n_const]
        n_in_refs = n_row_refs + n_const + len(casts)
        cast_in = refs[n_row_refs + n_const:n_in_refs]
        out_refs = refs[n_in_refs:n_in_refs + n_group_outs]
        cast_out = refs[n_in_refs + n_group_outs:n_in_refs + n_group_outs + len(casts)]
        own_bf16 = refs[n_in_refs + n_group_outs + len(casts):]
        i = pl.program_id(0)

        @pl.when(i == 0)
        def _():
            for j, s_ref in zip(convert_first, own_bf16):
                s_ref[...] = raw_consts[j][...].astype(BF16)

        const_refs = list(raw_consts)
        for j, s_ref in zip(convert_first, own_bf16):
            const_refs[j] = s_ref
        r0 = o0 = 0
        for g, (first, count, n_in, n_out) in enumerate(ranges):
            ins, outs = row_refs[r0:r0 + n_in], out_refs[o0:o0 + n_out]
            r0, o0 = r0 + n_in, o0 + n_out

            @pl.when((i >= first) & (i < first + count))
            def _(ins=ins, outs=outs, g=g):
                body(*ins, *const_refs, *outs)
                if g == 0:
                    for x_ref, o_ref in zip(cast_in, cast_out):
                        o_ref[...] = x_ref[...].astype(BF16)

    outs = pl.pallas_call(
        kern,
        grid=(start,),
        in_specs=in_specs + const_specs + cast_specs,
        out_specs=out_specs,
        out_shape=out_shape,
        scratch_shapes=[pltpu.VMEM(const_inputs[j].shape, BF16) for j in convert_first],
        compiler_params=pltpu.CompilerParams(dimension_semantics=("arbitrary",), vmem_limit_bytes=VMEM_LIMIT),
        name=name,
    )(*args, *const_args, *cast_args)
    grouped, o0 = [], 0
    for _, _, _, n_out in ranges:
        grouped.append(list(outs[o0:o0 + n_out]))
        o0 += n_out
    return grouped, list(outs[n_group_outs:])


def _gla_prompt_body(q_ref, k_ref, v_ref, la_ref, o_ref, sfin_ref, st_ref):
    t = pl.program_id(1)

    @pl.when(t == 0)
    def _():
        st_ref[...] = jnp.zeros_like(st_ref)

    for base in range(0, q_ref.shape[0], GLA_TILE):
        _gla_tile(slice(base, base + GLA_TILE), q_ref, k_ref, v_ref, la_ref, o_ref, st_ref)

    @pl.when(t == pl.num_programs(1) - 1)
    def _():
        sfin_ref[0] = st_ref[...]


def _gla_tile(tile, q_ref, k_ref, v_ref, la_ref, o_ref, st_ref):
    tg = GLA_TILE
    c_len = GLA_CHUNK
    n_chunks = tg // c_len
    chunk_rows = [slice(ci * c_len, (ci + 1) * c_len) for ci in range(n_chunks)]
    row = lax.broadcasted_iota(jnp.int32, (tg, tg), 0)
    col = lax.broadcasted_iota(jnp.int32, (tg, tg), 1)
    lower_b = (row // c_len == col // c_len) & (col <= row)
    lower = jnp.where(lower_b, 1.0, 0.0).astype(BF16)
    hi, mid, lo = _split3(la_ref[tile, :])
    cum = _mm(lower, hi) + _mm(lower, mid) + _mm(lower, lo)
    lane_chunk = lax.broadcasted_iota(jnp.int32, (GLA_DK_HEAD, tg), 1) // c_len

    qd, att, kv = [], [], []
    for h in range(GLA_HEADS):
        kc = slice(h * GLA_DK_HEAD, (h + 1) * GLA_DK_HEAD)
        cum_h = cum[:, kc]
        tot_h = jnp.concatenate([jnp.broadcast_to(cum_h[r.stop - 1:r.stop, :], (c_len, GLA_DK_HEAD)) for r in chunk_rows], axis=0)
        k_h = k_ref[tile, kc]
        qd_h = (q_ref[tile, kc] * jnp.exp(cum_h)).astype(BF16)
        ki_h = (k_h * jnp.exp(-cum_h)).astype(BF16)
        ke_t = (k_h * jnp.exp(tot_h - cum_h)).T
        qd.append(qd_h)
        att.append(jnp.where(lower_b, _mm_nt(qd_h, ki_h), 0.0).astype(BF16))
        v_h = v_ref[tile, h * GLA_DV_HEAD:(h + 1) * GLA_DV_HEAD]
        kv.append([_mm(jnp.where(lane_chunk == ci, ke_t, 0.0).astype(BF16), v_h) for ci in range(n_chunks)])

    s_before = []
    for h in range(GLA_HEADS):
        kc = slice(h * GLA_DK_HEAD, (h + 1) * GLA_DK_HEAD)
        st = st_ref[h]
        starts = []
        for ci, r in enumerate(chunk_rows):
            starts.append(st.astype(BF16))
            e_col = jnp.exp(cum[r.stop - SUBLANES:r.stop, kc]).T[:, SUBLANES - 1:SUBLANES]
            st = e_col * st + kv[h][ci]
        st_ref[h] = st
        s_before.append(starts)

    for h in range(GLA_HEADS):
        kc = slice(h * GLA_DK_HEAD, (h + 1) * GLA_DK_HEAD)
        vc = slice(h * GLA_DV_HEAD, (h + 1) * GLA_DV_HEAD)
        o_intra = _mm(att[h], v_ref[tile, vc])
        for ci, r in enumerate(chunk_rows):
            o_ref[tile.start + r.start:tile.start + r.stop, vc] = o_intra[r] + _mm(qd[h][r], s_before[h][ci])


def _gla_prompt(q, k, v, la, batch, seq):
    tg = GLA_STEP
    nt = seq // tg
    qk_spec = pl.BlockSpec((tg, GLA_DK), lambda b, t: (b * nt + t, 0))
    v_spec = pl.BlockSpec((tg, GLA_DV), lambda b, t: (b * nt + t, 0))
    st_shape = (GLA_HEADS, GLA_DK_HEAD, GLA_DV_HEAD)
    return pl.pallas_call(
        _gla_prompt_body,
        grid=(batch, nt),
        in_specs=[qk_spec, qk_spec, v_spec, qk_spec],
        out_specs=[v_spec, pl.BlockSpec((1,) + st_shape, lambda b, t: (b, 0, 0, 0))],
        out_shape=[jax.ShapeDtypeStruct((batch * seq, GLA_DV), F32),
                   jax.ShapeDtypeStruct((batch,) + st_shape, F32)],
        scratch_shapes=[pltpu.VMEM(st_shape, F32)],
        compiler_params=pltpu.CompilerParams(dimension_semantics=("arbitrary", "arbitrary"), vmem_limit_bytes=VMEM_LIMIT),
        name="gla_prompt",
    )(q, k, v, la)


def _gla_sample_body(q_ref, k_ref, v_ref, la_ref, s_hbm, o_ref, sn_ref, s_buf, sem):
    bt = q_ref.shape[0]
    i = pl.program_id(0)
    n = pl.num_programs(0)

    def fetch(block):
        slot = lax.rem(block, STATE_BUFFERS)
        return pltpu.make_async_copy(s_hbm.at[pl.ds(block * bt, bt)], s_buf.at[slot], sem.at[slot])

    @pl.when(i == 0)
    def _():
        fetch(0).start()
        fetch(1).start()

    @pl.when(i + 2 < n)
    def _():
        fetch(i + 2).start()

    fetch(i).wait()
    s_ref = s_buf.at[lax.rem(i, STATE_BUFFERS)]
    for h in range(GLA_HEADS):
        kc = slice(h * GLA_DK_HEAD, (h + 1) * GLA_DK_HEAD)
        vc = slice(h * GLA_DV_HEAD, (h + 1) * GLA_DV_HEAD)
        a_t = jnp.exp(la_ref[:, kc]).T
        k_t = k_ref[:, kc].T
        q_t = q_ref[:, kc].T
        for j in range(bt):
            s_new = a_t[:, j:j + 1] * s_ref[j, h] + k_t[:, j:j + 1] * v_ref[j:j + 1, vc]
            sn_ref[j, h] = s_new
            o_ref[j:j + 1, vc] = jnp.sum(q_t[:, j:j + 1] * s_new, axis=0, keepdims=True)


def _gla_sample(q, k, v, la, state):
    bt = STATE_TILE
    nb = q.shape[0]
    row = lambda w: pl.BlockSpec((bt, w), lambda i: (i, 0))
    st_block = (bt, GLA_HEADS, GLA_DK_HEAD, GLA_DV_HEAD)
    st_spec = pl.BlockSpec(st_block, lambda i: (i, 0, 0, 0))
    assert nb // bt >= 2
    return pl.pallas_call(
        _gla_sample_body,
        grid=(nb // bt,),
        in_specs=[row(GLA_DK), row(GLA_DK), row(GLA_DV), row(GLA_DK), pl.BlockSpec(memory_space=pl.ANY)],
        out_specs=[row(GLA_DV), st_spec],
        out_shape=[jax.ShapeDtypeStruct((nb, GLA_DV), F32), jax.ShapeDtypeStruct(state.shape, F32)],
        scratch_shapes=[pltpu.VMEM((STATE_BUFFERS,) + st_block, F32), pltpu.SemaphoreType.DMA((STATE_BUFFERS,))],
        compiler_params=pltpu.CompilerParams(dimension_semantics=("arbitrary",), vmem_limit_bytes=VMEM_LIMIT),
        name="gla_sample",
    )(q, k, v, la, state)


def _swa_attend_units(sink_ref, q_ref, k_full, v_full, has_prev, o_ref):
    w = SWA_WINDOW
    hd = SWA_HEAD_DIM
    tq = q_ref.shape[0]
    nkv = k_full.shape[0]
    lane_q = lax.broadcasted_iota(jnp.int32, (w, LANES), 1) < hd
    lane_kv = lax.broadcasted_iota(jnp.int32, (nkv, LANES), 1) < hd
    i = lax.broadcasted_iota(jnp.int32, (w, 2 * w), 0)
    j = lax.broadcasted_iota(jnp.int32, (w, 2 * w), 1)
    band = jnp.where(j < w, jnp.where(j >= i, 1, 0), jnp.where(j - w <= i, 1, 0))
    band_first = jnp.where(j < w, has_prev, 1) * band
    lane_2w = lax.broadcasted_iota(jnp.int32, (2 * w, LANES), 1) < hd
    ones_lo = jnp.where(lane_2w, 1.0, 0.0).astype(BF16)
    ones_hi = jnp.where(lane_2w, 0.0, 1.0).astype(BF16)
    c2 = (hd ** -0.5) * LOG2E

    k_prep, v_prep = [], []
    for p in range(SWA_KV // LANES):
        cols = slice(p * LANES, (p + 1) * LANES)
        k_p, v_p = k_full[:, cols], v_full[:, cols]
        k_prep.append((k_p.astype(BF16), pltpu.roll(k_p, hd, axis=1).astype(BF16)))
        v_r = pltpu.roll(v_p, hd, axis=1)
        v_prep.append(((jnp.where(lane_kv, v_p, 0.0).astype(BF16), jnp.where(lane_kv, 0.0, v_r).astype(BF16)),
                       (jnp.where(lane_kv, v_r, 0.0).astype(BF16), jnp.where(lane_kv, 0.0, v_p).astype(BF16))))

    def softmax_part(s, hh, mask):
        s2 = jnp.where(mask, s, NEG_BIG)
        sk2 = jnp.full((w, 1), sink_ref[hh], F32) * LOG2E
        m2 = jnp.maximum(jnp.max(s2, axis=-1, keepdims=True), sk2)
        return jnp.exp2(s2 - m2).astype(BF16), sk2 - m2

    for b in range(tq // w):
        rows = slice(b * w, (b + 1) * w)
        krows = slice(b * w, (b + 2) * w)
        mask = (band_first if b == 0 else band) > 0
        for p in range(SWA_KV // LANES):
            q_lo, q_hi = [], []
            for x in range(4):
                q_c = q_ref[rows, (4 * p + x) * LANES:(4 * p + x + 1) * LANES] * c2
                q_lo.append(jnp.where(lane_q, q_c, 0.0).astype(BF16))
                q_hi.append(jnp.where(lane_q, 0.0, q_c).astype(BF16))
            s_self = _mm_nt(jnp.concatenate([q_lo[0], q_lo[1], q_hi[2], q_hi[3]], axis=0), k_prep[p][0][krows])
            s_roll = _mm_nt(jnp.concatenate([q_hi[0], q_hi[1], q_lo[2], q_lo[3]], axis=0), k_prep[p][1][krows])
            for x in range(4):
                c = 4 * p + x
                gh = x // 2
                xr = slice(x * w, (x + 1) * w)
                s_lo, s_hi = (s_self[xr], s_roll[xr]) if gh == 0 else (s_roll[xr], s_self[xr])
                p_lo, d_lo = softmax_part(s_lo, 2 * c, mask)
                p_hi, d_hi = softmax_part(s_hi, 2 * c + 1, mask)
                v_lo, v_hi = v_prep[p][gh]
                rhs = jnp.concatenate([jnp.concatenate([v_lo[krows], ones_lo], axis=1),
                                       jnp.concatenate([v_hi[krows], ones_hi], axis=1)], axis=0)
                ext = _mm(jnp.concatenate([p_lo, p_hi], axis=1), rhs)
                den = ext[:, LANES:] + jnp.exp2(jnp.where(lane_q, d_lo, d_hi))
                o_ref[rows, c * LANES:(c + 1) * LANES] = (ext[:, :LANES] / den).astype(o_ref.dtype)
            yield


def _out_stage(at_bf16, h, wo_ref, bo_ref, gpost_ref, gfpre_ref, wup_ref, wdn_ref, gfpost_ref, between=None):
    m = _mm(at_bf16, wo_ref[...]) + bo_ref[...]
    h1 = h + _rms(m, gpost_ref[...])
    f = _ffn(_rms(h1, gfpre_ref[...]).astype(BF16), wup_ref, wdn_ref, between)
    return h1 + _rms(f, gfpost_ref[...])


def _swa_out_body(n_tiles, nt, sink_ref, q_ref, kc_ref, kp_ref, vc_ref, vp_ref, h_ref, ats_ref, hs_ref,
                  wo_ref, bo_ref, gpost_ref, gfpre_ref, wup_ref, wdn_ref, gfpost_ref,
                  y_ref, ys_ref, attn_scr):
    i = pl.program_id(0)
    consts = (wo_ref, bo_ref, gpost_ref, gfpre_ref, wup_ref, wdn_ref, gfpost_ref)

    def attention():
        has_prev = jnp.minimum(lax.rem(i, nt), 1)
        k_full = jnp.concatenate([kp_ref[...], kc_ref[...]], axis=0)
        v_full = jnp.concatenate([vp_ref[...], vc_ref[...]], axis=0)
        return _swa_attend_units(sink_ref, q_ref, k_full, v_full, has_prev, attn_scr)

    @pl.when(i == 0)
    def _():
        for _ in attention():
            pass

    @pl.when((i > 0) & (i < n_tiles))
    def _():
        at_prev = attn_scr[...]
        units = attention()
        y_ref[...] = _out_stage(at_prev, h_ref[...], *consts, between=lambda: next(units, None))
        for _ in units:
            pass

    @pl.when(i == n_tiles)
    def _():
        y_ref[...] = _out_stage(attn_scr[...], h_ref[...], *consts)

    @pl.when(i == n_tiles + 1)
    def _():
        ys_ref[...] = _out_stage(ats_ref[...].astype(BF16), hs_ref[...], *consts)


def _swa_out(sinks, q, k, v, h, attn_s, h_s, consts, batch, seq):
    tq = SWA_TILE
    w = SWA_WINDOW
    nt = seq // tq
    n_tiles = batch * nt
    n_s = h_s.shape[0]

    def att_tile(i):
        return (jnp.minimum(i, n_tiles - 1), 0)

    def prev_block(i):
        g = jnp.minimum(i, n_tiles - 1)
        return (g * (tq // w) - jnp.minimum(lax.rem(g, nt), 1), 0)

    def out_tile(i):
        return (jnp.clip(i - 1, 0, n_tiles - 1), 0)

    whole = lambda arr: pl.BlockSpec(arr.shape, lambda i: (0, 0))
    const_specs, const_args = _const_plan(consts)
    return pl.pallas_call(
        functools.partial(_swa_out_body, n_tiles, nt),
        grid=(n_tiles + 2,),
        in_specs=[pl.BlockSpec(memory_space=pltpu.SMEM),
                  pl.BlockSpec((tq, SWA_Q), att_tile), pl.BlockSpec((tq, SWA_KV), att_tile),
                  pl.BlockSpec((w, SWA_KV), prev_block), pl.BlockSpec((tq, SWA_KV), att_tile),
                  pl.BlockSpec((w, SWA_KV), prev_block), pl.BlockSpec((tq, D_MODEL), out_tile),
                  whole(attn_s), whole(h_s)] + const_specs,
        out_specs=[pl.BlockSpec((tq, D_MODEL), out_tile), whole(h_s)],
        out_shape=[jax.ShapeDtypeStruct((batch * seq, D_MODEL), F32), jax.ShapeDtypeStruct((n_s, D_MODEL), F32)],
        scratch_shapes=[pltpu.VMEM((tq, SWA_Q), BF16)],
        compiler_params=pltpu.CompilerParams(dimension_semantics=("arbitrary",), vmem_limit_bytes=VMEM_LIMIT),
        name="swa_out",
    )(sinks, q, k, k, v, v, h, attn_s, h_s, *const_args)


def _swa_sample_body(sk_ref, q_ref, kn_ref, vn_ref, ck_ref, cv_ref, o_ref, nk_ref, nv_ref):
    bt = q_ref.shape[0]
    w = ck_ref.shape[2]
    hd = SWA_HEAD_DIM
    hgroup = lax.broadcasted_iota(jnp.int32, (SWA_HEADS, 1), 0) // SWA_GROUP
    newest = lax.broadcasted_iota(jnp.int32, (SWA_KV, w), 1) == w - 1
    kn_t = kn_ref[...].T
    vn_t = vn_ref[...].T
    scale = hd ** -0.5
    sk = sk_ref[...]
    groups = [slice(g * hd, (g + 1) * hd) for g in range(SWA_KV_HEADS)]

    def per_head(pieces):
        out = pieces[0]
        for g in range(1, SWA_KV_HEADS):
            out = jnp.where(hgroup == g, pieces[g], out)
        return out

    for j in range(bt):
        nk_ref[j] = jnp.where(newest, kn_t[:, j:j + 1], pltpu.roll(ck_ref[j], w - 1, axis=1))
        nv_ref[j] = jnp.where(newest, vn_t[:, j:j + 1], pltpu.roll(cv_ref[j], w - 1, axis=1))

    s_old, s_new, v_sel = [], [], []
    for j in range(bt):
        q = q_ref[j]
        qb = q.astype(BF16)
        s_old.append(per_head([_mm(qb, ck_ref[j, rows, :].astype(BF16)) for rows in groups]))
        k_sel = per_head([kn_ref[j:j + 1, cols] for cols in groups])
        v_sel.append(per_head([vn_ref[j:j + 1, cols] for cols in groups]))
        s_new.append(jnp.sum(q * k_sel, axis=-1, keepdims=True))
    s_old = jnp.stack(s_old, axis=0) * scale
    s_new = jnp.stack(s_new, axis=0) * scale
    m = jnp.maximum(jnp.maximum(jnp.max(s_old, axis=-1, keepdims=True), s_new), sk)
    p_old = jnp.exp(s_old - m)
    p_new = jnp.exp(s_new - m)
    inv = 1.0 / (jnp.sum(p_old, axis=-1, keepdims=True) + p_new + jnp.exp(sk - m))
    p_old = p_old.astype(BF16)
    for j in range(bt):
        o = per_head([_mm_nt(p_old[j], cv_ref[j, rows, :].astype(BF16)) for rows in groups])
        o_ref[j] = (o + p_new[j] * v_sel[j]) * inv[j]


def _swa_sample(sinks, q3, k_new, v_new, cache_k, cache_v):
    bt = SEQ_TILE
    nb, _, w = cache_k.shape
    assert w == LANES
    row = lambda width: pl.BlockSpec((bt, width), lambda i: (i, 0))
    q_spec = pl.BlockSpec((bt, SWA_HEADS, SWA_HEAD_DIM), lambda i: (i, 0, 0))
    c_spec = pl.BlockSpec((bt, SWA_KV, w), lambda i: (i, 0, 0))
    return pl.pallas_call(
        _swa_sample_body,
        grid=(nb // bt,),
        in_specs=[pl.BlockSpec((SWA_HEADS, 1), lambda i: (0, 0)), q_spec, row(SWA_KV), row(SWA_KV), c_spec, c_spec],
        out_specs=[q_spec, c_spec, c_spec],
        out_shape=[jax.ShapeDtypeStruct((nb, SWA_HEADS, SWA_HEAD_DIM), F32),
                   jax.ShapeDtypeStruct(cache_k.shape, F32), jax.ShapeDtypeStruct(cache_v.shape, F32)],
        compiler_params=pltpu.CompilerParams(dimension_semantics=("arbitrary",)),
        name="swa_sample",
    )(sinks, q3, k_new, v_new, cache_k, cache_v)


def _rope_parts(tile_rows, tile_starts):
    d = jnp.arange(LANES) % SWA_HEAD_DIM
    inv = jnp.power(ROPE_THETA, -(d % ROPE_HALF).astype(F32) * 2.0 / ROPE_DIM)
    local = jnp.arange(tile_rows, dtype=F32)[:, None] * inv[None, :]
    base = jnp.repeat(tile_starts.astype(F32), SUBLANES)[:, None] * inv[None, :]
    return jnp.cos(local), jnp.sin(local), jnp.cos(base), jnp.sin(base)


def kernel(x_prompt, x_sample, state_gla, cache_swa_k, cache_swa_v, gla_w_in, gla_w_gate2, gla_b_gate, gla_g_head, gla_w_out, swa_w_qkv, swa_b_qkv, swa_sinks, swa_w_out, swa_b_out, norm_mix_pre, norm_mix_post, norm_ffn_pre, norm_ffn_post, ffn_w_up, ffn_w_down):
    batch, seq, _ = x_prompt.shape
    dec_batch, dec_seq, _ = x_sample.shape
    assert dec_seq == 1 and seq % SWA_WINDOW == 0
    past_len = seq
    n_p, n_s = batch * seq, dec_batch * dec_seq
    xp = x_prompt.reshape(n_p, D_MODEL)
    xs = x_sample.reshape(n_s, D_MODEL)

    w_g2 = gla_w_gate2[0].astype(BF16)
    b_g = gla_b_gate[0][None, :]
    g_head = gla_g_head[0][None, :]
    b_qkv = swa_b_qkv[0][None, :]
    b_sout = swa_b_out[0][None, :]
    row = lambda t, i: t[i][None, :]
    in0_widths = [GLA_DK, GLA_DK, GLA_DV, GLA_DV, GLA_DK]
    mid_widths = [D_MODEL, SWA_Q, SWA_KV, SWA_KV]

    tm, ts = TOKEN_TILE, n_s
    ((q, k, v, r, la), (qs, ks, vs, rs, las)), (w_gout, w_up0, w_dn0, w_qkv) = _tok_call(
        _in0_body,
        [(n_p, tm, [(xp, None)], in0_widths, [F32, F32, BF16, F32, F32]),
         (n_s, ts, [(xs, None)], in0_widths, [F32] * 5)],
        [row(norm_mix_pre, 0), gla_w_in[0].T, w_g2, b_g], "in0",
        casts=[(gla_w_out, 0), (ffn_w_up, 0), (ffn_w_down, 0), (swa_w_qkv, 0)], convert_first=(1,))
    o, s_fin_p = _gla_prompt(q, k, v, la, batch, seq)
    o_s, s_new = _gla_sample(qs, ks, vs, las, state_gla[0])
    lc, ls, bc, bs = _rope_parts(tm, jnp.arange(seq // tm) * tm)
    lc_s, ls_s, bc_s, bs_s = _rope_parts(1, jnp.full((1,), past_len))
    same = lambda t: (0, 0)
    start_map = lambda t: (t % (seq // tm), 0)
    rope_p = [(lc, same), (ls, same), (bc, start_map, SUBLANES), (bs, start_map, SUBLANES)]
    rope_s = [(jnp.broadcast_to(lc_s, (ts, LANES)), None), (jnp.broadcast_to(ls_s, (ts, LANES)), None),
              (bc_s, same, SUBLANES), (bs_s, same, SUBLANES)]
    mid_consts = [g_head, w_gout, row(norm_mix_post, 0), row(norm_ffn_pre, 0), w_up0, w_dn0, row(norm_ffn_post, 0),
                  row(norm_mix_pre, 1), w_qkv, b_qkv]
    ((h2, q1, k1, v1), (h2s, q1s, k1s, v1s)), (w_sout, w_up1, w_dn1) = _tok_call(
        _mid_body,
        [(n_p, tm, [(o, None), (r, None), (xp, None)] + rope_p, mid_widths, [F32] * 4),
         (n_s, ts, [(o_s, None), (rs, None), (xs, None)] + rope_s, mid_widths, [F32] * 4)],
        mid_consts, "mid",
        casts=[(swa_w_out, 0), (ffn_w_up, 1), (ffn_w_down, 1)])
    out_consts = [w_sout, b_sout, row(norm_mix_post, 1), row(norm_ffn_pre, 1), w_up1, w_dn1, row(norm_ffn_post, 1)]
    win = cache_swa_k.shape[2]
    to_t = lambda c: jnp.transpose(c[0].reshape(dec_batch, win, SWA_KV), (0, 2, 1))
    from_t = lambda c: jnp.transpose(c, (0, 2, 1)).reshape(1, dec_batch, win, SWA_KV_HEADS, SWA_HEAD_DIM)
    attn_s, nk, nv = _swa_sample(swa_sinks[0][:, None], q1s.reshape(n_s, SWA_HEADS, SWA_HEAD_DIM), k1s, v1s,
                                 to_t(cache_swa_k), to_t(cache_swa_v))
    y_p, y_s = _swa_out(swa_sinks[0], q1, k1, v1, h2, attn_s.reshape(n_s, SWA_Q), h2s, out_consts, batch, seq)
    wp = min(SWA_WINDOW, seq)
    tail = lambda t: t.reshape(batch, seq, SWA_KV)[:, seq - wp:].reshape(batch, wp, SWA_KV_HEADS, SWA_HEAD_DIM)
    k_tail, v_tail = tail(k1), tail(v1)

    return (y_p.reshape(batch, seq, D_MODEL), y_s.reshape(dec_batch, dec_seq, D_MODEL),
            s_fin_p[None], s_new[None], k_tail[None], v_tail[None], from_t(nk), from_t(nv))
```
